```python
import math
import jax, jax.numpy as jnp
from jax import lax
import numpy as np

D_MODEL = 1024
BATCH = 2
SEQ = 8192
DEPTH = 1

N_META = 16
BLOCK = 128
PAD_FRONT = BLOCK - N_META
N_HEADS = 8
QK_NOPE = 64
QK_ROPE = 32
V_DIM = 64
Q_LORA = 256
KV_LORA = 128
ROPE_THETA = 10000.0
ATTN_SCALE = (QK_NOPE + QK_ROPE) ** -0.5
ATTN_WIDTH = N_HEADS * V_DIM
CONV_DIM = 512
CONV_GROUPS = 8
CONV_W = 3
MIX_WIDTH = ATTN_WIDTH + CONV_DIM
IN_COLS = Q_LORA + KV_LORA + QK_ROPE + 3 * CONV_DIM
SPLITS = [Q_LORA, Q_LORA + KV_LORA, Q_LORA + KV_LORA + QK_ROPE,
          Q_LORA + KV_LORA + QK_ROPE + CONV_DIM, Q_LORA + KV_LORA + QK_ROPE + 2 * CONV_DIM]
N_GROUPS = 4
EXPERTS_PER_GROUP = 8
N_EXPERTS = N_GROUPS * EXPERTS_PER_GROUP
TOP_K = 2
D_FF = 256
EPS = 1e-6
NEG_INF = -1e30

kernel_name = "hymba_mla_shortconv_hier_moe"


def rms_norm(x, g):
    xf = x.astype(jnp.float32)
    y = xf * lax.rsqrt(jnp.mean(xf * xf, axis=-1, keepdims=True) + EPS)
    return (y * g.astype(jnp.float32)).astype(x.dtype)


def rope_angles(pos):
    inv_freq = 1.0 / (ROPE_THETA ** (jnp.arange(0, QK_ROPE, 2, dtype=jnp.float32) / QK_ROPE))
    ang = pos.astype(jnp.float32)[..., None] * inv_freq
    return jnp.cos(ang), jnp.sin(ang)


def apply_rope(x, cos, sin):
    xf = x.astype(jnp.float32)
    x1, x2 = jnp.split(xf, 2, axis=-1)
    return jnp.concatenate([x1 * cos - x2 * sin, x2 * cos + x1 * sin], axis=-1).astype(x.dtype)


def causal_block_attention(q_nope, q_rope, k_nope, k_rope, v):
    bsz, plen = q_nope.shape[0], q_nope.shape[1]
    nb = plen // BLOCK
    qn = q_nope.reshape(bsz, nb, BLOCK, N_HEADS, QK_NOPE).transpose(1, 0, 2, 3, 4)
    qr = q_rope.reshape(bsz, nb, BLOCK, N_HEADS, QK_ROPE).transpose(1, 0, 2, 3, 4)
    starts = jnp.arange(nb, dtype=jnp.int32) * BLOCK
    k_idx = jnp.arange(plen, dtype=jnp.int32)

    def one_block(args):
        qn_b, qr_b, s0 = args
        s = (jnp.einsum('bqhd,bkhd->bhqk', qn_b, k_nope, preferred_element_type=jnp.float32)
             + jnp.einsum('bqhr,bkr->bhqk', qr_b, k_rope, preferred_element_type=jnp.float32)) * ATTN_SCALE
        q_idx = s0 + jnp.arange(BLOCK, dtype=jnp.int32)
        mask = (k_idx[None, :] >= PAD_FRONT) & (k_idx[None, :] <= q_idx[:, None])
        s = jnp.where(mask[None, None], s, NEG_INF)
        p = jax.nn.softmax(s, axis=-1).astype(v.dtype)
        return jnp.einsum('bhqk,bkhd->bqhd', p, v)

    out = lax.map(one_block, (qn, qr, starts))
    return out.transpose(1, 0, 2, 3, 4).reshape(bsz, plen, N_HEADS * V_DIM)


def hybrid_mixer(h, cos, sin, w_in, q_norm, w_uq, kv_norm, w_ukv, conv_w,
                 attn_out_norm, conv_out_norm, w_out):
    bsz = h.shape[0]
    hp = jnp.pad(h, ((0, 0), (PAD_FRONT, 0), (0, 0)))
    plen = hp.shape[1]
    z = hp @ w_in
    c_q, c_kv, k_pe, gate_b, gate_c, x_in = jnp.split(z, SPLITS, axis=-1)

    q = (rms_norm(c_q, q_norm) @ w_uq).reshape(bsz, plen, N_HEADS, QK_NOPE + QK_ROPE)
    q_nope = q[..., :QK_NOPE]
    q_rope = apply_rope(q[..., QK_NOPE:], cos[:, :, None, :], sin[:, :, None, :])
    kv = (rms_norm(c_kv, kv_norm) @ w_ukv).reshape(bsz, plen, N_HEADS, QK_NOPE + V_DIM)
    k_nope, v = kv[..., :QK_NOPE], kv[..., QK_NOPE:]
    k_rope = apply_rope(k_pe, cos, sin)
    attn = causal_block_attention(q_nope, q_rope, k_nope, k_rope, v)

    u = gate_c * x_in
    u_pad = jnp.pad(u, ((0, 0), (CONV_W - 1, 0), (0, 0)))
    y = sum(conv_w[k] * u_pad[:, k:k + plen] for k in range(CONV_W))
    conv = gate_b * y

    merged = jnp.concatenate([rms_norm(attn[:, PAD_FRONT:], attn_out_norm),
                              rms_norm(conv[:, PAD_FRONT:], conv_out_norm)], axis=-1)
    return merged @ w_out


def hier_moe(h, w_group_router, b_group_router, w_expert_router, b_expert_router,
             w_gate, w_up, w_down, w_sh_gate, w_sh_up, w_sh_down):
    bsz, slen, d = h.shape
    t = h.reshape(-1, d)
    g_prob = jax.nn.softmax((t @ w_group_router).astype(jnp.float32) + b_group_router.astype(jnp.float32), axis=-1)
    g_w, g_idx = lax.top_k(g_prob, 1)
    e_logits = ((t @ w_expert_router).astype(jnp.float32) + b_expert_router.astype(jnp.float32))
    e_logits = e_logits.reshape(-1, N_GROUPS, EXPERTS_PER_GROUP)
    e_logits = jnp.take_along_axis(e_logits, g_idx[:, :, None], axis=1)[:, 0]
    e_prob = jax.nn.softmax(e_logits, axis=-1)
    e_w, e_idx = lax.top_k(e_prob, TOP_K)
    e_w = e_w / jnp.sum(e_w, axis=-1, keepdims=True)
    weights = g_w * e_w
    eid = g_idx * EXPERTS_PER_GROUP + e_idx
    combine = jnp.einsum('nk,nke->ne', weights,
                         jax.nn.one_hot(eid, N_EXPERTS, dtype=jnp.float32)).astype(t.dtype)
    a = jnp.einsum('nd,edf->nef', t, w_gate)
    b = jnp.einsum('nd,edf->nef', t, w_up)
    hid = jax.nn.silu(a) * b * combine[:, :, None]
    routed = jnp.einsum('nef,efd->nd', hid, w_down)
    shared = (jax.nn.silu(t @ w_sh_gate) * (t @ w_sh_up)) @ w_sh_down
    return (routed + shared).reshape(bsz, slen, d)


def setup_inputs(seed: int = 0) -> dict:
    key = jax.random.key(seed)
    ks = jax.random.split(key, 32)
    f32 = jnp.float32

    def nrm(k, shape, fan_in):
        return jax.random.normal(k, shape, f32) * (fan_in ** -0.5)

    def gain(k, shape):
        return 1.0 + 0.05 * jax.random.normal(k, shape, f32)

    L = DEPTH
    return {
        "x": jax.random.normal(ks[0], (BATCH, SEQ, D_MODEL), f32),
        "positions": jnp.broadcast_to(jnp.arange(SEQ, dtype=jnp.int32), (BATCH, SEQ)),
        "meta_tokens": jax.random.normal(ks[1], (N_META, D_MODEL), f32),
        "pre_mix_norm": gain(ks[2], (L, D_MODEL)),
        "w_in": nrm(ks[3], (L, D_MODEL, IN_COLS), D_MODEL),
        "q_norm": gain(ks[4], (L, Q_LORA)),
        "w_uq": nrm(ks[5], (L, Q_LORA, N_HEADS * (QK_NOPE + QK_ROPE)), Q_LORA),
        "kv_norm": gain(ks[6], (L, KV_LORA)),
        "w_ukv": nrm(ks[7], (L, KV_LORA, N_HEADS * (QK_NOPE + V_DIM)), KV_LORA),
        "conv_w": nrm(ks[8], (L, CONV_W, CONV_DIM), CONV_W),
        "attn_out_norm": gain(ks[9], (L, ATTN_WIDTH)),
        "conv_out_norm": gain(ks[10], (L, CONV_DIM)),
        "w_out": nrm(ks[11], (L, MIX_WIDTH, D_MODEL), MIX_WIDTH),
        "post_mix_norm": gain(ks[12], (L, D_MODEL)),
        "pre_ffn_norm": gain(ks[13], (L, D_MODEL)),
        "w_group_router": nrm(ks[14], (L, D_MODEL, N_GROUPS), D_MODEL),
        "b_group_router": 0.01 * jax.random.normal(ks[15], (L, N_GROUPS), f32),
        "w_expert_router": nrm(ks[16], (L, D_MODEL, N_EXPERTS), D_MODEL),
        "b_expert_router": 0.01 * jax.random.normal(ks[17], (L, N_EXPERTS), f32),
        "w_gate": nrm(ks[18], (L, N_EXPERTS, D_MODEL, D_FF), D_MODEL),
        "w_up": nrm(ks[19], (L, N_EXPERTS, D_MODEL, D_FF), D_MODEL),
        "w_down": nrm(ks[20], (L, N_EXPERTS, D_FF, D_MODEL), D_FF),
        "w_sh_gate": nrm(ks[21], (L, D_MODEL, D_FF), D_MODEL),
        "w_sh_up": nrm(ks[22], (L, D_MODEL, D_FF), D_MODEL),
        "w_sh_down": nrm(ks[23], (L, D_FF, D_MODEL), D_FF),
        "post_ffn_norm": gain(ks[24], (L, D_MODEL)),
    }


def reference(x, positions, meta_tokens, pre_mix_norm, w_in, q_norm, w_uq, kv_norm, w_ukv, conv_w,
              attn_out_norm, conv_out_norm, w_out, post_mix_norm, pre_ffn_norm,
              w_group_router, b_group_router, w_expert_router, b_expert_router,
              w_gate, w_up, w_down, w_sh_gate, w_sh_up, w_sh_down, post_ffn_norm):
    bsz = x.shape[0]
    meta = jnp.broadcast_to(meta_tokens[None].astype(x.dtype), (bsz, N_META, D_MODEL))
    h = jnp.concatenate([meta, x], axis=1)
    rope_pos = jnp.concatenate([
        jnp.zeros((bsz, PAD_FRONT), jnp.int32),
        jnp.broadcast_to(jnp.arange(N_META, dtype=jnp.int32), (bsz, N_META)),
        positions.astype(jnp.int32) + N_META], axis=1)
    cos, sin = rope_angles(rope_pos)

    for l in range(DEPTH):
        mix = hybrid_mixer(rms_norm(h, pre_mix_norm[l]), cos, sin, w_in[l], q_norm[l], w_uq[l],
                           kv_norm[l], w_ukv[l], conv_w[l], attn_out_norm[l], conv_out_norm[l], w_out[l])
        h = h + rms_norm(mix, post_mix_norm[l])
        ffn = hier_moe(rms_norm(h, pre_ffn_norm[l]), w_group_router[l], b_group_router[l],
                       w_expert_router[l], b_expert_router[l], w_gate[l], w_up[l], w_down[l],
                       w_sh_gate[l], w_sh_up[l], w_sh_down[l])
        h = h + rms_norm(ffn, post_ffn_norm[l])

    return h[:, N_META:]
```

```python
import functools

import jax
import jax.numpy as jnp
from jax import lax
from jax.experimental import pallas as pl
from jax.experimental.pallas import tpu as pltpu

D_MODEL = 1024
N_META = 16
BLOCK = 128
PAD_FRONT = BLOCK - N_META
N_HEADS = 8
QK_NOPE = 64
QK_ROPE = 32
V_DIM = 64
Q_LORA = 256
KV_LORA = 128
ROPE_THETA = 10000.0
ATTN_SCALE = (QK_NOPE + QK_ROPE) ** -0.5
ATTN_WIDTH = N_HEADS * V_DIM
CONV_DIM = 512
CONV_W = 3
N_GROUPS = 4
EXPERTS_PER_GROUP = 8
N_EXPERTS = N_GROUPS * EXPERTS_PER_GROUP
D_FF = 256
EPS = 1e-6
NEG_INF = -1e30

LANES = 128
KEY_WIDTH = 2 * LANES
IN_COLS_PADDED = Q_LORA + KV_LORA + LANES + 3 * CONV_DIM

F32 = jnp.float32
BF16 = jnp.bfloat16
VMEM_LIMIT = 56 * 1024 * 1024


def _rms(x, g):
    return x * lax.rsqrt(jnp.mean(x * x, axis=-1, keepdims=True) + EPS) * g


def _dot(a, b):
    return jnp.dot(a, b, preferred_element_type=F32)


def _dot_nt(a, b):
    return lax.dot_general(a, b, (((1,), (1,)), ((), ())), preferred_element_type=F32)


def _rope_table_kernel(pos_ref, invf_ref, out_ref):
    pos = pos_ref[...].astype(F32)
    ang = invf_ref[...] * pos
    c = jnp.cos(ang)
    s = jnp.sin(ang)
    planes = jnp.concatenate([c, c, -s, s, jnp.zeros((LANES - 4 * (QK_ROPE // 2), ang.shape[1]), F32)], axis=0)
    out_ref[...] = planes.T


def _rope_table(pos_row, inv_freq_col, chunk):
    n = pos_row.shape[1]
    return pl.pallas_call(
        _rope_table_kernel,
        grid=(n // chunk,),
        in_specs=[pl.BlockSpec((1, chunk), lambda i: (0, i)),
                  pl.BlockSpec((QK_ROPE // 2, 1), lambda i: (0, 0))],
        out_specs=pl.BlockSpec((chunk, LANES), lambda i: (i, 0)),
        out_shape=jax.ShapeDtypeStruct((n, LANES), F32),
        name="rope_table",
    )(pos_row, inv_freq_col)


def _premix_kernel(x_ref, tab_ref, carry_in_ref, g_pre_ref, w1_ref, gq_ref, gkv_ref, wqn_ref, bduk_ref,
                   wqr_ref, convw_ref, gconv_ref, qp_ref, kc_ref, convn_ref, utail_ref, carry_ref):
    t = pl.program_id(1)
    rows = x_ref.shape[1]

    @pl.when(t == 0)
    def _():
        carry_ref[...] = carry_in_ref[...]

    x = x_ref[0]
    xn = _rms(x, g_pre_ref[...]).astype(BF16)
    z = _dot(xn, w1_ref[...])
    tab = tab_ref[...]
    lane = lax.broadcasted_iota(jnp.int32, (rows, LANES), 1)

    def rope(pair):
        prod = pair * tab
        return jnp.where(lane < QK_ROPE, prod + pltpu.roll(prod, LANES - QK_ROPE, axis=1), 0.0)

    ckvn = _rms(z[:, Q_LORA:Q_LORA + KV_LORA], gkv_ref[...])
    krope = rope(z[:, Q_LORA + KV_LORA:Q_LORA + KV_LORA + LANES])
    kc_ref[0] = jnp.concatenate([ckvn, krope], axis=1).astype(BF16)

    cqn = _rms(z[:, :Q_LORA], gq_ref[...]).astype(BF16)
    qn = _dot(cqn, wqn_ref[...]).astype(BF16)
    qabs = _dot(qn, bduk_ref[...])
    qr = _dot(cqn, wqr_ref[...])
    for h in range(N_HEADS):
        sl = slice(h * LANES, (h + 1) * LANES)
        qp_ref[0, h, :, 0:LANES] = (qabs[:, sl] * ATTN_SCALE).astype(BF16)
        qp_ref[0, h, :, LANES:KEY_WIDTH] = (rope(qr[:, sl]) * ATTN_SCALE).astype(BF16)

    c0 = Q_LORA + KV_LORA + LANES
    gate_b = z[:, c0:c0 + CONV_DIM]
    u = z[:, c0 + CONV_DIM:c0 + 2 * CONV_DIM] * z[:, c0 + 2 * CONV_DIM:c0 + 3 * CONV_DIM]
    prev = carry_ref[...]
    row8 = lax.broadcasted_iota(jnp.int32, (8, CONV_DIM), 0)
    r1 = pltpu.roll(u, 1, axis=0)
    r2 = pltpu.roll(u, 2, axis=0)
    p1 = pltpu.roll(prev, 1, axis=0)
    p2 = pltpu.roll(prev, 2, axis=0)
    u1 = jnp.concatenate([jnp.where(row8 < 1, p1, r1[0:8]), r1[8:]], axis=0)
    u2 = jnp.concatenate([jnp.where(row8 < 2, p2, r2[0:8]), r2[8:]], axis=0)
    cw = convw_ref[...]
    y = cw[0:1] * u2 + cw[1:2] * u1 + cw[2:3] * u
    convn_ref[0] = _rms(gate_b * y, gconv_ref[...]).astype(BF16)
    tail = u[rows - 8:rows]
    carry_ref[...] = tail
    utail_ref[...] = tail


def _premix(x3, tab, tab_block0, carry_in, wts, rows):
    bsz, seq, _ = x3.shape
    nt = seq // rows
    g_pre, w1, gq, gkv, wqn, bduk, wqr, convw, gconv = wts
    const = lambda a: pl.BlockSpec(a.shape, lambda b, t: (0,) * a.ndim)
    return pl.pallas_call(
        _premix_kernel,
        grid=(bsz, nt),
        in_specs=[pl.BlockSpec((1, rows, D_MODEL), lambda b, t: (b, t, 0)),
                  pl.BlockSpec((rows, LANES), lambda b, t: (tab_block0 + b * nt + t, 0)),
                  const(carry_in), const(g_pre), const(w1), const(gq), const(gkv), const(wqn), const(bduk),
                  const(wqr), const(convw), const(gconv)],
        out_specs=[pl.BlockSpec((1, N_HEADS, rows, KEY_WIDTH), lambda b, t: (b, 0, t, 0)),
                   pl.BlockSpec((1, rows, KEY_WIDTH), lambda b, t: (b, t, 0)),
                   pl.BlockSpec((1, rows, CONV_DIM), lambda b, t: (b, t, 0)),
                   pl.BlockSpec((8, CONV_DIM), lambda b, t: (0, 0))],
        out_shape=[jax.ShapeDtypeStruct((bsz, N_HEADS, seq, KEY_WIDTH), BF16),
                   jax.ShapeDtypeStruct((bsz, seq, KEY_WIDTH), BF16),
                   jax.ShapeDtypeStruct((bsz, seq, CONV_DIM), BF16),
                   jax.ShapeDtypeStruct((8, CONV_DIM), F32)],
        scratch_shapes=[pltpu.VMEM((8, CONV_DIM), F32)],
        compiler_params=pltpu.CompilerParams(dimension_semantics=("arbitrary", "arbitrary"),
                                             vmem_limit_bytes=VMEM_LIMIT),
        name="premix",
    )(x3, tab, carry_in, g_pre, w1, gq, gkv, wqn, bduk, wqr, convw, gconv)


def _attention_kernel(q_ref, k_ref, km_ref, o_ref, *, bq, bk):
    i = pl.program_id(1)
    m_rows = N_HEADS * bq
    q = q_ref[0].reshape(m_rows, KEY_WIDTH)

    def step(carry, kb, mask):
        m, l, acc = carry
        s = _dot_nt(q, kb)
        if mask is not None:
            s = jnp.where(mask, s, NEG_INF)
        m_new = jnp.maximum(m, jnp.max(s, axis=1, keepdims=True))
        alpha = jnp.exp(m - m_new)
        p = jnp.exp(s - m_new)
        l = alpha * l + jnp.sum(p, axis=1, keepdims=True)
        acc = alpha * acc + _dot(p.astype(BF16), kb[:, :KV_LORA])
        return m_new, l, acc

    col = lax.broadcasted_iota(jnp.int32, (m_rows, BLOCK), 1)
    init = (jnp.full((m_rows, 1), NEG_INF, F32), jnp.zeros((m_rows, 1), F32), jnp.zeros((m_rows, KV_LORA), F32))
    carry = step(init, km_ref[...], col >= PAD_FRONT)

    n_full = (i * bq) // bk

    def body(j, c):
        kb = k_ref[0, pl.ds(pl.multiple_of(j * bk, bk), bk), :]
        return step(c, kb, None)

    carry = lax.fori_loop(0, n_full, body, carry)

    kb = k_ref[0, pl.ds(pl.multiple_of(n_full * bk, bk), bk), :]
    qpos = i * bq + lax.rem(lax.broadcasted_iota(jnp.int32, (m_rows, bk), 0), bq)
    kpos = n_full * bk + lax.broadcasted_iota(jnp.int32, (m_rows, bk), 1)
    m, l, acc = step(carry, kb, kpos <= qpos)

    o = acc / l
    for h in range(N_HEADS):
        o_ref[0, :, h * KV_LORA:(h + 1) * KV_LORA] = o[h * bq:(h + 1) * bq].astype(BF16)


def _attention(qp, kc, kmeta, bq, bk):
    bsz, _, seq, _ = qp.shape
    return pl.pallas_call(
        functools.partial(_attention_kernel, bq=bq, bk=bk),
        grid=(bsz, seq // bq),
        in_specs=[pl.BlockSpec((1, N_HEADS, bq, KEY_WIDTH), lambda b, i: (b, 0, i, 0)),
                  pl.BlockSpec((1, seq, KEY_WIDTH), lambda b, i: (b, 0, 0)),
                  pl.BlockSpec((BLOCK, KEY_WIDTH), lambda b, i: (0, 0))],
        out_specs=pl.BlockSpec((1, bq, N_HEADS * KV_LORA), lambda b, i: (b, i, 0)),
        out_shape=jax.ShapeDtypeStruct((bsz, seq, N_HEADS * KV_LORA), BF16),
        compiler_params=pltpu.CompilerParams(dimension_semantics=("arbitrary", "arbitrary"),
                                             vmem_limit_bytes=VMEM_LIMIT),
        name="attention",
    )(qp, kc, kmeta)


def _postmix_kernel(x_ref, olat_ref, convn_ref, wuv_ref, gattn_ref, wout_ref, gpost_ref, gffn_ref, wr_ref, br_ref,
                    h1_ref, t_ref, comb_ref):
    rows = x_ref.shape[0]
    attn = _dot(olat_ref[...], wuv_ref[...])
    attn_n = _rms(attn, gattn_ref[...]).astype(BF16)
    wout = wout_ref[...]
    mix = _dot(attn_n, wout[:ATTN_WIDTH]) + _dot(convn_ref[...], wout[ATTN_WIDTH:])
    h1 = x_ref[...] + _rms(mix, gpost_ref[...])
    h1_ref[...] = h1
    t = _rms(h1, gffn_ref[...]).astype(BF16)
    t_ref[...] = t

    logits = _dot(t, wr_ref[...]) + br_ref[...]
    lane = lax.broadcasted_iota(jnp.int32, (rows, LANES), 1)

    def first_argmax(v):
        mx = jnp.max(v, axis=1, keepdims=True)
        return mx, jnp.min(jnp.where(v == mx, lane, LANES), axis=1, keepdims=True)

    gl = jnp.where(lane < N_GROUPS, logits[:, :LANES], NEG_INF)
    gmax, gidx = first_argmax(gl)
    g_w = 1.0 / jnp.sum(jnp.exp(gl - gmax), axis=1, keepdims=True)
    el = jnp.where((lane >> 3) == gidx, logits[:, LANES:], NEG_INF)
    m1, i1 = first_argmax(el)
    m2, i2 = first_argmax(jnp.where(lane == i1, NEG_INF, el))
    e2 = jnp.exp(m2 - m1)
    den = 1.0 + e2
    comb_ref[...] = jnp.where(lane == i1, g_w / den, 0.0) + jnp.where(lane == i2, g_w * e2 / den, 0.0)


def _postmix(x2, olat2, convn2, wts, rows):
    n = x2.shape[0]
    wuv, gattn, wout, gpost, gffn, wr, br = wts
    const = lambda a: pl.BlockSpec(a.shape, lambda i: (0,) * a.ndim)
    tile = lambda w: pl.BlockSpec((rows, w), lambda i: (i, 0))
    return pl.pallas_call(
        _postmix_kernel,
        grid=(n // rows,),
        in_specs=[tile(D_MODEL), tile(N_HEADS * KV_LORA), tile(CONV_DIM),
                  const(wuv), const(gattn), const(wout), const(gpost), const(gffn), const(wr), const(br)],
        out_specs=[tile(D_MODEL), tile(D_MODEL), tile(LANES)],
        out_shape=[jax.ShapeDtypeStruct((n, D_MODEL), F32),
                   jax.ShapeDtypeStruct((n, D_MODEL), BF16),
                   jax.ShapeDtypeStruct((n, LANES), F32)],
        compiler_params=pltpu.CompilerParams(dimension_semantics=("arbitrary",), vmem_limit_bytes=VMEM_LIMIT),
        name="postmix",
    )(x2, olat2, convn2, wuv, gattn, wout, gpost, gffn, wr, br)


def _silu(a):
    return a * (1.0 / (1.0 + jnp.exp(-a)))


def _moe_kernel(t_ref, comb_ref, h1_ref, wg_ref, wu_ref, wd_ref, wsg_ref, wsu_ref, wsd_ref, gout_ref,
                out_ref, acc_ref):
    e = pl.program_id(1)
    t = t_ref[...]

    @pl.when(e == 0)
    def _():
        hid = _silu(_dot(t, wsg_ref[...])) * _dot(t, wsu_ref[...])
        acc_ref[...] = _dot(hid.astype(BF16), wsd_ref[...])

    lane = lax.broadcasted_iota(jnp.int32, comb_ref.shape, 1)
    cw = jnp.sum(jnp.where(lane == e, comb_ref[...], 0.0), axis=1, keepdims=True)
    hid = _silu(_dot(t, wg_ref[0])) * _dot(t, wu_ref[0]) * cw
    acc_ref[...] += _dot(hid.astype(BF16), wd_ref[0])

    @pl.when(e == N_EXPERTS - 1)
    def _():
        out_ref[...] = h1_ref[...] + _rms(acc_ref[...], gout_ref[...])


def _moe(t2, comb, h1, wts, rows):
    n = t2.shape[0]
    wg, wu, wd, wsg, wsu, wsd, gout = wts
    const = lambda a: pl.BlockSpec(a.shape, lambda i, e: (0,) * a.ndim)
    tile = lambda w: pl.BlockSpec((rows, w), lambda i, e: (i, 0))
    return pl.pallas_call(
        _moe_kernel,
        grid=(n // rows, N_EXPERTS),
        in_specs=[tile(D_MODEL), tile(LANES), tile(D_MODEL),
                  pl.BlockSpec((1, D_MODEL, D_FF), lambda i, e: (e, 0, 0)),
                  pl.BlockSpec((1, D_MODEL, D_FF), lambda i, e: (e, 0, 0)),
                  pl.BlockSpec((1, D_FF, D_MODEL), lambda i, e: (e, 0, 0)),
                  const(wsg), const(wsu), const(wsd), const(gout)],
        out_specs=tile(D_MODEL),
        out_shape=jax.ShapeDtypeStruct((n, D_MODEL), F32),
        scratch_shapes=[pltpu.VMEM((rows, D_MODEL), F32)],
        compiler_params=pltpu.CompilerParams(dimension_semantics=("arbitrary", "arbitrary"),
                                             vmem_limit_bytes=VMEM_LIMIT),
        name="moe",
    )(t2, comb, h1, wg, wu, wd, wsg, wsu, wsd, gout)


def _swap_halves(w):
    half = w.shape[-1] // 2
    return jnp.concatenate([w[..., half:], w[..., :half]], axis=-1)


def _block_diag(blocks):
    h, r, c = blocks.shape
    eye = jnp.eye(h, dtype=blocks.dtype)
    return jnp.einsum('hrc,hg->hrgc', blocks, eye).reshape(h * r, h * c)


def kernel(x, positions, meta_tokens, pre_mix_norm, w_in, q_norm, w_uq, kv_norm, w_ukv, conv_w, attn_out_norm,
           conv_out_norm, w_out, post_mix_norm, pre_ffn_norm, w_group_router, b_group_router, w_expert_router,
           b_expert_router, w_gate, w_up, w_down, w_sh_gate, w_sh_up, w_sh_down, post_ffn_norm):
    bsz, seq, d = x.shape
    n = bsz * seq
    row = lambda v: v.reshape(1, -1).astype(F32)

    wi = w_in[0]
    o_kv, o_pe, o_b = Q_LORA, Q_LORA + KV_LORA, Q_LORA + KV_LORA + QK_ROPE
    k_pe = wi[:, o_pe:o_b]
    w1 = jnp.concatenate([wi[:, :o_pe], k_pe, _swap_halves(k_pe), jnp.zeros((d, LANES - 2 * QK_ROPE), F32),
                          wi[:, o_b:]], axis=1).astype(BF16)
    wq = w_uq[0].reshape(Q_LORA, N_HEADS, QK_NOPE + QK_ROPE)
    wqn = wq[:, :, :QK_NOPE].reshape(Q_LORA, N_HEADS * QK_NOPE).astype(BF16)
    wq_rope = wq[:, :, QK_NOPE:]
    wqr = jnp.concatenate([wq_rope, _swap_halves(wq_rope),
                           jnp.zeros((Q_LORA, N_HEADS, LANES - 2 * QK_ROPE), F32)], axis=2)
    wqr = wqr.reshape(Q_LORA, N_HEADS * LANES).astype(BF16)
    wkv = w_ukv[0].reshape(KV_LORA, N_HEADS, QK_NOPE + V_DIM)
    bduk = _block_diag(jnp.transpose(wkv[:, :, :QK_NOPE], (1, 2, 0))).astype(BF16)
    wuv = _block_diag(jnp.transpose(wkv[:, :, QK_NOPE:], (1, 0, 2))).astype(BF16)
    premix_w = (row(pre_mix_norm[0]), w1, row(q_norm[0]), row(kv_norm[0]), wqn, bduk, wqr,
                conv_w[0].astype(F32), row(conv_out_norm[0]))
    wr = jnp.concatenate([w_group_router[0], jnp.zeros((d, LANES - N_GROUPS), F32),
                          w_expert_router[0], jnp.zeros((d, LANES - N_EXPERTS), F32)], axis=1).astype(BF16)
    br = jnp.concatenate([b_group_router[0], jnp.zeros((LANES - N_GROUPS,), F32),
                          b_expert_router[0], jnp.zeros((LANES - N_EXPERTS,), F32)]).reshape(1, -1).astype(F32)
    postmix_w = (wuv, row(attn_out_norm[0]), w_out[0].astype(BF16), row(post_mix_norm[0]), row(pre_ffn_norm[0]),
                 wr, br)
    moe_w = (w_gate[0].astype(BF16), w_up[0].astype(BF16), w_down[0].astype(BF16), w_sh_gate[0].astype(BF16),
             w_sh_up[0].astype(BF16), w_sh_down[0].astype(BF16), row(post_ffn_norm[0]))

    pos_all = jnp.concatenate([positions.astype(jnp.int32).reshape(-1) + N_META,
                               jnp.zeros((PAD_FRONT,), jnp.int32), jnp.arange(N_META, dtype=jnp.int32)])
    inv_freq = 1.0 / (ROPE_THETA ** (jnp.arange(0, QK_ROPE, 2, dtype=F32) / QK_ROPE))
    n_tab = n + BLOCK
    tab = _rope_table(pos_all.reshape(1, n_tab), inv_freq.reshape(-1, 1), n_tab // 3)

    meta_blk = jnp.pad(meta_tokens.astype(F32), ((PAD_FRONT, 0), (0, 0)))[None]
    zero_carry = jnp.zeros((8, CONV_DIM), F32)
    _, kmeta, _, meta_tail = _premix(meta_blk, tab, n // BLOCK, zero_carry, premix_w, BLOCK)
    qp, kc, convn, _ = _premix(x, tab, 0, meta_tail, premix_w, 512)

    olat = _attention(qp, kc, kmeta[0], 256, 512)

    h1, t, comb = _postmix(x.reshape(n, d), olat.reshape(n, -1), convn.reshape(n, -1), postmix_w, 512)
    out = _moe(t, comb, h1, moe_w, 1024)
    return out.reshape(bsz, seq, d)
```

```python
import functools
import math

import jax
import jax.numpy as jnp
from jax import lax
from jax.experimental import pallas as pl
from jax.experimental.pallas import tpu as pltpu

D_MODEL = 1024
N_META = 16
BLOCK = 128
PAD_FRONT = BLOCK - N_META
N_HEADS = 8
QK_NOPE = 64
QK_ROPE = 32
V_DIM = 64
Q_LORA = 256
KV_LORA = 128
ROPE_THETA = 10000.0
ATTN_SCALE = (QK_NOPE + QK_ROPE) ** -0.5
Q_SCALE = ATTN_SCALE * math.log2(math.e)
ATTN_WIDTH = N_HEADS * V_DIM
CONV_DIM = 512
CONV_W = 3
N_GROUPS = 4
EXPERTS_PER_GROUP = 8
N_EXPERTS = N_GROUPS * EXPERTS_PER_GROUP
D_FF = 256
EPS = 1e-6
NEG_INF = -1e30

LANES = 128
KEY_WIDTH = 2 * LANES
IN_COLS_PADDED = Q_LORA + KV_LORA + LANES + 3 * CONV_DIM

F32 = jnp.float32
BF16 = jnp.bfloat16
VMEM_LIMIT = 56 * 1024 * 1024


def _rms(x, g):
    return x * lax.rsqrt(jnp.mean(x * x, axis=-1, keepdims=True) + EPS) * g


def _dot(a, b):
    return jnp.dot(a, b, preferred_element_type=F32)


def _dot_nt(a, b):
    return lax.dot_general(a, b, (((1,), (1,)), ((), ())), preferred_element_type=F32)


def _rope_table_kernel(pos_ref, invf_ref, out_ref):
    pos = pos_ref[...].astype(F32)
    ang = invf_ref[...] * pos
    c = jnp.cos(ang)
    s = jnp.sin(ang)
    planes = jnp.concatenate([c, c, -s, s, jnp.zeros((LANES - 4 * (QK_ROPE // 2), ang.shape[1]), F32)], axis=0)
    out_ref[...] = planes.T


def _rope_table(pos_row, inv_freq_col, chunk):
    n = pos_row.shape[1]
    return pl.pallas_call(
        _rope_table_kernel,
        grid=(n // chunk,),
        in_specs=[pl.BlockSpec((1, chunk), lambda i: (0, i)),
                  pl.BlockSpec((QK_ROPE // 2, 1), lambda i: (0, 0))],
        out_specs=pl.BlockSpec((chunk, LANES), lambda i: (i, 0)),
        out_shape=jax.ShapeDtypeStruct((n, LANES), F32),
        name="rope_table",
    )(pos_row, inv_freq_col)


def _premix_kernel(x_ref, tab_ref, carry_in_ref, g_pre_ref, w1_ref, gq_ref, gkv_ref, wqn_ref, bduk_ref,
                   wqr_ref, convw_ref, gconv_ref, qp_ref, kc_ref, convn_ref, utail_ref, carry_ref):
    t = pl.program_id(1)
    rows = x_ref.shape[1]

    @pl.when(t == 0)
    def _():
        carry_ref[...] = carry_in_ref[...]

    x = x_ref[0]
    xn = _rms(x, g_pre_ref[...]).astype(BF16)
    z = _dot(xn, w1_ref[...])
    tab = tab_ref[...]
    lane = lax.broadcasted_iota(jnp.int32, (rows, LANES), 1)

    def rope(pair):
        prod = pair * tab
        return jnp.where(lane < QK_ROPE, prod + pltpu.roll(prod, LANES - QK_ROPE, axis=1), 0.0)

    ckvn = _rms(z[:, Q_LORA:Q_LORA + KV_LORA], gkv_ref[...])
    krope = rope(z[:, Q_LORA + KV_LORA:Q_LORA + KV_LORA + LANES])
    kc_ref[0] = jnp.concatenate([ckvn, krope], axis=1).astype(BF16)

    cqn = _rms(z[:, :Q_LORA], gq_ref[...]).astype(BF16)
    qn = _dot(cqn, wqn_ref[...]).astype(BF16)
    qabs = _dot(qn, bduk_ref[...])
    qr = _dot(cqn, wqr_ref[...])
    for h in range(N_HEADS):
        sl = slice(h * LANES, (h + 1) * LANES)
        qp_ref[0, h, :, 0:LANES] = (qabs[:, sl] * Q_SCALE).astype(BF16)
        qp_ref[0, h, :, LANES:KEY_WIDTH] = (rope(qr[:, sl]) * Q_SCALE).astype(BF16)

    c0 = Q_LORA + KV_LORA + LANES
    gate_b = z[:, c0:c0 + CONV_DIM]
    u = z[:, c0 + CONV_DIM:c0 + 2 * CONV_DIM] * z[:, c0 + 2 * CONV_DIM:c0 + 3 * CONV_DIM]
    prev = carry_ref[...]
    row8 = lax.broadcasted_iota(jnp.int32, (8, CONV_DIM), 0)
    r1 = pltpu.roll(u, 1, axis=0)
    r2 = pltpu.roll(u, 2, axis=0)
    p1 = pltpu.roll(prev, 1, axis=0)
    p2 = pltpu.roll(prev, 2, axis=0)
    u1 = jnp.concatenate([jnp.where(row8 < 1, p1, r1[0:8]), r1[8:]], axis=0)
    u2 = jnp.concatenate([jnp.where(row8 < 2, p2, r2[0:8]), r2[8:]], axis=0)
    cw = convw_ref[...]
    y = cw[0:1] * u2 + cw[1:2] * u1 + cw[2:3] * u
    convn_ref[0] = _rms(gate_b * y, gconv_ref[...]).astype(BF16)
    tail = u[rows - 8:rows]
    carry_ref[...] = tail
    utail_ref[...] = tail


def _premix(x3, tab, tab_block0, carry_in, wts, rows):
    bsz, seq, _ = x3.shape
    nt = seq // rows
    g_pre, w1, gq, gkv, wqn, bduk, wqr, convw, gconv = wts
    const = lambda a: pl.BlockSpec(a.shape, lambda b, t: (0,) * a.ndim)
    return pl.pallas_call(
        _premix_kernel,
        grid=(bsz, nt),
        in_specs=[pl.BlockSpec((1, rows, D_MODEL), lambda b, t: (b, t, 0)),
                  pl.BlockSpec((rows, LANES), lambda b, t: (tab_block0 + b * nt + t, 0)),
                  const(carry_in), const(g_pre), const(w1), const(gq), const(gkv), const(wqn), const(bduk),
                  const(wqr), const(convw), const(gconv)],
        out_specs=[pl.BlockSpec((1, N_HEADS, rows, KEY_WIDTH), lambda b, t: (b, 0, t, 0)),
                   pl.BlockSpec((1, rows, KEY_WIDTH), lambda b, t: (b, t, 0)),
                   pl.BlockSpec((1, rows, CONV_DIM), lambda b, t: (b, t, 0)),
                   pl.BlockSpec((8, CONV_DIM), lambda b, t: (0, 0))],
        out_shape=[jax.ShapeDtypeStruct((bsz, N_HEADS, seq, KEY_WIDTH), BF16),
                   jax.ShapeDtypeStruct((bsz, seq, KEY_WIDTH), BF16),
                   jax.ShapeDtypeStruct((bsz, seq, CONV_DIM), BF16),
                   jax.ShapeDtypeStruct((8, CONV_DIM), F32)],
        scratch_shapes=[pltpu.VMEM((8, CONV_DIM), F32)],
        compiler_params=pltpu.CompilerParams(dimension_semantics=("arbitrary", "arbitrary"),
                                             vmem_limit_bytes=VMEM_LIMIT),
        name="premix",
    )(x3, tab, carry_in, g_pre, w1, gq, gkv, wqn, bduk, wqr, convw, gconv)


def _attention_kernel(q_ref, k_ref, km_ref, o_ref, s_scr, m_scr, l_scr, acc_scr, *, bq, bk, chunks):
    i = pl.program_id(1)
    hpc = N_HEADS // chunks
    c_rows = hpc * bq

    def scores(c, kb):
        qc = q_ref[0, c * hpc:(c + 1) * hpc].reshape(c_rows, KEY_WIDTH)
        return _dot_nt(qc, kb)

    def values(kb):
        return jnp.concatenate([kb[:, :KV_LORA], jnp.ones((kb.shape[0], LANES), BF16)], axis=1)

    def consume(c, s, v_ones, mask):
        if mask is not None:
            s = jnp.where(mask, s, NEG_INF)
        m = m_scr[c]
        m_new = jnp.maximum(m, jnp.max(s, axis=1, keepdims=True))
        alpha = jnp.exp2(m - m_new)
        p = jnp.exp2(s - m_new).astype(BF16)
        pv = _dot(p, v_ones)
        m_scr[c] = m_new
        l_scr[c] = alpha * l_scr[c] + pv[:, KV_LORA:KV_LORA + 1]
        acc_scr[c] = alpha * acc_scr[c] + pv[:, :KV_LORA]

    km = km_ref[...]
    col = lax.broadcasted_iota(jnp.int32, (c_rows, BLOCK), 1)
    m_scr[...] = jnp.full(m_scr.shape, NEG_INF, F32)
    l_scr[...] = jnp.zeros(l_scr.shape, F32)
    acc_scr[...] = jnp.zeros(acc_scr.shape, F32)
    for c in range(chunks):
        consume(c, scores(c, km), values(km), col >= PAD_FRONT)

    def key_block(j):
        return k_ref[0, pl.ds(pl.multiple_of(j * bk, bk), bk), :]

    n_full = (i * bq) // bk
    kb0 = key_block(0)
    for c in range(chunks):
        s_scr[c] = scores(c, kb0)

    def body(j, carry):
        v_ones = values(key_block(j))
        kb_next = key_block(j + 1)
        for c in range(chunks):
            s = s_scr[c]
            s_scr[c] = scores(c, kb_next)
            consume(c, s, v_ones, None)
        return carry

    lax.fori_loop(0, n_full, body, 0)

    v_ones = values(key_block(n_full))
    qpos = i * bq + lax.rem(lax.broadcasted_iota(jnp.int32, (c_rows, bk), 0), bq)
    kpos = n_full * bk + lax.broadcasted_iota(jnp.int32, (c_rows, bk), 1)
    for c in range(chunks):
        consume(c, s_scr[c], v_ones, kpos <= qpos)

    for c in range(chunks):
        o = acc_scr[c] / l_scr[c]
        for hh in range(hpc):
            h = c * hpc + hh
            o_ref[0, :, h * KV_LORA:(h + 1) * KV_LORA] = o[hh * bq:(hh + 1) * bq].astype(BF16)


def _attention(qp, kc, kmeta, bq, bk, chunks):
    bsz, _, seq, _ = qp.shape
    c_rows = N_HEADS * bq // chunks
    scratch = [pltpu.VMEM((chunks, c_rows, bk), F32), pltpu.VMEM((chunks, c_rows, 1), F32),
               pltpu.VMEM((chunks, c_rows, 1), F32), pltpu.VMEM((chunks, c_rows, KV_LORA), F32)]
    return pl.pallas_call(
        functools.partial(_attention_kernel, bq=bq, bk=bk, chunks=chunks),
        grid=(bsz, seq // bq),
        in_specs=[pl.BlockSpec((1, N_HEADS, bq, KEY_WIDTH), lambda b, i: (b, 0, i, 0)),
                  pl.BlockSpec((1, seq, KEY_WIDTH), lambda b, i: (b, 0, 0)),
                  pl.BlockSpec((BLOCK, KEY_WIDTH), lambda b, i: (0, 0))],
        out_specs=pl.BlockSpec((1, bq, N_HEADS * KV_LORA), lambda b, i: (b, i, 0)),
        out_shape=jax.ShapeDtypeStruct((bsz, seq, N_HEADS * KV_LORA), BF16),
        scratch_shapes=scratch,
        compiler_params=pltpu.CompilerParams(dimension_semantics=("arbitrary", "arbitrary"),
                                             vmem_limit_bytes=VMEM_LIMIT),
        name="attention",
    )(qp, kc, kmeta)


def _postmix_kernel(x_ref, olat_ref, convn_ref, wuv_ref, gattn_ref, wout_ref, gpost_ref, gffn_ref, wr_ref, br_ref,
                    h1_ref, t_ref, comb_ref):
    rows = x_ref.shape[0]
    attn = _dot(olat_ref[...], wuv_ref[...])
    attn_n = _rms(attn, gattn_ref[...]).astype(BF16)
    wout = wout_ref[...]
    mix = _dot(attn_n, wout[:ATTN_WIDTH]) + _dot(convn_ref[...], wout[ATTN_WIDTH:])
    h1 = x_ref[...] + _rms(mix, gpost_ref[...])
    h1_ref[...] = h1
    t = _rms(h1, gffn_ref[...]).astype(BF16)
    t_ref[...] = t

    logits = _dot(t, wr_ref[...]) + br_ref[...]
    lane = lax.broadcasted_iota(jnp.int32, (rows, LANES), 1)

    def first_argmax(v):
        mx = jnp.max(v, axis=1, keepdims=True)
        return mx, jnp.min(jnp.where(v == mx, lane, LANES), axis=1, keepdims=True)

    gl = jnp.where(lane < N_GROUPS, logits[:, :LANES], NEG_INF)
    gmax, gidx = first_argmax(gl)
    g_w = 1.0 / jnp.sum(jnp.exp(gl - gmax), axis=1, keepdims=True)
    el = jnp.where((lane >> 3) == gidx, logits[:, LANES:], NEG_INF)
    m1, i1 = first_argmax(el)
    m2, i2 = first_argmax(jnp.where(lane == i1, NEG_INF, el))
    e2 = jnp.exp(m2 - m1)
    den = 1.0 + e2
    comb_ref[...] = jnp.where(lane == i1, g_w / den, 0.0) + jnp.where(lane == i2, g_w * e2 / den, 0.0)


def _postmix(x2, olat2, convn2, wts, rows):
    n = x2.shape[0]
    wuv, gattn, wout, gpost, gffn, wr, br = wts
    const = lambda a: pl.BlockSpec(a.shape, lambda i: (0,) * a.ndim)
    tile = lambda w: pl.BlockSpec((rows, w), lambda i: (i, 0))
    return pl.pallas_call(
        _postmix_kernel,
        grid=(n // rows,),
        in_specs=[tile(D_MODEL), tile(N_HEADS * KV_LORA), tile(CONV_DIM),
                  const(wuv), const(gattn), const(wout), const(gpost), const(gffn), const(wr), const(br)],
        out_specs=[tile(D_MODEL), tile(D_MODEL), tile(LANES)],
        out_shape=[jax.ShapeDtypeStruct((n, D_MODEL), F32),
                   jax.ShapeDtypeStruct((n, D_MODEL), BF16),
                   jax.ShapeDtypeStruct((n, LANES), F32)],
        compiler_params=pltpu.CompilerParams(dimension_semantics=("arbitrary",), vmem_limit_bytes=VMEM_LIMIT),
        name="postmix",
    )(x2, olat2, convn2, wuv, gattn, wout, gpost, gffn, wr, br)


def _silu(a):
    return a * (1.0 / (1.0 + jnp.exp(-a)))


def _moe_kernel(t_ref, comb_ref, h1_ref, wg_ref, wu_ref, wd_ref, wsg_ref, wsu_ref, wsd_ref, gout_ref,
                out_ref, acc_ref):
    e = pl.program_id(1)
    t = t_ref[...]

    @pl.when(e == 0)
    def _():
        hid = _silu(_dot(t, wsg_ref[...])) * _dot(t, wsu_ref[...])
        acc_ref[...] = _dot(hid.astype(BF16), wsd_ref[...])

    lane = lax.broadcasted_iota(jnp.int32, comb_ref.shape, 1)
    cw = jnp.sum(jnp.where(lane == e, comb_ref[...], 0.0), axis=1, keepdims=True)
    hid = _silu(_dot(t, wg_ref[0])) * _dot(t, wu_ref[0]) * cw
    acc_ref[...] += _dot(hid.astype(BF16), wd_ref[0])

    @pl.when(e == N_EXPERTS - 1)
    def _():
        out_ref[...] = h1_ref[...] + _rms(acc_ref[...], gout_ref[...])


def _moe(t2, comb, h1, wts, rows):
    n = t2.shape[0]
    wg, wu, wd, wsg, wsu, wsd, gout = wts
    const = lambda a: pl.BlockSpec(a.shape, lambda i, e: (0,) * a.ndim)
    tile = lambda w: pl.BlockSpec((rows, w), lambda i, e: (i, 0))
    return pl.pallas_call(
        _moe_kernel,
        grid=(n // rows, N_EXPERTS),
        in_specs=[tile(D_MODEL), tile(LANES), tile(D_MODEL),
                  pl.BlockSpec((1, D_MODEL, D_FF), lambda i, e: (e, 0, 0)),
                  pl.BlockSpec((1, D_MODEL, D_FF), lambda i, e: (e, 0, 0)),
                  pl.BlockSpec((1, D_FF, D_MODEL), lambda i, e: (e, 0, 0)),
                  const(wsg), const(wsu), const(wsd), const(gout)],
        out_specs=tile(D_MODEL),
        out_shape=jax.ShapeDtypeStruct((n, D_MODEL), F32),
        scratch_shapes=[pltpu.VMEM((rows, D_MODEL), F32)],
        compiler_params=pltpu.CompilerParams(dimension_semantics=("arbitrary", "arbitrary"),
                                             vmem_limit_bytes=VMEM_LIMIT),
        name="moe",
    )(t2, comb, h1, wg, wu, wd, wsg, wsu, wsd, gout)


def _swap_halves(w):
    half = w.shape[-1] // 2
    return jnp.concatenate([w[..., half:], w[..., :half]], axis=-1)


def _block_diag(blocks):
    h, r, c = blocks.shape
    eye = jnp.eye(h, dtype=blocks.dtype)
    return jnp.einsum('hrc,hg->hrgc', blocks, eye).reshape(h * r, h * c)


def kernel(x, positions, meta_tokens, pre_mix_norm, w_in, q_norm, w_uq, kv_norm, w_ukv, conv_w, attn_out_norm,
           conv_out_norm, w_out, post_mix_norm, pre_ffn_norm, w_group_router, b_group_router, w_expert_router,
           b_expert_router, w_gate, w_up, w_down, w_sh_gate, w_sh_up, w_sh_down, post_ffn_norm):
    bsz, seq, d = x.shape
    n = bsz * seq
    row = lambda v: v.reshape(1, -1).astype(F32)

    wi = w_in[0]
    o_kv, o_pe, o_b = Q_LORA, Q_LORA + KV_LORA, Q_LORA + KV_LORA + QK_ROPE
    k_pe = wi[:, o_pe:o_b]
    w1 = jnp.concatenate([wi[:, :o_pe], k_pe, _swap_halves(k_pe), jnp.zeros((d, LANES - 2 * QK_ROPE), F32),
                          wi[:, o_b:]], axis=1).astype(BF16)
    wq = w_uq[0].reshape(Q_LORA, N_HEADS, QK_NOPE + QK_ROPE)
    wqn = wq[:, :, :QK_NOPE].reshape(Q_LORA, N_HEADS * QK_NOPE).astype(BF16)
    wq_rope = wq[:, :, QK_NOPE:]
    wqr = jnp.concatenate([wq_rope, _swap_halves(wq_rope),
                           jnp.zeros((Q_LORA, N_HEADS, LANES - 2 * QK_ROPE), F32)], axis=2)
    wqr = wqr.reshape(Q_LORA, N_HEADS * LANES).astype(BF16)
    wkv = w_ukv[0].reshape(KV_LORA, N_HEADS, QK_NOPE + V_DIM)
    bduk = _block_diag(jnp.transpose(wkv[:, :, :QK_NOPE], (1, 2, 0))).astype(BF16)
    wuv = _block_diag(jnp.transpose(wkv[:, :, QK_NOPE:], (1, 0, 2))).astype(BF16)
    premix_w = (row(pre_mix_norm[0]), w1, row(q_norm[0]), row(kv_norm[0]), wqn, bduk, wqr,
                conv_w[0].astype(F32), row(conv_out_norm[0]))
    wr = jnp.concatenate([w_group_router[0], jnp.zeros((d, LANES - N_GROUPS), F32),
                          w_expert_router[0], jnp.zeros((d, LANES - N_EXPERTS), F32)], axis=1).astype(BF16)
    br = jnp.concatenate([b_group_router[0], jnp.zeros((LANES - N_GROUPS,), F32),
                          b_expert_router[0], jnp.zeros((LANES - N_EXPERTS,), F32)]).reshape(1, -1).astype(F32)
    postmix_w = (wuv, row(attn_out_norm[0]), w_out[0].astype(BF16), row(post_mix_norm[0]), row(pre_ffn_norm[0]),
                 wr, br)
    moe_w = (w_gate[0].astype(BF16), w_up[0].astype(BF16), w_down[0].astype(BF16), w_sh_gate[0].astype(BF16),
             w_sh_up[0].astype(BF16), w_sh_down[0].astype(BF16), row(post_ffn_norm[0]))

    pos_all = jnp.concatenate([positions.astype(jnp.int32).reshape(-1) + N_META,
                               jnp.zeros((PAD_FRONT,), jnp.int32), jnp.arange(N_META, dtype=jnp.int32)])
    inv_freq = 1.0 / (ROPE_THETA ** (jnp.arange(0, QK_ROPE, 2, dtype=F32) / QK_ROPE))
    n_tab = n + BLOCK
    tab = _rope_table(pos_all.reshape(1, n_tab), inv_freq.reshape(-1, 1), n_tab // 3)

    meta_blk = jnp.pad(meta_tokens.astype(F32), ((PAD_FRONT, 0), (0, 0)))[None]
    zero_carry = jnp.zeros((8, CONV_DIM), F32)
    _, kmeta, _, meta_tail = _premix(meta_blk, tab, n // BLOCK, zero_carry, premix_w, BLOCK)
    qp, kc, convn, _ = _premix(x, tab, 0, meta_tail, premix_w, 512)

    olat = _attention(qp, kc, kmeta[0], 256, 512, 4)

    h1, t, comb = _postmix(x.reshape(n, d), olat.reshape(n, -1), convn.reshape(n, -1), postmix_w, 512)
    out = _moe(t, comb, h1, moe_w, 1024)
    return out.reshape(bsz, seq, d)
```

```python
import functools
import math

import jax
import jax.numpy as jnp
from jax import lax
from jax.experimental import pallas as pl
from jax.experimental.pallas import tpu as pltpu

D_MODEL = 1024
N_META = 16
BLOCK = 128
PAD_FRONT = BLOCK - N_META
N_HEADS = 8
QK_NOPE = 64
QK_ROPE = 32
V_DIM = 64
Q_LORA = 256
KV_LORA = 128
ROPE_THETA = 10000.0
ATTN_SCALE = (QK_NOPE + QK_ROPE) ** -0.5
Q_SCALE = ATTN_SCALE * math.log2(math.e)
ATTN_WIDTH = N_HEADS * V_DIM
CONV_DIM = 512
CONV_W = 3
N_GROUPS = 4
EXPERTS_PER_GROUP = 8
N_EXPERTS = N_GROUPS * EXPERTS_PER_GROUP
D_FF = 256
EPS = 1e-6
NEG_INF = -1e30

LANES = 128
SUBLANES = 8
TOP_K = 2
EXPERT_TILE = 256
KEY_WIDTH = 2 * LANES
IN_COLS_PADDED = Q_LORA + KV_LORA + LANES + 3 * CONV_DIM

F32 = jnp.float32
BF16 = jnp.bfloat16
VMEM_LIMIT = 56 * 1024 * 1024


def _rms(x, g):
    return x * lax.rsqrt(jnp.mean(x * x, axis=-1, keepdims=True) + EPS) * g


def _dot(a, b):
    return jnp.dot(a, b, preferred_element_type=F32)


def _dot_nt(a, b):
    return lax.dot_general(a, b, (((1,), (1,)), ((), ())), preferred_element_type=F32)


def _store_token_rows(ref, value):
    rows = value.shape[0]
    for c in range(D_MODEL // LANES):
        ref[pl.ds(c, rows, stride=SUBLANES), :] = value[:, c * LANES:(c + 1) * LANES]


def _load_token_rows(ref, rows):
    return jnp.concatenate([ref[pl.ds(c, rows, stride=SUBLANES), :] for c in range(D_MODEL // LANES)], axis=1)


def _rope_table_kernel(pos_ref, invf_ref, out_ref):
    pos = pos_ref[...].astype(F32)
    ang = invf_ref[...] * pos
    c = jnp.cos(ang)
    s = jnp.sin(ang)
    planes = jnp.concatenate([c, c, -s, s, jnp.zeros((LANES - 4 * (QK_ROPE // 2), ang.shape[1]), F32)], axis=0)
    out_ref[...] = planes.T


def _rope_table(pos_row, inv_freq_col, chunk):
    n = pos_row.shape[1]
    return pl.pallas_call(
        _rope_table_kernel,
        grid=(n // chunk,),
        in_specs=[pl.BlockSpec((1, chunk), lambda i: (0, i)),
                  pl.BlockSpec((QK_ROPE // 2, 1), lambda i: (0, 0))],
        out_specs=pl.BlockSpec((chunk, LANES), lambda i: (i, 0)),
        out_shape=jax.ShapeDtypeStruct((n, LANES), F32),
        name="rope_table",
    )(pos_row, inv_freq_col)


def _premix_kernel(x_ref, tab_ref, carry_in_ref, g_pre_ref, w1_ref, gq_ref, gkv_ref, wqn_ref, bduk_ref,
                   wqr_ref, convw_ref, gconv_ref, qp_ref, kc_ref, convn_ref, utail_ref, carry_ref):
    t = pl.program_id(1)
    rows = x_ref.shape[1]

    @pl.when(t == 0)
    def _():
        carry_ref[...] = carry_in_ref[...]

    x = x_ref[0]
    xn = _rms(x, g_pre_ref[...]).astype(BF16)
    z = _dot(xn, w1_ref[...])
    tab = tab_ref[...]
    lane = lax.broadcasted_iota(jnp.int32, (rows, LANES), 1)

    def rope(pair):
        prod = pair * tab
        return jnp.where(lane < QK_ROPE, prod + pltpu.roll(prod, LANES - QK_ROPE, axis=1), 0.0)

    ckvn = _rms(z[:, Q_LORA:Q_LORA + KV_LORA], gkv_ref[...])
    krope = rope(z[:, Q_LORA + KV_LORA:Q_LORA + KV_LORA + LANES])
    kc_ref[0] = jnp.concatenate([ckvn, krope], axis=1).astype(BF16)

    cqn = _rms(z[:, :Q_LORA], gq_ref[...]).astype(BF16)
    qn = _dot(cqn, wqn_ref[...]).astype(BF16)
    qabs = _dot(qn, bduk_ref[...])
    qr = _dot(cqn, wqr_ref[...])
    for h in range(N_HEADS):
        sl = slice(h * LANES, (h + 1) * LANES)
        qp_ref[0, h, :, 0:LANES] = (qabs[:, sl] * Q_SCALE).astype(BF16)
        qp_ref[0, h, :, LANES:KEY_WIDTH] = (rope(qr[:, sl]) * Q_SCALE).astype(BF16)

    c0 = Q_LORA + KV_LORA + LANES
    gate_b = z[:, c0:c0 + CONV_DIM]
    u = z[:, c0 + CONV_DIM:c0 + 2 * CONV_DIM] * z[:, c0 + 2 * CONV_DIM:c0 + 3 * CONV_DIM]
    prev = carry_ref[...]
    row8 = lax.broadcasted_iota(jnp.int32, (8, CONV_DIM), 0)
    r1 = pltpu.roll(u, 1, axis=0)
    r2 = pltpu.roll(u, 2, axis=0)
    p1 = pltpu.roll(prev, 1, axis=0)
    p2 = pltpu.roll(prev, 2, axis=0)
    u1 = jnp.concatenate([jnp.where(row8 < 1, p1, r1[0:8]), r1[8:]], axis=0)
    u2 = jnp.concatenate([jnp.where(row8 < 2, p2, r2[0:8]), r2[8:]], axis=0)
    cw = convw_ref[...]
    y = cw[0:1] * u2 + cw[1:2] * u1 + cw[2:3] * u
    convn_ref[0] = _rms(gate_b * y, gconv_ref[...]).astype(BF16)
    tail = u[rows - 8:rows]
    carry_ref[...] = tail
    utail_ref[...] = tail


def _premix(x3, tab, tab_block0, carry_in, wts, rows):
    bsz, seq, _ = x3.shape
    nt = seq // rows
    g_pre, w1, gq, gkv, wqn, bduk, wqr, convw, gconv = wts
    const = lambda a: pl.BlockSpec(a.shape, lambda b, t: (0,) * a.ndim)
    return pl.pallas_call(
        _premix_kernel,
        grid=(bsz, nt),
        in_specs=[pl.BlockSpec((1, rows, D_MODEL), lambda b, t: (b, t, 0)),
                  pl.BlockSpec((rows, LANES), lambda b, t: (tab_block0 + b * nt + t, 0)),
                  const(carry_in), const(g_pre), const(w1), const(gq), const(gkv), const(wqn), const(bduk),
                  const(wqr), const(convw), const(gconv)],
        out_specs=[pl.BlockSpec((1, N_HEADS, rows, KEY_WIDTH), lambda b, t: (b, 0, t, 0)),
                   pl.BlockSpec((1, rows, KEY_WIDTH), lambda b, t: (b, t, 0)),
                   pl.BlockSpec((1, rows, CONV_DIM), lambda b, t: (b, t, 0)),
                   pl.BlockSpec((8, CONV_DIM), lambda b, t: (0, 0))],
        out_shape=[jax.ShapeDtypeStruct((bsz, N_HEADS, seq, KEY_WIDTH), BF16),
                   jax.ShapeDtypeStruct((bsz, seq, KEY_WIDTH), BF16),
                   jax.ShapeDtypeStruct((bsz, seq, CONV_DIM), BF16),
                   jax.ShapeDtypeStruct((8, CONV_DIM), F32)],
        scratch_shapes=[pltpu.VMEM((8, CONV_DIM), F32)],
        compiler_params=pltpu.CompilerParams(dimension_semantics=("arbitrary", "arbitrary"),
                                             vmem_limit_bytes=VMEM_LIMIT),
        name="premix",
    )(x3, tab, carry_in, g_pre, w1, gq, gkv, wqn, bduk, wqr, convw, gconv)


def _attention_kernel(q_ref, k_ref, km_ref, o_ref, s_scr, m_scr, l_scr, acc_scr, *, bq, bk, chunks):
    i = pl.program_id(1)
    hpc = N_HEADS // chunks
    c_rows = hpc * bq

    def scores(c, kb):
        qc = q_ref[0, c * hpc:(c + 1) * hpc].reshape(c_rows, KEY_WIDTH)
        return _dot_nt(qc, kb)

    def values(kb):
        return jnp.concatenate([kb[:, :KV_LORA], jnp.ones((kb.shape[0], LANES), BF16)], axis=1)

    def consume(c, s, v_ones, mask):
        if mask is not None:
            s = jnp.where(mask, s, NEG_INF)
        m = m_scr[c]
        m_new = jnp.maximum(m, jnp.max(s, axis=1, keepdims=True))
        alpha = jnp.exp2(m - m_new)
        p = jnp.exp2(s - m_new).astype(BF16)
        pv = _dot(p, v_ones)
        m_scr[c] = m_new
        l_scr[c] = alpha * l_scr[c] + pv[:, KV_LORA:KV_LORA + 1]
        acc_scr[c] = alpha * acc_scr[c] + pv[:, :KV_LORA]

    km = km_ref[...]
    col = lax.broadcasted_iota(jnp.int32, (c_rows, BLOCK), 1)
    m_scr[...] = jnp.full(m_scr.shape, NEG_INF, F32)
    l_scr[...] = jnp.zeros(l_scr.shape, F32)
    acc_scr[...] = jnp.zeros(acc_scr.shape, F32)
    for c in range(chunks):
        consume(c, scores(c, km), values(km), col >= PAD_FRONT)

    def key_block(j):
        return k_ref[0, pl.ds(pl.multiple_of(j * bk, bk), bk), :]

    n_full = (i * bq) // bk
    kb0 = key_block(0)
    for c in range(chunks):
        s_scr[c] = scores(c, kb0)

    def body(j, carry):
        v_ones = values(key_block(j))
        kb_next = key_block(j + 1)
        for c in range(chunks):
            s = s_scr[c]
            s_scr[c] = scores(c, kb_next)
            consume(c, s, v_ones, None)
        return carry

    lax.fori_loop(0, n_full, body, 0)

    v_ones = values(key_block(n_full))
    qpos = i * bq + lax.rem(lax.broadcasted_iota(jnp.int32, (c_rows, bk), 0), bq)
    kpos = n_full * bk + lax.broadcasted_iota(jnp.int32, (c_rows, bk), 1)
    for c in range(chunks):
        consume(c, s_scr[c], v_ones, kpos <= qpos)

    for c in range(chunks):
        o = acc_scr[c] / l_scr[c]
        for hh in range(hpc):
            h = c * hpc + hh
            o_ref[0, :, h * KV_LORA:(h + 1) * KV_LORA] = o[hh * bq:(hh + 1) * bq].astype(BF16)


def _attention(qp, kc, kmeta, bq, bk, chunks):
    bsz, _, seq, _ = qp.shape
    c_rows = N_HEADS * bq // chunks
    scratch = [pltpu.VMEM((chunks, c_rows, bk), F32), pltpu.VMEM((chunks, c_rows, 1), F32),
               pltpu.VMEM((chunks, c_rows, 1), F32), pltpu.VMEM((chunks, c_rows, KV_LORA), F32)]
    return pl.pallas_call(
        functools.partial(_attention_kernel, bq=bq, bk=bk, chunks=chunks),
        grid=(bsz, seq // bq),
        in_specs=[pl.BlockSpec((1, N_HEADS, bq, KEY_WIDTH), lambda b, i: (b, 0, i, 0)),
                  pl.BlockSpec((1, seq, KEY_WIDTH), lambda b, i: (b, 0, 0)),
                  pl.BlockSpec((BLOCK, KEY_WIDTH), lambda b, i: (0, 0))],
        out_specs=pl.BlockSpec((1, bq, N_HEADS * KV_LORA), lambda b, i: (b, i, 0)),
        out_shape=jax.ShapeDtypeStruct((bsz, seq, N_HEADS * KV_LORA), BF16),
        scratch_shapes=scratch,
        compiler_params=pltpu.CompilerParams(dimension_semantics=("arbitrary", "arbitrary"),
                                             vmem_limit_bytes=VMEM_LIMIT),
        name="attention",
    )(qp, kc, kmeta)


def _postmix_kernel(x_ref, olat_ref, convn_ref, wuv_ref, gattn_ref, wout_ref, gpost_ref, gffn_ref, wr_ref, br_ref,
                    h1_ref, t_ref, trow_ref, oh1_ref, oh2_ref, wts_ref):
    rows = x_ref.shape[0]
    attn = _dot(olat_ref[...], wuv_ref[...])
    attn_n = _rms(attn, gattn_ref[...]).astype(BF16)
    wout = wout_ref[...]
    mix = _dot(attn_n, wout[:ATTN_WIDTH]) + _dot(convn_ref[...], wout[ATTN_WIDTH:])
    h1 = x_ref[...] + _rms(mix, gpost_ref[...])
    h1_ref[...] = h1
    tf = _rms(h1, gffn_ref[...])
    t = tf.astype(BF16)
    t_ref[...] = t
    _store_token_rows(trow_ref, tf)

    logits = _dot(t, wr_ref[...]) + br_ref[...]
    lane = lax.broadcasted_iota(jnp.int32, (rows, LANES), 1)

    def first_argmax(v):
        mx = jnp.max(v, axis=1, keepdims=True)
        return mx, jnp.min(jnp.where(v == mx, lane, LANES), axis=1, keepdims=True)

    gl = jnp.where(lane < N_GROUPS, logits[:, :LANES], NEG_INF)
    gmax, gidx = first_argmax(gl)
    g_w = 1.0 / jnp.sum(jnp.exp(gl - gmax), axis=1, keepdims=True)
    el = jnp.where((lane >> 3) == gidx, logits[:, LANES:], NEG_INF)
    m1, i1 = first_argmax(el)
    m2, i2 = first_argmax(jnp.where(lane == i1, NEG_INF, el))
    e2 = jnp.exp(m2 - m1)
    den = 1.0 + e2
    oh1_ref[...] = jnp.where(lane == i1, 1.0, 0.0).astype(BF16)
    oh2_ref[...] = jnp.where(lane == i2, 1.0, 0.0).astype(BF16)
    wts_ref[...] = jnp.where(lane == 0, g_w / den, jnp.where(lane == 1, g_w * e2 / den, 0.0))


def _postmix(x2, olat2, convn2, wts, rows):
    n = x2.shape[0]
    wuv, gattn, wout, gpost, gffn, wr, br = wts
    const = lambda a: pl.BlockSpec(a.shape, lambda i: (0,) * a.ndim)
    tile = lambda w: pl.BlockSpec((rows, w), lambda i: (i, 0))
    return pl.pallas_call(
        _postmix_kernel,
        grid=(n // rows,),
        in_specs=[tile(D_MODEL), tile(N_HEADS * KV_LORA), tile(CONV_DIM),
                  const(wuv), const(gattn), const(wout), const(gpost), const(gffn), const(wr), const(br)],
        out_specs=[tile(D_MODEL), tile(D_MODEL), pl.BlockSpec((rows * SUBLANES, LANES), lambda i: (i, 0)),
                   tile(LANES), tile(LANES), tile(LANES)],
        out_shape=[jax.ShapeDtypeStruct((n, D_MODEL), F32),
                   jax.ShapeDtypeStruct((n, D_MODEL), BF16),
                   jax.ShapeDtypeStruct((n * SUBLANES, LANES), F32),
                   jax.ShapeDtypeStruct((n, LANES), BF16),
                   jax.ShapeDtypeStruct((n, LANES), BF16),
                   jax.ShapeDtypeStruct((n, LANES), F32)],
        compiler_params=pltpu.CompilerParams(dimension_semantics=("arbitrary",), vmem_limit_bytes=VMEM_LIMIT),
        name="postmix",
    )(x2, olat2, convn2, wuv, gattn, wout, gpost, gffn, wr, br)


def _silu(a):
    return a * (1.0 / (1.0 + jnp.exp(-a)))


def _plan_kernel(oh1_ref, oh2_ref, pos_ref, te_ref, misc_ref, *, chunk):
    n_tok = oh1_ref.shape[0]
    n_chunks = n_tok // chunk
    lane = lax.broadcasted_iota(jnp.int32, (SUBLANES, LANES), 1)

    def rows_of(ref, c):
        return ref[pl.ds(pl.multiple_of(c * chunk, chunk), chunk), :]

    def count_body(c, acc):
        both = rows_of(oh1_ref, c).astype(F32) + rows_of(oh2_ref, c).astype(F32)
        return acc + jnp.sum(both, axis=0, keepdims=True)

    counts = lax.fori_loop(0, n_chunks, count_body, jnp.zeros((1, LANES), F32))
    counts = jnp.broadcast_to(counts, (SUBLANES, LANES)).astype(jnp.int32)
    tile_shift = EXPERT_TILE.bit_length() - 1
    padded = ((counts + (EXPERT_TILE - 1)) >> tile_shift) << tile_shift
    ends = padded
    shift = 1
    while shift < N_EXPERTS:
        ends = ends + jnp.where(lane >= shift, pltpu.roll(ends, shift, axis=1), 0)
        shift *= 2
    base = (ends - padded)[0:1].astype(F32)

    r_iota = lax.broadcasted_iota(jnp.int32, (chunk, chunk), 0)
    c_iota = lax.broadcasted_iota(jnp.int32, (chunk, chunk), 1)
    earlier = jnp.where(c_iota < r_iota, 1.0, 0.0).astype(BF16)

    def rank_pass(ref, slot, run):
        def body(c, run):
            oh = rows_of(ref, c)
            ohf = oh.astype(F32)
            before = _dot(earlier, oh) + run
            posv = jnp.sum(ohf * before, axis=1, keepdims=True)
            pos_row = jnp.broadcast_to(posv, (chunk, LANES)).T[0:1, :]
            pos_ref[pl.ds(slot * n_chunks + c, 1), :] = pos_row.astype(jnp.int32)
            return run + jnp.sum(ohf, axis=0, keepdims=True)
        return lax.fori_loop(0, n_chunks, body, run)

    rank_pass(oh2_ref, 1, rank_pass(oh1_ref, 0, base))

    n_tile_lanes = te_ref.shape[1]
    tile_start = lax.broadcasted_iota(jnp.int32, (SUBLANES, n_tile_lanes), 1) * EXPERT_TILE
    te = jnp.zeros((SUBLANES, n_tile_lanes), jnp.int32)
    for e in range(N_EXPERTS):
        te = te + jnp.where(ends[:, e:e + 1] <= tile_start, 1, 0)
    te_ref[...] = jnp.minimum(te, N_EXPERTS - 1)
    used_tiles = ends[:, N_EXPERTS - 1:N_EXPERTS] >> tile_shift
    misc_ref[...] = jnp.where(lane < N_EXPERTS, jnp.where(padded > 0, ends - EXPERT_TILE, -1), used_tiles)


def _plan(oh1, oh2, n_tiles, chunk):
    n = oh1.shape[0]
    n_tile_lanes = -(-n_tiles // LANES) * LANES
    whole = lambda shape: pl.BlockSpec(shape, lambda i: (0, 0))
    return pl.pallas_call(
        functools.partial(_plan_kernel, chunk=chunk),
        grid=(1,),
        in_specs=[whole((n, LANES)), whole((n, LANES))],
        out_specs=[whole((TOP_K * n // chunk, chunk)), whole((SUBLANES, n_tile_lanes)), whole((SUBLANES, LANES))],
        out_shape=[jax.ShapeDtypeStruct((TOP_K * n // chunk, chunk), jnp.int32),
                   jax.ShapeDtypeStruct((SUBLANES, n_tile_lanes), jnp.int32),
                   jax.ShapeDtypeStruct((SUBLANES, LANES), jnp.int32)],
        compiler_params=pltpu.CompilerParams(dimension_semantics=("arbitrary",), vmem_limit_bytes=VMEM_LIMIT),
        name="moe_plan",
    )(oh1, oh2)


def _row_slice(ref, row, n_rows=1):
    start = row * SUBLANES if isinstance(row, int) else pl.multiple_of(row * SUBLANES, SUBLANES)
    return ref.at[pl.ds(start, n_rows * SUBLANES)]


def _dispatch_kernel(zstart_ref, pos_ref, trow_hbm, xs_hbm, zero_buf, zero_sem, row_sem, *, tokens):
    i = pl.program_id(0)

    @pl.when(i == 0)
    def _():
        zero_buf[...] = jnp.zeros(zero_buf.shape, F32)
        n_tiles = xs_hbm.shape[0] // (EXPERT_TILE * SUBLANES)
        used = zstart_ref[N_EXPERTS]

        def zero_tile(first_row):
            return pltpu.make_async_copy(zero_buf, _row_slice(xs_hbm, first_row, EXPERT_TILE), zero_sem)

        for phase in ("start", "wait"):
            for e in range(N_EXPERTS):
                @pl.when(zstart_ref[e] >= 0)
                def _():
                    copy = zero_tile(zstart_ref[e])
                    copy.start() if phase == "start" else copy.wait()

            def tail(t, carry):
                copy = zero_tile(t * EXPERT_TILE)
                copy.start() if phase == "start" else copy.wait()
                return carry

            lax.fori_loop(used, n_tiles, tail, 0)

    unroll = 8

    def issue(g, carry):
        for u in range(unroll):
            n = g * unroll + u
            src = _row_slice(trow_hbm, i * tokens + n)
            for k in range(TOP_K):
                pltpu.make_async_copy(src, _row_slice(xs_hbm, pos_ref[k, n]), row_sem).start()
        return carry

    lax.fori_loop(0, tokens // unroll, issue, 0)
    pltpu.make_async_copy(_row_slice(trow_hbm, 0, TOP_K * tokens), _row_slice(xs_hbm, 0, TOP_K * tokens),
                          row_sem).wait()


def _dispatch(zstart, pos, trow, sorted_rows, tokens):
    n = pos.shape[1]
    return pl.pallas_call(
        functools.partial(_dispatch_kernel, tokens=tokens),
        grid=(n // tokens,),
        in_specs=[pl.BlockSpec(memory_space=pltpu.SMEM),
                  pl.BlockSpec((TOP_K, tokens), lambda i: (0, i), memory_space=pltpu.SMEM),
                  pl.BlockSpec(memory_space=pl.ANY)],
        out_specs=pl.BlockSpec(memory_space=pl.ANY),
        out_shape=jax.ShapeDtypeStruct((sorted_rows * SUBLANES, LANES), F32),
        scratch_shapes=[pltpu.VMEM((EXPERT_TILE * SUBLANES, LANES), F32), pltpu.SemaphoreType.DMA,
                        pltpu.SemaphoreType.DMA],
        compiler_params=pltpu.CompilerParams(dimension_semantics=("arbitrary",)),
        name="moe_dispatch",
    )(zstart, pos, trow)


def _experts_kernel(te_ref, used_ref, xs_ref, wg_ref, wu_ref, wd_ref, ys_ref):
    t = pl.program_id(0)

    @pl.when(t < used_ref[0])
    def _():
        x = _load_token_rows(xs_ref, EXPERT_TILE).astype(BF16)
        hid = _silu(_dot(x, wg_ref[0])) * _dot(x, wu_ref[0])
        _store_token_rows(ys_ref, _dot(hid.astype(BF16), wd_ref[0]))

    @pl.when(t >= used_ref[0])
    def _():
        ys_ref[...] = jnp.zeros(ys_ref.shape, F32)


def _experts(te, used, xs, wg, wu, wd, n_tiles):
    blk = EXPERT_TILE * SUBLANES
    return pl.pallas_call(
        _experts_kernel,
        grid_spec=pltpu.PrefetchScalarGridSpec(
            num_scalar_prefetch=2,
            grid=(n_tiles,),
            in_specs=[pl.BlockSpec((blk, LANES), lambda t, te, used: (jnp.minimum(t, used[0] - 1), 0)),
                      pl.BlockSpec((1, D_MODEL, D_FF), lambda t, te, used: (te[t], 0, 0)),
                      pl.BlockSpec((1, D_MODEL, D_FF), lambda t, te, used: (te[t], 0, 0)),
                      pl.BlockSpec((1, D_FF, D_MODEL), lambda t, te, used: (te[t], 0, 0))],
            out_specs=pl.BlockSpec((blk, LANES), lambda t, te, used: (t, 0))),
        out_shape=jax.ShapeDtypeStruct(xs.shape, F32),
        compiler_params=pltpu.CompilerParams(dimension_semantics=("arbitrary",), vmem_limit_bytes=VMEM_LIMIT),
        name="moe_experts",
    )(te, used, xs, wg, wu, wd)


def _combine_kernel(pos_ref, t_ref, h1_ref, wts_ref, ys_hbm, wsg_ref, wsu_ref, wsd_ref, gout_ref, out_ref,
                    ybuf, sem, *, tokens):
    unroll = 8

    def issue(g, carry):
        for u in range(unroll):
            n = g * unroll + u
            for k in range(TOP_K):
                pltpu.make_async_copy(_row_slice(ys_hbm, pos_ref[k, n]), _row_slice(ybuf.at[k], n), sem).start()
        return carry

    lax.fori_loop(0, tokens // unroll, issue, 0)

    t = t_ref[...]
    hid = _silu(_dot(t, wsg_ref[...])) * _dot(t, wsu_ref[...])
    shared = _dot(hid.astype(BF16), wsd_ref[...])

    for k in range(TOP_K):
        pltpu.make_async_copy(_row_slice(ys_hbm, 0, tokens), ybuf.at[k], sem).wait()
    wts = wts_ref[...]
    routed = wts[:, 0:1] * _load_token_rows(ybuf.at[0], tokens) + wts[:, 1:2] * _load_token_rows(ybuf.at[1], tokens)
    out_ref[...] = h1_ref[...] + _rms(routed + shared, gout_ref[...])


def _combine(pos, t2, h1, wts, ys, shared_w, tokens):
    n = t2.shape[0]
    wsg, wsu, wsd, gout = shared_w
    const = lambda a: pl.BlockSpec(a.shape, lambda i: (0,) * a.ndim)
    tile = lambda w: pl.BlockSpec((tokens, w), lambda i: (i, 0))
    return pl.pallas_call(
        functools.partial(_combine_kernel, tokens=tokens),
        grid=(n // tokens,),
        in_specs=[pl.BlockSpec((TOP_K, tokens), lambda i: (0, i), memory_space=pltpu.SMEM),
                  tile(D_MODEL), tile(D_MODEL), tile(LANES), pl.BlockSpec(memory_space=pl.ANY),
                  const(wsg), const(wsu), const(wsd), const(gout)],
        out_specs=tile(D_MODEL),
        out_shape=jax.ShapeDtypeStruct((n, D_MODEL), F32),
        scratch_shapes=[pltpu.VMEM((TOP_K, tokens * SUBLANES, LANES), F32), pltpu.SemaphoreType.DMA],
        compiler_params=pltpu.CompilerParams(dimension_semantics=("arbitrary",), vmem_limit_bytes=VMEM_LIMIT),
        name="moe_combine",
    )(pos, t2, h1, wts, ys, wsg, wsu, wsd, gout)


def _swap_halves(w):
    half = w.shape[-1] // 2
    return jnp.concatenate([w[..., half:], w[..., :half]], axis=-1)


def _block_diag(blocks):
    h, r, c = blocks.shape
    eye = jnp.eye(h, dtype=blocks.dtype)
    return jnp.einsum('hrc,hg->hrgc', blocks, eye).reshape(h * r, h * c)


def kernel(x, positions, meta_tokens, pre_mix_norm, w_in, q_norm, w_uq, kv_norm, w_ukv, conv_w, attn_out_norm,
           conv_out_norm, w_out, post_mix_norm, pre_ffn_norm, w_group_router, b_group_router, w_expert_router,
           b_expert_router, w_gate, w_up, w_down, w_sh_gate, w_sh_up, w_sh_down, post_ffn_norm):
    bsz, seq, d = x.shape
    n = bsz * seq
    row = lambda v: v.reshape(1, -1).astype(F32)

    wi = w_in[0]
    o_kv, o_pe, o_b = Q_LORA, Q_LORA + KV_LORA, Q_LORA + KV_LORA + QK_ROPE
    k_pe = wi[:, o_pe:o_b]
    w1 = jnp.concatenate([wi[:, :o_pe], k_pe, _swap_halves(k_pe), jnp.zeros((d, LANES - 2 * QK_ROPE), F32),
                          wi[:, o_b:]], axis=1).astype(BF16)
    wq = w_uq[0].reshape(Q_LORA, N_HEADS, QK_NOPE + QK_ROPE)
    wqn = wq[:, :, :QK_NOPE].reshape(Q_LORA, N_HEADS * QK_NOPE).astype(BF16)
    wq_rope = wq[:, :, QK_NOPE:]
    wqr = jnp.concatenate([wq_rope, _swap_halves(wq_rope),
                           jnp.zeros((Q_LORA, N_HEADS, LANES - 2 * QK_ROPE), F32)], axis=2)
    wqr = wqr.reshape(Q_LORA, N_HEADS * LANES).astype(BF16)
    wkv = w_ukv[0].reshape(KV_LORA, N_HEADS, QK_NOPE + V_DIM)
    bduk = _block_diag(jnp.transpose(wkv[:, :, :QK_NOPE], (1, 2, 0))).astype(BF16)
    wuv = _block_diag(jnp.transpose(wkv[:, :, QK_NOPE:], (1, 0, 2))).astype(BF16)
    premix_w = (row(pre_mix_norm[0]), w1, row(q_norm[0]), row(kv_norm[0]), wqn, bduk, wqr,
                conv_w[0].astype(F32), row(conv_out_norm[0]))
    wr = jnp.concatenate([w_group_router[0], jnp.zeros((d, LANES - N_GROUPS), F32),
                          w_expert_router[0], jnp.zeros((d, LANES - N_EXPERTS), F32)], axis=1).astype(BF16)
    br = jnp.concatenate([b_group_router[0], jnp.zeros((LANES - N_GROUPS,), F32),
                          b_expert_router[0], jnp.zeros((LANES - N_EXPERTS,), F32)]).reshape(1, -1).astype(F32)
    postmix_w = (wuv, row(attn_out_norm[0]), w_out[0].astype(BF16), row(post_mix_norm[0]), row(pre_ffn_norm[0]),
                 wr, br)
    expert_w = (w_gate[0].astype(BF16), w_up[0].astype(BF16), w_down[0].astype(BF16))
    shared_w = (w_sh_gate[0].astype(BF16), w_sh_up[0].astype(BF16), w_sh_down[0].astype(BF16),
                row(post_ffn_norm[0]))

    pos_all = jnp.concatenate([positions.astype(jnp.int32).reshape(-1) + N_META,
                               jnp.zeros((PAD_FRONT,), jnp.int32), jnp.arange(N_META, dtype=jnp.int32)])
    inv_freq = 1.0 / (ROPE_THETA ** (jnp.arange(0, QK_ROPE, 2, dtype=F32) / QK_ROPE))
    n_tab = n + BLOCK
    tab = _rope_table(pos_all.reshape(1, n_tab), inv_freq.reshape(-1, 1), n_tab // 3)

    meta_blk = jnp.pad(meta_tokens.astype(F32), ((PAD_FRONT, 0), (0, 0)))[None]
    zero_carry = jnp.zeros((8, CONV_DIM), F32)
    _, kmeta, _, meta_tail = _premix(meta_blk, tab, n // BLOCK, zero_carry, premix_w, BLOCK)
    qp, kc, convn, _ = _premix(x, tab, 0, meta_tail, premix_w, 512)

    olat = _attention(qp, kc, kmeta[0], 256, 512, 4)

    h1, t, trow, oh1, oh2, wts = _postmix(x.reshape(n, d), olat.reshape(n, -1), convn.reshape(n, -1), postmix_w, 512)

    sorted_rows = TOP_K * n + N_EXPERTS * EXPERT_TILE
    n_tiles = sorted_rows // EXPERT_TILE
    pos2, te8, misc = _plan(oh1, oh2, n_tiles, 512)
    pos = pos2.reshape(TOP_K, n)
    xs = _dispatch(misc[0, :N_EXPERTS + 1], pos, trow, sorted_rows, 1024)
    ys = _experts(te8[0, :n_tiles], misc[0, N_EXPERTS:N_EXPERTS + 1], xs, *expert_w, n_tiles)
    out = _combine(pos, t, h1, wts, ys, shared_w, 512)
    return out.reshape(bsz, seq, d)
```

```python
import functools
import math

import jax
import jax.numpy as jnp
from jax import lax
from jax.experimental import pallas as pl
from jax.experimental.pallas import tpu as pltpu

D_MODEL = 1024
N_META = 16
BLOCK = 128
PAD_FRONT = BLOCK - N_META
N_HEADS = 8
QK_NOPE = 64
QK_ROPE = 32
V_DIM = 64
Q_LORA = 256
KV_LORA = 128
ROPE_THETA = 10000.0
ATTN_SCALE = (QK_NOPE + QK_ROPE) ** -0.5
Q_SCALE = ATTN_SCALE * math.log2(math.e)
ATTN_WIDTH = N_HEADS * V_DIM
CONV_DIM = 512
CONV_W = 3
N_GROUPS = 4
EXPERTS_PER_GROUP = 8
N_EXPERTS = N_GROUPS * EXPERTS_PER_GROUP
D_FF = 256
EPS = 1e-6
NEG_INF = -1e30

LANES = 128
SUBLANES = 8
TOP_K = 2
EXPERT_TILE = 256
KEY_WIDTH = 2 * LANES
IN_COLS_PADDED = Q_LORA + KV_LORA + LANES + 3 * CONV_DIM

F32 = jnp.float32
BF16 = jnp.bfloat16
VMEM_LIMIT = 56 * 1024 * 1024


def _rms(x, g):
    return x * lax.rsqrt(jnp.mean(x * x, axis=-1, keepdims=True) + EPS) * g


def _dot(a, b):
    return jnp.dot(a, b, preferred_element_type=F32)


def _dot_nt(a, b):
    return lax.dot_general(a, b, (((1,), (1,)), ((), ())), preferred_element_type=F32)


def _store_token_rows(ref, value):
    rows = value.shape[0]
    for c in range(D_MODEL // LANES):
        ref[pl.ds(c, rows, stride=SUBLANES), :] = value[:, c * LANES:(c + 1) * LANES]


def _load_token_rows(ref, rows):
    return jnp.concatenate([ref[pl.ds(c, rows, stride=SUBLANES), :] for c in range(D_MODEL // LANES)], axis=1)


def _rope_table_kernel(pos_ref, invf_ref, out_ref):
    pos = pos_ref[...].astype(F32)
    ang = invf_ref[...] * pos
    c = jnp.cos(ang)
    s = jnp.sin(ang)
    planes = jnp.concatenate([c, c, -s, s, jnp.zeros((LANES - 4 * (QK_ROPE // 2), ang.shape[1]), F32)], axis=0)
    out_ref[...] = planes.T


def _rope_table(pos_row, inv_freq_col, chunk):
    n = pos_row.shape[1]
    return pl.pallas_call(
        _rope_table_kernel,
        grid=(n // chunk,),
        in_specs=[pl.BlockSpec((1, chunk), lambda i: (0, i)),
                  pl.BlockSpec((QK_ROPE // 2, 1), lambda i: (0, 0))],
        out_specs=pl.BlockSpec((chunk, LANES), lambda i: (i, 0)),
        out_shape=jax.ShapeDtypeStruct((n, LANES), F32),
        name="rope_table",
    )(pos_row, inv_freq_col)


def _premix_kernel(x_ref, tab_ref, carry_in_ref, g_pre_ref, w1_ref, gq_ref, gkv_ref, wqn_ref, bduk_ref,
                   wqr_ref, convw_ref, gconv_ref, qp_ref, kc_ref, convn_ref, utail_ref, carry_ref):
    t = pl.program_id(1)
    rows = x_ref.shape[1]

    @pl.when(t == 0)
    def _():
        carry_ref[...] = carry_in_ref[...]

    x = x_ref[0]
    xn = _rms(x, g_pre_ref[...]).astype(BF16)
    z = _dot(xn, w1_ref[...])
    tab = tab_ref[...]
    lane = lax.broadcasted_iota(jnp.int32, (rows, LANES), 1)

    def rope(pair):
        prod = pair * tab
        return jnp.where(lane < QK_ROPE, prod + pltpu.roll(prod, LANES - QK_ROPE, axis=1), 0.0)

    ckvn = _rms(z[:, Q_LORA:Q_LORA + KV_LORA], gkv_ref[...])
    krope = rope(z[:, Q_LORA + KV_LORA:Q_LORA + KV_LORA + LANES])
    kc_ref[0] = jnp.concatenate([ckvn, krope], axis=1).astype(BF16)

    cqn = _rms(z[:, :Q_LORA], gq_ref[...]).astype(BF16)
    qn = _dot(cqn, wqn_ref[...]).astype(BF16)
    qabs = _dot(qn, bduk_ref[...])
    qr = _dot(cqn, wqr_ref[...])
    for h in range(N_HEADS):
        sl = slice(h * LANES, (h + 1) * LANES)
        qp_ref[0, h, :, 0:LANES] = (qabs[:, sl] * Q_SCALE).astype(BF16)
        qp_ref[0, h, :, LANES:KEY_WIDTH] = (rope(qr[:, sl]) * Q_SCALE).astype(BF16)

    c0 = Q_LORA + KV_LORA + LANES
    gate_b = z[:, c0:c0 + CONV_DIM]
    u = z[:, c0 + CONV_DIM:c0 + 2 * CONV_DIM] * z[:, c0 + 2 * CONV_DIM:c0 + 3 * CONV_DIM]
    prev = carry_ref[...]
    row8 = lax.broadcasted_iota(jnp.int32, (8, CONV_DIM), 0)
    r1 = pltpu.roll(u, 1, axis=0)
    r2 = pltpu.roll(u, 2, axis=0)
    p1 = pltpu.roll(prev, 1, axis=0)
    p2 = pltpu.roll(prev, 2, axis=0)
    u1 = jnp.concatenate([jnp.where(row8 < 1, p1, r1[0:8]), r1[8:]], axis=0)
    u2 = jnp.concatenate([jnp.where(row8 < 2, p2, r2[0:8]), r2[8:]], axis=0)
    cw = convw_ref[...]
    y = cw[0:1] * u2 + cw[1:2] * u1 + cw[2:3] * u
    convn_ref[0] = _rms(gate_b * y, gconv_ref[...]).astype(BF16)
    tail = u[rows - 8:rows]
    carry_ref[...] = tail
    utail_ref[...] = tail


def _premix(x3, tab, tab_block0, carry_in, wts, rows):
    bsz, seq, _ = x3.shape
    nt = seq // rows
    g_pre, w1, gq, gkv, wqn, bduk, wqr, convw, gconv = wts
    const = lambda a: pl.BlockSpec(a.shape, lambda b, t: (0,) * a.ndim)
    return pl.pallas_call(
        _premix_kernel,
        grid=(bsz, nt),
        in_specs=[pl.BlockSpec((1, rows, D_MODEL), lambda b, t: (b, t, 0)),
                  pl.BlockSpec((rows, LANES), lambda b, t: (tab_block0 + b * nt + t, 0)),
                  const(carry_in), const(g_pre), const(w1), const(gq), const(gkv), const(wqn), const(bduk),
                  const(wqr), const(convw), const(gconv)],
        out_specs=[pl.BlockSpec((1, N_HEADS, rows, KEY_WIDTH), lambda b, t: (b, 0, t, 0)),
                   pl.BlockSpec((1, rows, KEY_WIDTH), lambda b, t: (b, t, 0)),
                   pl.BlockSpec((1, rows, CONV_DIM), lambda b, t: (b, t, 0)),
                   pl.BlockSpec((8, CONV_DIM), lambda b, t: (0, 0))],
        out_shape=[jax.ShapeDtypeStruct((bsz, N_HEADS, seq, KEY_WIDTH), BF16),
                   jax.ShapeDtypeStruct((bsz, seq, KEY_WIDTH), BF16),
                   jax.ShapeDtypeStruct((bsz, seq, CONV_DIM), BF16),
                   jax.ShapeDtypeStruct((8, CONV_DIM), F32)],
        scratch_shapes=[pltpu.VMEM((8, CONV_DIM), F32)],
        compiler_params=pltpu.CompilerParams(dimension_semantics=("arbitrary", "arbitrary"),
                                             vmem_limit_bytes=VMEM_LIMIT),
        name="premix",
    )(x3, tab, carry_in, g_pre, w1, gq, gkv, wqn, bduk, wqr, convw, gconv)


def _attention_kernel(q_ref, k_ref, km_ref, o_ref, s_scr, m_scr, l_scr, acc_scr, *, bq, bk, chunks):
    i = pl.program_id(1)
    hpc = N_HEADS // chunks
    c_rows = hpc * bq

    def scores(c, kb):
        qc = q_ref[0, c * hpc:(c + 1) * hpc].reshape(c_rows, KEY_WIDTH)
        return _dot_nt(qc, kb)

    def values(kb):
        return jnp.concatenate([kb[:, :KV_LORA], jnp.ones((kb.shape[0], LANES), BF16)], axis=1)

    def consume(c, s, v_ones, mask):
        if mask is not None:
            s = jnp.where(mask, s, NEG_INF)
        m = m_scr[c]
        m_new = jnp.maximum(m, jnp.max(s, axis=1, keepdims=True))
        alpha = jnp.exp2(m - m_new)
        p = jnp.exp2(s - m_new).astype(BF16)
        pv = _dot(p, v_ones)
        m_scr[c] = m_new
        l_scr[c] = alpha * l_scr[c] + pv[:, KV_LORA:KV_LORA + 1]
        acc_scr[c] = alpha * acc_scr[c] + pv[:, :KV_LORA]

    km = km_ref[...]
    col = lax.broadcasted_iota(jnp.int32, (c_rows, BLOCK), 1)
    m_scr[...] = jnp.full(m_scr.shape, NEG_INF, F32)
    l_scr[...] = jnp.zeros(l_scr.shape, F32)
    acc_scr[...] = jnp.zeros(acc_scr.shape, F32)
    for c in range(chunks):
        consume(c, scores(c, km), values(km), col >= PAD_FRONT)

    def key_block(j):
        return k_ref[0, pl.ds(pl.multiple_of(j * bk, bk), bk), :]

    n_full = (i * bq) // bk
    kb0 = key_block(0)
    for c in range(chunks):
        s_scr[c] = scores(c, kb0)

    def body(j, carry):
        v_ones = values(key_block(j))
        kb_next = key_block(j + 1)
        for c in range(chunks):
            s = s_scr[c]
            s_scr[c] = scores(c, kb_next)
            consume(c, s, v_ones, None)
        return carry

    lax.fori_loop(0, n_full, body, 0)

    v_ones = values(key_block(n_full))
    qpos = i * bq + lax.rem(lax.broadcasted_iota(jnp.int32, (c_rows, bk), 0), bq)
    kpos = n_full * bk + lax.broadcasted_iota(jnp.int32, (c_rows, bk), 1)
    for c in range(chunks):
        consume(c, s_scr[c], v_ones, kpos <= qpos)

    for c in range(chunks):
        o = acc_scr[c] / l_scr[c]
        for hh in range(hpc):
            h = c * hpc + hh
            o_ref[0, :, h * KV_LORA:(h + 1) * KV_LORA] = o[hh * bq:(hh + 1) * bq].astype(BF16)


def _attention(qp, kc, kmeta, bq, bk, chunks):
    bsz, _, seq, _ = qp.shape
    c_rows = N_HEADS * bq // chunks
    scratch = [pltpu.VMEM((chunks, c_rows, bk), F32), pltpu.VMEM((chunks, c_rows, 1), F32),
               pltpu.VMEM((chunks, c_rows, 1), F32), pltpu.VMEM((chunks, c_rows, KV_LORA), F32)]
    return pl.pallas_call(
        functools.partial(_attention_kernel, bq=bq, bk=bk, chunks=chunks),
        grid=(bsz, seq // bq),
        in_specs=[pl.BlockSpec((1, N_HEADS, bq, KEY_WIDTH), lambda b, i: (b, 0, i, 0)),
                  pl.BlockSpec((1, seq, KEY_WIDTH), lambda b, i: (b, 0, 0)),
                  pl.BlockSpec((BLOCK, KEY_WIDTH), lambda b, i: (0, 0))],
        out_specs=pl.BlockSpec((1, bq, N_HEADS * KV_LORA), lambda b, i: (b, i, 0)),
        out_shape=jax.ShapeDtypeStruct((bsz, seq, N_HEADS * KV_LORA), BF16),
        scratch_shapes=scratch,
        compiler_params=pltpu.CompilerParams(dimension_semantics=("arbitrary", "arbitrary"),
                                             vmem_limit_bytes=VMEM_LIMIT),
        name="attention",
    )(qp, kc, kmeta)


def _postmix_kernel(x_ref, olat_ref, convn_ref, wuv_ref, gattn_ref, wout_ref, gpost_ref, gffn_ref, wr_ref, br_ref,
                    h1_ref, t_ref, trow_ref, oh1_ref, oh2_ref, wts_ref):
    rows = x_ref.shape[0]
    attn = _dot(olat_ref[...], wuv_ref[...])
    attn_n = _rms(attn, gattn_ref[...]).astype(BF16)
    wout = wout_ref[...]
    mix = _dot(attn_n, wout[:ATTN_WIDTH]) + _dot(convn_ref[...], wout[ATTN_WIDTH:])
    h1 = x_ref[...] + _rms(mix, gpost_ref[...])
    h1_ref[...] = h1
    tf = _rms(h1, gffn_ref[...])
    t = tf.astype(BF16)
    t_ref[...] = t
    _store_token_rows(trow_ref, tf)

    logits = _dot(t, wr_ref[...]) + br_ref[...]
    lane = lax.broadcasted_iota(jnp.int32, (rows, LANES), 1)

    def first_argmax(v):
        mx = jnp.max(v, axis=1, keepdims=True)
        return mx, jnp.min(jnp.where(v == mx, lane, LANES), axis=1, keepdims=True)

    gl = jnp.where(lane < N_GROUPS, logits[:, :LANES], NEG_INF)
    gmax, gidx = first_argmax(gl)
    g_w = 1.0 / jnp.sum(jnp.exp(gl - gmax), axis=1, keepdims=True)
    el = jnp.where((lane >> 3) == gidx, logits[:, LANES:], NEG_INF)
    m1, i1 = first_argmax(el)
    m2, i2 = first_argmax(jnp.where(lane == i1, NEG_INF, el))
    e2 = jnp.exp(m2 - m1)
    den = 1.0 + e2
    oh1_ref[...] = jnp.where(lane == i1, 1.0, 0.0).astype(BF16)
    oh2_ref[...] = jnp.where(lane == i2, 1.0, 0.0).astype(BF16)
    wts_ref[...] = jnp.where(lane == 0, g_w / den, jnp.where(lane == 1, g_w * e2 / den, 0.0))


def _postmix(x2, olat2, convn2, wts, rows):
    n = x2.shape[0]
    wuv, gattn, wout, gpost, gffn, wr, br = wts
    const = lambda a: pl.BlockSpec(a.shape, lambda i: (0,) * a.ndim)
    tile = lambda w: pl.BlockSpec((rows, w), lambda i: (i, 0))
    return pl.pallas_call(
        _postmix_kernel,
        grid=(n // rows,),
        in_specs=[tile(D_MODEL), tile(N_HEADS * KV_LORA), tile(CONV_DIM),
                  const(wuv), const(gattn), const(wout), const(gpost), const(gffn), const(wr), const(br)],
        out_specs=[tile(D_MODEL), tile(D_MODEL), pl.BlockSpec((rows * SUBLANES, LANES), lambda i: (i, 0)),
                   tile(LANES), tile(LANES), tile(LANES)],
        out_shape=[jax.ShapeDtypeStruct((n, D_MODEL), F32),
                   jax.ShapeDtypeStruct((n, D_MODEL), BF16),
                   jax.ShapeDtypeStruct((n * SUBLANES, LANES), F32),
                   jax.ShapeDtypeStruct((n, LANES), BF16),
                   jax.ShapeDtypeStruct((n, LANES), BF16),
                   jax.ShapeDtypeStruct((n, LANES), F32)],
        compiler_params=pltpu.CompilerParams(dimension_semantics=("arbitrary",), vmem_limit_bytes=VMEM_LIMIT),
        name="postmix",
    )(x2, olat2, convn2, wuv, gattn, wout, gpost, gffn, wr, br)


def _silu(a):
    return a * (1.0 / (1.0 + jnp.exp(-a)))


def _plan_kernel(oh1_ref, oh2_ref, pos_ref, te_ref, misc_ref, *, chunk):
    n_tok = oh1_ref.shape[0]
    n_chunks = n_tok // chunk
    lane = lax.broadcasted_iota(jnp.int32, (SUBLANES, LANES), 1)

    def rows_of(ref, c):
        return ref[pl.ds(pl.multiple_of(c * chunk, chunk), chunk), :]

    def count_body(c, acc):
        both = rows_of(oh1_ref, c).astype(F32) + rows_of(oh2_ref, c).astype(F32)
        return acc + jnp.sum(both, axis=0, keepdims=True)

    counts = lax.fori_loop(0, n_chunks, count_body, jnp.zeros((1, LANES), F32))
    counts = jnp.broadcast_to(counts, (SUBLANES, LANES)).astype(jnp.int32)
    tile_shift = EXPERT_TILE.bit_length() - 1
    padded = ((counts + (EXPERT_TILE - 1)) >> tile_shift) << tile_shift
    ends = padded
    shift = 1
    while shift < N_EXPERTS:
        ends = ends + jnp.where(lane >= shift, pltpu.roll(ends, shift, axis=1), 0)
        shift *= 2
    base = (ends - padded)[0:1].astype(F32)

    r_iota = lax.broadcasted_iota(jnp.int32, (chunk, chunk), 0)
    c_iota = lax.broadcasted_iota(jnp.int32, (chunk, chunk), 1)
    earlier = jnp.where(c_iota < r_iota, 1.0, 0.0).astype(BF16)

    def rank_pass(ref, slot, run):
        def body(c, run):
            oh = rows_of(ref, c)
            ohf = oh.astype(F32)
            before = _dot(earlier, oh) + run
            posv = jnp.sum(ohf * before, axis=1, keepdims=True)
            pos_row = jnp.broadcast_to(posv, (chunk, LANES)).T[0:1, :]
            pos_ref[pl.ds(slot * n_chunks + c, 1), :] = pos_row.astype(jnp.int32)
            return run + jnp.sum(ohf, axis=0, keepdims=True)
        return lax.fori_loop(0, n_chunks, body, run)

    rank_pass(oh2_ref, 1, rank_pass(oh1_ref, 0, base))

    n_tile_lanes = te_ref.shape[1]
    tile_start = lax.broadcasted_iota(jnp.int32, (SUBLANES, n_tile_lanes), 1) * EXPERT_TILE
    te = jnp.zeros((SUBLANES, n_tile_lanes), jnp.int32)
    for e in range(N_EXPERTS):
        te = te + jnp.where(ends[:, e:e + 1] <= tile_start, 1, 0)
    te_ref[...] = jnp.minimum(te, N_EXPERTS - 1)
    used_tiles = ends[:, N_EXPERTS - 1:N_EXPERTS] >> tile_shift
    misc_ref[...] = jnp.where(lane < N_EXPERTS, jnp.where(padded > 0, ends - EXPERT_TILE, -1), used_tiles)


def _plan(oh1, oh2, n_tiles, chunk):
    n = oh1.shape[0]
    n_tile_lanes = -(-n_tiles // LANES) * LANES
    whole = lambda shape: pl.BlockSpec(shape, lambda i: (0, 0))
    return pl.pallas_call(
        functools.partial(_plan_kernel, chunk=chunk),
        grid=(1,),
        in_specs=[whole((n, LANES)), whole((n, LANES))],
        out_specs=[whole((TOP_K * n // chunk, chunk)), whole((SUBLANES, n_tile_lanes)), whole((SUBLANES, LANES))],
        out_shape=[jax.ShapeDtypeStruct((TOP_K * n // chunk, chunk), jnp.int32),
                   jax.ShapeDtypeStruct((SUBLANES, n_tile_lanes), jnp.int32),
                   jax.ShapeDtypeStruct((SUBLANES, LANES), jnp.int32)],
        compiler_params=pltpu.CompilerParams(dimension_semantics=("arbitrary",), vmem_limit_bytes=VMEM_LIMIT),
        name="moe_plan",
    )(oh1, oh2)


def _row_slice(ref, row, n_rows=1):
    start = row * SUBLANES if isinstance(row, int) else pl.multiple_of(row * SUBLANES, SUBLANES)
    return ref.at[pl.ds(start, n_rows * SUBLANES)]


def _dispatch_kernel(zstart_ref, pos_ref, trow_ref, xs_hbm, zero_buf, zero_sem, row_sem, *, tokens):
    i = pl.program_id(0)

    @pl.when(i == 0)
    def _():
        zero_buf[...] = jnp.zeros(zero_buf.shape, F32)
        n_tiles = xs_hbm.shape[0] // (EXPERT_TILE * SUBLANES)
        used = zstart_ref[N_EXPERTS]

        def zero_tile(first_row):
            return pltpu.make_async_copy(zero_buf, _row_slice(xs_hbm, first_row, EXPERT_TILE), zero_sem)

        for phase in ("start", "wait"):
            for e in range(N_EXPERTS):
                @pl.when(zstart_ref[e] >= 0)
                def _():
                    copy = zero_tile(zstart_ref[e])
                    copy.start() if phase == "start" else copy.wait()

            def tail(t, carry):
                copy = zero_tile(t * EXPERT_TILE)
                copy.start() if phase == "start" else copy.wait()
                return carry

            lax.fori_loop(used, n_tiles, tail, 0)

    unroll = 8

    def issue(g, carry):
        for u in range(unroll):
            n = g * unroll + u
            src = _row_slice(trow_ref, n)
            for k in range(TOP_K):
                pltpu.make_async_copy(src, _row_slice(xs_hbm, pos_ref[k, n]), row_sem).start()
        return carry

    lax.fori_loop(0, tokens // unroll, issue, 0)
    for k in range(TOP_K):
        pltpu.make_async_copy(trow_ref, _row_slice(xs_hbm, 0, tokens), row_sem).wait()


def _dispatch(zstart, pos, trow, sorted_rows, tokens):
    n = pos.shape[1]
    return pl.pallas_call(
        functools.partial(_dispatch_kernel, tokens=tokens),
        grid=(n // tokens,),
        in_specs=[pl.BlockSpec(memory_space=pltpu.SMEM),
                  pl.BlockSpec((TOP_K, tokens), lambda i: (0, i), memory_space=pltpu.SMEM),
                  pl.BlockSpec((tokens * SUBLANES, LANES), lambda i: (i, 0))],
        out_specs=pl.BlockSpec(memory_space=pl.ANY),
        out_shape=jax.ShapeDtypeStruct((sorted_rows * SUBLANES, LANES), F32),
        scratch_shapes=[pltpu.VMEM((EXPERT_TILE * SUBLANES, LANES), F32), pltpu.SemaphoreType.DMA,
                        pltpu.SemaphoreType.DMA],
        compiler_params=pltpu.CompilerParams(dimension_semantics=("arbitrary",)),
        name="moe_dispatch",
    )(zstart, pos, trow)


def _experts_kernel(te_ref, used_ref, xs_ref, wg_ref, wu_ref, wd_ref, ys_ref):
    t = pl.program_id(0)

    @pl.when(t < used_ref[0])
    def _():
        x = _load_token_rows(xs_ref, EXPERT_TILE).astype(BF16)
        hid = _silu(_dot(x, wg_ref[0])) * _dot(x, wu_ref[0])
        _store_token_rows(ys_ref, _dot(hid.astype(BF16), wd_ref[0]))

    @pl.when(t >= used_ref[0])
    def _():
        ys_ref[...] = jnp.zeros(ys_ref.shape, F32)


def _experts(te, used, xs, wg, wu, wd, n_tiles):
    blk = EXPERT_TILE * SUBLANES
    return pl.pallas_call(
        _experts_kernel,
        grid_spec=pltpu.PrefetchScalarGridSpec(
            num_scalar_prefetch=2,
            grid=(n_tiles,),
            in_specs=[pl.BlockSpec((blk, LANES), lambda t, te, used: (jnp.minimum(t, used[0] - 1), 0)),
                      pl.BlockSpec((1, D_MODEL, D_FF), lambda t, te, used: (te[t], 0, 0)),
                      pl.BlockSpec((1, D_MODEL, D_FF), lambda t, te, used: (te[t], 0, 0)),
                      pl.BlockSpec((1, D_FF, D_MODEL), lambda t, te, used: (te[t], 0, 0))],
            out_specs=pl.BlockSpec((blk, LANES), lambda t, te, used: (t, 0))),
        out_shape=jax.ShapeDtypeStruct(xs.shape, F32),
        compiler_params=pltpu.CompilerParams(dimension_semantics=("arbitrary",), vmem_limit_bytes=VMEM_LIMIT),
        name="moe_experts",
    )(te, used, xs, wg, wu, wd)


def _combine_kernel(pos_ref, t_ref, h1_ref, wts_ref, ys_hbm, wsg_ref, wsu_ref, wsd_ref, gout_ref, out_ref,
                    ybuf, sem, *, tokens):
    unroll = 8

    def issue(g, carry):
        for u in range(unroll):
            n = g * unroll + u
            for k in range(TOP_K):
                pltpu.make_async_copy(_row_slice(ys_hbm, pos_ref[k, n]), _row_slice(ybuf.at[k], n), sem).start()
        return carry

    lax.fori_loop(0, tokens // unroll, issue, 0)

    t = t_ref[...]
    hid = _silu(_dot(t, wsg_ref[...])) * _dot(t, wsu_ref[...])
    shared = _dot(hid.astype(BF16), wsd_ref[...])

    for k in range(TOP_K):
        pltpu.make_async_copy(_row_slice(ys_hbm, 0, tokens), ybuf.at[k], sem).wait()
    wts = wts_ref[...]
    routed = wts[:, 0:1] * _load_token_rows(ybuf.at[0], tokens) + wts[:, 1:2] * _load_token_rows(ybuf.at[1], tokens)
    out_ref[...] = h1_ref[...] + _rms(routed + shared, gout_ref[...])


def _combine(pos, t2, h1, wts, ys, shared_w, tokens):
    n = t2.shape[0]
    wsg, wsu, wsd, gout = shared_w
    const = lambda a: pl.BlockSpec(a.shape, lambda i: (0,) * a.ndim)
    tile = lambda w: pl.BlockSpec((tokens, w), lambda i: (i, 0))
    return pl.pallas_call(
        functools.partial(_combine_kernel, tokens=tokens),
        grid=(n // tokens,),
        in_specs=[pl.BlockSpec((TOP_K, tokens), lambda i: (0, i), memory_space=pltpu.SMEM),
                  tile(D_MODEL), tile(D_MODEL), tile(LANES), pl.BlockSpec(memory_space=pl.ANY),
                  const(wsg), const(wsu), const(wsd), const(gout)],
        out_specs=tile(D_MODEL),
        out_shape=jax.ShapeDtypeStruct((n, D_MODEL), F32),
        scratch_shapes=[pltpu.VMEM((TOP_K, tokens * SUBLANES, LANES), F32), pltpu.SemaphoreType.DMA],
        compiler_params=pltpu.CompilerParams(dimension_semantics=("arbitrary",), vmem_limit_bytes=VMEM_LIMIT),
        name="moe_combine",
    )(pos, t2, h1, wts, ys, wsg, wsu, wsd, gout)


def _swap_halves(w):
    half = w.shape[-1] // 2
    return jnp.concatenate([w[..., half:], w[..., :half]], axis=-1)


def _block_diag(blocks):
    h, r, c = blocks.shape
    eye = jnp.eye(h, dtype=blocks.dtype)
    return jnp.einsum('hrc,hg->hrgc', blocks, eye).reshape(h * r, h * c)


def kernel(x, positions, meta_tokens, pre_mix_norm, w_in, q_norm, w_uq, kv_norm, w_ukv, conv_w, attn_out_norm,
           conv_out_norm, w_out, post_mix_norm, pre_ffn_norm, w_group_router, b_group_router, w_expert_router,
           b_expert_router, w_gate, w_up, w_down, w_sh_gate, w_sh_up, w_sh_down, post_ffn_norm):
    bsz, seq, d = x.shape
    n = bsz * seq
    row = lambda v: v.reshape(1, -1).astype(F32)

    wi = w_in[0]
    o_kv, o_pe, o_b = Q_LORA, Q_LORA + KV_LORA, Q_LORA + KV_LORA + QK_ROPE
    k_pe = wi[:, o_pe:o_b]
    w1 = jnp.concatenate([wi[:, :o_pe], k_pe, _swap_halves(k_pe), jnp.zeros((d, LANES - 2 * QK_ROPE), F32),
                          wi[:, o_b:]], axis=1).astype(BF16)
    wq = w_uq[0].reshape(Q_LORA, N_HEADS, QK_NOPE + QK_ROPE)
    wqn = wq[:, :, :QK_NOPE].reshape(Q_LORA, N_HEADS * QK_NOPE).astype(BF16)
    wq_rope = wq[:, :, QK_NOPE:]
    wqr = jnp.concatenate([wq_rope, _swap_halves(wq_rope),
                           jnp.zeros((Q_LORA, N_HEADS, LANES - 2 * QK_ROPE), F32)], axis=2)
    wqr = wqr.reshape(Q_LORA, N_HEADS * LANES).astype(BF16)
    wkv = w_ukv[0].reshape(KV_LORA, N_HEADS, QK_NOPE + V_DIM)
    bduk = _block_diag(jnp.transpose(wkv[:, :, :QK_NOPE], (1, 2, 0))).astype(BF16)
    wuv = _block_diag(jnp.transpose(wkv[:, :, QK_NOPE:], (1, 0, 2))).astype(BF16)
    premix_w = (row(pre_mix_norm[0]), w1, row(q_norm[0]), row(kv_norm[0]), wqn, bduk, wqr,
                conv_w[0].astype(F32), row(conv_out_norm[0]))
    wr = jnp.concatenate([w_group_router[0], jnp.zeros((d, LANES - N_GROUPS), F32),
                          w_expert_router[0], jnp.zeros((d, LANES - N_EXPERTS), F32)], axis=1).astype(BF16)
    br = jnp.concatenate([b_group_router[0], jnp.zeros((LANES - N_GROUPS,), F32),
                          b_expert_router[0], jnp.zeros((LANES - N_EXPERTS,), F32)]).reshape(1, -1).astype(F32)
    postmix_w = (wuv, row(attn_out_norm[0]), w_out[0].astype(BF16), row(post_mix_norm[0]), row(pre_ffn_norm[0]),
                 wr, br)
    expert_w = (w_gate[0].astype(BF16), w_up[0].astype(BF16), w_down[0].astype(BF16))
    shared_w = (w_sh_gate[0].astype(BF16), w_sh_up[0].astype(BF16), w_sh_down[0].astype(BF16),
                row(post_ffn_norm[0]))

    pos_all = jnp.concatenate([positions.astype(jnp.int32).reshape(-1) + N_META,
                               jnp.zeros((PAD_FRONT,), jnp.int32), jnp.arange(N_META, dtype=jnp.int32)])
    inv_freq = 1.0 / (ROPE_THETA ** (jnp.arange(0, QK_ROPE, 2, dtype=F32) / QK_ROPE))
    n_tab = n + BLOCK
    tab = _rope_table(pos_all.reshape(1, n_tab), inv_freq.reshape(-1, 1), n_tab // 3)

    meta_blk = jnp.pad(meta_tokens.astype(F32), ((PAD_FRONT, 0), (0, 0)))[None]
    zero_carry = jnp.zeros((8, CONV_DIM), F32)
    _, kmeta, _, meta_tail = _premix(meta_blk, tab, n // BLOCK, zero_carry, premix_w, BLOCK)
    qp, kc, convn, _ = _premix(x, tab, 0, meta_tail, premix_w, 512)

    olat = _attention(qp, kc, kmeta[0], 256, 512, 4)

    h1, t, trow, oh1, oh2, wts = _postmix(x.reshape(n, d), olat.reshape(n, -1), convn.reshape(n, -1), postmix_w, 512)

    sorted_rows = TOP_K * n + N_EXPERTS * EXPERT_TILE
    n_tiles = sorted_rows // EXPERT_TILE
    pos2, te8, misc = _plan(oh1, oh2, n_tiles, 512)
    pos = pos2.reshape(TOP_K, n)
    xs = _dispatch(misc[0, :N_EXPERTS + 1], pos, trow, sorted_rows, 1024)
    ys = _experts(te8[0, :n_tiles], misc[0, N_EXPERTS:N_EXPERTS + 1], xs, *expert_w, n_tiles)
    out = _combine(pos, t, h1, wts, ys, shared_w, 512)
    return out.reshape(bsz, seq, d)
```

```python
import functools
import math

import jax
import jax.numpy as jnp
from jax import lax
from jax.experimental import pallas as pl
from jax.experimental.pallas import tpu as pltpu

D_MODEL = 1024
N_META = 16
BLOCK = 128
PAD_FRONT = BLOCK - N_META
N_HEADS = 8
QK_NOPE = 64
QK_ROPE = 32
V_DIM = 64
Q_LORA = 256
KV_LORA = 128
ROPE_THETA = 10000.0
ATTN_SCALE = (QK_NOPE + QK_ROPE) ** -0.5
Q_SCALE = ATTN_SCALE * math.log2(math.e)
ATTN_WIDTH = N_HEADS * V_DIM
CONV_DIM = 512
CONV_W = 3
N_GROUPS = 4
EXPERTS_PER_GROUP = 8
N_EXPERTS = N_GROUPS * EXPERTS_PER_GROUP
D_FF = 256
EPS = 1e-6
NEG_INF = -1e30

LANES = 128
SUBLANES = 8
TOP_K = 2
EXPERT_TILE = 512
KEY_WIDTH = 2 * LANES
IN_COLS_PADDED = Q_LORA + KV_LORA + LANES + 3 * CONV_DIM

F32 = jnp.float32
BF16 = jnp.bfloat16
VMEM_LIMIT = 56 * 1024 * 1024


def _rms(x, g):
    return x * lax.rsqrt(jnp.mean(x * x, axis=-1, keepdims=True) + EPS) * g


def _dot(a, b):
    return jnp.dot(a, b, preferred_element_type=F32)


def _dot_nt(a, b):
    return lax.dot_general(a, b, (((1,), (1,)), ((), ())), preferred_element_type=F32)


def _store_token_rows(ref, value):
    rows = value.shape[0]
    for c in range(D_MODEL // LANES):
        ref[pl.ds(c, rows, stride=SUBLANES), :] = value[:, c * LANES:(c + 1) * LANES]


def _load_token_rows(ref, rows):
    return jnp.concatenate([ref[pl.ds(c, rows, stride=SUBLANES), :] for c in range(D_MODEL // LANES)], axis=1)


def _rope_table_kernel(pos_ref, invf_ref, out_ref):
    pos = pos_ref[...].astype(F32)
    ang = invf_ref[...] * pos
    c = jnp.cos(ang)
    s = jnp.sin(ang)
    planes = jnp.concatenate([c, c, -s, s, jnp.zeros((LANES - 4 * (QK_ROPE // 2), ang.shape[1]), F32)], axis=0)
    out_ref[...] = planes.T


def _rope_table(pos_row, inv_freq_col, chunk):
    n = pos_row.shape[1]
    return pl.pallas_call(
        _rope_table_kernel,
        grid=(n // chunk,),
        in_specs=[pl.BlockSpec((1, chunk), lambda i: (0, i)),
                  pl.BlockSpec((QK_ROPE // 2, 1), lambda i: (0, 0))],
        out_specs=pl.BlockSpec((chunk, LANES), lambda i: (i, 0)),
        out_shape=jax.ShapeDtypeStruct((n, LANES), F32),
        name="rope_table",
    )(pos_row, inv_freq_col)


def _premix_kernel(x_ref, tab_ref, carry_in_ref, g_pre_ref, w1_ref, gq_ref, gkv_ref, wqn_ref, bduk_ref,
                   wqr_ref, convw_ref, gconv_ref, qp_ref, kc_ref, convn_ref, utail_ref, carry_ref):
    t = pl.program_id(1)
    rows = x_ref.shape[1]

    @pl.when(t == 0)
    def _():
        carry_ref[...] = carry_in_ref[...]

    x = x_ref[0]
    xn = _rms(x, g_pre_ref[...]).astype(BF16)
    z = _dot(xn, w1_ref[...])
    tab = tab_ref[...]
    lane = lax.broadcasted_iota(jnp.int32, (rows, LANES), 1)

    def rope(pair):
        prod = pair * tab
        return jnp.where(lane < QK_ROPE, prod + pltpu.roll(prod, LANES - QK_ROPE, axis=1), 0.0)

    ckvn = _rms(z[:, Q_LORA:Q_LORA + KV_LORA], gkv_ref[...])
    krope = rope(z[:, Q_LORA + KV_LORA:Q_LORA + KV_LORA + LANES])
    kc_ref[0] = jnp.concatenate([ckvn, krope], axis=1).astype(BF16)

    cqn = _rms(z[:, :Q_LORA], gq_ref[...]).astype(BF16)
    qn = _dot(cqn, wqn_ref[...]).astype(BF16)
    qabs = _dot(qn, bduk_ref[...])
    qr = _dot(cqn, wqr_ref[...])
    for h in range(N_HEADS):
        sl = slice(h * LANES, (h + 1) * LANES)
        qp_ref[0, h, :, 0:LANES] = (qabs[:, sl] * Q_SCALE).astype(BF16)
        qp_ref[0, h, :, LANES:KEY_WIDTH] = (rope(qr[:, sl]) * Q_SCALE).astype(BF16)

    c0 = Q_LORA + KV_LORA + LANES
    gate_b = z[:, c0:c0 + CONV_DIM]
    u = z[:, c0 + CONV_DIM:c0 + 2 * CONV_DIM] * z[:, c0 + 2 * CONV_DIM:c0 + 3 * CONV_DIM]
    prev = carry_ref[...]
    row8 = lax.broadcasted_iota(jnp.int32, (8, CONV_DIM), 0)
    r1 = pltpu.roll(u, 1, axis=0)
    r2 = pltpu.roll(u, 2, axis=0)
    p1 = pltpu.roll(prev, 1, axis=0)
    p2 = pltpu.roll(prev, 2, axis=0)
    u1 = jnp.concatenate([jnp.where(row8 < 1, p1, r1[0:8]), r1[8:]], axis=0)
    u2 = jnp.concatenate([jnp.where(row8 < 2, p2, r2[0:8]), r2[8:]], axis=0)
    cw = convw_ref[...]
    y = cw[0:1] * u2 + cw[1:2] * u1 + cw[2:3] * u
    convn_ref[0] = _rms(gate_b * y, gconv_ref[...]).astype(BF16)
    tail = u[rows - 8:rows]
    carry_ref[...] = tail
    utail_ref[...] = tail


def _premix(x3, tab, tab_block0, carry_in, wts, rows):
    bsz, seq, _ = x3.shape
    nt = seq // rows
    g_pre, w1, gq, gkv, wqn, bduk, wqr, convw, gconv = wts
    const = lambda a: pl.BlockSpec(a.shape, lambda b, t: (0,) * a.ndim)
    return pl.pallas_call(
        _premix_kernel,
        grid=(bsz, nt),
        in_specs=[pl.BlockSpec((1, rows, D_MODEL), lambda b, t: (b, t, 0)),
                  pl.BlockSpec((rows, LANES), lambda b, t: (tab_block0 + b * nt + t, 0)),
                  const(carry_in), const(g_pre), const(w1), const(gq), const(gkv), const(wqn), const(bduk),
                  const(wqr), const(convw), const(gconv)],
        out_specs=[pl.BlockSpec((1, N_HEADS, rows, KEY_WIDTH), lambda b, t: (b, 0, t, 0)),
                   pl.BlockSpec((1, rows, KEY_WIDTH), lambda b, t: (b, t, 0)),
                   pl.BlockSpec((1, rows, CONV_DIM), lambda b, t: (b, t, 0)),
                   pl.BlockSpec((8, CONV_DIM), lambda b, t: (0, 0))],
        out_shape=[jax.ShapeDtypeStruct((bsz, N_HEADS, seq, KEY_WIDTH), BF16),
                   jax.ShapeDtypeStruct((bsz, seq, KEY_WIDTH), BF16),
                   jax.ShapeDtypeStruct((bsz, seq, CONV_DIM), BF16),
                   jax.ShapeDtypeStruct((8, CONV_DIM), F32)],
        scratch_shapes=[pltpu.VMEM((8, CONV_DIM), F32)],
        compiler_params=pltpu.CompilerParams(dimension_semantics=("arbitrary", "arbitrary"),
                                             vmem_limit_bytes=VMEM_LIMIT),
        name="premix",
    )(x3, tab, carry_in, g_pre, w1, gq, gkv, wqn, bduk, wqr, convw, gconv)


def _attention_kernel(q_ref, k_ref, km_ref, o_ref, s_scr, m_scr, l_scr, acc_scr, *, bq, bk, chunks):
    i = pl.program_id(1)
    hpc = N_HEADS // chunks
    c_rows = hpc * bq

    def scores(c, kb):
        qc = q_ref[0, c * hpc:(c + 1) * hpc].reshape(c_rows, KEY_WIDTH)
        return _dot_nt(qc, kb)

    def values(kb):
        return jnp.concatenate([kb[:, :KV_LORA], jnp.ones((kb.shape[0], LANES), BF16)], axis=1)

    def consume(c, s, v_ones, mask):
        if mask is not None:
            s = jnp.where(mask, s, NEG_INF)
        m = m_scr[c]
        m_new = jnp.maximum(m, jnp.max(s, axis=1, keepdims=True))
        alpha = jnp.exp2(m - m_new)
        p = jnp.exp2(s - m_new).astype(BF16)
        pv = _dot(p, v_ones)
        m_scr[c] = m_new
        l_scr[c] = alpha * l_scr[c] + pv[:, KV_LORA:KV_LORA + 1]
        acc_scr[c] = alpha * acc_scr[c] + pv[:, :KV_LORA]

    km = km_ref[...]
    col = lax.broadcasted_iota(jnp.int32, (c_rows, BLOCK), 1)
    m_scr[...] = jnp.full(m_scr.shape, NEG_INF, F32)
    l_scr[...] = jnp.zeros(l_scr.shape, F32)
    acc_scr[...] = jnp.zeros(acc_scr.shape, F32)
    for c in range(chunks):
        consume(c, scores(c, km), values(km), col >= PAD_FRONT)

    def key_block(j):
        return k_ref[0, pl.ds(pl.multiple_of(j * bk, bk), bk), :]

    n_full = (i * bq) // bk
    kb0 = key_block(0)
    for c in range(chunks):
        s_scr[c] = scores(c, kb0)

    def body(j, carry):
        v_ones = values(key_block(j))
        kb_next = key_block(j + 1)
        for c in range(chunks):
            s = s_scr[c]
            s_scr[c] = scores(c, kb_next)
            consume(c, s, v_ones, None)
        return carry

    lax.fori_loop(0, n_full, body, 0)

    v_ones = values(key_block(n_full))
    qpos = i * bq + lax.rem(lax.broadcasted_iota(jnp.int32, (c_rows, bk), 0), bq)
    kpos = n_full * bk + lax.broadcasted_iota(jnp.int32, (c_rows, bk), 1)
    for c in range(chunks):
        consume(c, s_scr[c], v_ones, kpos <= qpos)

    for c in range(chunks):
        o = acc_scr[c] / l_scr[c]
        for hh in range(hpc):
            h = c * hpc + hh
            o_ref[0, :, h * KV_LORA:(h + 1) * KV_LORA] = o[hh * bq:(hh + 1) * bq].astype(BF16)


def _attention(qp, kc, kmeta, bq, bk, chunks):
    bsz, _, seq, _ = qp.shape
    c_rows = N_HEADS * bq // chunks
    scratch = [pltpu.VMEM((chunks, c_rows, bk), F32), pltpu.VMEM((chunks, c_rows, 1), F32),
               pltpu.VMEM((chunks, c_rows, 1), F32), pltpu.VMEM((chunks, c_rows, KV_LORA), F32)]
    return pl.pallas_call(
        functools.partial(_attention_kernel, bq=bq, bk=bk, chunks=chunks),
        grid=(bsz, seq // bq),
        in_specs=[pl.BlockSpec((1, N_HEADS, bq, KEY_WIDTH), lambda b, i: (b, 0, i, 0)),
                  pl.BlockSpec((1, seq, KEY_WIDTH), lambda b, i: (b, 0, 0)),
                  pl.BlockSpec((BLOCK, KEY_WIDTH), lambda b, i: (0, 0))],
        out_specs=pl.BlockSpec((1, bq, N_HEADS * KV_LORA), lambda b, i: (b, i, 0)),
        out_shape=jax.ShapeDtypeStruct((bsz, seq, N_HEADS * KV_LORA), BF16),
        scratch_shapes=scratch,
        compiler_params=pltpu.CompilerParams(dimension_semantics=("arbitrary", "arbitrary"),
                                             vmem_limit_bytes=VMEM_LIMIT),
        name="attention",
    )(qp, kc, kmeta)


def _postmix_kernel(x_ref, olat_ref, convn_ref, wuv_ref, gattn_ref, wout_ref, gpost_ref, gffn_ref, wr_ref, br_ref,
                    h1_ref, t_ref, trow_ref, oh1_ref, oh2_ref, wts_ref):
    rows = x_ref.shape[0]
    attn = _dot(olat_ref[...], wuv_ref[...])
    attn_n = _rms(attn, gattn_ref[...]).astype(BF16)
    wout = wout_ref[...]
    mix = _dot(attn_n, wout[:ATTN_WIDTH]) + _dot(convn_ref[...], wout[ATTN_WIDTH:])
    h1 = x_ref[...] + _rms(mix, gpost_ref[...])
    h1_ref[...] = h1
    tf = _rms(h1, gffn_ref[...])
    t = tf.astype(BF16)
    t_ref[...] = t
    _store_token_rows(trow_ref, tf)

    logits = _dot(t, wr_ref[...]) + br_ref[...]
    lane = lax.broadcasted_iota(jnp.int32, (rows, LANES), 1)

    def first_argmax(v):
        mx = jnp.max(v, axis=1, keepdims=True)
        return mx, jnp.min(jnp.where(v == mx, lane, LANES), axis=1, keepdims=True)

    gl = jnp.where(lane < N_GROUPS, logits[:, :LANES], NEG_INF)
    gmax, gidx = first_argmax(gl)
    g_w = 1.0 / jnp.sum(jnp.exp(gl - gmax), axis=1, keepdims=True)
    el = jnp.where((lane >> 3) == gidx, logits[:, LANES:], NEG_INF)
    m1, i1 = first_argmax(el)
    m2, i2 = first_argmax(jnp.where(lane == i1, NEG_INF, el))
    e2 = jnp.exp(m2 - m1)
    den = 1.0 + e2
    oh1_ref[...] = jnp.where(lane == i1, 1.0, 0.0).astype(BF16)
    oh2_ref[...] = jnp.where(lane == i2, 1.0, 0.0).astype(BF16)
    wts_ref[...] = jnp.where(lane == 0, g_w / den, jnp.where(lane == 1, g_w * e2 / den, 0.0))


def _postmix(x2, olat2, convn2, wts, rows):
    n = x2.shape[0]
    wuv, gattn, wout, gpost, gffn, wr, br = wts
    const = lambda a: pl.BlockSpec(a.shape, lambda i: (0,) * a.ndim)
    tile = lambda w: pl.BlockSpec((rows, w), lambda i: (i, 0))
    return pl.pallas_call(
        _postmix_kernel,
        grid=(n // rows,),
        in_specs=[tile(D_MODEL), tile(N_HEADS * KV_LORA), tile(CONV_DIM),
                  const(wuv), const(gattn), const(wout), const(gpost), const(gffn), const(wr), const(br)],
        out_specs=[tile(D_MODEL), tile(D_MODEL), pl.BlockSpec((rows * SUBLANES, LANES), lambda i: (i, 0)),
                   tile(LANES), tile(LANES), tile(LANES)],
        out_shape=[jax.ShapeDtypeStruct((n, D_MODEL), F32),
                   jax.ShapeDtypeStruct((n, D_MODEL), BF16),
                   jax.ShapeDtypeStruct((n * SUBLANES, LANES), F32),
                   jax.ShapeDtypeStruct((n, LANES), BF16),
                   jax.ShapeDtypeStruct((n, LANES), BF16),
                   jax.ShapeDtypeStruct((n, LANES), F32)],
        compiler_params=pltpu.CompilerParams(dimension_semantics=("arbitrary",), vmem_limit_bytes=VMEM_LIMIT),
        name="postmix",
    )(x2, olat2, convn2, wuv, gattn, wout, gpost, gffn, wr, br)


def _silu(a):
    return a * (1.0 / (1.0 + jnp.exp(-a)))


def _plan_kernel(oh1_ref, oh2_ref, pos_ref, te_ref, misc_ref, *, chunk):
    n_tok = oh1_ref.shape[0]
    n_chunks = n_tok // chunk
    lane = lax.broadcasted_iota(jnp.int32, (SUBLANES, LANES), 1)

    def rows_of(ref, c):
        return ref[pl.ds(pl.multiple_of(c * chunk, chunk), chunk), :]

    def count_body(c, acc):
        both = rows_of(oh1_ref, c).astype(F32) + rows_of(oh2_ref, c).astype(F32)
        return acc + jnp.sum(both, axis=0, keepdims=True)

    counts = lax.fori_loop(0, n_chunks, count_body, jnp.zeros((1, LANES), F32))
    counts = jnp.broadcast_to(counts, (SUBLANES, LANES)).astype(jnp.int32)
    tile_shift = EXPERT_TILE.bit_length() - 1
    padded = ((counts + (EXPERT_TILE - 1)) >> tile_shift) << tile_shift
    ends = padded
    shift = 1
    while shift < N_EXPERTS:
        ends = ends + jnp.where(lane >= shift, pltpu.roll(ends, shift, axis=1), 0)
        shift *= 2
    base = (ends - padded)[0:1].astype(F32)

    r_iota = lax.broadcasted_iota(jnp.int32, (chunk, chunk), 0)
    c_iota = lax.broadcasted_iota(jnp.int32, (chunk, chunk), 1)
    earlier = jnp.where(c_iota < r_iota, 1.0, 0.0).astype(BF16)

    def rank_pass(ref, slot, run):
        def body(c, run):
            oh = rows_of(ref, c)
            ohf = oh.astype(F32)
            before = _dot(earlier, oh) + run
            posv = jnp.sum(ohf * before, axis=1, keepdims=True)
            pos_row = jnp.broadcast_to(posv, (chunk, LANES)).T[0:1, :]
            pos_ref[pl.ds(slot * n_chunks + c, 1), :] = pos_row.astype(jnp.int32)
            return run + jnp.sum(ohf, axis=0, keepdims=True)
        return lax.fori_loop(0, n_chunks, body, run)

    rank_pass(oh2_ref, 1, rank_pass(oh1_ref, 0, base))

    n_tile_lanes = te_ref.shape[1]
    tile_start = lax.broadcasted_iota(jnp.int32, (SUBLANES, n_tile_lanes), 1) * EXPERT_TILE
    te = jnp.zeros((SUBLANES, n_tile_lanes), jnp.int32)
    for e in range(N_EXPERTS):
        te = te + jnp.where(ends[:, e:e + 1] <= tile_start, 1, 0)
    te_ref[...] = jnp.minimum(te, N_EXPERTS - 1)
    used_tiles = ends[:, N_EXPERTS - 1:N_EXPERTS] >> tile_shift
    misc_ref[...] = jnp.where(lane < N_EXPERTS, jnp.where(padded > 0, ends - EXPERT_TILE, -1), used_tiles)


def _plan(oh1, oh2, n_tiles, chunk):
    n = oh1.shape[0]
    n_tile_lanes = -(-n_tiles // LANES) * LANES
    whole = lambda shape: pl.BlockSpec(shape, lambda i: (0, 0))
    return pl.pallas_call(
        functools.partial(_plan_kernel, chunk=chunk),
        grid=(1,),
        in_specs=[whole((n, LANES)), whole((n, LANES))],
        out_specs=[whole((TOP_K * n // chunk, chunk)), whole((SUBLANES, n_tile_lanes)), whole((SUBLANES, LANES))],
        out_shape=[jax.ShapeDtypeStruct((TOP_K * n // chunk, chunk), jnp.int32),
                   jax.ShapeDtypeStruct((SUBLANES, n_tile_lanes), jnp.int32),
                   jax.ShapeDtypeStruct((SUBLANES, LANES), jnp.int32)],
        compiler_params=pltpu.CompilerParams(dimension_semantics=("arbitrary",), vmem_limit_bytes=VMEM_LIMIT),
        name="moe_plan",
    )(oh1, oh2)


def _row_slice(ref, row, n_rows=1):
    start = row * SUBLANES if isinstance(row, int) else pl.multiple_of(row * SUBLANES, SUBLANES)
    return ref.at[pl.ds(start, n_rows * SUBLANES)]


def _dispatch_kernel(zstart_ref, pos_ref, trow_ref, xs_hbm, zero_buf, zero_sem, row_sem, *, tokens):
    i = pl.program_id(0)

    @pl.when(i == 0)
    def _():
        zero_buf[...] = jnp.zeros(zero_buf.shape, F32)
        n_tiles = xs_hbm.shape[0] // (EXPERT_TILE * SUBLANES)
        used = zstart_ref[N_EXPERTS]

        def zero_tile(first_row):
            return pltpu.make_async_copy(zero_buf, _row_slice(xs_hbm, first_row, EXPERT_TILE), zero_sem)

        for phase in ("start", "wait"):
            for e in range(N_EXPERTS):
                @pl.when(zstart_ref[e] >= 0)
                def _():
                    copy = zero_tile(zstart_ref[e])
                    copy.start() if phase == "start" else copy.wait()

            def tail(t, carry):
                copy = zero_tile(t * EXPERT_TILE)
                copy.start() if phase == "start" else copy.wait()
                return carry

            lax.fori_loop(used, n_tiles, tail, 0)

    unroll = 8

    def issue(g, carry):
        for u in range(unroll):
            n = g * unroll + u
            src = _row_slice(trow_ref, n)
            for k in range(TOP_K):
                pltpu.make_async_copy(src, _row_slice(xs_hbm, pos_ref[k, n]), row_sem).start(priority=(u + k) % 2)
        return carry

    lax.fori_loop(0, tokens // unroll, issue, 0)
    for k in range(TOP_K):
        pltpu.make_async_copy(trow_ref, _row_slice(xs_hbm, 0, tokens), row_sem).wait()


def _dispatch(zstart, pos, trow, sorted_rows, tokens):
    n = pos.shape[1]
    return pl.pallas_call(
        functools.partial(_dispatch_kernel, tokens=tokens),
        grid=(n // tokens,),
        in_specs=[pl.BlockSpec(memory_space=pltpu.SMEM),
                  pl.BlockSpec((TOP_K, tokens), lambda i: (0, i), memory_space=pltpu.SMEM),
                  pl.BlockSpec((tokens * SUBLANES, LANES), lambda i: (i, 0))],
        out_specs=pl.BlockSpec(memory_space=pl.ANY),
        out_shape=jax.ShapeDtypeStruct((sorted_rows * SUBLANES, LANES), F32),
        scratch_shapes=[pltpu.VMEM((EXPERT_TILE * SUBLANES, LANES), F32), pltpu.SemaphoreType.DMA,
                        pltpu.SemaphoreType.DMA],
        compiler_params=pltpu.CompilerParams(dimension_semantics=("arbitrary",)),
        name="moe_dispatch",
    )(zstart, pos, trow)


def _experts_kernel(te_ref, used_ref, xs_ref, wg_ref, wu_ref, wd_ref, ys_ref):
    t = pl.program_id(0)

    @pl.when(t < used_ref[0])
    def _():
        x = _load_token_rows(xs_ref, EXPERT_TILE).astype(BF16)
        hid = _silu(_dot(x, wg_ref[0])) * _dot(x, wu_ref[0])
        _store_token_rows(ys_ref, _dot(hid.astype(BF16), wd_ref[0]))

    @pl.when(t >= used_ref[0])
    def _():
        ys_ref[...] = jnp.zeros(ys_ref.shape, F32)


def _experts(te, used, xs, wg, wu, wd, n_tiles):
    blk = EXPERT_TILE * SUBLANES
    return pl.pallas_call(
        _experts_kernel,
        grid_spec=pltpu.PrefetchScalarGridSpec(
            num_scalar_prefetch=2,
            grid=(n_tiles,),
            in_specs=[pl.BlockSpec((blk, LANES), lambda t, te, used: (jnp.minimum(t, used[0] - 1), 0)),
                      pl.BlockSpec((1, D_MODEL, D_FF), lambda t, te, used: (te[t], 0, 0)),
                      pl.BlockSpec((1, D_MODEL, D_FF), lambda t, te, used: (te[t], 0, 0)),
                      pl.BlockSpec((1, D_FF, D_MODEL), lambda t, te, used: (te[t], 0, 0))],
            out_specs=pl.BlockSpec((blk, LANES), lambda t, te, used: (t, 0))),
        out_shape=jax.ShapeDtypeStruct(xs.shape, F32),
        compiler_params=pltpu.CompilerParams(dimension_semantics=("arbitrary",), vmem_limit_bytes=VMEM_LIMIT),
        name="moe_experts",
    )(te, used, xs, wg, wu, wd)


def _combine_kernel(pos_ref, t_ref, h1_ref, wts_ref, ys_hbm, wsg_ref, wsu_ref, wsd_ref, gout_ref, out_ref,
                    ybuf, sem, *, tokens):
    unroll = 8

    def issue(g, carry):
        for u in range(unroll):
            n = g * unroll + u
            for k in range(TOP_K):
                pltpu.make_async_copy(_row_slice(ys_hbm, pos_ref[k, n]), _row_slice(ybuf.at[k], n),
                                      sem).start(priority=(u + k) % 2)
        return carry

    lax.fori_loop(0, tokens // unroll, issue, 0)

    t = t_ref[...]
    hid = _silu(_dot(t, wsg_ref[...])) * _dot(t, wsu_ref[...])
    shared = _dot(hid.astype(BF16), wsd_ref[...])

    for k in range(TOP_K):
        pltpu.make_async_copy(_row_slice(ys_hbm, 0, tokens), ybuf.at[k], sem).wait()
    wts = wts_ref[...]
    routed = wts[:, 0:1] * _load_token_rows(ybuf.at[0], tokens) + wts[:, 1:2] * _load_token_rows(ybuf.at[1], tokens)
    out_ref[...] = h1_ref[...] + _rms(routed + shared, gout_ref[...])


def _combine(pos, t2, h1, wts, ys, shared_w, tokens):
    n = t2.shape[0]
    wsg, wsu, wsd, gout = shared_w
    const = lambda a: pl.BlockSpec(a.shape, lambda i: (0,) * a.ndim)
    tile = lambda w: pl.BlockSpec((tokens, w), lambda i: (i, 0))
    return pl.pallas_call(
        functools.partial(_combine_kernel, tokens=tokens),
        grid=(n // tokens,),
        in_specs=[pl.BlockSpec((TOP_K, tokens), lambda i: (0, i), memory_space=pltpu.SMEM),
                  tile(D_MODEL), tile(D_MODEL), tile(LANES), pl.BlockSpec(memory_space=pl.ANY),
                  const(wsg), const(wsu), const(wsd), const(gout)],
        out_specs=tile(D_MODEL),
        out_shape=jax.ShapeDtypeStruct((n, D_MODEL), F32),
        scratch_shapes=[pltpu.VMEM((TOP_K, tokens * SUBLANES, LANES), F32), pltpu.SemaphoreType.DMA],
        compiler_params=pltpu.CompilerParams(dimension_semantics=("arbitrary",), vmem_limit_bytes=VMEM_LIMIT),
        name="moe_combine",
    )(pos, t2, h1, wts, ys, wsg, wsu, wsd, gout)


def _swap_halves(w):
    half = w.shape[-1] // 2
    return jnp.concatenate([w[..., half:], w[..., :half]], axis=-1)


def _block_diag(blocks):
    h, r, c = blocks.shape
    eye = jnp.eye(h, dtype=blocks.dtype)
    return jnp.einsum('hrc,hg->hrgc', blocks, eye).reshape(h * r, h * c)


def kernel(x, positions, meta_tokens, pre_mix_norm, w_in, q_norm, w_uq, kv_norm, w_ukv, conv_w, attn_out_norm,
           conv_out_norm, w_out, post_mix_norm, pre_ffn_norm, w_group_router, b_group_router, w_expert_router,
           b_expert_router, w_gate, w_up, w_down, w_sh_gate, w_sh_up, w_sh_down, post_ffn_norm):
    bsz, seq, d = x.shape
    n = bsz * seq
    row = lambda v: v.reshape(1, -1).astype(F32)

    wi = w_in[0]
    o_kv, o_pe, o_b = Q_LORA, Q_LORA + KV_LORA, Q_LORA + KV_LORA + QK_ROPE
    k_pe = wi[:, o_pe:o_b]
    w1 = jnp.concatenate([wi[:, :o_pe], k_pe, _swap_halves(k_pe), jnp.zeros((d, LANES - 2 * QK_ROPE), F32),
                          wi[:, o_b:]], axis=1).astype(BF16)
    wq = w_uq[0].reshape(Q_LORA, N_HEADS, QK_NOPE + QK_ROPE)
    wqn = wq[:, :, :QK_NOPE].reshape(Q_LORA, N_HEADS * QK_NOPE).astype(BF16)
    wq_rope = wq[:, :, QK_NOPE:]
    wqr = jnp.concatenate([wq_rope, _swap_halves(wq_rope),
                           jnp.zeros((Q_LORA, N_HEADS, LANES - 2 * QK_ROPE), F32)], axis=2)
    wqr = wqr.reshape(Q_LORA, N_HEADS * LANES).astype(BF16)
    wkv = w_ukv[0].reshape(KV_LORA, N_HEADS, QK_NOPE + V_DIM)
    bduk = _block_diag(jnp.transpose(wkv[:, :, :QK_NOPE], (1, 2, 0))).astype(BF16)
    wuv = _block_diag(jnp.transpose(wkv[:, :, QK_NOPE:], (1, 0, 2))).astype(BF16)
    premix_w = (row(pre_mix_norm[0]), w1, row(q_norm[0]), row(kv_norm[0]), wqn, bduk, wqr,
                conv_w[0].astype(F32), row(conv_out_norm[0]))
    wr = jnp.concatenate([w_group_router[0], jnp.zeros((d, LANES - N_GROUPS), F32),
                          w_expert_router[0], jnp.zeros((d, LANES - N_EXPERTS), F32)], axis=1).astype(BF16)
    br = jnp.concatenate([b_group_router[0], jnp.zeros((LANES - N_GROUPS,), F32),
                          b_expert_router[0], jnp.zeros((LANES - N_EXPERTS,), F32)]).reshape(1, -1).astype(F32)
    postmix_w = (wuv, row(attn_out_norm[0]), w_out[0].astype(BF16), row(post_mix_norm[0]), row(pre_ffn_norm[0]),
                 wr, br)
    expert_w = (w_gate[0].astype(BF16), w_up[0].astype(BF16), w_down[0].astype(BF16))
    shared_w = (w_sh_gate[0].astype(BF16), w_sh_up[0].astype(BF16), w_sh_down[0].astype(BF16),
                row(post_ffn_norm[0]))

    pos_all = jnp.concatenate([positions.astype(jnp.int32).reshape(-1) + N_META,
                               jnp.zeros((PAD_FRONT,), jnp.int32), jnp.arange(N_META, dtype=jnp.int32)])
    inv_freq = 1.0 / (ROPE_THETA ** (jnp.arange(0, QK_ROPE, 2, dtype=F32) / QK_ROPE))
    n_tab = n + BLOCK
    tab = _rope_table(pos_all.reshape(1, n_tab), inv_freq.reshape(-1, 1), n_tab // 3)

    meta_blk = jnp.pad(meta_tokens.astype(F32), ((PAD_FRONT, 0), (0, 0)))[None]
    zero_carry = jnp.zeros((8, CONV_DIM), F32)
    _, kmeta, _, meta_tail = _premix(meta_blk, tab, n // BLOCK, zero_carry, premix_w, BLOCK)
    qp, kc, convn, _ = _premix(x, tab, 0, meta_tail, premix_w, 512)

    olat = _attention(qp, kc, kmeta[0], 512, 512, 8)

    h1, t, trow, oh1, oh2, wts = _postmix(x.reshape(n, d), olat.reshape(n, -1), convn.reshape(n, -1), postmix_w, 512)

    sorted_rows = TOP_K * n + N_EXPERTS * EXPERT_TILE
    n_tiles = sorted_rows // EXPERT_TILE
    pos2, te8, misc = _plan(oh1, oh2, n_tiles, 512)
    pos = pos2.reshape(TOP_K, n)
    xs = _dispatch(misc[0, :N_EXPERTS + 1], pos, trow, sorted_rows, 1024)
    ys = _experts(te8[0, :n_tiles], misc[0, N_EXPERTS:N_EXPERTS + 1], xs, *expert_w, n_tiles)
    out = _combine(pos, t, h1, wts, ys, shared_w, 512)
    return out.reshape(bsz, seq, d)
```

```python
import functools
import math

import jax
import jax.numpy as jnp
from jax import lax
from jax.experimental import pallas as pl
from jax.experimental.pallas import tpu as pltpu

D_MODEL = 1024
N_META = 16
BLOCK = 128
PAD_FRONT = BLOCK - N_META
N_HEADS = 8
QK_NOPE = 64
QK_ROPE = 32
V_DIM = 64
Q_LORA = 256
KV_LORA = 128
ROPE_THETA = 10000.0
ATTN_SCALE = (QK_NOPE + QK_ROPE) ** -0.5
Q_SCALE = ATTN_SCALE * math.log2(math.e)
ATTN_WIDTH = N_HEADS * V_DIM
CONV_DIM = 512
CONV_W = 3
N_GROUPS = 4
EXPERTS_PER_GROUP = 8
N_EXPERTS = N_GROUPS * EXPERTS_PER_GROUP
D_FF = 256
EPS = 1e-6
NEG_INF = -1e30

LANES = 128
SUBLANES = 8
TOP_K = 2
EXPERT_TILE = 512
KEY_WIDTH = 2 * LANES
IN_COLS_PADDED = Q_LORA + KV_LORA + LANES + 3 * CONV_DIM

F32 = jnp.float32
BF16 = jnp.bfloat16
VMEM_LIMIT = 56 * 1024 * 1024


def _rms(x, g):
    return x * lax.rsqrt(jnp.mean(x * x, axis=-1, keepdims=True) + EPS) * g


def _dot(a, b):
    return jnp.dot(a, b, preferred_element_type=F32)


def _dot_nt(a, b):
    return lax.dot_general(a, b, (((1,), (1,)), ((), ())), preferred_element_type=F32)


def _store_token_rows(ref, value):
    rows = value.shape[0]
    for c in range(D_MODEL // LANES):
        ref[pl.ds(c, rows, stride=SUBLANES), :] = value[:, c * LANES:(c + 1) * LANES]


def _load_token_rows(ref, rows):
    return jnp.concatenate([ref[pl.ds(c, rows, stride=SUBLANES), :] for c in range(D_MODEL // LANES)], axis=1)


def _rope_table_kernel(pos_ref, invf_ref, out_ref):
    pos = pos_ref[...].astype(F32)
    ang = invf_ref[...] * pos
    c = jnp.cos(ang)
    s = jnp.sin(ang)
    planes = jnp.concatenate([c, c, -s, s, jnp.zeros((LANES - 4 * (QK_ROPE // 2), ang.shape[1]), F32)], axis=0)
    out_ref[...] = planes.T


def _rope_table(pos_row, inv_freq_col, chunk):
    n = pos_row.shape[1]
    return pl.pallas_call(
        _rope_table_kernel,
        grid=(n // chunk,),
        in_specs=[pl.BlockSpec((1, chunk), lambda i: (0, i)),
                  pl.BlockSpec((QK_ROPE // 2, 1), lambda i: (0, 0))],
        out_specs=pl.BlockSpec((chunk, LANES), lambda i: (i, 0)),
        out_shape=jax.ShapeDtypeStruct((n, LANES), F32),
        name="rope_table",
    )(pos_row, inv_freq_col)


def _premix_kernel(x_ref, tab_ref, carry_in_ref, g_pre_ref, w1_ref, gq_ref, gkv_ref, wqn_ref, bduk_ref,
                   wqr_ref, convw_ref, gconv_ref, qp_ref, kc_ref, convn_ref, utail_ref, carry_ref):
    t = pl.program_id(1)
    rows = x_ref.shape[1]

    @pl.when(t == 0)
    def _():
        carry_ref[...] = carry_in_ref[...]

    x = x_ref[0]
    xn = _rms(x, g_pre_ref[...]).astype(BF16)
    z = _dot(xn, w1_ref[...])
    tab = tab_ref[...]
    lane = lax.broadcasted_iota(jnp.int32, (rows, LANES), 1)

    def rope(pair):
        prod = pair * tab
        return jnp.where(lane < QK_ROPE, prod + pltpu.roll(prod, LANES - QK_ROPE, axis=1), 0.0)

    ckvn = _rms(z[:, Q_LORA:Q_LORA + KV_LORA], gkv_ref[...])
    krope = rope(z[:, Q_LORA + KV_LORA:Q_LORA + KV_LORA + LANES])
    kc_ref[0] = jnp.concatenate([ckvn, krope], axis=1).astype(BF16)

    cqn = _rms(z[:, :Q_LORA], gq_ref[...]).astype(BF16)
    qn = _dot(cqn, wqn_ref[...]).astype(BF16)
    qabs = _dot(qn, bduk_ref[...])
    qr = _dot(cqn, wqr_ref[...])
    for h in range(N_HEADS):
        sl = slice(h * LANES, (h + 1) * LANES)
        qp_ref[0, h, :, 0:LANES] = (qabs[:, sl] * Q_SCALE).astype(BF16)
        qp_ref[0, h, :, LANES:KEY_WIDTH] = (rope(qr[:, sl]) * Q_SCALE).astype(BF16)

    c0 = Q_LORA + KV_LORA + LANES
    gate_b = z[:, c0:c0 + CONV_DIM]
    u = z[:, c0 + CONV_DIM:c0 + 2 * CONV_DIM] * z[:, c0 + 2 * CONV_DIM:c0 + 3 * CONV_DIM]
    prev = carry_ref[...]
    row8 = lax.broadcasted_iota(jnp.int32, (8, CONV_DIM), 0)
    r1 = pltpu.roll(u, 1, axis=0)
    r2 = pltpu.roll(u, 2, axis=0)
    p1 = pltpu.roll(prev, 1, axis=0)
    p2 = pltpu.roll(prev, 2, axis=0)
    u1 = jnp.concatenate([jnp.where(row8 < 1, p1, r1[0:8]), r1[8:]], axis=0)
    u2 = jnp.concatenate([jnp.where(row8 < 2, p2, r2[0:8]), r2[8:]], axis=0)
    cw = convw_ref[...]
    y = cw[0:1] * u2 + cw[1:2] * u1 + cw[2:3] * u
    convn_ref[0] = _rms(gate_b * y, gconv_ref[...]).astype(BF16)
    tail = u[rows - 8:rows]
    carry_ref[...] = tail
    utail_ref[...] = tail


def _premix(x3, tab, tab_block0, carry_in, wts, rows):
    bsz, seq, _ = x3.shape
    nt = seq // rows
    g_pre, w1, gq, gkv, wqn, bduk, wqr, convw, gconv = wts
    const = lambda a: pl.BlockSpec(a.shape, lambda b, t: (0,) * a.ndim)
    return pl.pallas_call(
        _premix_kernel,
        grid=(bsz, nt),
        in_specs=[pl.BlockSpec((1, rows, D_MODEL), lambda b, t: (b, t, 0)),
                  pl.BlockSpec((rows, LANES), lambda b, t: (tab_block0 + b * nt + t, 0)),
                  const(carry_in), const(g_pre), const(w1), const(gq), const(gkv), const(wqn), const(bduk),
                  const(wqr), const(convw), const(gconv)],
        out_specs=[pl.BlockSpec((1, N_HEADS, rows, KEY_WIDTH), lambda b, t: (b, 0, t, 0)),
                   pl.BlockSpec((1, rows, KEY_WIDTH), lambda b, t: (b, t, 0)),
                   pl.BlockSpec((1, rows, CONV_DIM), lambda b, t: (b, t, 0)),
                   pl.BlockSpec((8, CONV_DIM), lambda b, t: (0, 0))],
        out_shape=[jax.ShapeDtypeStruct((bsz, N_HEADS, seq, KEY_WIDTH), BF16),
                   jax.ShapeDtypeStruct((bsz, seq, KEY_WIDTH), BF16),
                   jax.ShapeDtypeStruct((bsz, seq, CONV_DIM), BF16),
                   jax.ShapeDtypeStruct((8, CONV_DIM), F32)],
        scratch_shapes=[pltpu.VMEM((8, CONV_DIM), F32)],
        compiler_params=pltpu.CompilerParams(dimension_semantics=("arbitrary", "arbitrary"),
                                             vmem_limit_bytes=VMEM_LIMIT),
        name="premix",
    )(x3, tab, carry_in, g_pre, w1, gq, gkv, wqn, bduk, wqr, convw, gconv)


def _attention_kernel(q_ref, qnext_ref, k_ref, km_ref, o_ref, s_scr, smeta_scr, tri_scr, m_scr, l_scr, acc_scr, *,
                      bq, bk, chunks):
    assert bq == bk
    b = pl.program_id(0)
    i = pl.program_id(1)
    hpc = N_HEADS // chunks
    c_rows = hpc * bq
    km = km_ref[...]

    def scores(qr, c, kb):
        qc = qr[0, c * hpc:(c + 1) * hpc].reshape(c_rows, KEY_WIDTH)
        return _dot_nt(qc, kb)

    def values(kb):
        return jnp.concatenate([kb[:, :KV_LORA], jnp.ones((kb.shape[0], LANES), BF16)], axis=1)

    def consume(c, s, v_ones, mask, first=False):
        if mask is not None:
            s = jnp.where(mask, s, NEG_INF)
        m_blk = jnp.broadcast_to(jnp.max(s, axis=1, keepdims=True), (c_rows, LANES))
        m_new = m_blk if first else jnp.maximum(m_scr[c], m_blk)
        p = jnp.exp2(s - jnp.concatenate([m_new] * (s.shape[1] // LANES), axis=1)).astype(BF16)
        pv = _dot(p, v_ones)
        if first:
            l_scr[c] = pv[:, KV_LORA:]
            acc_scr[c] = pv[:, :KV_LORA]
        else:
            alpha = jnp.exp2(m_scr[c] - m_new)
            l_scr[c] = alpha * l_scr[c] + pv[:, KV_LORA:]
            acc_scr[c] = alpha * acc_scr[c] + pv[:, :KV_LORA]
        m_scr[c] = m_new

    def key_block(j):
        return k_ref[0, pl.ds(pl.multiple_of(j * bk, bk), bk), :]

    @pl.when((b == 0) & (i == 0))
    def _():
        for c in range(chunks):
            smeta_scr[c] = scores(q_ref, c, km)
        rows_i = lax.broadcasted_iota(jnp.int32, (bq, bk), 0)
        cols_i = lax.broadcasted_iota(jnp.int32, (bq, bk), 1)
        tri_scr[...] = jnp.where(cols_i <= rows_i, 0.0, NEG_INF)

    n_full = (i * bq) // bk
    kb0 = key_block(0)
    v_meta = values(km)
    col = lax.broadcasted_iota(jnp.int32, (c_rows, BLOCK), 1)
    for c in range(chunks):
        s_scr[c] = scores(q_ref, c, kb0)
        consume(c, smeta_scr[c], v_meta, col >= PAD_FRONT, first=True)

    def body(j, carry):
        v_ones = values(key_block(j))
        kb_next = key_block(j + 1)
        for c in range(chunks):
            s = s_scr[c]
            s_scr[c] = scores(q_ref, c, kb_next)
            consume(c, s, v_ones, None)
        return carry

    lax.fori_loop(0, n_full, body, 0)

    v_ones = values(key_block(n_full))
    for c in range(chunks):
        s = s_scr[c] + jnp.concatenate([tri_scr[...]] * hpc, axis=0)
        smeta_scr[c] = scores(qnext_ref, c, km)
        consume(c, s, v_ones, None)

    for c in range(chunks):
        o = acc_scr[c] / l_scr[c]
        for hh in range(hpc):
            h = c * hpc + hh
            o_ref[0, :, h * KV_LORA:(h + 1) * KV_LORA] = o[hh * bq:(hh + 1) * bq].astype(BF16)


def _attention(qp, kc, kmeta, bq, bk, chunks):
    bsz, _, seq, _ = qp.shape
    nq = seq // bq
    c_rows = N_HEADS * bq // chunks
    scratch = [pltpu.VMEM((chunks, c_rows, bk), F32), pltpu.VMEM((chunks, c_rows, BLOCK), F32),
               pltpu.VMEM((bq, bk), F32), pltpu.VMEM((chunks, c_rows, LANES), F32),
               pltpu.VMEM((chunks, c_rows, LANES), F32), pltpu.VMEM((chunks, c_rows, KV_LORA), F32)]

    def next_q(b, i):
        wrap = (i + 1) // nq
        return (jnp.minimum(b + wrap, bsz - 1), 0, i + 1 - wrap * nq, 0)

    return pl.pallas_call(
        functools.partial(_attention_kernel, bq=bq, bk=bk, chunks=chunks),
        grid=(bsz, nq),
        in_specs=[pl.BlockSpec((1, N_HEADS, bq, KEY_WIDTH), lambda b, i: (b, 0, i, 0)),
                  pl.BlockSpec((1, N_HEADS, bq, KEY_WIDTH), next_q),
                  pl.BlockSpec((1, seq, KEY_WIDTH), lambda b, i: (b, 0, 0)),
                  pl.BlockSpec((BLOCK, KEY_WIDTH), lambda b, i: (0, 0))],
        out_specs=pl.BlockSpec((1, bq, N_HEADS * KV_LORA), lambda b, i: (b, i, 0)),
        out_shape=jax.ShapeDtypeStruct((bsz, seq, N_HEADS * KV_LORA), BF16),
        scratch_shapes=scratch,
        compiler_params=pltpu.CompilerParams(dimension_semantics=("arbitrary", "arbitrary"),
                                             vmem_limit_bytes=VMEM_LIMIT),
        name="attention",
    )(qp, qp, kc, kmeta)


def _postmix_kernel(x_ref, olat_ref, convn_ref, wuv_ref, gattn_ref, wout_ref, gpost_ref, gffn_ref, wr_ref, br_ref,
                    h1_ref, t_ref, trow_ref, oh1_ref, oh2_ref, wts_ref):
    rows = x_ref.shape[0]
    attn = _dot(olat_ref[...], wuv_ref[...])
    attn_n = _rms(attn, gattn_ref[...]).astype(BF16)
    wout = wout_ref[...]
    mix = _dot(attn_n, wout[:ATTN_WIDTH]) + _dot(convn_ref[...], wout[ATTN_WIDTH:])
    h1 = x_ref[...] + _rms(mix, gpost_ref[...])
    h1_ref[...] = h1
    tf = _rms(h1, gffn_ref[...])
    t = tf.astype(BF16)
    t_ref[...] = t
    _store_token_rows(trow_ref, tf)

    logits = _dot(t, wr_ref[...]) + br_ref[...]
    lane = lax.broadcasted_iota(jnp.int32, (rows, LANES), 1)

    def first_argmax(v):
        mx = jnp.max(v, axis=1, keepdims=True)
        return mx, jnp.min(jnp.where(v == mx, lane, LANES), axis=1, keepdims=True)

    gl = jnp.where(lane < N_GROUPS, logits[:, :LANES], NEG_INF)
    gmax, gidx = first_argmax(gl)
    g_w = 1.0 / jnp.sum(jnp.exp(gl - gmax), axis=1, keepdims=True)
    el = jnp.where((lane >> 3) == gidx, logits[:, LANES:], NEG_INF)
    m1, i1 = first_argmax(el)
    m2, i2 = first_argmax(jnp.where(lane == i1, NEG_INF, el))
    e2 = jnp.exp(m2 - m1)
    den = 1.0 + e2
    oh1_ref[...] = jnp.where(lane == i1, 1.0, 0.0).astype(BF16)
    oh2_ref[...] = jnp.where(lane == i2, 1.0, 0.0).astype(BF16)
    wts_ref[...] = jnp.where(lane == 0, g_w / den, jnp.where(lane == 1, g_w * e2 / den, 0.0))


def _postmix(x2, olat2, convn2, wts, rows):
    n = x2.shape[0]
    wuv, gattn, wout, gpost, gffn, wr, br = wts
    const = lambda a: pl.BlockSpec(a.shape, lambda i: (0,) * a.ndim)
    tile = lambda w: pl.BlockSpec((rows, w), lambda i: (i, 0))
    return pl.pallas_call(
        _postmix_kernel,
        grid=(n // rows,),
        in_specs=[tile(D_MODEL), tile(N_HEADS * KV_LORA), tile(CONV_DIM),
                  const(wuv), const(gattn), const(wout), const(gpost), const(gffn), const(wr), const(br)],
        out_specs=[tile(D_MODEL), tile(D_MODEL), pl.BlockSpec((rows * SUBLANES, LANES), lambda i: (i, 0)),
                   tile(LANES), tile(LANES), tile(LANES)],
        out_shape=[jax.ShapeDtypeStruct((n, D_MODEL), F32),
                   jax.ShapeDtypeStruct((n, D_MODEL), BF16),
                   jax.ShapeDtypeStruct((n * SUBLANES, LANES), F32),
                   jax.ShapeDtypeStruct((n, LANES), BF16),
                   jax.ShapeDtypeStruct((n, LANES), BF16),
                   jax.ShapeDtypeStruct((n, LANES), F32)],
        compiler_params=pltpu.CompilerParams(dimension_semantics=("arbitrary",), vmem_limit_bytes=VMEM_LIMIT),
        name="postmix",
    )(x2, olat2, convn2, wuv, gattn, wout, gpost, gffn, wr, br)


def _silu(a):
    return a * (1.0 / (1.0 + jnp.exp(-a)))


def _plan_kernel(oh1_ref, oh2_ref, pos_ref, te_ref, misc_ref, *, chunk):
    n_tok = oh1_ref.shape[0]
    n_chunks = n_tok // chunk
    lane = lax.broadcasted_iota(jnp.int32, (SUBLANES, LANES), 1)

    def rows_of(ref, c):
        return ref[pl.ds(pl.multiple_of(c * chunk, chunk), chunk), :]

    def count_body(c, acc):
        both = rows_of(oh1_ref, c).astype(F32) + rows_of(oh2_ref, c).astype(F32)
        return acc + jnp.sum(both, axis=0, keepdims=True)

    counts = lax.fori_loop(0, n_chunks, count_body, jnp.zeros((1, LANES), F32))
    counts = jnp.broadcast_to(counts, (SUBLANES, LANES)).astype(jnp.int32)
    tile_shift = EXPERT_TILE.bit_length() - 1
    padded = ((counts + (EXPERT_TILE - 1)) >> tile_shift) << tile_shift
    ends = padded
    shift = 1
    while shift < N_EXPERTS:
        ends = ends + jnp.where(lane >= shift, pltpu.roll(ends, shift, axis=1), 0)
        shift *= 2
    base = (ends - padded)[0:1].astype(F32)

    r_iota = lax.broadcasted_iota(jnp.int32, (chunk, chunk), 0)
    c_iota = lax.broadcasted_iota(jnp.int32, (chunk, chunk), 1)
    earlier = jnp.where(c_iota < r_iota, 1.0, 0.0).astype(BF16)

    def rank_pass(ref, slot, run):
        def body(c, run):
            oh = rows_of(ref, c)
            ohf = oh.astype(F32)
            before = _dot(earlier, oh) + run
            posv = jnp.sum(ohf * before, axis=1, keepdims=True)
            pos_row = jnp.broadcast_to(posv, (chunk, LANES)).T[0:1, :]
            pos_ref[pl.ds(slot * n_chunks + c, 1), :] = pos_row.astype(jnp.int32)
            return run + jnp.sum(ohf, axis=0, keepdims=True)
        return lax.fori_loop(0, n_chunks, body, run)

    rank_pass(oh2_ref, 1, rank_pass(oh1_ref, 0, base))

    n_tile_lanes = te_ref.shape[1]
    tile_start = lax.broadcasted_iota(jnp.int32, (SUBLANES, n_tile_lanes), 1) * EXPERT_TILE
    te = jnp.zeros((SUBLANES, n_tile_lanes), jnp.int32)
    for e in range(N_EXPERTS):
        te = te + jnp.where(ends[:, e:e + 1] <= tile_start, 1, 0)
    te_ref[...] = jnp.minimum(te, N_EXPERTS - 1)
    used_tiles = ends[:, N_EXPERTS - 1:N_EXPERTS] >> tile_shift
    misc_ref[...] = jnp.where(lane < N_EXPERTS, jnp.where(padded > 0, ends - EXPERT_TILE, -1), used_tiles)


def _plan(oh1, oh2, n_tiles, chunk):
    n = oh1.shape[0]
    n_tile_lanes = -(-n_tiles // LANES) * LANES
    whole = lambda shape: pl.BlockSpec(shape, lambda i: (0, 0))
    return pl.pallas_call(
        functools.partial(_plan_kernel, chunk=chunk),
        grid=(1,),
        in_specs=[whole((n, LANES)), whole((n, LANES))],
        out_specs=[whole((TOP_K * n // chunk, chunk)), whole((SUBLANES, n_tile_lanes)), whole((SUBLANES, LANES))],
        out_shape=[jax.ShapeDtypeStruct((TOP_K * n // chunk, chunk), jnp.int32),
                   jax.ShapeDtypeStruct((SUBLANES, n_tile_lanes), jnp.int32),
                   jax.ShapeDtypeStruct((SUBLANES, LANES), jnp.int32)],
        compiler_params=pltpu.CompilerParams(dimension_semantics=("arbitrary",), vmem_limit_bytes=VMEM_LIMIT),
        name="moe_plan",
    )(oh1, oh2)


def _row_slice(ref, row, n_rows=1):
    start = row * SUBLANES if isinstance(row, int) else pl.multiple_of(row * SUBLANES, SUBLANES)
    return ref.at[pl.ds(start, n_rows * SUBLANES)]


def _dispatch_kernel(zstart_ref, pos_ref, trow_ref, xs_hbm, zero_buf, zero_sem, row_sem, *, tokens):
    i = pl.program_id(0)

    @pl.when(i == 0)
    def _():
        zero_buf[...] = jnp.zeros(zero_buf.shape, F32)
        n_tiles = xs_hbm.shape[0] // (EXPERT_TILE * SUBLANES)
        used = zstart_ref[N_EXPERTS]

        def zero_tile(first_row):
            return pltpu.make_async_copy(zero_buf, _row_slice(xs_hbm, first_row, EXPERT_TILE), zero_sem)

        for phase in ("start", "wait"):
            for e in range(N_EXPERTS):
                @pl.when(zstart_ref[e] >= 0)
                def _():
                    copy = zero_tile(zstart_ref[e])
                    copy.start() if phase == "start" else copy.wait()

            def tail(t, carry):
                copy = zero_tile(t * EXPERT_TILE)
                copy.start() if phase == "start" else copy.wait()
                return carry

            lax.fori_loop(used, n_tiles, tail, 0)

    unroll = 8

    def issue(g, carry):
        for u in range(unroll):
            n = g * unroll + u
            src = _row_slice(trow_ref, n)
            for k in range(TOP_K):
                pltpu.make_async_copy(src, _row_slice(xs_hbm, pos_ref[k, n]), row_sem).start(priority=(u + k) % 2)
        return carry

    lax.fori_loop(0, tokens // unroll, issue, 0)
    for k in range(TOP_K):
        pltpu.make_async_copy(trow_ref, _row_slice(xs_hbm, 0, tokens), row_sem).wait()


def _dispatch(zstart, pos, trow, sorted_rows, tokens):
    n = pos.shape[1]
    return pl.pallas_call(
        functools.partial(_dispatch_kernel, tokens=tokens),
        grid=(n // tokens,),
        in_specs=[pl.BlockSpec(memory_space=pltpu.SMEM),
                  pl.BlockSpec((TOP_K, tokens), lambda i: (0, i), memory_space=pltpu.SMEM),
                  pl.BlockSpec((tokens * SUBLANES, LANES), lambda i: (i, 0))],
        out_specs=pl.BlockSpec(memory_space=pl.ANY),
        out_shape=jax.ShapeDtypeStruct((sorted_rows * SUBLANES, LANES), F32),
        scratch_shapes=[pltpu.VMEM((EXPERT_TILE * SUBLANES, LANES), F32), pltpu.SemaphoreType.DMA,
                        pltpu.SemaphoreType.DMA],
        compiler_params=pltpu.CompilerParams(dimension_semantics=("arbitrary",)),
        name="moe_dispatch",
    )(zstart, pos, trow)


def _experts_kernel(te_ref, used_ref, xs_ref, wg_ref, wu_ref, wd_ref, ys_ref):
    t = pl.program_id(0)

    @pl.when(t < used_ref[0])
    def _():
        x = _load_token_rows(xs_ref, EXPERT_TILE).astype(BF16)
        hid = _silu(_dot(x, wg_ref[0])) * _dot(x, wu_ref[0])
        _store_token_rows(ys_ref, _dot(hid.astype(BF16), wd_ref[0]))

    @pl.when(t >= used_ref[0])
    def _():
        ys_ref[...] = jnp.zeros(ys_ref.shape, F32)


def _experts(te, used, xs, wg, wu, wd, n_tiles):
    blk = EXPERT_TILE * SUBLANES
    return pl.pallas_call(
        _experts_kernel,
        grid_spec=pltpu.PrefetchScalarGridSpec(
            num_scalar_prefetch=2,
            grid=(n_tiles,),
            in_specs=[pl.BlockSpec((blk, LANES), lambda t, te, used: (jnp.minimum(t, used[0] - 1), 0)),
                      pl.BlockSpec((1, D_MODEL, D_FF), lambda t, te, used: (te[t], 0, 0)),
                      pl.BlockSpec((1, D_MODEL, D_FF), lambda t, te, used: (te[t], 0, 0)),
                      pl.BlockSpec((1, D_FF, D_MODEL), lambda t, te, used: (te[t], 0, 0))],
            out_specs=pl.BlockSpec((blk, LANES), lambda t, te, used: (t, 0))),
        out_shape=jax.ShapeDtypeStruct(xs.shape, F32),
        compiler_params=pltpu.CompilerParams(dimension_semantics=("arbitrary",), vmem_limit_bytes=VMEM_LIMIT),
        name="moe_experts",
    )(te, used, xs, wg, wu, wd)


def _combine_kernel(pos_ref, t_ref, h1_ref, wts_ref, ys_hbm, wsg_ref, wsu_ref, wsd_ref, gout_ref, out_ref,
                    ybuf, sem, *, tokens):
    unroll = 8

    def issue(g, carry):
        for u in range(unroll):
            n = g * unroll + u
            for k in range(TOP_K):
                pltpu.make_async_copy(_row_slice(ys_hbm, pos_ref[k, n]), _row_slice(ybuf.at[k], n),
                                      sem).start(priority=(u + k) % 2)
        return carry

    lax.fori_loop(0, tokens // unroll, issue, 0)

    t = t_ref[...]
    hid = _silu(_dot(t, wsg_ref[...])) * _dot(t, wsu_ref[...])
    shared = _dot(hid.astype(BF16), wsd_ref[...])

    for k in range(TOP_K):
        pltpu.make_async_copy(_row_slice(ys_hbm, 0, tokens), ybuf.at[k], sem).wait()
    wts = wts_ref[...]
    routed = wts[:, 0:1] * _load_token_rows(ybuf.at[0], tokens) + wts[:, 1:2] * _load_token_rows(ybuf.at[1], tokens)
    out_ref[...] = h1_ref[...] + _rms(routed + shared, gout_ref[...])


def _combine(pos, t2, h1, wts, ys, shared_w, tokens):
    n = t2.shape[0]
    wsg, wsu, wsd, gout = shared_w
    const = lambda a: pl.BlockSpec(a.shape, lambda i: (0,) * a.ndim)
    tile = lambda w: pl.BlockSpec((tokens, w), lambda i: (i, 0))
    return pl.pallas_call(
        functools.partial(_combine_kernel, tokens=tokens),
        grid=(n // tokens,),
        in_specs=[pl.BlockSpec((TOP_K, tokens), lambda i: (0, i), memory_space=pltpu.SMEM),
                  tile(D_MODEL), tile(D_MODEL), tile(LANES), pl.BlockSpec(memory_space=pl.ANY),
                  const(wsg), const(wsu), const(wsd), const(gout)],
        out_specs=tile(D_MODEL),
        out_shape=jax.ShapeDtypeStruct((n, D_MODEL), F32),
        scratch_shapes=[pltpu.VMEM((TOP_K, tokens * SUBLANES, LANES), F32), pltpu.SemaphoreType.DMA],
        compiler_params=pltpu.CompilerParams(dimension_semantics=("arbitrary",), vmem_limit_bytes=VMEM_LIMIT),
        name="moe_combine",
    )(pos, t2, h1, wts, ys, wsg, wsu, wsd, gout)


def _swap_halves(w):
    half = w.shape[-1] // 2
    return jnp.concatenate([w[..., half:], w[..., :half]], axis=-1)


def _block_diag(blocks):
    h, r, c = blocks.shape
    eye = jnp.eye(h, dtype=blocks.dtype)
    return jnp.einsum('hrc,hg->hrgc', blocks, eye).reshape(h * r, h * c)


def kernel(x, positions, meta_tokens, pre_mix_norm, w_in, q_norm, w_uq, kv_norm, w_ukv, conv_w, attn_out_norm,
           conv_out_norm, w_out, post_mix_norm, pre_ffn_norm, w_group_router, b_group_router, w_expert_router,
           b_expert_router, w_gate, w_up, w_down, w_sh_gate, w_sh_up, w_sh_down, post_ffn_norm):
    bsz, seq, d = x.shape
    n = bsz * seq
    row = lambda v: v.reshape(1, -1).astype(F32)

    wi = w_in[0]
    o_kv, o_pe, o_b = Q_LORA, Q_LORA + KV_LORA, Q_LORA + KV_LORA + QK_ROPE
    k_pe = wi[:, o_pe:o_b]
    w1 = jnp.concatenate([wi[:, :o_pe], k_pe, _swap_halves(k_pe), jnp.zeros((d, LANES - 2 * QK_ROPE), F32),
                          wi[:, o_b:]], axis=1).astype(BF16)
    wq = w_uq[0].reshape(Q_LORA, N_HEADS, QK_NOPE + QK_ROPE)
    wqn = wq[:, :, :QK_NOPE].reshape(Q_LORA, N_HEADS * QK_NOPE).astype(BF16)
    wq_rope = wq[:, :, QK_NOPE:]
    wqr = jnp.concatenate([wq_rope, _swap_halves(wq_rope),
                           jnp.zeros((Q_LORA, N_HEADS, LANES - 2 * QK_ROPE), F32)], axis=2)
    wqr = wqr.reshape(Q_LORA, N_HEADS * LANES).astype(BF16)
    wkv = w_ukv[0].reshape(KV_LORA, N_HEADS, QK_NOPE + V_DIM)
    bduk = _block_diag(jnp.transpose(wkv[:, :, :QK_NOPE], (1, 2, 0))).astype(BF16)
    wuv = _block_diag(jnp.transpose(wkv[:, :, QK_NOPE:], (1, 0, 2))).astype(BF16)
    premix_w = (row(pre_mix_norm[0]), w1, row(q_norm[0]), row(kv_norm[0]), wqn, bduk, wqr,
                conv_w[0].astype(F32), row(conv_out_norm[0]))
    wr = jnp.concatenate([w_group_router[0], jnp.zeros((d, LANES - N_GROUPS), F32),
                          w_expert_router[0], jnp.zeros((d, LANES - N_EXPERTS), F32)], axis=1).astype(BF16)
    br = jnp.concatenate([b_group_router[0], jnp.zeros((LANES - N_GROUPS,), F32),
                          b_expert_router[0], jnp.zeros((LANES - N_EXPERTS,), F32)]).reshape(1, -1).astype(F32)
    postmix_w = (wuv, row(attn_out_norm[0]), w_out[0].astype(BF16), row(post_mix_norm[0]), row(pre_ffn_norm[0]),
                 wr, br)
    expert_w = (w_gate[0].astype(BF16), w_up[0].astype(BF16), w_down[0].astype(BF16))
    shared_w = (w_sh_gate[0].astype(BF16), w_sh_up[0].astype(BF16), w_sh_down[0].astype(BF16),
                row(post_ffn_norm[0]))

    pos_all = jnp.concatenate([positions.astype(jnp.int32).reshape(-1) + N_META,
                               jnp.zeros((PAD_FRONT,), jnp.int32), jnp.arange(N_META, dtype=jnp.int32)])
    inv_freq = 1.0 / (ROPE_THETA ** (jnp.arange(0, QK_ROPE, 2, dtype=F32) / QK_ROPE))
    n_tab = n + BLOCK
    tab = _rope_table(pos_all.reshape(1, n_tab), inv_freq.reshape(-1, 1), n_tab // 3)

    meta_blk = jnp.pad(meta_tokens.astype(F32), ((PAD_FRONT, 0), (0, 0)))[None]
    zero_carry = jnp.zeros((8, CONV_DIM), F32)
    _, kmeta, _, meta_tail = _premix(meta_blk, tab, n // BLOCK, zero_carry, premix_w, BLOCK)
    qp, kc, convn, _ = _premix(x, tab, 0, meta_tail, premix_w, 512)

    olat = _attention(qp, kc, kmeta[0], 512, 512, 8)

    h1, t, trow, oh1, oh2, wts = _postmix(x.reshape(n, d), olat.reshape(n, -1), convn.reshape(n, -1), postmix_w, 512)

    sorted_rows = TOP_K * n + N_EXPERTS * EXPERT_TILE
    n_tiles = sorted_rows // EXPERT_TILE
    pos2, te8, misc = _plan(oh1, oh2, n_tiles, 512)
    pos = pos2.reshape(TOP_K, n)
    xs = _dispatch(misc[0, :N_EXPERTS + 1], pos, trow, sorted_rows, 1024)
    ys = _experts(te8[0, :n_tiles], misc[0, N_EXPERTS:N_EXPERTS + 1], xs, *expert_w, n_tiles)
    out = _combine(pos, t, h1, wts, ys, shared_w, 512)
    return out.reshape(bsz, seq, d)
```

```python
import functools
import math

import jax
import jax.numpy as jnp
from jax import lax
from jax.experimental import pallas as pl
from jax.experimental.pallas import tpu as pltpu

D_MODEL = 1024
N_META = 16
BLOCK = 128
PAD_FRONT = BLOCK - N_META
N_HEADS = 8
QK_NOPE = 64
QK_ROPE = 32
V_DIM = 64
Q_LORA = 256
KV_LORA = 128
ROPE_THETA = 10000.0
ATTN_SCALE = (QK_NOPE + QK_ROPE) ** -0.5
Q_SCALE = ATTN_SCALE * math.log2(math.e)
ATTN_WIDTH = N_HEADS * V_DIM
CONV_DIM = 512
CONV_W = 3
N_GROUPS = 4
EXPERTS_PER_GROUP = 8
N_EXPERTS = N_GROUPS * EXPERTS_PER_GROUP
D_FF = 256
EPS = 1e-6
NEG_INF = -1e30

LANES = 128
SUBLANES = 8
TOP_K = 2
EXPERT_TILE = 512
KEY_WIDTH = 2 * LANES
IN_COLS_PADDED = Q_LORA + KV_LORA + LANES + 3 * CONV_DIM

F32 = jnp.float32
BF16 = jnp.bfloat16
VMEM_LIMIT = 56 * 1024 * 1024


def _rms(x, g):
    return x * lax.rsqrt(jnp.mean(x * x, axis=-1, keepdims=True) + EPS) * g


def _dot(a, b):
    return jnp.dot(a, b, preferred_element_type=F32)


def _dot_nt(a, b):
    return lax.dot_general(a, b, (((1,), (1,)), ((), ())), preferred_element_type=F32)


def _store_token_rows(ref, value):
    rows = value.shape[0]
    for c in range(D_MODEL // LANES):
        ref[pl.ds(c, rows, stride=SUBLANES), :] = value[:, c * LANES:(c + 1) * LANES]


def _load_token_rows(ref, rows):
    return jnp.concatenate([ref[pl.ds(c, rows, stride=SUBLANES), :] for c in range(D_MODEL // LANES)], axis=1)


def _rope_table_kernel(pos_ref, invf_ref, out_ref):
    pos = pos_ref[...].astype(F32)
    ang = invf_ref[...] * pos
    c = jnp.cos(ang)
    s = jnp.sin(ang)
    planes = jnp.concatenate([c, c, -s, s, jnp.zeros((LANES - 4 * (QK_ROPE // 2), ang.shape[1]), F32)], axis=0)
    out_ref[...] = planes.T


def _rope_table(pos_row, inv_freq_col, chunk):
    n = pos_row.shape[1]
    return pl.pallas_call(
        _rope_table_kernel,
        grid=(n // chunk,),
        in_specs=[pl.BlockSpec((1, chunk), lambda i: (0, i)),
                  pl.BlockSpec((QK_ROPE // 2, 1), lambda i: (0, 0))],
        out_specs=pl.BlockSpec((chunk, LANES), lambda i: (i, 0)),
        out_shape=jax.ShapeDtypeStruct((n, LANES), F32),
        name="rope_table",
    )(pos_row, inv_freq_col)


def _premix_kernel(x_ref, tab_ref, carry_in_ref, g_pre_ref, w1_ref, gq_ref, gkv_ref, wqn_ref, bduk_ref,
                   wqr_ref, convw_ref, gconv_ref, qp_ref, kc_ref, convn_ref, utail_ref, carry_ref, wq_scr):
    t = pl.program_id(1)
    rows = x_ref.shape[1]

    @pl.when(t == 0)
    def _():
        carry_ref[...] = carry_in_ref[...]

    @pl.when((pl.program_id(0) == 0) & (t == 0))
    def _():
        wabs = jnp.dot(wqn_ref[...], bduk_ref[...], preferred_element_type=F32,
                       precision=lax.Precision.HIGHEST) * Q_SCALE
        wrope = wqr_ref[...] * Q_SCALE
        for h in range(N_HEADS):
            sl = slice(h * LANES, (h + 1) * LANES)
            wq_scr[:, h * KEY_WIDTH:h * KEY_WIDTH + LANES] = wabs[:, sl].astype(BF16)
            wq_scr[:, h * KEY_WIDTH + LANES:(h + 1) * KEY_WIDTH] = wrope[:, sl].astype(BF16)

    x = x_ref[0]
    xn = _rms(x, g_pre_ref[...]).astype(BF16)
    z = _dot(xn, w1_ref[...])
    tab = tab_ref[...]
    lane = lax.broadcasted_iota(jnp.int32, (rows, LANES), 1)

    def rope(pair):
        prod = pair * tab
        return jnp.where(lane < QK_ROPE, prod + pltpu.roll(prod, LANES - QK_ROPE, axis=1), 0.0)

    ckvn = _rms(z[:, Q_LORA:Q_LORA + KV_LORA], gkv_ref[...])
    krope = rope(z[:, Q_LORA + KV_LORA:Q_LORA + KV_LORA + LANES])
    kc_ref[0] = jnp.concatenate([ckvn, krope], axis=1).astype(BF16)

    cqn = _rms(z[:, :Q_LORA], gq_ref[...]).astype(BF16)
    qall = _dot(cqn, wq_scr[...])
    for h in range(N_HEADS):
        qp_ref[0, h, :, 0:LANES] = qall[:, h * KEY_WIDTH:h * KEY_WIDTH + LANES].astype(BF16)
        qp_ref[0, h, :, LANES:KEY_WIDTH] = rope(qall[:, h * KEY_WIDTH + LANES:(h + 1) * KEY_WIDTH]).astype(BF16)

    c0 = Q_LORA + KV_LORA + LANES
    gate_b = z[:, c0:c0 + CONV_DIM]
    u = z[:, c0 + CONV_DIM:c0 + 2 * CONV_DIM] * z[:, c0 + 2 * CONV_DIM:c0 + 3 * CONV_DIM]
    prev = carry_ref[...]
    row8 = lax.broadcasted_iota(jnp.int32, (8, CONV_DIM), 0)
    r1 = pltpu.roll(u, 1, axis=0)
    r2 = pltpu.roll(u, 2, axis=0)
    p1 = pltpu.roll(prev, 1, axis=0)
    p2 = pltpu.roll(prev, 2, axis=0)
    u1 = jnp.concatenate([jnp.where(row8 < 1, p1, r1[0:8]), r1[8:]], axis=0)
    u2 = jnp.concatenate([jnp.where(row8 < 2, p2, r2[0:8]), r2[8:]], axis=0)
    cw = convw_ref[...]
    y = cw[0:1] * u2 + cw[1:2] * u1 + cw[2:3] * u
    convn_ref[0] = _rms(gate_b * y, gconv_ref[...]).astype(BF16)
    tail = u[rows - 8:rows]
    carry_ref[...] = tail
    utail_ref[...] = tail


def _premix(x3, tab, tab_block0, carry_in, wts, rows):
    bsz, seq, _ = x3.shape
    nt = seq // rows
    g_pre, w1, gq, gkv, wqn, bduk, wqr, convw, gconv = wts
    const = lambda a: pl.BlockSpec(a.shape, lambda b, t: (0,) * a.ndim)
    return pl.pallas_call(
        _premix_kernel,
        grid=(bsz, nt),
        in_specs=[pl.BlockSpec((1, rows, D_MODEL), lambda b, t: (b, t, 0)),
                  pl.BlockSpec((rows, LANES), lambda b, t: (tab_block0 + b * nt + t, 0)),
                  const(carry_in), const(g_pre), const(w1), const(gq), const(gkv), const(wqn), const(bduk),
                  const(wqr), const(convw), const(gconv)],
        out_specs=[pl.BlockSpec((1, N_HEADS, rows, KEY_WIDTH), lambda b, t: (b, 0, t, 0)),
                   pl.BlockSpec((1, rows, KEY_WIDTH), lambda b, t: (b, t, 0)),
                   pl.BlockSpec((1, rows, CONV_DIM), lambda b, t: (b, t, 0)),
                   pl.BlockSpec((8, CONV_DIM), lambda b, t: (0, 0))],
        out_shape=[jax.ShapeDtypeStruct((bsz, N_HEADS, seq, KEY_WIDTH), BF16),
                   jax.ShapeDtypeStruct((bsz, seq, KEY_WIDTH), BF16),
                   jax.ShapeDtypeStruct((bsz, seq, CONV_DIM), BF16),
                   jax.ShapeDtypeStruct((8, CONV_DIM), F32)],
        scratch_shapes=[pltpu.VMEM((8, CONV_DIM), F32), pltpu.VMEM((Q_LORA, N_HEADS * KEY_WIDTH), BF16)],
        compiler_params=pltpu.CompilerParams(dimension_semantics=("arbitrary", "arbitrary"),
                                             vmem_limit_bytes=VMEM_LIMIT),
        name="premix",
    )(x3, tab, carry_in, g_pre, w1, gq, gkv, wqn, bduk, wqr, convw, gconv)


def _attention_kernel(q_ref, qnext_ref, k_ref, km_ref, o_ref, s_scr, smeta_scr, tri_scr, m_scr, l_scr, acc_scr, *,
                      bq, bk, chunks):
    assert bq == bk
    b = pl.program_id(0)
    i = pl.program_id(1)
    hpc = N_HEADS // chunks
    c_rows = hpc * bq
    km = km_ref[...]

    def scores(qr, c, kb):
        qc = qr[0, c * hpc:(c + 1) * hpc].reshape(c_rows, KEY_WIDTH)
        return _dot_nt(qc, kb)

    def values(kb):
        return jnp.concatenate([kb[:, :KV_LORA], jnp.ones((kb.shape[0], LANES), BF16)], axis=1)

    def consume(c, s, v_ones, mask, first=False):
        if mask is not None:
            s = jnp.where(mask, s, NEG_INF)
        m_blk = jnp.broadcast_to(jnp.max(s, axis=1, keepdims=True), (c_rows, LANES))
        m_new = m_blk if first else jnp.maximum(m_scr[c], m_blk)
        p = jnp.exp2(s - jnp.concatenate([m_new] * (s.shape[1] // LANES), axis=1)).astype(BF16)
        pv = _dot(p, v_ones)
        if first:
            l_scr[c] = pv[:, KV_LORA:]
            acc_scr[c] = pv[:, :KV_LORA]
        else:
            alpha = jnp.exp2(m_scr[c] - m_new)
            l_scr[c] = alpha * l_scr[c] + pv[:, KV_LORA:]
            acc_scr[c] = alpha * acc_scr[c] + pv[:, :KV_LORA]
        m_scr[c] = m_new

    def key_block(j):
        return k_ref[0, pl.ds(pl.multiple_of(j * bk, bk), bk), :]

    @pl.when((b == 0) & (i == 0))
    def _():
        for c in range(chunks):
            smeta_scr[c] = scores(q_ref, c, km)
        rows_i = lax.broadcasted_iota(jnp.int32, (bq, bk), 0)
        cols_i = lax.broadcasted_iota(jnp.int32, (bq, bk), 1)
        tri_scr[...] = jnp.where(cols_i <= rows_i, 0.0, NEG_INF)

    n_full = (i * bq) // bk
    kb0 = key_block(0)
    v_meta = values(km)
    col = lax.broadcasted_iota(jnp.int32, (c_rows, BLOCK), 1)
    for c in range(chunks):
        s_scr[c] = scores(q_ref, c, kb0)
        consume(c, smeta_scr[c], v_meta, col >= PAD_FRONT, first=True)

    def body(j, carry):
        v_ones = values(key_block(j))
        kb_next = key_block(j + 1)
        for c in range(chunks):
            s = s_scr[c]
            s_scr[c] = scores(q_ref, c, kb_next)
            consume(c, s, v_ones, None)
        return carry

    lax.fori_loop(0, n_full, body, 0)

    v_ones = values(key_block(n_full))
    for c in range(chunks):
        s = s_scr[c] + jnp.concatenate([tri_scr[...]] * hpc, axis=0)
        smeta_scr[c] = scores(qnext_ref, c, km)
        consume(c, s, v_ones, None)

    for c in range(chunks):
        o = acc_scr[c] / l_scr[c]
        for hh in range(hpc):
            h = c * hpc + hh
            o_ref[0, :, h * KV_LORA:(h + 1) * KV_LORA] = o[hh * bq:(hh + 1) * bq].astype(BF16)


def _attention(qp, kc, kmeta, bq, bk, chunks):
    bsz, _, seq, _ = qp.shape
    nq = seq // bq
    c_rows = N_HEADS * bq // chunks
    scratch = [pltpu.VMEM((chunks, c_rows, bk), F32), pltpu.VMEM((chunks, c_rows, BLOCK), F32),
               pltpu.VMEM((bq, bk), F32), pltpu.VMEM((chunks, c_rows, LANES), F32),
               pltpu.VMEM((chunks, c_rows, LANES), F32), pltpu.VMEM((chunks, c_rows, KV_LORA), F32)]

    def next_q(b, i):
        wrap = (i + 1) // nq
        return (jnp.minimum(b + wrap, bsz - 1), 0, i + 1 - wrap * nq, 0)

    return pl.pallas_call(
        functools.partial(_attention_kernel, bq=bq, bk=bk, chunks=chunks),
        grid=(bsz, nq),
        in_specs=[pl.BlockSpec((1, N_HEADS, bq, KEY_WIDTH), lambda b, i: (b, 0, i, 0)),
                  pl.BlockSpec((1, N_HEADS, bq, KEY_WIDTH), next_q),
                  pl.BlockSpec((1, seq, KEY_WIDTH), lambda b, i: (b, 0, 0)),
                  pl.BlockSpec((BLOCK, KEY_WIDTH), lambda b, i: (0, 0))],
        out_specs=pl.BlockSpec((1, bq, N_HEADS * KV_LORA), lambda b, i: (b, i, 0)),
        out_shape=jax.ShapeDtypeStruct((bsz, seq, N_HEADS * KV_LORA), BF16),
        scratch_shapes=scratch,
        compiler_params=pltpu.CompilerParams(dimension_semantics=("arbitrary", "arbitrary"),
                                             vmem_limit_bytes=VMEM_LIMIT),
        name="attention",
    )(qp, qp, kc, kmeta)


def _postmix_kernel(x_ref, olat_ref, convn_ref, wuv_ref, gattn_ref, wout_ref, gpost_ref, gffn_ref, wr_ref, br_ref,
                    h1_ref, t_ref, trow_ref, oh1_ref, oh2_ref, wts_ref):
    rows = x_ref.shape[0]
    attn = _dot(olat_ref[...], wuv_ref[...])
    attn_n = _rms(attn, gattn_ref[...]).astype(BF16)
    wout = wout_ref[...]
    mix = _dot(attn_n, wout[:ATTN_WIDTH]) + _dot(convn_ref[...], wout[ATTN_WIDTH:])
    h1 = x_ref[...] + _rms(mix, gpost_ref[...])
    h1_ref[...] = h1
    tf = _rms(h1, gffn_ref[...])
    t = tf.astype(BF16)
    t_ref[...] = t
    _store_token_rows(trow_ref, tf)

    logits = _dot(t, wr_ref[...]) + br_ref[...]
    lane = lax.broadcasted_iota(jnp.int32, (rows, LANES), 1)

    def first_argmax(v):
        mx = jnp.max(v, axis=1, keepdims=True)
        return mx, jnp.min(jnp.where(v == mx, lane, LANES), axis=1, keepdims=True)

    gl = jnp.where(lane < N_GROUPS, logits[:, :LANES], NEG_INF)
    gmax, gidx = first_argmax(gl)
    g_w = 1.0 / jnp.sum(jnp.exp(gl - gmax), axis=1, keepdims=True)
    el = jnp.where((lane >> 3) == gidx, logits[:, LANES:], NEG_INF)
    m1, i1 = first_argmax(el)
    m2, i2 = first_argmax(jnp.where(lane == i1, NEG_INF, el))
    e2 = jnp.exp(m2 - m1)
    den = 1.0 + e2
    oh1_ref[...] = jnp.where(lane == i1, 1.0, 0.0).astype(BF16)
    oh2_ref[...] = jnp.where(lane == i2, 1.0, 0.0).astype(BF16)
    wts_ref[...] = jnp.where(lane == 0, g_w / den, jnp.where(lane == 1, g_w * e2 / den, 0.0))


def _postmix(x2, olat2, convn2, wts, rows):
    n = x2.shape[0]
    wuv, gattn, wout, gpost, gffn, wr, br = wts
    const = lambda a: pl.BlockSpec(a.shape, lambda i: (0,) * a.ndim)
    tile = lambda w: pl.BlockSpec((rows, w), lambda i: (i, 0))
    return pl.pallas_call(
        _postmix_kernel,
        grid=(n // rows,),
        in_specs=[tile(D_MODEL), tile(N_HEADS * KV_LORA), tile(CONV_DIM),
                  const(wuv), const(gattn), const(wout), const(gpost), const(gffn), const(wr), const(br)],
        out_specs=[tile(D_MODEL), tile(D_MODEL), pl.BlockSpec((rows * SUBLANES, LANES), lambda i: (i, 0)),
                   tile(LANES), tile(LANES), tile(LANES)],
        out_shape=[jax.ShapeDtypeStruct((n, D_MODEL), F32),
                   jax.ShapeDtypeStruct((n, D_MODEL), BF16),
                   jax.ShapeDtypeStruct((n * SUBLANES, LANES), F32),
                   jax.ShapeDtypeStruct((n, LANES), BF16),
                   jax.ShapeDtypeStruct((n, LANES), BF16),
                   jax.ShapeDtypeStruct((n, LANES), F32)],
        compiler_params=pltpu.CompilerParams(dimension_semantics=("arbitrary",), vmem_limit_bytes=VMEM_LIMIT),
        name="postmix",
    )(x2, olat2, convn2, wuv, gattn, wout, gpost, gffn, wr, br)


def _silu(a):
    return a * (1.0 / (1.0 + jnp.exp(-a)))


def _plan_kernel(oh1_ref, oh2_ref, pos_ref, te_ref, misc_ref, *, chunk):
    n_tok = oh1_ref.shape[0]
    n_chunks = n_tok // chunk
    lane = lax.broadcasted_iota(jnp.int32, (SUBLANES, LANES), 1)

    def rows_of(ref, c):
        return ref[pl.ds(pl.multiple_of(c * chunk, chunk), chunk), :]

    def count_body(c, acc):
        both = rows_of(oh1_ref, c).astype(F32) + rows_of(oh2_ref, c).astype(F32)
        return acc + jnp.sum(both, axis=0, keepdims=True)

    counts = lax.fori_loop(0, n_chunks, count_body, jnp.zeros((1, LANES), F32))
    counts = jnp.broadcast_to(counts, (SUBLANES, LANES)).astype(jnp.int32)
    tile_shift = EXPERT_TILE.bit_length() - 1
    padded = ((counts + (EXPERT_TILE - 1)) >> tile_shift) << tile_shift
    ends = padded
    shift = 1
    while shift < N_EXPERTS:
        ends = ends + jnp.where(lane >= shift, pltpu.roll(ends, shift, axis=1), 0)
        shift *= 2
    base = (ends - padded)[0:1].astype(F32)

    r_iota = lax.broadcasted_iota(jnp.int32, (chunk, chunk), 0)
    c_iota = lax.broadcasted_iota(jnp.int32, (chunk, chunk), 1)
    earlier = jnp.where(c_iota < r_iota, 1.0, 0.0).astype(BF16)

    def rank_pass(ref, slot, run):
        def body(c, run):
            oh = rows_of(ref, c)
            ohf = oh.astype(F32)
            before = _dot(earlier, oh) + run
            posv = jnp.sum(ohf * before, axis=1, keepdims=True)
            pos_row = jnp.broadcast_to(posv, (chunk, LANES)).T[0:1, :]
            pos_ref[pl.ds(slot * n_chunks + c, 1), :] = pos_row.astype(jnp.int32)
            return run + jnp.sum(ohf, axis=0, keepdims=True)
        return lax.fori_loop(0, n_chunks, body, run)

    rank_pass(oh2_ref, 1, rank_pass(oh1_ref, 0, base))

    n_tile_lanes = te_ref.shape[1]
    tile_start = lax.broadcasted_iota(jnp.int32, (SUBLANES, n_tile_lanes), 1) * EXPERT_TILE
    te = jnp.zeros((SUBLANES, n_tile_lanes), jnp.int32)
    for e in range(N_EXPERTS):
        te = te + jnp.where(ends[:, e:e + 1] <= tile_start, 1, 0)
    te_ref[...] = jnp.minimum(te, N_EXPERTS - 1)
    used_tiles = ends[:, N_EXPERTS - 1:N_EXPERTS] >> tile_shift
    misc_ref[...] = jnp.where(lane < N_EXPERTS, jnp.where(padded > 0, ends - EXPERT_TILE, -1), used_tiles)


def _plan(oh1, oh2, n_tiles, chunk):
    n = oh1.shape[0]
    n_tile_lanes = -(-n_tiles // LANES) * LANES
    whole = lambda shape: pl.BlockSpec(shape, lambda i: (0, 0))
    return pl.pallas_call(
        functools.partial(_plan_kernel, chunk=chunk),
        grid=(1,),
        in_specs=[whole((n, LANES)), whole((n, LANES))],
        out_specs=[whole((TOP_K * n // chunk, chunk)), whole((SUBLANES, n_tile_lanes)), whole((SUBLANES, LANES))],
        out_shape=[jax.ShapeDtypeStruct((TOP_K * n // chunk, chunk), jnp.int32),
                   jax.ShapeDtypeStruct((SUBLANES, n_tile_lanes), jnp.int32),
                   jax.ShapeDtypeStruct((SUBLANES, LANES), jnp.int32)],
        compiler_params=pltpu.CompilerParams(dimension_semantics=("arbitrary",), vmem_limit_bytes=VMEM_LIMIT),
        name="moe_plan",
    )(oh1, oh2)


def _row_slice(ref, row, n_rows=1):
    start = row * SUBLANES if isinstance(row, int) else pl.multiple_of(row * SUBLANES, SUBLANES)
    return ref.at[pl.ds(start, n_rows * SUBLANES)]


def _dispatch_kernel(zstart_ref, pos_ref, trow_ref, xs_hbm, zero_buf, zero_sem, row_sem, *, tokens):
    i = pl.program_id(0)

    @pl.when(i == 0)
    def _():
        zero_buf[...] = jnp.zeros(zero_buf.shape, F32)
        n_tiles = xs_hbm.shape[0] // (EXPERT_TILE * SUBLANES)
        used = zstart_ref[N_EXPERTS]

        def zero_tile(first_row):
            return pltpu.make_async_copy(zero_buf, _row_slice(xs_hbm, first_row, EXPERT_TILE), zero_sem)

        for phase in ("start", "wait"):
            for e in range(N_EXPERTS):
                @pl.when(zstart_ref[e] >= 0)
                def _():
                    copy = zero_tile(zstart_ref[e])
                    copy.start() if phase == "start" else copy.wait()

            def tail(t, carry):
                copy = zero_tile(t * EXPERT_TILE)
                copy.start() if phase == "start" else copy.wait()
                return carry

            lax.fori_loop(used, n_tiles, tail, 0)

    unroll = 8

    def issue(g, carry):
        for u in range(unroll):
            n = g * unroll + u
            src = _row_slice(trow_ref, n)
            for k in range(TOP_K):
                pltpu.make_async_copy(src, _row_slice(xs_hbm, pos_ref[k, n]), row_sem).start(priority=(u + k) % 2)
        return carry

    lax.fori_loop(0, tokens // unroll, issue, 0)
    for k in range(TOP_K):
        pltpu.make_async_copy(trow_ref, _row_slice(xs_hbm, 0, tokens), row_sem).wait()


def _dispatch(zstart, pos, trow, sorted_rows, tokens):
    n = pos.shape[1]
    return pl.pallas_call(
        functools.partial(_dispatch_kernel, tokens=tokens),
        grid=(n // tokens,),
        in_specs=[pl.BlockSpec(memory_space=pltpu.SMEM),
                  pl.BlockSpec((TOP_K, tokens), lambda i: (0, i), memory_space=pltpu.SMEM),
                  pl.BlockSpec((tokens * SUBLANES, LANES), lambda i: (i, 0))],
        out_specs=pl.BlockSpec(memory_space=pl.ANY),
        out_shape=jax.ShapeDtypeStruct((sorted_rows * SUBLANES, LANES), F32),
        scratch_shapes=[pltpu.VMEM((EXPERT_TILE * SUBLANES, LANES), F32), pltpu.SemaphoreType.DMA,
                        pltpu.SemaphoreType.DMA],
        compiler_params=pltpu.CompilerParams(dimension_semantics=("arbitrary",)),
        name="moe_dispatch",
    )(zstart, pos, trow)


def _experts_kernel(te_ref, used_ref, xs_ref, wg_ref, wu_ref, wd_ref, ys_ref):
    t = pl.program_id(0)

    @pl.when(t < used_ref[0])
    def _():
        x = _load_token_rows(xs_ref, EXPERT_TILE).astype(BF16)
        hid = _silu(_dot(x, wg_ref[0].astype(BF16))) * _dot(x, wu_ref[0].astype(BF16))
        _store_token_rows(ys_ref, _dot(hid.astype(BF16), wd_ref[0].astype(BF16)))

    @pl.when(t >= used_ref[0])
    def _():
        ys_ref[...] = jnp.zeros(ys_ref.shape, F32)


def _experts(te, used, xs, wg, wu, wd, n_tiles):
    blk = EXPERT_TILE * SUBLANES
    return pl.pallas_call(
        _experts_kernel,
        grid_spec=pltpu.PrefetchScalarGridSpec(
            num_scalar_prefetch=2,
            grid=(n_tiles,),
            in_specs=[pl.BlockSpec((blk, LANES), lambda t, te, used: (jnp.minimum(t, used[0] - 1), 0)),
                      pl.BlockSpec((1, D_MODEL, D_FF), lambda t, te, used: (te[t], 0, 0)),
                      pl.BlockSpec((1, D_MODEL, D_FF), lambda t, te, used: (te[t], 0, 0)),
                      pl.BlockSpec((1, D_FF, D_MODEL), lambda t, te, used: (te[t], 0, 0))],
            out_specs=pl.BlockSpec((blk, LANES), lambda t, te, used: (t, 0))),
        out_shape=jax.ShapeDtypeStruct(xs.shape, F32),
        compiler_params=pltpu.CompilerParams(dimension_semantics=("arbitrary",), vmem_limit_bytes=VMEM_LIMIT),
        name="moe_experts",
    )(te, used, xs, wg, wu, wd)


def _combine_kernel(pos_ref, pos_next_ref, t_ref, h1_ref, wts_ref, ys_hbm, wsg_ref, wsu_ref, wsd_ref, gout_ref,
                    out_ref, ybuf, sems, *, tokens):
    i = pl.program_id(0)
    slot = lax.rem(i, 2)
    unroll = 8

    def start_gathers(p_ref, dst_slot):
        def issue(g, carry):
            for u in range(unroll):
                n = g * unroll + u
                for k in range(TOP_K):
                    pltpu.make_async_copy(_row_slice(ys_hbm, p_ref[k, n]), _row_slice(ybuf.at[dst_slot].at[k], n),
                                          sems.at[dst_slot]).start(priority=(u + k) % 2)
            return carry

        lax.fori_loop(0, tokens // unroll, issue, 0)

    @pl.when(i == 0)
    def _():
        start_gathers(pos_ref, slot)

    @pl.when(i + 1 < pl.num_programs(0))
    def _():
        start_gathers(pos_next_ref, 1 - slot)

    t = t_ref[...]
    hid = _silu(_dot(t, wsg_ref[...])) * _dot(t, wsu_ref[...])
    shared = _dot(hid.astype(BF16), wsd_ref[...])

    for k in range(TOP_K):
        pltpu.make_async_copy(_row_slice(ys_hbm, 0, tokens), ybuf.at[slot].at[k], sems.at[slot]).wait()
    wts = wts_ref[...]
    routed = (wts[:, 0:1] * _load_token_rows(ybuf.at[slot].at[0], tokens)
              + wts[:, 1:2] * _load_token_rows(ybuf.at[slot].at[1], tokens))
    out_ref[...] = h1_ref[...] + _rms(routed + shared, gout_ref[...])


def _combine(pos, t2, h1, wts, ys, shared_w, tokens):
    n = t2.shape[0]
    wsg, wsu, wsd, gout = shared_w
    const = lambda a: pl.BlockSpec(a.shape, lambda i: (0,) * a.ndim)
    tile = lambda w: pl.BlockSpec((tokens, w), lambda i: (i, 0))
    steps = n // tokens
    return pl.pallas_call(
        functools.partial(_combine_kernel, tokens=tokens),
        grid=(steps,),
        in_specs=[pl.BlockSpec((TOP_K, tokens), lambda i: (0, i), memory_space=pltpu.SMEM),
                  pl.BlockSpec((TOP_K, tokens), lambda i: (0, jnp.minimum(i + 1, steps - 1)),
                               memory_space=pltpu.SMEM),
                  tile(D_MODEL), tile(D_MODEL), tile(LANES), pl.BlockSpec(memory_space=pl.ANY),
                  const(wsg), const(wsu), const(wsd), const(gout)],
        out_specs=tile(D_MODEL),
        out_shape=jax.ShapeDtypeStruct((n, D_MODEL), F32),
        scratch_shapes=[pltpu.VMEM((2, TOP_K, tokens * SUBLANES, LANES), F32), pltpu.SemaphoreType.DMA((2,))],
        compiler_params=pltpu.CompilerParams(dimension_semantics=("arbitrary",), vmem_limit_bytes=VMEM_LIMIT),
        name="moe_combine",
    )(pos, pos, t2, h1, wts, ys, wsg, wsu, wsd, gout)


def _swap_halves(w):
    half = w.shape[-1] // 2
    return jnp.concatenate([w[..., half:], w[..., :half]], axis=-1)


def _block_diag(blocks):
    h, r, c = blocks.shape
    eye = jnp.eye(h, dtype=blocks.dtype)
    return jnp.einsum('hrc,hg->hrgc', blocks, eye).reshape(h * r, h * c)


def kernel(x, positions, meta_tokens, pre_mix_norm, w_in, q_norm, w_uq, kv_norm, w_ukv, conv_w, attn_out_norm,
           conv_out_norm, w_out, post_mix_norm, pre_ffn_norm, w_group_router, b_group_router, w_expert_router,
           b_expert_router, w_gate, w_up, w_down, w_sh_gate, w_sh_up, w_sh_down, post_ffn_norm):
    bsz, seq, d = x.shape
    n = bsz * seq
    row = lambda v: v.reshape(1, -1).astype(F32)

    wi = w_in[0]
    o_kv, o_pe, o_b = Q_LORA, Q_LORA + KV_LORA, Q_LORA + KV_LORA + QK_ROPE
    k_pe = wi[:, o_pe:o_b]
    w1 = jnp.concatenate([wi[:, :o_pe], k_pe, _swap_halves(k_pe), jnp.zeros((d, LANES - 2 * QK_ROPE), F32),
                          wi[:, o_b:]], axis=1).astype(BF16)
    wq = w_uq[0].reshape(Q_LORA, N_HEADS, QK_NOPE + QK_ROPE)
    wqn = wq[:, :, :QK_NOPE].reshape(Q_LORA, N_HEADS * QK_NOPE).astype(F32)
    wq_rope = wq[:, :, QK_NOPE:]
    wqr = jnp.concatenate([wq_rope, _swap_halves(wq_rope),
                           jnp.zeros((Q_LORA, N_HEADS, LANES - 2 * QK_ROPE), F32)], axis=2)
    wqr = wqr.reshape(Q_LORA, N_HEADS * LANES).astype(F32)
    wkv = w_ukv[0].reshape(KV_LORA, N_HEADS, QK_NOPE + V_DIM)
    bduk = _block_diag(jnp.transpose(wkv[:, :, :QK_NOPE], (1, 2, 0))).astype(F32)
    wuv = _block_diag(jnp.transpose(wkv[:, :, QK_NOPE:], (1, 0, 2))).astype(BF16)
    premix_w = (row(pre_mix_norm[0]), w1, row(q_norm[0]), row(kv_norm[0]), wqn, bduk, wqr,
                conv_w[0].astype(F32), row(conv_out_norm[0]))
    wr = jnp.concatenate([w_group_router[0], jnp.zeros((d, LANES - N_GROUPS), F32),
                          w_expert_router[0], jnp.zeros((d, LANES - N_EXPERTS), F32)], axis=1).astype(BF16)
    br = jnp.concatenate([b_group_router[0], jnp.zeros((LANES - N_GROUPS,), F32),
                          b_expert_router[0], jnp.zeros((LANES - N_EXPERTS,), F32)]).reshape(1, -1).astype(F32)
    postmix_w = (wuv, row(attn_out_norm[0]), w_out[0].astype(BF16), row(post_mix_norm[0]), row(pre_ffn_norm[0]),
                 wr, br)
    expert_w = (w_gate[0], w_up[0], w_down[0])
    shared_w = (w_sh_gate[0].astype(BF16), w_sh_up[0].astype(BF16), w_sh_down[0].astype(BF16),
                row(post_ffn_norm[0]))

    pos_all = jnp.concatenate([positions.astype(jnp.int32).reshape(-1) + N_META,
                               jnp.zeros((PAD_FRONT,), jnp.int32), jnp.arange(N_META, dtype=jnp.int32)])
    inv_freq = 1.0 / (ROPE_THETA ** (jnp.arange(0, QK_ROPE, 2, dtype=F32) / QK_ROPE))
    n_tab = n + BLOCK
    tab = _rope_table(pos_all.reshape(1, n_tab), inv_freq.reshape(-1, 1), n_tab // 3)

    meta_blk = jnp.pad(meta_tokens.astype(F32), ((PAD_FRONT, 0), (0, 0)))[None]
    zero_carry = jnp.zeros((8, CONV_DIM), F32)
    _, kmeta, _, meta_tail = _premix(meta_blk, tab, n // BLOCK, zero_carry, premix_w, BLOCK)
    qp, kc, convn, _ = _premix(x, tab, 0, meta_tail, premix_w, 512)

    olat = _attention(qp, kc, kmeta[0], 512, 512, 8)

    h1, t, trow, oh1, oh2, wts = _postmix(x.reshape(n, d), olat.reshape(n, -1), convn.reshape(n, -1), postmix_w, 512)

    sorted_rows = TOP_K * n + N_EXPERTS * EXPERT_TILE
    n_tiles = sorted_rows // EXPERT_TILE
    pos2, te8, misc = _plan(oh1, oh2, n_tiles, 512)
    pos = pos2.reshape(TOP_K, n)
    xs = _dispatch(misc[0, :N_EXPERTS + 1], pos, trow, sorted_rows, 1024)
    ys = _experts(te8[0, :n_tiles], misc[0, N_EXPERTS:N_EXPERTS + 1], xs, *expert_w, n_tiles)
    out = _combine(pos, t, h1, wts, ys, shared_w, 512)
    return out.reshape(bsz, seq, d)
```

```python
import functools
import math

import jax
import jax.numpy as jnp
from jax import lax
from jax.experimental import pallas as pl
from jax.experimental.pallas import tpu as pltpu

D_MODEL = 1024
N_META = 16
BLOCK = 128
PAD_FRONT = BLOCK - N_META
N_HEADS = 8
QK_NOPE = 64
QK_ROPE = 32
V_DIM = 64
Q_LORA = 256
KV_LORA = 128
ROPE_THETA = 10000.0
ATTN_SCALE = (QK_NOPE + QK_ROPE) ** -0.5
Q_SCALE = ATTN_SCALE * math.log2(math.e)
ATTN_WIDTH = N_HEADS * V_DIM
CONV_DIM = 512
CONV_W = 3
N_GROUPS = 4
EXPERTS_PER_GROUP = 8
N_EXPERTS = N_GROUPS * EXPERTS_PER_GROUP
D_FF = 256
EPS = 1e-6
NEG_INF = -1e30

LANES = 128
SUBLANES = 8
TOP_K = 2
EXPERT_TILE = 512
KEY_WIDTH = 2 * LANES
IN_COLS_PADDED = Q_LORA + KV_LORA + LANES + 3 * CONV_DIM

F32 = jnp.float32
BF16 = jnp.bfloat16
VMEM_LIMIT = 56 * 1024 * 1024


def _rms(x, g):
    return x * lax.rsqrt(jnp.mean(x * x, axis=-1, keepdims=True) + EPS) * g


def _dot(a, b):
    return jnp.dot(a, b, preferred_element_type=F32)


def _dot_nt(a, b):
    return lax.dot_general(a, b, (((1,), (1,)), ((), ())), preferred_element_type=F32)


def _store_token_rows(ref, value, first_token=0):
    rows = value.shape[0]
    for c in range(D_MODEL // LANES):
        ref[pl.ds(first_token * SUBLANES + c, rows, stride=SUBLANES), :] = value[:, c * LANES:(c + 1) * LANES]


def _load_token_rows(ref, rows):
    return jnp.concatenate([ref[pl.ds(c, rows, stride=SUBLANES), :] for c in range(D_MODEL // LANES)], axis=1)


def _rope_table_kernel(pos_ref, invf_ref, out_ref):
    pos = pos_ref[...].astype(F32)
    ang = invf_ref[...] * pos
    c = jnp.cos(ang)
    s = jnp.sin(ang)
    planes = jnp.concatenate([c, c, -s, s, jnp.zeros((LANES - 4 * (QK_ROPE // 2), ang.shape[1]), F32)], axis=0)
    out_ref[...] = planes.T


def _rope_table(pos_row, inv_freq_col, chunk):
    n = pos_row.shape[1]
    return pl.pallas_call(
        _rope_table_kernel,
        grid=(n // chunk,),
        in_specs=[pl.BlockSpec((1, chunk), lambda i: (0, i)),
                  pl.BlockSpec((QK_ROPE // 2, 1), lambda i: (0, 0))],
        out_specs=pl.BlockSpec((chunk, LANES), lambda i: (i, 0)),
        out_shape=jax.ShapeDtypeStruct((n, LANES), F32),
        name="rope_table",
    )(pos_row, inv_freq_col)


def _premix_kernel(x_ref, tab_ref, carry_in_ref, g_pre_ref, w1_ref, gq_ref, gkv_ref, wqn_ref, bduk_ref,
                   wqr_ref, convw_ref, gconv_ref, qp_ref, kc_ref, convn_ref, utail_ref, carry_ref, wq_scr, *,
                   sub_blocks):
    t = pl.program_id(1)
    rows = x_ref.shape[1]

    @pl.when(t == 0)
    def _():
        carry_ref[...] = carry_in_ref[...]

    @pl.when((pl.program_id(0) == 0) & (t == 0))
    def _():
        wabs = jnp.dot(wqn_ref[...], bduk_ref[...], preferred_element_type=F32,
                       precision=lax.Precision.HIGHEST) * Q_SCALE
        wrope = wqr_ref[...] * Q_SCALE
        for h in range(N_HEADS):
            sl = slice(h * LANES, (h + 1) * LANES)
            wq_scr[:, h * KEY_WIDTH:h * KEY_WIDTH + LANES] = wabs[:, sl].astype(BF16)
            wq_scr[:, h * KEY_WIDTH + LANES:(h + 1) * KEY_WIDTH] = wrope[:, sl].astype(BF16)

    rows = rows // sub_blocks
    lane = lax.broadcasted_iota(jnp.int32, (rows, LANES), 1)
    row8 = lax.broadcasted_iota(jnp.int32, (8, CONV_DIM), 0)
    cw = convw_ref[...]
    prev = carry_ref[...]
    for sb in range(sub_blocks):
        r = pl.ds(sb * rows, rows)
        xn = _rms(x_ref[0, r, :], g_pre_ref[...]).astype(BF16)
        z = _dot(xn, w1_ref[...])
        tab = tab_ref[r, :]

        def rope(pair):
            prod = pair * tab
            return jnp.where(lane < QK_ROPE, prod + pltpu.roll(prod, LANES - QK_ROPE, axis=1), 0.0)

        ckvn = _rms(z[:, Q_LORA:Q_LORA + KV_LORA], gkv_ref[...])
        krope = rope(z[:, Q_LORA + KV_LORA:Q_LORA + KV_LORA + LANES])
        kc_ref[0, r, :] = jnp.concatenate([ckvn, krope], axis=1).astype(BF16)

        cqn = _rms(z[:, :Q_LORA], gq_ref[...]).astype(BF16)
        qall = _dot(cqn, wq_scr[...])
        for h in range(N_HEADS):
            qp_ref[0, h, r, 0:LANES] = qall[:, h * KEY_WIDTH:h * KEY_WIDTH + LANES].astype(BF16)
            qp_ref[0, h, r, LANES:KEY_WIDTH] = rope(
                qall[:, h * KEY_WIDTH + LANES:(h + 1) * KEY_WIDTH]).astype(BF16)

        c0 = Q_LORA + KV_LORA + LANES
        gate_b = z[:, c0:c0 + CONV_DIM]
        u = z[:, c0 + CONV_DIM:c0 + 2 * CONV_DIM] * z[:, c0 + 2 * CONV_DIM:c0 + 3 * CONV_DIM]
        r1 = pltpu.roll(u, 1, axis=0)
        r2 = pltpu.roll(u, 2, axis=0)
        p1 = pltpu.roll(prev, 1, axis=0)
        p2 = pltpu.roll(prev, 2, axis=0)
        u1 = jnp.concatenate([jnp.where(row8 < 1, p1, r1[0:8]), r1[8:]], axis=0)
        u2 = jnp.concatenate([jnp.where(row8 < 2, p2, r2[0:8]), r2[8:]], axis=0)
        y = cw[0:1] * u2 + cw[1:2] * u1 + cw[2:3] * u
        convn_ref[0, r, :] = _rms(gate_b * y, gconv_ref[...]).astype(BF16)
        prev = u[rows - 8:rows]
    carry_ref[...] = prev
    utail_ref[...] = prev


def _premix(x3, tab, tab_block0, carry_in, wts, rows, sub_blocks):
    bsz, seq, _ = x3.shape
    nt = seq // rows
    g_pre, w1, gq, gkv, wqn, bduk, wqr, convw, gconv = wts
    const = lambda a: pl.BlockSpec(a.shape, lambda b, t: (0,) * a.ndim)
    return pl.pallas_call(
        functools.partial(_premix_kernel, sub_blocks=sub_blocks),
        grid=(bsz, nt),
        in_specs=[pl.BlockSpec((1, rows, D_MODEL), lambda b, t: (b, t, 0)),
                  pl.BlockSpec((rows, LANES), lambda b, t: (tab_block0 + b * nt + t, 0)),
                  const(carry_in), const(g_pre), const(w1), const(gq), const(gkv), const(wqn), const(bduk),
                  const(wqr), const(convw), const(gconv)],
        out_specs=[pl.BlockSpec((1, N_HEADS, rows, KEY_WIDTH), lambda b, t: (b, 0, t, 0)),
                   pl.BlockSpec((1, rows, KEY_WIDTH), lambda b, t: (b, t, 0)),
                   pl.BlockSpec((1, rows, CONV_DIM), lambda b, t: (b, t, 0)),
                   pl.BlockSpec((8, CONV_DIM), lambda b, t: (0, 0))],
        out_shape=[jax.ShapeDtypeStruct((bsz, N_HEADS, seq, KEY_WIDTH), BF16),
                   jax.ShapeDtypeStruct((bsz, seq, KEY_WIDTH), BF16),
                   jax.ShapeDtypeStruct((bsz, seq, CONV_DIM), BF16),
                   jax.ShapeDtypeStruct((8, CONV_DIM), F32)],
        scratch_shapes=[pltpu.VMEM((8, CONV_DIM), F32), pltpu.VMEM((Q_LORA, N_HEADS * KEY_WIDTH), BF16)],
        compiler_params=pltpu.CompilerParams(dimension_semantics=("arbitrary", "arbitrary"),
                                             vmem_limit_bytes=VMEM_LIMIT),
        name="premix",
    )(x3, tab, carry_in, g_pre, w1, gq, gkv, wqn, bduk, wqr, convw, gconv)


def _attention_kernel(q_ref, qnext_ref, k_ref, km_ref, o_ref, s_scr, smeta_scr, tri_scr, m_scr, l_scr, acc_scr, *,
                      bq, bk, chunks):
    assert bq == bk
    b = pl.program_id(0)
    i = pl.program_id(1)
    hpc = N_HEADS // chunks
    c_rows = hpc * bq
    km = km_ref[...]

    def scores(qr, c, kb):
        qc = qr[0, c * hpc:(c + 1) * hpc].reshape(c_rows, KEY_WIDTH)
        return _dot_nt(qc, kb)

    def values(kb):
        return jnp.concatenate([kb[:, :KV_LORA], jnp.ones((kb.shape[0], LANES), BF16)], axis=1)

    def consume(c, s, v_ones, mask, first=False):
        if mask is not None:
            s = jnp.where(mask, s, NEG_INF)
        m_blk = jnp.broadcast_to(jnp.max(s, axis=1, keepdims=True), (c_rows, LANES))
        m_new = m_blk if first else jnp.maximum(m_scr[c], m_blk)
        p = jnp.exp2(s - jnp.concatenate([m_new] * (s.shape[1] // LANES), axis=1)).astype(BF16)
        pv = _dot(p, v_ones)
        if first:
            l_scr[c] = pv[:, KV_LORA:]
            acc_scr[c] = pv[:, :KV_LORA]
        else:
            alpha = jnp.exp2(m_scr[c] - m_new)
            l_scr[c] = alpha * l_scr[c] + pv[:, KV_LORA:]
            acc_scr[c] = alpha * acc_scr[c] + pv[:, :KV_LORA]
        m_scr[c] = m_new

    def key_block(j):
        return k_ref[0, pl.ds(pl.multiple_of(j * bk, bk), bk), :]

    @pl.when((b == 0) & (i == 0))
    def _():
        for c in range(chunks):
            smeta_scr[c] = scores(q_ref, c, km)
        rows_i = lax.broadcasted_iota(jnp.int32, (bq, bk), 0)
        cols_i = lax.broadcasted_iota(jnp.int32, (bq, bk), 1)
        tri_scr[...] = jnp.where(cols_i <= rows_i, 0.0, NEG_INF)

    n_full = (i * bq) // bk
    kb0 = key_block(0)
    v_meta = values(km)
    col = lax.broadcasted_iota(jnp.int32, (c_rows, BLOCK), 1)
    for c in range(chunks):
        s_scr[c] = scores(q_ref, c, kb0)
        consume(c, smeta_scr[c], v_meta, col >= PAD_FRONT, first=True)

    def body(j, carry):
        v_ones = values(key_block(j))
        kb_next = key_block(j + 1)
        for c in range(chunks):
            s = s_scr[c]
            s_scr[c] = scores(q_ref, c, kb_next)
            consume(c, s, v_ones, None)
        return carry

    lax.fori_loop(0, n_full, body, 0)

    v_ones = values(key_block(n_full))
    for c in range(chunks):
        s = s_scr[c] + jnp.concatenate([tri_scr[...]] * hpc, axis=0)
        smeta_scr[c] = scores(qnext_ref, c, km)
        consume(c, s, v_ones, None)

    for c in range(chunks):
        o = acc_scr[c] / l_scr[c]
        for hh in range(hpc):
            h = c * hpc + hh
            o_ref[0, :, h * KV_LORA:(h + 1) * KV_LORA] = o[hh * bq:(hh + 1) * bq].astype(BF16)


def _attention(qp, kc, kmeta, bq, bk, chunks):
    bsz, _, seq, _ = qp.shape
    nq = seq // bq
    c_rows = N_HEADS * bq // chunks
    scratch = [pltpu.VMEM((chunks, c_rows, bk), F32), pltpu.VMEM((chunks, c_rows, BLOCK), F32),
               pltpu.VMEM((bq, bk), F32), pltpu.VMEM((chunks, c_rows, LANES), F32),
               pltpu.VMEM((chunks, c_rows, LANES), F32), pltpu.VMEM((chunks, c_rows, KV_LORA), F32)]

    def next_q(b, i):
        wrap = (i + 1) // nq
        return (jnp.minimum(b + wrap, bsz - 1), 0, i + 1 - wrap * nq, 0)

    return pl.pallas_call(
        functools.partial(_attention_kernel, bq=bq, bk=bk, chunks=chunks),
        grid=(bsz, nq),
        in_specs=[pl.BlockSpec((1, N_HEADS, bq, KEY_WIDTH), lambda b, i: (b, 0, i, 0)),
                  pl.BlockSpec((1, N_HEADS, bq, KEY_WIDTH), next_q),
                  pl.BlockSpec((1, seq, KEY_WIDTH), lambda b, i: (b, 0, 0)),
                  pl.BlockSpec((BLOCK, KEY_WIDTH), lambda b, i: (0, 0))],
        out_specs=pl.BlockSpec((1, bq, N_HEADS * KV_LORA), lambda b, i: (b, i, 0)),
        out_shape=jax.ShapeDtypeStruct((bsz, seq, N_HEADS * KV_LORA), BF16),
        scratch_shapes=scratch,
        compiler_params=pltpu.CompilerParams(dimension_semantics=("arbitrary", "arbitrary"),
                                             vmem_limit_bytes=VMEM_LIMIT),
        name="attention",
    )(qp, qp, kc, kmeta)


def _postmix_kernel(x_ref, olat_ref, convn_ref, wuv_ref, gattn_ref, wout_ref, gpost_ref, gffn_ref, wr_ref, br_ref,
                    h1_ref, t_ref, trow_ref, oh1_ref, oh2_ref, wts_ref, *, sub_blocks):
    rows = x_ref.shape[0] // sub_blocks
    wout = wout_ref[...]
    lane = lax.broadcasted_iota(jnp.int32, (rows, LANES), 1)

    def first_argmax(v):
        mx = jnp.max(v, axis=1, keepdims=True)
        return mx, jnp.min(jnp.where(v == mx, lane, LANES), axis=1, keepdims=True)

    for sb in range(sub_blocks):
        r = pl.ds(sb * rows, rows)
        attn = _dot(olat_ref[r, :], wuv_ref[...])
        attn_n = _rms(attn, gattn_ref[...]).astype(BF16)
        mix = _dot(attn_n, wout[:ATTN_WIDTH]) + _dot(convn_ref[r, :], wout[ATTN_WIDTH:])
        h1 = x_ref[r, :] + _rms(mix, gpost_ref[...])
        h1_ref[r, :] = h1
        tf = _rms(h1, gffn_ref[...])
        t = tf.astype(BF16)
        t_ref[r, :] = t
        _store_token_rows(trow_ref, tf, sb * rows)

        logits = _dot(t, wr_ref[...]) + br_ref[...]
        gl = jnp.where(lane < N_GROUPS, logits[:, :LANES], NEG_INF)
        gmax, gidx = first_argmax(gl)
        g_w = 1.0 / jnp.sum(jnp.exp(gl - gmax), axis=1, keepdims=True)
        el = jnp.where((lane >> 3) == gidx, logits[:, LANES:], NEG_INF)
        m1, i1 = first_argmax(el)
        m2, i2 = first_argmax(jnp.where(lane == i1, NEG_INF, el))
        e2 = jnp.exp(m2 - m1)
        den = 1.0 + e2
        oh1_ref[r, :] = jnp.where(lane == i1, 1.0, 0.0).astype(BF16)
        oh2_ref[r, :] = jnp.where(lane == i2, 1.0, 0.0).astype(BF16)
        wts_ref[r, :] = jnp.where(lane == 0, g_w / den, jnp.where(lane == 1, g_w * e2 / den, 0.0))


def _postmix(x2, olat2, convn2, wts, rows, sub_blocks):
    n = x2.shape[0]
    wuv, gattn, wout, gpost, gffn, wr, br = wts
    const = lambda a: pl.BlockSpec(a.shape, lambda i: (0,) * a.ndim)
    tile = lambda w: pl.BlockSpec((rows, w), lambda i: (i, 0))
    return pl.pallas_call(
        functools.partial(_postmix_kernel, sub_blocks=sub_blocks),
        grid=(n // rows,),
        in_specs=[tile(D_MODEL), tile(N_HEADS * KV_LORA), tile(CONV_DIM),
                  const(wuv), const(gattn), const(wout), const(gpost), const(gffn), const(wr), const(br)],
        out_specs=[tile(D_MODEL), tile(D_MODEL), pl.BlockSpec((rows * SUBLANES, LANES), lambda i: (i, 0)),
                   tile(LANES), tile(LANES), tile(LANES)],
        out_shape=[jax.ShapeDtypeStruct((n, D_MODEL), F32),
                   jax.ShapeDtypeStruct((n, D_MODEL), BF16),
                   jax.ShapeDtypeStruct((n * SUBLANES, LANES), F32),
                   jax.ShapeDtypeStruct((n, LANES), BF16),
                   jax.ShapeDtypeStruct((n, LANES), BF16),
                   jax.ShapeDtypeStruct((n, LANES), F32)],
        compiler_params=pltpu.CompilerParams(dimension_semantics=("arbitrary",), vmem_limit_bytes=VMEM_LIMIT),
        name="postmix",
    )(x2, olat2, convn2, wuv, gattn, wout, gpost, gffn, wr, br)


def _silu(a):
    return a * (1.0 / (1.0 + jnp.exp(-a)))


def _plan_kernel(oh1_ref, oh2_ref, pos_ref, te_ref, misc_ref, *, chunk):
    n_tok = oh1_ref.shape[0]
    n_chunks = n_tok // chunk
    lane = lax.broadcasted_iota(jnp.int32, (SUBLANES, LANES), 1)

    def rows_of(ref, c):
        return ref[pl.ds(pl.multiple_of(c * chunk, chunk), chunk), :]

    def count_body(c, acc):
        both = rows_of(oh1_ref, c).astype(F32) + rows_of(oh2_ref, c).astype(F32)
        return acc + jnp.sum(both, axis=0, keepdims=True)

    counts = lax.fori_loop(0, n_chunks, count_body, jnp.zeros((1, LANES), F32))
    counts = jnp.broadcast_to(counts, (SUBLANES, LANES)).astype(jnp.int32)
    tile_shift = EXPERT_TILE.bit_length() - 1
    padded = ((counts + (EXPERT_TILE - 1)) >> tile_shift) << tile_shift
    ends = padded
    shift = 1
    while shift < N_EXPERTS:
        ends = ends + jnp.where(lane >= shift, pltpu.roll(ends, shift, axis=1), 0)
        shift *= 2
    base = (ends - padded)[0:1].astype(F32)

    r_iota = lax.broadcasted_iota(jnp.int32, (chunk, chunk), 0)
    c_iota = lax.broadcasted_iota(jnp.int32, (chunk, chunk), 1)
    earlier = jnp.where(c_iota < r_iota, 1.0, 0.0).astype(BF16)

    def rank_pass(ref, slot, run):
        def body(c, run):
            oh = rows_of(ref, c)
            ohf = oh.astype(F32)
            before = _dot(earlier, oh) + run
            posv = jnp.sum(ohf * before, axis=1, keepdims=True)
            pos_row = jnp.broadcast_to(posv, (chunk, LANES)).T[0:1, :]
            pos_ref[pl.ds(slot * n_chunks + c, 1), :] = pos_row.astype(jnp.int32)
            return run + jnp.sum(ohf, axis=0, keepdims=True)
        return lax.fori_loop(0, n_chunks, body, run)

    rank_pass(oh2_ref, 1, rank_pass(oh1_ref, 0, base))

    n_tile_lanes = te_ref.shape[1]
    tile_start = lax.broadcasted_iota(jnp.int32, (SUBLANES, n_tile_lanes), 1) * EXPERT_TILE
    te = jnp.zeros((SUBLANES, n_tile_lanes), jnp.int32)
    for e in range(N_EXPERTS):
        te = te + jnp.where(ends[:, e:e + 1] <= tile_start, 1, 0)
    te_ref[...] = jnp.minimum(te, N_EXPERTS - 1)
    used_tiles = ends[:, N_EXPERTS - 1:N_EXPERTS] >> tile_shift
    misc_ref[...] = jnp.where(lane < N_EXPERTS, jnp.where(padded > 0, ends - EXPERT_TILE, -1), used_tiles)


def _plan(oh1, oh2, n_tiles, chunk):
    n = oh1.shape[0]
    n_tile_lanes = -(-n_tiles // LANES) * LANES
    whole = lambda shape: pl.BlockSpec(shape, lambda i: (0, 0))
    return pl.pallas_call(
        functools.partial(_plan_kernel, chunk=chunk),
        grid=(1,),
        in_specs=[whole((n, LANES)), whole((n, LANES))],
        out_specs=[whole((TOP_K * n // chunk, chunk)), whole((SUBLANES, n_tile_lanes)), whole((SUBLANES, LANES))],
        out_shape=[jax.ShapeDtypeStruct((TOP_K * n // chunk, chunk), jnp.int32),
                   jax.ShapeDtypeStruct((SUBLANES, n_tile_lanes), jnp.int32),
                   jax.ShapeDtypeStruct((SUBLANES, LANES), jnp.int32)],
        compiler_params=pltpu.CompilerParams(dimension_semantics=("arbitrary",), vmem_limit_bytes=VMEM_LIMIT),
        name="moe_plan",
    )(oh1, oh2)


def _row_slice(ref, row, n_rows=1):
    start = row * SUBLANES if isinstance(row, int) else pl.multiple_of(row * SUBLANES, SUBLANES)
    return ref.at[pl.ds(start, n_rows * SUBLANES)]


def _dispatch_kernel(zstart_ref, pos_ref, trow_ref, xs_hbm, zero_buf, zero_sem, row_sem, *, tokens):
    i = pl.program_id(0)

    @pl.when(i == 0)
    def _():
        zero_buf[...] = jnp.zeros(zero_buf.shape, F32)
        n_tiles = xs_hbm.shape[0] // (EXPERT_TILE * SUBLANES)
        used = zstart_ref[N_EXPERTS]

        def zero_tile(first_row):
            return pltpu.make_async_copy(zero_buf, _row_slice(xs_hbm, first_row, EXPERT_TILE), zero_sem)

        for phase in ("start", "wait"):
            for e in range(N_EXPERTS):
                @pl.when(zstart_ref[e] >= 0)
                def _():
                    copy = zero_tile(zstart_ref[e])
                    copy.start() if phase == "start" else copy.wait()

            def tail(t, carry):
                copy = zero_tile(t * EXPERT_TILE)
                copy.start() if phase == "start" else copy.wait()
                return carry

            lax.fori_loop(used, n_tiles, tail, 0)

    unroll = 8

    def issue(g, carry):
        for u in range(unroll):
            n = g * unroll + u
            src = _row_slice(trow_ref, n)
            for k in range(TOP_K):
                pltpu.make_async_copy(src, _row_slice(xs_hbm, pos_ref[k, n]), row_sem).start(priority=(u + k) % 2)
        return carry

    lax.fori_loop(0, tokens // unroll, issue, 0)
    for k in range(TOP_K):
        pltpu.make_async_copy(trow_ref, _row_slice(xs_hbm, 0, tokens), row_sem).wait()


def _dispatch(zstart, pos, trow, sorted_rows, tokens):
    n = pos.shape[1]
    return pl.pallas_call(
        functools.partial(_dispatch_kernel, tokens=tokens),
        grid=(n // tokens,),
        in_specs=[pl.BlockSpec(memory_space=pltpu.SMEM),
                  pl.BlockSpec((TOP_K, tokens), lambda i: (0, i), memory_space=pltpu.SMEM),
                  pl.BlockSpec((tokens * SUBLANES, LANES), lambda i: (i, 0))],
        out_specs=pl.BlockSpec(memory_space=pl.ANY),
        out_shape=jax.ShapeDtypeStruct((sorted_rows * SUBLANES, LANES), F32),
        scratch_shapes=[pltpu.VMEM((EXPERT_TILE * SUBLANES, LANES), F32), pltpu.SemaphoreType.DMA,
                        pltpu.SemaphoreType.DMA],
        compiler_params=pltpu.CompilerParams(dimension_semantics=("arbitrary",)),
        name="moe_dispatch",
    )(zstart, pos, trow)


def _experts_kernel(te_ref, used_ref, xs_ref, wg_ref, wu_ref, wd_ref, ys_ref):
    t = pl.program_id(0)

    @pl.when(t < used_ref[0])
    def _():
        x = _load_token_rows(xs_ref, EXPERT_TILE).astype(BF16)
        hid = _silu(_dot(x, wg_ref[0].astype(BF16))) * _dot(x, wu_ref[0].astype(BF16))
        _store_token_rows(ys_ref, _dot(hid.astype(BF16), wd_ref[0].astype(BF16)))

    @pl.when(t >= used_ref[0])
    def _():
        ys_ref[...] = jnp.zeros(ys_ref.shape, F32)


def _experts(te, used, xs, wg, wu, wd, n_tiles):
    blk = EXPERT_TILE * SUBLANES
    return pl.pallas_call(
        _experts_kernel,
        grid_spec=pltpu.PrefetchScalarGridSpec(
            num_scalar_prefetch=2,
            grid=(n_tiles,),
            in_specs=[pl.BlockSpec((blk, LANES), lambda t, te, used: (jnp.minimum(t, used[0] - 1), 0)),
                      pl.BlockSpec((1, D_MODEL, D_FF), lambda t, te, used: (te[t], 0, 0)),
                      pl.BlockSpec((1, D_MODEL, D_FF), lambda t, te, used: (te[t], 0, 0)),
                      pl.BlockSpec((1, D_FF, D_MODEL), lambda t, te, used: (te[t], 0, 0))],
            out_specs=pl.BlockSpec((blk, LANES), lambda t, te, used: (t, 0))),
        out_shape=jax.ShapeDtypeStruct(xs.shape, F32),
        compiler_params=pltpu.CompilerParams(dimension_semantics=("arbitrary",), vmem_limit_bytes=VMEM_LIMIT),
        name="moe_experts",
    )(te, used, xs, wg, wu, wd)


def _combine_kernel(pos_ref, pos_next_ref, t_ref, h1_ref, wts_ref, ys_hbm, wsg_ref, wsu_ref, wsd_ref, gout_ref,
                    out_ref, ybuf, sems, *, tokens):
    i = pl.program_id(0)
    slot = lax.rem(i, 2)
    unroll = 8

    def start_gathers(p_ref, dst_slot):
        def issue(g, carry):
            for u in range(unroll):
                n = g * unroll + u
                for k in range(TOP_K):
                    pltpu.make_async_copy(_row_slice(ys_hbm, p_ref[k, n]), _row_slice(ybuf.at[dst_slot].at[k], n),
                                          sems.at[dst_slot]).start(priority=(u + k) % 2)
            return carry

        lax.fori_loop(0, tokens // unroll, issue, 0)

    @pl.when(i == 0)
    def _():
        start_gathers(pos_ref, slot)

    @pl.when(i + 1 < pl.num_programs(0))
    def _():
        start_gathers(pos_next_ref, 1 - slot)

    t = t_ref[...]
    hid = _silu(_dot(t, wsg_ref[...])) * _dot(t, wsu_ref[...])
    shared = _dot(hid.astype(BF16), wsd_ref[...])

    for k in range(TOP_K):
        pltpu.make_async_copy(_row_slice(ys_hbm, 0, tokens), ybuf.at[slot].at[k], sems.at[slot]).wait()
    wts = wts_ref[...]
    routed = (wts[:, 0:1] * _load_token_rows(ybuf.at[slot].at[0], tokens)
              + wts[:, 1:2] * _load_token_rows(ybuf.at[slot].at[1], tokens))
    out_ref[...] = h1_ref[...] + _rms(routed + shared, gout_ref[...])


def _combine(pos, t2, h1, wts, ys, shared_w, tokens):
    n = t2.shape[0]
    wsg, wsu, wsd, gout = shared_w
    const = lambda a: pl.BlockSpec(a.shape, lambda i: (0,) * a.ndim)
    tile = lambda w: pl.BlockSpec((tokens, w), lambda i: (i, 0))
    steps = n // tokens
    return pl.pallas_call(
        functools.partial(_combine_kernel, tokens=tokens),
        grid=(steps,),
        in_specs=[pl.BlockSpec((TOP_K, tokens), lambda i: (0, i), memory_space=pltpu.SMEM),
                  pl.BlockSpec((TOP_K, tokens), lambda i: (0, jnp.minimum(i + 1, steps - 1)),
                               memory_space=pltpu.SMEM),
                  tile(D_MODEL), tile(D_MODEL), tile(LANES), pl.BlockSpec(memory_space=pl.ANY),
                  const(wsg), const(wsu), const(wsd), const(gout)],
        out_specs=tile(D_MODEL),
        out_shape=jax.ShapeDtypeStruct((n, D_MODEL), F32),
        scratch_shapes=[pltpu.VMEM((2, TOP_K, tokens * SUBLANES, LANES), F32), pltpu.SemaphoreType.DMA((2,))],
        compiler_params=pltpu.CompilerParams(dimension_semantics=("arbitrary",), vmem_limit_bytes=VMEM_LIMIT),
        name="moe_combine",
    )(pos, pos, t2, h1, wts, ys, wsg, wsu, wsd, gout)


def _swap_halves(w):
    half = w.shape[-1] // 2
    return jnp.concatenate([w[..., half:], w[..., :half]], axis=-1)


def _block_diag(blocks):
    h, r, c = blocks.shape
    eye = jnp.eye(h, dtype=blocks.dtype)
    return jnp.einsum('hrc,hg->hrgc', blocks, eye).reshape(h * r, h * c)


def kernel(x, positions, meta_tokens, pre_mix_norm, w_in, q_norm, w_uq, kv_norm, w_ukv, conv_w, attn_out_norm,
           conv_out_norm, w_out, post_mix_norm, pre_ffn_norm, w_group_router, b_group_router, w_expert_router,
           b_expert_router, w_gate, w_up, w_down, w_sh_gate, w_sh_up, w_sh_down, post_ffn_norm):
    bsz, seq, d = x.shape
    n = bsz * seq
    row = lambda v: v.reshape(1, -1).astype(F32)

    wi = w_in[0]
    o_kv, o_pe, o_b = Q_LORA, Q_LORA + KV_LORA, Q_LORA + KV_LORA + QK_ROPE
    k_pe = wi[:, o_pe:o_b]
    w1 = jnp.concatenate([wi[:, :o_pe], k_pe, _swap_halves(k_pe), jnp.zeros((d, LANES - 2 * QK_ROPE), F32),
                          wi[:, o_b:]], axis=1).astype(BF16)
    wq = w_uq[0].reshape(Q_LORA, N_HEADS, QK_NOPE + QK_ROPE)
    wqn = wq[:, :, :QK_NOPE].reshape(Q_LORA, N_HEADS * QK_NOPE).astype(F32)
    wq_rope = wq[:, :, QK_NOPE:]
    wqr = jnp.concatenate([wq_rope, _swap_halves(wq_rope),
                           jnp.zeros((Q_LORA, N_HEADS, LANES - 2 * QK_ROPE), F32)], axis=2)
    wqr = wqr.reshape(Q_LORA, N_HEADS * LANES).astype(F32)
    wkv = w_ukv[0].reshape(KV_LORA, N_HEADS, QK_NOPE + V_DIM)
    bduk = _block_diag(jnp.transpose(wkv[:, :, :QK_NOPE], (1, 2, 0))).astype(F32)
    wuv = _block_diag(jnp.transpose(wkv[:, :, QK_NOPE:], (1, 0, 2))).astype(BF16)
    premix_w = (row(pre_mix_norm[0]), w1, row(q_norm[0]), row(kv_norm[0]), wqn, bduk, wqr,
                conv_w[0].astype(F32), row(conv_out_norm[0]))
    wr = jnp.concatenate([w_group_router[0], jnp.zeros((d, LANES - N_GROUPS), F32),
                          w_expert_router[0], jnp.zeros((d, LANES - N_EXPERTS), F32)], axis=1).astype(BF16)
    br = jnp.concatenate([b_group_router[0], jnp.zeros((LANES - N_GROUPS,), F32),
                          b_expert_router[0], jnp.zeros((LANES - N_EXPERTS,), F32)]).reshape(1, -1).astype(F32)
    postmix_w = (wuv, row(attn_out_norm[0]), w_out[0].astype(BF16), row(post_mix_norm[0]), row(pre_ffn_norm[0]),
                 wr, br)
    expert_w = (w_gate[0], w_up[0], w_down[0])
    shared_w = (w_sh_gate[0].astype(BF16), w_sh_up[0].astype(BF16), w_sh_down[0].astype(BF16),
                row(post_ffn_norm[0]))

    pos_all = jnp.concatenate([positions.astype(jnp.int32).reshape(-1) + N_META,
                               jnp.zeros((PAD_FRONT,), jnp.int32), jnp.arange(N_META, dtype=jnp.int32)])
    inv_freq = 1.0 / (ROPE_THETA ** (jnp.arange(0, QK_ROPE, 2, dtype=F32) / QK_ROPE))
    n_tab = n + BLOCK
    tab = _rope_table(pos_all.reshape(1, n_tab), inv_freq.reshape(-1, 1), n_tab // 3)

    meta_blk = jnp.pad(meta_tokens.astype(F32), ((PAD_FRONT, 0), (0, 0)))[None]
    zero_carry = jnp.zeros((8, CONV_DIM), F32)
    _, kmeta, _, meta_tail = _premix(meta_blk, tab, n // BLOCK, zero_carry, premix_w, BLOCK, 1)
    qp, kc, convn, _ = _premix(x, tab, 0, meta_tail, premix_w, 1024, 2)

    olat = _attention(qp, kc, kmeta[0], 512, 512, 8)

    h1, t, trow, oh1, oh2, wts = _postmix(x.reshape(n, d), olat.reshape(n, -1), convn.reshape(n, -1), postmix_w, 1024, 2)

    sorted_rows = TOP_K * n + N_EXPERTS * EXPERT_TILE
    n_tiles = sorted_rows // EXPERT_TILE
    pos2, te8, misc = _plan(oh1, oh2, n_tiles, 512)
    pos = pos2.reshape(TOP_K, n)
    xs = _dispatch(misc[0, :N_EXPERTS + 1], pos, trow, sorted_rows, 1024)
    ys = _experts(te8[0, :n_tiles], misc[0, N_EXPERTS:N_EXPERTS + 1], xs, *expert_w, n_tiles)
    out = _combine(pos, t, h1, wts, ys, shared_w, 512)
    return out.reshape(bsz, seq, d)
```

```python
import functools
import math

import jax
import jax.numpy as jnp
from jax import lax
from jax.experimental import pallas as pl
from jax.experimental.pallas import tpu as pltpu

D_MODEL = 1024
N_META = 16
BLOCK = 128
PAD_FRONT = BLOCK - N_META
N_HEADS = 8
QK_NOPE = 64
QK_ROPE = 32
V_DIM = 64
Q_LORA = 256
KV_LORA = 128
ROPE_THETA = 10000.0
ATTN_SCALE = (QK_NOPE + QK_ROPE) ** -0.5
Q_SCALE = ATTN_SCALE * math.log2(math.e)
ATTN_WIDTH = N_HEADS * V_DIM
CONV_DIM = 512
CONV_W = 3
N_GROUPS = 4
EXPERTS_PER_GROUP = 8
N_EXPERTS = N_GROUPS * EXPERTS_PER_GROUP
D_FF = 256
EPS = 1e-6
NEG_INF = -1e30

LANES = 128
SUBLANES = 8
TOP_K = 2
EXPERT_TILE = 512
KEY_WIDTH = 2 * LANES
IN_COLS_PADDED = Q_LORA + KV_LORA + LANES + 3 * CONV_DIM

F32 = jnp.float32
BF16 = jnp.bfloat16
VMEM_LIMIT = 56 * 1024 * 1024


def _rms(x, g):
    return x * lax.rsqrt(jnp.mean(x * x, axis=-1, keepdims=True) + EPS) * g


def _dot(a, b):
    return jnp.dot(a, b, preferred_element_type=F32)


def _dot_nt(a, b):
    return lax.dot_general(a, b, (((1,), (1,)), ((), ())), preferred_element_type=F32)


def _store_token_rows(ref, value, first_token=0):
    rows = value.shape[0]
    for c in range(D_MODEL // LANES):
        ref[pl.ds(first_token * SUBLANES + c, rows, stride=SUBLANES), :] = value[:, c * LANES:(c + 1) * LANES]


def _load_token_rows(ref, rows):
    return jnp.concatenate([ref[pl.ds(c, rows, stride=SUBLANES), :] for c in range(D_MODEL // LANES)], axis=1)


def _rope_table_kernel(pos_ref, invf_ref, out_ref):
    pos = pos_ref[...].astype(F32)
    ang = invf_ref[...] * pos
    c = jnp.cos(ang)
    s = jnp.sin(ang)
    planes = jnp.concatenate([c, c, -s, s, jnp.zeros((LANES - 4 * (QK_ROPE // 2), ang.shape[1]), F32)], axis=0)
    out_ref[...] = planes.T


def _rope_table(pos_row, inv_freq_col, chunk):
    n = pos_row.shape[1]
    return pl.pallas_call(
        _rope_table_kernel,
        grid=(n // chunk,),
        in_specs=[pl.BlockSpec((1, chunk), lambda i: (0, i)),
                  pl.BlockSpec((QK_ROPE // 2, 1), lambda i: (0, 0))],
        out_specs=pl.BlockSpec((chunk, LANES), lambda i: (i, 0)),
        out_shape=jax.ShapeDtypeStruct((n, LANES), F32),
        name="rope_table",
    )(pos_row, inv_freq_col)


def _premix_kernel(x_ref, tab_ref, carry_in_ref, g_pre_ref, w1_ref, gq_ref, gkv_ref, wqn_ref, bduk_ref,
                   wqr_ref, convw_ref, gconv_ref, qp_ref, kc_ref, convn_ref, utail_ref, carry_ref, wq_scr, *,
                   sub_blocks):
    t = pl.program_id(1)
    rows = x_ref.shape[1]

    @pl.when(t == 0)
    def _():
        carry_ref[...] = carry_in_ref[...]

    @pl.when((pl.program_id(0) == 0) & (t == 0))
    def _():
        wabs = jnp.dot(wqn_ref[...], bduk_ref[...], preferred_element_type=F32,
                       precision=lax.Precision.HIGHEST) * Q_SCALE
        wrope = wqr_ref[...] * Q_SCALE
        for h in range(N_HEADS):
            sl = slice(h * LANES, (h + 1) * LANES)
            wq_scr[:, h * KEY_WIDTH:h * KEY_WIDTH + LANES] = wabs[:, sl].astype(BF16)
            wq_scr[:, h * KEY_WIDTH + LANES:(h + 1) * KEY_WIDTH] = wrope[:, sl].astype(BF16)

    rows = rows // sub_blocks
    lane = lax.broadcasted_iota(jnp.int32, (rows, LANES), 1)
    row8 = lax.broadcasted_iota(jnp.int32, (8, CONV_DIM), 0)
    cw = convw_ref[...]
    prev = carry_ref[...]
    for sb in range(sub_blocks):
        r = pl.ds(sb * rows, rows)
        xn = _rms(x_ref[0, r, :], g_pre_ref[...]).astype(BF16)
        z = _dot(xn, w1_ref[...])
        tab = tab_ref[r, :]

        def rope(pair):
            prod = pair * tab
            return jnp.where(lane < QK_ROPE, prod + pltpu.roll(prod, LANES - QK_ROPE, axis=1), 0.0)

        ckvn = _rms(z[:, Q_LORA:Q_LORA + KV_LORA], gkv_ref[...])
        krope = rope(z[:, Q_LORA + KV_LORA:Q_LORA + KV_LORA + LANES])
        kc_ref[0, r, :] = jnp.concatenate([ckvn, krope], axis=1).astype(BF16)

        cqn = _rms(z[:, :Q_LORA], gq_ref[...]).astype(BF16)
        qall = _dot(cqn, wq_scr[...])
        for h in range(N_HEADS):
            qp_ref[0, h, r, 0:LANES] = qall[:, h * KEY_WIDTH:h * KEY_WIDTH + LANES].astype(BF16)
            qp_ref[0, h, r, LANES:KEY_WIDTH] = rope(
                qall[:, h * KEY_WIDTH + LANES:(h + 1) * KEY_WIDTH]).astype(BF16)

        c0 = Q_LORA + KV_LORA + LANES
        gate_b = z[:, c0:c0 + CONV_DIM]
        u = z[:, c0 + CONV_DIM:c0 + 2 * CONV_DIM] * z[:, c0 + 2 * CONV_DIM:c0 + 3 * CONV_DIM]
        r1 = pltpu.roll(u, 1, axis=0)
        r2 = pltpu.roll(u, 2, axis=0)
        p1 = pltpu.roll(prev, 1, axis=0)
        p2 = pltpu.roll(prev, 2, axis=0)
        u1 = jnp.concatenate([jnp.where(row8 < 1, p1, r1[0:8]), r1[8:]], axis=0)
        u2 = jnp.concatenate([jnp.where(row8 < 2, p2, r2[0:8]), r2[8:]], axis=0)
        y = cw[0:1] * u2 + cw[1:2] * u1 + cw[2:3] * u
        convn_ref[0, r, :] = _rms(gate_b * y, gconv_ref[...]).astype(BF16)
        prev = u[rows - 8:rows]
    carry_ref[...] = prev
    utail_ref[...] = prev


def _premix(x3, tab, tab_block0, carry_in, wts, rows, sub_blocks):
    bsz, seq, _ = x3.shape
    nt = seq // rows
    g_pre, w1, gq, gkv, wqn, bduk, wqr, convw, gconv = wts
    const = lambda a: pl.BlockSpec(a.shape, lambda b, t: (0,) * a.ndim)
    return pl.pallas_call(
        functools.partial(_premix_kernel, sub_blocks=sub_blocks),
        grid=(bsz, nt),
        in_specs=[pl.BlockSpec((1, rows, D_MODEL), lambda b, t: (b, t, 0)),
                  pl.BlockSpec((rows, LANES), lambda b, t: (tab_block0 + b * nt + t, 0)),
                  const(carry_in), const(g_pre), const(w1), const(gq), const(gkv), const(wqn), const(bduk),
                  const(wqr), const(convw), const(gconv)],
        out_specs=[pl.BlockSpec((1, N_HEADS, rows, KEY_WIDTH), lambda b, t: (b, 0, t, 0)),
                   pl.BlockSpec((1, rows, KEY_WIDTH), lambda b, t: (b, t, 0)),
                   pl.BlockSpec((1, rows, CONV_DIM), lambda b, t: (b, t, 0)),
                   pl.BlockSpec((8, CONV_DIM), lambda b, t: (0, 0))],
        out_shape=[jax.ShapeDtypeStruct((bsz, N_HEADS, seq, KEY_WIDTH), BF16),
                   jax.ShapeDtypeStruct((bsz, seq, KEY_WIDTH), BF16),
                   jax.ShapeDtypeStruct((bsz, seq, CONV_DIM), BF16),
                   jax.ShapeDtypeStruct((8, CONV_DIM), F32)],
        scratch_shapes=[pltpu.VMEM((8, CONV_DIM), F32), pltpu.VMEM((Q_LORA, N_HEADS * KEY_WIDTH), BF16)],
        compiler_params=pltpu.CompilerParams(dimension_semantics=("arbitrary", "arbitrary"),
                                             vmem_limit_bytes=VMEM_LIMIT),
        name="premix",
    )(x3, tab, carry_in, g_pre, w1, gq, gkv, wqn, bduk, wqr, convw, gconv)


def _attention_kernel(q_ref, qnext_ref, k_ref, km_ref, o_ref, s_scr, smeta_scr, tri_scr, m_scr, l_scr, acc_scr, *,
                      bq, bk, chunks):
    assert bq == bk
    b = pl.program_id(0)
    i = pl.program_id(1)
    assert chunks % N_HEADS == 0
    row_splits = chunks // N_HEADS
    c_rows = bq // row_splits
    km = km_ref[...]

    def chunk_rows(c):
        return c // row_splits, pl.ds((c % row_splits) * c_rows, c_rows)

    def scores(qr, c, kb):
        h, r = chunk_rows(c)
        return _dot_nt(qr[0, h, r, :], kb)

    def values(kb):
        return jnp.concatenate([kb[:, :KV_LORA], jnp.ones((kb.shape[0], LANES), BF16)], axis=1)

    def consume(c, s, v_ones, mask, first=False):
        if mask is not None:
            s = jnp.where(mask, s, NEG_INF)
        m_blk = jnp.broadcast_to(jnp.max(s, axis=1, keepdims=True), (c_rows, LANES))
        m_new = m_blk if first else jnp.maximum(m_scr[c], m_blk)
        p = jnp.exp2(s - jnp.concatenate([m_new] * (s.shape[1] // LANES), axis=1)).astype(BF16)
        pv = _dot(p, v_ones)
        if first:
            l_scr[c] = pv[:, KV_LORA:]
            acc_scr[c] = pv[:, :KV_LORA]
        else:
            alpha = jnp.exp2(m_scr[c] - m_new)
            l_scr[c] = alpha * l_scr[c] + pv[:, KV_LORA:]
            acc_scr[c] = alpha * acc_scr[c] + pv[:, :KV_LORA]
        m_scr[c] = m_new

    def key_block(j):
        return k_ref[0, pl.ds(pl.multiple_of(j * bk, bk), bk), :]

    @pl.when((b == 0) & (i == 0))
    def _():
        for c in range(chunks):
            smeta_scr[c] = scores(q_ref, c, km)
        rows_i = lax.broadcasted_iota(jnp.int32, (bq, bk), 0)
        cols_i = lax.broadcasted_iota(jnp.int32, (bq, bk), 1)
        tri_scr[...] = jnp.where(cols_i <= rows_i, 0.0, NEG_INF)

    n_full = (i * bq) // bk
    kb0 = key_block(0)
    v_meta = values(km)
    col = lax.broadcasted_iota(jnp.int32, (c_rows, BLOCK), 1)
    for c in range(chunks):
        s_scr[c] = scores(q_ref, c, kb0)
        consume(c, smeta_scr[c], v_meta, col >= PAD_FRONT, first=True)

    def full_block(j):
        v_ones = values(key_block(j))
        kb_next = key_block(j + 1)
        for c in range(chunks):
            s = s_scr[c]
            s_scr[c] = scores(q_ref, c, kb_next)
            consume(c, s, v_ones, None)

    def body(j, carry):
        full_block(j)
        return carry

    lax.fori_loop(0, n_full, body, 0)

    v_ones = values(key_block(n_full))
    for c in range(chunks):
        _, r = chunk_rows(c)
        n_keys = (c % row_splits + 1) * c_rows
        s = s_scr[c, :, :n_keys] + tri_scr[r, :n_keys]
        smeta_scr[c] = scores(qnext_ref, c, km)
        consume(c, s, v_ones[:n_keys], None)

    for c in range(chunks):
        h, r = chunk_rows(c)
        o_ref[0, r, h * KV_LORA:(h + 1) * KV_LORA] = (acc_scr[c] / l_scr[c]).astype(BF16)


def _attention(qp, kc, kmeta, bq, bk, chunks):
    bsz, _, seq, _ = qp.shape
    nq = seq // bq
    c_rows = N_HEADS * bq // chunks
    scratch = [pltpu.VMEM((chunks, c_rows, bk), F32), pltpu.VMEM((chunks, c_rows, BLOCK), F32),
               pltpu.VMEM((bq, bk), F32), pltpu.VMEM((chunks, c_rows, LANES), F32),
               pltpu.VMEM((chunks, c_rows, LANES), F32), pltpu.VMEM((chunks, c_rows, KV_LORA), F32)]

    def next_q(b, i):
        wrap = (i + 1) // nq
        return (jnp.minimum(b + wrap, bsz - 1), 0, i + 1 - wrap * nq, 0)

    return pl.pallas_call(
        functools.partial(_attention_kernel, bq=bq, bk=bk, chunks=chunks),
        grid=(bsz, nq),
        in_specs=[pl.BlockSpec((1, N_HEADS, bq, KEY_WIDTH), lambda b, i: (b, 0, i, 0)),
                  pl.BlockSpec((1, N_HEADS, bq, KEY_WIDTH), next_q),
                  pl.BlockSpec((1, seq, KEY_WIDTH), lambda b, i: (b, 0, 0)),
                  pl.BlockSpec((BLOCK, KEY_WIDTH), lambda b, i: (0, 0))],
        out_specs=pl.BlockSpec((1, bq, N_HEADS * KV_LORA), lambda b, i: (b, i, 0)),
        out_shape=jax.ShapeDtypeStruct((bsz, seq, N_HEADS * KV_LORA), BF16),
        scratch_shapes=scratch,
        compiler_params=pltpu.CompilerParams(dimension_semantics=("arbitrary", "arbitrary"),
                                             vmem_limit_bytes=VMEM_LIMIT),
        name="attention",
    )(qp, qp, kc, kmeta)


def _postmix_kernel(x_ref, olat_ref, convn_ref, wuv_ref, gattn_ref, wout_ref, gpost_ref, gffn_ref, wr_ref, br_ref,
                    h1_ref, t_ref, trow_ref, oh1_ref, oh2_ref, wts_ref, *, sub_blocks):
    rows = x_ref.shape[0] // sub_blocks
    wout = wout_ref[...]
    lane = lax.broadcasted_iota(jnp.int32, (rows, LANES), 1)

    def first_argmax(v):
        mx = jnp.max(v, axis=1, keepdims=True)
        return mx, jnp.min(jnp.where(v == mx, lane, LANES), axis=1, keepdims=True)

    for sb in range(sub_blocks):
        r = pl.ds(sb * rows, rows)
        attn = _dot(olat_ref[r, :], wuv_ref[...])
        attn_n = _rms(attn, gattn_ref[...]).astype(BF16)
        mix = _dot(attn_n, wout[:ATTN_WIDTH]) + _dot(convn_ref[r, :], wout[ATTN_WIDTH:])
        h1 = x_ref[r, :] + _rms(mix, gpost_ref[...])
        h1_ref[r, :] = h1
        tf = _rms(h1, gffn_ref[...])
        t = tf.astype(BF16)
        t_ref[r, :] = t
        _store_token_rows(trow_ref, tf, sb * rows)

        logits = _dot(t, wr_ref[...]) + br_ref[...]
        gl = jnp.where(lane < N_GROUPS, logits[:, :LANES], NEG_INF)
        gmax, gidx = first_argmax(gl)
        g_w = 1.0 / jnp.sum(jnp.exp(gl - gmax), axis=1, keepdims=True)
        el = jnp.where((lane >> 3) == gidx, logits[:, LANES:], NEG_INF)
        m1, i1 = first_argmax(el)
        m2, i2 = first_argmax(jnp.where(lane == i1, NEG_INF, el))
        e2 = jnp.exp(m2 - m1)
        den = 1.0 + e2
        oh1_ref[r, :] = jnp.where(lane == i1, 1.0, 0.0).astype(BF16)
        oh2_ref[r, :] = jnp.where(lane == i2, 1.0, 0.0).astype(BF16)
        wts_ref[r, :] = jnp.where(lane == 0, g_w / den, jnp.where(lane == 1, g_w * e2 / den, 0.0))


def _postmix(x2, olat2, convn2, wts, rows, sub_blocks):
    n = x2.shape[0]
    wuv, gattn, wout, gpost, gffn, wr, br = wts
    const = lambda a: pl.BlockSpec(a.shape, lambda i: (0,) * a.ndim)
    tile = lambda w: pl.BlockSpec((rows, w), lambda i: (i, 0))
    return pl.pallas_call(
        functools.partial(_postmix_kernel, sub_blocks=sub_blocks),
        grid=(n // rows,),
        in_specs=[tile(D_MODEL), tile(N_HEADS * KV_LORA), tile(CONV_DIM),
                  const(wuv), const(gattn), const(wout), const(gpost), const(gffn), const(wr), const(br)],
        out_specs=[tile(D_MODEL), tile(D_MODEL), pl.BlockSpec((rows * SUBLANES, LANES), lambda i: (i, 0)),
                   tile(LANES), tile(LANES), tile(LANES)],
        out_shape=[jax.ShapeDtypeStruct((n, D_MODEL), F32),
                   jax.ShapeDtypeStruct((n, D_MODEL), BF16),
                   jax.ShapeDtypeStruct((n * SUBLANES, LANES), F32),
                   jax.ShapeDtypeStruct((n, LANES), BF16),
                   jax.ShapeDtypeStruct((n, LANES), BF16),
                   jax.ShapeDtypeStruct((n, LANES), F32)],
        compiler_params=pltpu.CompilerParams(dimension_semantics=("arbitrary",), vmem_limit_bytes=VMEM_LIMIT),
        name="postmix",
    )(x2, olat2, convn2, wuv, gattn, wout, gpost, gffn, wr, br)


def _silu(a):
    return a * (1.0 / (1.0 + jnp.exp(-a)))


def _plan_kernel(oh1_ref, oh2_ref, pos_ref, te_ref, misc_ref, *, chunk):
    n_tok = oh1_ref.shape[0]
    n_chunks = n_tok // chunk
    lane = lax.broadcasted_iota(jnp.int32, (SUBLANES, LANES), 1)

    def rows_of(ref, c):
        return ref[pl.ds(pl.multiple_of(c * chunk, chunk), chunk), :]

    def count_body(c, acc):
        both = rows_of(oh1_ref, c).astype(F32) + rows_of(oh2_ref, c).astype(F32)
        return acc + jnp.sum(both, axis=0, keepdims=True)

    counts = lax.fori_loop(0, n_chunks, count_body, jnp.zeros((1, LANES), F32))
    counts = jnp.broadcast_to(counts, (SUBLANES, LANES)).astype(jnp.int32)
    tile_shift = EXPERT_TILE.bit_length() - 1
    padded = ((counts + (EXPERT_TILE - 1)) >> tile_shift) << tile_shift
    ends = padded
    shift = 1
    while shift < N_EXPERTS:
        ends = ends + jnp.where(lane >= shift, pltpu.roll(ends, shift, axis=1), 0)
        shift *= 2
    base = (ends - padded)[0:1].astype(F32)

    r_iota = lax.broadcasted_iota(jnp.int32, (chunk, chunk), 0)
    c_iota = lax.broadcasted_iota(jnp.int32, (chunk, chunk), 1)
    earlier = jnp.where(c_iota < r_iota, 1.0, 0.0).astype(BF16)

    def rank_pass(ref, slot, run):
        def body(c, run):
            oh = rows_of(ref, c)
            ohf = oh.astype(F32)
            before = _dot(earlier, oh) + run
            posv = jnp.sum(ohf * before, axis=1, keepdims=True)
            pos_row = jnp.broadcast_to(posv, (chunk, LANES)).T[0:1, :]
            pos_ref[pl.ds(slot * n_chunks + c, 1), :] = pos_row.astype(jnp.int32)
            return run + jnp.sum(ohf, axis=0, keepdims=True)
        return lax.fori_loop(0, n_chunks, body, run)

    rank_pass(oh2_ref, 1, rank_pass(oh1_ref, 0, base))

    n_tile_lanes = te_ref.shape[1]
    tile_start = lax.broadcasted_iota(jnp.int32, (SUBLANES, n_tile_lanes), 1) * EXPERT_TILE
    te = jnp.zeros((SUBLANES, n_tile_lanes), jnp.int32)
    for e in range(N_EXPERTS):
        te = te + jnp.where(ends[:, e:e + 1] <= tile_start, 1, 0)
    te_ref[...] = jnp.minimum(te, N_EXPERTS - 1)
    used_tiles = ends[:, N_EXPERTS - 1:N_EXPERTS] >> tile_shift
    misc_ref[...] = jnp.where(lane < N_EXPERTS, jnp.where(padded > 0, ends - EXPERT_TILE, -1), used_tiles)


def _plan(oh1, oh2, n_tiles, chunk):
    n = oh1.shape[0]
    n_tile_lanes = -(-n_tiles // LANES) * LANES
    whole = lambda shape: pl.BlockSpec(shape, lambda i: (0, 0))
    return pl.pallas_call(
        functools.partial(_plan_kernel, chunk=chunk),
        grid=(1,),
        in_specs=[whole((n, LANES)), whole((n, LANES))],
        out_specs=[whole((TOP_K * n // chunk, chunk)), whole((SUBLANES, n_tile_lanes)), whole((SUBLANES, LANES))],
        out_shape=[jax.ShapeDtypeStruct((TOP_K * n // chunk, chunk), jnp.int32),
                   jax.ShapeDtypeStruct((SUBLANES, n_tile_lanes), jnp.int32),
                   jax.ShapeDtypeStruct((SUBLANES, LANES), jnp.int32)],
        compiler_params=pltpu.CompilerParams(dimension_semantics=("arbitrary",), vmem_limit_bytes=VMEM_LIMIT),
        name="moe_plan",
    )(oh1, oh2)


def _row_slice(ref, row, n_rows=1):
    start = row * SUBLANES if isinstance(row, int) else pl.multiple_of(row * SUBLANES, SUBLANES)
    return ref.at[pl.ds(start, n_rows * SUBLANES)]


def _dispatch_kernel(zstart_ref, pos_ref, trow_ref, xs_hbm, zero_buf, zero_sem, row_sem, *, tokens):
    i = pl.program_id(0)

    @pl.when(i == 0)
    def _():
        zero_buf[...] = jnp.zeros(zero_buf.shape, F32)
        n_tiles = xs_hbm.shape[0] // (EXPERT_TILE * SUBLANES)
        used = zstart_ref[N_EXPERTS]

        def zero_tile(first_row):
            return pltpu.make_async_copy(zero_buf, _row_slice(xs_hbm, first_row, EXPERT_TILE), zero_sem)

        for phase in ("start", "wait"):
            for e in range(N_EXPERTS):
                @pl.when(zstart_ref[e] >= 0)
                def _():
                    copy = zero_tile(zstart_ref[e])
                    copy.start() if phase == "start" else copy.wait()

            def tail(t, carry):
                copy = zero_tile(t * EXPERT_TILE)
                copy.start() if phase == "start" else copy.wait()
                return carry

            lax.fori_loop(used, n_tiles, tail, 0)

    unroll = 8

    def issue(g, carry):
        for u in range(unroll):
            n = g * unroll + u
            src = _row_slice(trow_ref, n)
            for k in range(TOP_K):
                pltpu.make_async_copy(src, _row_slice(xs_hbm, pos_ref[k, n]), row_sem).start(priority=(u + k) % 2)
        return carry

    lax.fori_loop(0, tokens // unroll, issue, 0)
    for k in range(TOP_K):
        pltpu.make_async_copy(trow_ref, _row_slice(xs_hbm, 0, tokens), row_sem).wait()


def _dispatch(zstart, pos, trow, sorted_rows, tokens):
    n = pos.shape[1]
    return pl.pallas_call(
        functools.partial(_dispatch_kernel, tokens=tokens),
        grid=(n // tokens,),
        in_specs=[pl.BlockSpec(memory_space=pltpu.SMEM),
                  pl.BlockSpec((TOP_K, tokens), lambda i: (0, i), memory_space=pltpu.SMEM),
                  pl.BlockSpec((tokens * SUBLANES, LANES), lambda i: (i, 0))],
        out_specs=pl.BlockSpec(memory_space=pl.ANY),
        out_shape=jax.ShapeDtypeStruct((sorted_rows * SUBLANES, LANES), F32),
        scratch_shapes=[pltpu.VMEM((EXPERT_TILE * SUBLANES, LANES), F32), pltpu.SemaphoreType.DMA,
                        pltpu.SemaphoreType.DMA],
        compiler_params=pltpu.CompilerParams(dimension_semantics=("arbitrary",)),
        name="moe_dispatch",
    )(zstart, pos, trow)


def _experts_kernel(te_ref, used_ref, xs_ref, wg_ref, wu_ref, wd_ref, ys_ref):
    t = pl.program_id(0)

    @pl.when(t < used_ref[0])
    def _():
        x = _load_token_rows(xs_ref, EXPERT_TILE).astype(BF16)
        hid = _silu(_dot(x, wg_ref[0].astype(BF16))) * _dot(x, wu_ref[0].astype(BF16))
        _store_token_rows(ys_ref, _dot(hid.astype(BF16), wd_ref[0].astype(BF16)))

    @pl.when(t >= used_ref[0])
    def _():
        ys_ref[...] = jnp.zeros(ys_ref.shape, F32)


def _experts(te, used, xs, wg, wu, wd, n_tiles):
    blk = EXPERT_TILE * SUBLANES
    return pl.pallas_call(
        _experts_kernel,
        grid_spec=pltpu.PrefetchScalarGridSpec(
            num_scalar_prefetch=2,
            grid=(n_tiles,),
            in_specs=[pl.BlockSpec((blk, LANES), lambda t, te, used: (jnp.minimum(t, used[0] - 1), 0)),
                      pl.BlockSpec((1, D_MODEL, D_FF), lambda t, te, used: (te[t], 0, 0)),
                      pl.BlockSpec((1, D_MODEL, D_FF), lambda t, te, used: (te[t], 0, 0)),
                      pl.BlockSpec((1, D_FF, D_MODEL), lambda t, te, used: (te[t], 0, 0))],
            out_specs=pl.BlockSpec((blk, LANES), lambda t, te, used: (t, 0))),
        out_shape=jax.ShapeDtypeStruct(xs.shape, F32),
        compiler_params=pltpu.CompilerParams(dimension_semantics=("arbitrary",), vmem_limit_bytes=VMEM_LIMIT),
        name="moe_experts",
    )(te, used, xs, wg, wu, wd)


def _combine_kernel(pos_ref, pos_next_ref, t_ref, h1_ref, wts_ref, ys_hbm, wsg_ref, wsu_ref, wsd_ref, gout_ref,
                    out_ref, ybuf, sems, *, tokens):
    i = pl.program_id(0)
    slot = lax.rem(i, 2)
    unroll = 8

    def start_gathers(p_ref, dst_slot):
        def issue(g, carry):
            for u in range(unroll):
                n = g * unroll + u
                for k in range(TOP_K):
                    pltpu.make_async_copy(_row_slice(ys_hbm, p_ref[k, n]), _row_slice(ybuf.at[dst_slot].at[k], n),
                                          sems.at[dst_slot]).start(priority=(u + k) % 2)
            return carry

        lax.fori_loop(0, tokens // unroll, issue, 0)

    @pl.when(i == 0)
    def _():
        start_gathers(pos_ref, slot)

    @pl.when(i + 1 < pl.num_programs(0))
    def _():
        start_gathers(pos_next_ref, 1 - slot)

    t = t_ref[...]
    hid = _silu(_dot(t, wsg_ref[...])) * _dot(t, wsu_ref[...])
    shared = _dot(hid.astype(BF16), wsd_ref[...])

    for k in range(TOP_K):
        pltpu.make_async_copy(_row_slice(ys_hbm, 0, tokens), ybuf.at[slot].at[k], sems.at[slot]).wait()
    wts = wts_ref[...]
    routed = (wts[:, 0:1] * _load_token_rows(ybuf.at[slot].at[0], tokens)
              + wts[:, 1:2] * _load_token_rows(ybuf.at[slot].at[1], tokens))
    out_ref[...] = h1_ref[...] + _rms(routed + shared, gout_ref[...])


def _combine(pos, t2, h1, wts, ys, shared_w, tokens):
    n = t2.shape[0]
    wsg, wsu, wsd, gout = shared_w
    const = lambda a: pl.BlockSpec(a.shape, lambda i: (0,) * a.ndim)
    tile = lambda w: pl.BlockSpec((tokens, w), lambda i: (i, 0))
    steps = n // tokens
    return pl.pallas_call(
        functools.partial(_combine_kernel, tokens=tokens),
        grid=(steps,),
        in_specs=[pl.BlockSpec((TOP_K, tokens), lambda i: (0, i), memory_space=pltpu.SMEM),
                  pl.BlockSpec((TOP_K, tokens), lambda i: (0, jnp.minimum(i + 1, steps - 1)),
                               memory_space=pltpu.SMEM),
                  tile(D_MODEL), tile(D_MODEL), tile(LANES), pl.BlockSpec(memory_space=pl.ANY),
                  const(wsg), const(wsu), const(wsd), const(gout)],
        out_specs=tile(D_MODEL),
        out_shape=jax.ShapeDtypeStruct((n, D_MODEL), F32),
        scratch_shapes=[pltpu.VMEM((2, TOP_K, tokens * SUBLANES, LANES), F32), pltpu.SemaphoreType.DMA((2,))],
        compiler_params=pltpu.CompilerParams(dimension_semantics=("arbitrary",), vmem_limit_bytes=VMEM_LIMIT),
        name="moe_combine",
    )(pos, pos, t2, h1, wts, ys, wsg, wsu, wsd, gout)


def _swap_halves(w):
    half = w.shape[-1] // 2
    return jnp.concatenate([w[..., half:], w[..., :half]], axis=-1)


def _block_diag(blocks):
    h, r, c = blocks.shape
    eye = jnp.eye(h, dtype=blocks.dtype)
    return jnp.einsum('hrc,hg->hrgc', blocks, eye).reshape(h * r, h * c)


def kernel(x, positions, meta_tokens, pre_mix_norm, w_in, q_norm, w_uq, kv_norm, w_ukv, conv_w, attn_out_norm,
           conv_out_norm, w_out, post_mix_norm, pre_ffn_norm, w_group_router, b_group_router, w_expert_router,
           b_expert_router, w_gate, w_up, w_down, w_sh_gate, w_sh_up, w_sh_down, post_ffn_norm):
    bsz, seq, d = x.shape
    n = bsz * seq
    row = lambda v: v.reshape(1, -1).astype(F32)

    wi = w_in[0]
    o_kv, o_pe, o_b = Q_LORA, Q_LORA + KV_LORA, Q_LORA + KV_LORA + QK_ROPE
    k_pe = wi[:, o_pe:o_b]
    w1 = jnp.concatenate([wi[:, :o_pe], k_pe, _swap_halves(k_pe), jnp.zeros((d, LANES - 2 * QK_ROPE), F32),
                          wi[:, o_b:]], axis=1).astype(BF16)
    wq = w_uq[0].reshape(Q_LORA, N_HEADS, QK_NOPE + QK_ROPE)
    wqn = wq[:, :, :QK_NOPE].reshape(Q_LORA, N_HEADS * QK_NOPE).astype(F32)
    wq_rope = wq[:, :, QK_NOPE:]
    wqr = jnp.concatenate([wq_rope, _swap_halves(wq_rope),
                           jnp.zeros((Q_LORA, N_HEADS, LANES - 2 * QK_ROPE), F32)], axis=2)
    wqr = wqr.reshape(Q_LORA, N_HEADS * LANES).astype(F32)
    wkv = w_ukv[0].reshape(KV_LORA, N_HEADS, QK_NOPE + V_DIM)
    bduk = _block_diag(jnp.transpose(wkv[:, :, :QK_NOPE], (1, 2, 0))).astype(F32)
    wuv = _block_diag(jnp.transpose(wkv[:, :, QK_NOPE:], (1, 0, 2))).astype(BF16)
    premix_w = (row(pre_mix_norm[0]), w1, row(q_norm[0]), row(kv_norm[0]), wqn, bduk, wqr,
                conv_w[0].astype(F32), row(conv_out_norm[0]))
    wr = jnp.concatenate([w_group_router[0], jnp.zeros((d, LANES - N_GROUPS), F32),
                          w_expert_router[0], jnp.zeros((d, LANES - N_EXPERTS), F32)], axis=1).astype(BF16)
    br = jnp.concatenate([b_group_router[0], jnp.zeros((LANES - N_GROUPS,), F32),
                          b_expert_router[0], jnp.zeros((LANES - N_EXPERTS,), F32)]).reshape(1, -1).astype(F32)
    postmix_w = (wuv, row(attn_out_norm[0]), w_out[0].astype(BF16), row(post_mix_norm[0]), row(pre_ffn_norm[0]),
                 wr, br)
    expert_w = (w_gate[0], w_up[0], w_down[0])
    shared_w = (w_sh_gate[0].astype(BF16), w_sh_up[0].astype(BF16), w_sh_down[0].astype(BF16),
                row(post_ffn_norm[0]))

    pos_all = jnp.concatenate([positions.astype(jnp.int32).reshape(-1) + N_META,
                               jnp.zeros((PAD_FRONT,), jnp.int32), jnp.arange(N_META, dtype=jnp.int32)])
    inv_freq = 1.0 / (ROPE_THETA ** (jnp.arange(0, QK_ROPE, 2, dtype=F32) / QK_ROPE))
    n_tab = n + BLOCK
    tab = _rope_table(pos_all.reshape(1, n_tab), inv_freq.reshape(-1, 1), n_tab // 3)

    meta_blk = jnp.pad(meta_tokens.astype(F32), ((PAD_FRONT, 0), (0, 0)))[None]
    zero_carry = jnp.zeros((8, CONV_DIM), F32)
    _, kmeta, _, meta_tail = _premix(meta_blk, tab, n // BLOCK, zero_carry, premix_w, BLOCK, 1)
    qp, kc, convn, _ = _premix(x, tab, 0, meta_tail, premix_w, 1024, 2)

    olat = _attention(qp, kc, kmeta[0], 512, 512, 16)

    h1, t, trow, oh1, oh2, wts = _postmix(x.reshape(n, d), olat.reshape(n, -1), convn.reshape(n, -1), postmix_w, 1024, 2)

    sorted_rows = TOP_K * n + N_EXPERTS * EXPERT_TILE
    n_tiles = sorted_rows // EXPERT_TILE
    pos2, te8, misc = _plan(oh1, oh2, n_tiles, 512)
    pos = pos2.reshape(TOP_K, n)
    xs = _dispatch(misc[0, :N_EXPERTS + 1], pos, trow, sorted_rows, 1024)
    ys = _experts(te8[0, :n_tiles], misc[0, N_EXPERTS:N_EXPERTS + 1], xs, *expert_w, n_tiles)
    out = _combine(pos, t, h1, wts, ys, shared_w, 512)
    return out.reshape(bsz, seq, d)
```

```python
import functools
import math

import jax
import jax.numpy as jnp
from jax import lax
from jax.experimental import pallas as pl
from jax.experimental.pallas import tpu as pltpu

D_MODEL = 1024
N_META = 16
BLOCK = 128
PAD_FRONT = BLOCK - N_META
N_HEADS = 8
QK_NOPE = 64
QK_ROPE = 32
V_DIM = 64
Q_LORA = 256
KV_LORA = 128
ROPE_THETA = 10000.0
ATTN_SCALE = (QK_NOPE + QK_ROPE) ** -0.5
Q_SCALE = ATTN_SCALE * math.log2(math.e)
ATTN_WIDTH = N_HEADS * V_DIM
CONV_DIM = 512
CONV_W = 3
N_GROUPS = 4
EXPERTS_PER_GROUP = 8
N_EXPERTS = N_GROUPS * EXPERTS_PER_GROUP
D_FF = 256
EPS = 1e-6
NEG_INF = -1e30

LANES = 128
SUBLANES = 8
TOP_K = 2
EXPERT_TILE = 512
KEY_WIDTH = 2 * LANES
IN_COLS_PADDED = Q_LORA + KV_LORA + LANES + 3 * CONV_DIM

F32 = jnp.float32
BF16 = jnp.bfloat16
VMEM_LIMIT = 56 * 1024 * 1024


def _rms(x, g):
    return x * lax.rsqrt(jnp.mean(x * x, axis=-1, keepdims=True) + EPS) * g


def _dot(a, b):
    return jnp.dot(a, b, preferred_element_type=F32)


def _dot_nt(a, b):
    return lax.dot_general(a, b, (((1,), (1,)), ((), ())), preferred_element_type=F32)


def _store_token_rows(ref, value, first_token=0):
    rows = value.shape[0]
    for c in range(D_MODEL // LANES):
        ref[pl.ds(first_token * SUBLANES + c, rows, stride=SUBLANES), :] = value[:, c * LANES:(c + 1) * LANES]


def _load_token_rows(ref, rows):
    return jnp.concatenate([ref[pl.ds(c, rows, stride=SUBLANES), :] for c in range(D_MODEL // LANES)], axis=1)


def _rope_table_kernel(pos_ref, invf_ref, out_ref):
    pos = pos_ref[...].astype(F32)
    ang = invf_ref[...] * pos
    c = jnp.cos(ang)
    s = jnp.sin(ang)
    planes = jnp.concatenate([c, c, -s, s, jnp.zeros((LANES - 4 * (QK_ROPE // 2), ang.shape[1]), F32)], axis=0)
    out_ref[...] = planes.T


def _rope_table(pos_row, inv_freq_col, chunk):
    n = pos_row.shape[1]
    return pl.pallas_call(
        _rope_table_kernel,
        grid=(n // chunk,),
        in_specs=[pl.BlockSpec((1, chunk), lambda i: (0, i)),
                  pl.BlockSpec((QK_ROPE // 2, 1), lambda i: (0, 0))],
        out_specs=pl.BlockSpec((chunk, LANES), lambda i: (i, 0)),
        out_shape=jax.ShapeDtypeStruct((n, LANES), F32),
        name="rope_table",
    )(pos_row, inv_freq_col)


def _premix_kernel(x_ref, tab_ref, carry_in_ref, g_pre_ref, w1_ref, gq_ref, gkv_ref, wqn_ref, bduk_ref,
                   wqr_ref, convw_ref, gconv_ref, qp_ref, kc_ref, convn_ref, utail_ref, carry_ref, wq_scr, *,
                   sub_blocks):
    t = pl.program_id(1)
    rows = x_ref.shape[1]

    @pl.when(t == 0)
    def _():
        carry_ref[...] = carry_in_ref[...]

    @pl.when((pl.program_id(0) == 0) & (t == 0))
    def _():
        wabs = jnp.dot(wqn_ref[...], bduk_ref[...], preferred_element_type=F32,
                       precision=lax.Precision.HIGHEST) * Q_SCALE
        wrope = wqr_ref[...] * Q_SCALE
        for h in range(N_HEADS):
            sl = slice(h * LANES, (h + 1) * LANES)
            wq_scr[:, h * KEY_WIDTH:h * KEY_WIDTH + LANES] = wabs[:, sl].astype(BF16)
            wq_scr[:, h * KEY_WIDTH + LANES:(h + 1) * KEY_WIDTH] = wrope[:, sl].astype(BF16)

    rows = rows // sub_blocks
    lane = lax.broadcasted_iota(jnp.int32, (rows, LANES), 1)
    row8 = lax.broadcasted_iota(jnp.int32, (8, CONV_DIM), 0)
    cw = convw_ref[...]
    prev = carry_ref[...]
    for sb in range(sub_blocks):
        r = pl.ds(sb * rows, rows)
        xn = _rms(x_ref[0, r, :], g_pre_ref[...]).astype(BF16)
        z = _dot(xn, w1_ref[...])
        tab = tab_ref[r, :]

        def rope(pair):
            prod = pair * tab
            return jnp.where(lane < QK_ROPE, prod + pltpu.roll(prod, LANES - QK_ROPE, axis=1), 0.0)

        ckvn = _rms(z[:, Q_LORA:Q_LORA + KV_LORA], gkv_ref[...])
        krope = rope(z[:, Q_LORA + KV_LORA:Q_LORA + KV_LORA + LANES])
        kc_ref[0, r, :] = jnp.concatenate([ckvn, krope], axis=1).astype(BF16)

        cqn = _rms(z[:, :Q_LORA], gq_ref[...]).astype(BF16)
        qall = _dot(cqn, wq_scr[...])
        for h in range(N_HEADS):
            qp_ref[0, h, r, 0:LANES] = qall[:, h * KEY_WIDTH:h * KEY_WIDTH + LANES].astype(BF16)
            qp_ref[0, h, r, LANES:KEY_WIDTH] = rope(
                qall[:, h * KEY_WIDTH + LANES:(h + 1) * KEY_WIDTH]).astype(BF16)

        c0 = Q_LORA + KV_LORA + LANES
        gate_b = z[:, c0:c0 + CONV_DIM]
        u = z[:, c0 + CONV_DIM:c0 + 2 * CONV_DIM] * z[:, c0 + 2 * CONV_DIM:c0 + 3 * CONV_DIM]
        r1 = pltpu.roll(u, 1, axis=0)
        r2 = pltpu.roll(u, 2, axis=0)
        p1 = pltpu.roll(prev, 1, axis=0)
        p2 = pltpu.roll(prev, 2, axis=0)
        u1 = jnp.concatenate([jnp.where(row8 < 1, p1, r1[0:8]), r1[8:]], axis=0)
        u2 = jnp.concatenate([jnp.where(row8 < 2, p2, r2[0:8]), r2[8:]], axis=0)
        y = cw[0:1] * u2 + cw[1:2] * u1 + cw[2:3] * u
        convn_ref[0, r, :] = _rms(gate_b * y, gconv_ref[...]).astype(BF16)
        prev = u[rows - 8:rows]
    carry_ref[...] = prev
    utail_ref[...] = prev


def _premix(x3, tab, tab_block0, carry_in, wts, rows, sub_blocks):
    bsz, seq, _ = x3.shape
    nt = seq // rows
    g_pre, w1, gq, gkv, wqn, bduk, wqr, convw, gconv = wts
    const = lambda a: pl.BlockSpec(a.shape, lambda b, t: (0,) * a.ndim)
    return pl.pallas_call(
        functools.partial(_premix_kernel, sub_blocks=sub_blocks),
        grid=(bsz, nt),
        in_specs=[pl.BlockSpec((1, rows, D_MODEL), lambda b, t: (b, t, 0)),
                  pl.BlockSpec((rows, LANES), lambda b, t: (tab_block0 + b * nt + t, 0)),
                  const(carry_in), const(g_pre), const(w1), const(gq), const(gkv), const(wqn), const(bduk),
                  const(wqr), const(convw), const(gconv)],
        out_specs=[pl.BlockSpec((1, N_HEADS, rows, KEY_WIDTH), lambda b, t: (b, 0, t, 0)),
                   pl.BlockSpec((1, rows, KEY_WIDTH), lambda b, t: (b, t, 0)),
                   pl.BlockSpec((1, rows, CONV_DIM), lambda b, t: (b, t, 0)),
                   pl.BlockSpec((8, CONV_DIM), lambda b, t: (0, 0))],
        out_shape=[jax.ShapeDtypeStruct((bsz, N_HEADS, seq, KEY_WIDTH), BF16),
                   jax.ShapeDtypeStruct((bsz, seq, KEY_WIDTH), BF16),
                   jax.ShapeDtypeStruct((bsz, seq, CONV_DIM), BF16),
                   jax.ShapeDtypeStruct((8, CONV_DIM), F32)],
        scratch_shapes=[pltpu.VMEM((8, CONV_DIM), F32), pltpu.VMEM((Q_LORA, N_HEADS * KEY_WIDTH), BF16)],
        compiler_params=pltpu.CompilerParams(dimension_semantics=("arbitrary", "arbitrary"),
                                             vmem_limit_bytes=VMEM_LIMIT),
        name="premix",
    )(x3, tab, carry_in, g_pre, w1, gq, gkv, wqn, bduk, wqr, convw, gconv)


def _attention_kernel(q_ref, qnext_ref, k_ref, km_ref, o_ref, s_scr, smeta_scr, tri_scr, m_scr, l_scr, acc_scr, *,
                      bq, bk, chunks):
    assert bq == bk
    b = pl.program_id(0)
    i = pl.program_id(1)
    assert chunks % N_HEADS == 0
    row_splits = chunks // N_HEADS
    c_rows = bq // row_splits
    km = km_ref[...]

    def chunk_rows(c):
        return c // row_splits, pl.ds((c % row_splits) * c_rows, c_rows)

    def scores(qr, c, kb):
        h, r = chunk_rows(c)
        return _dot_nt(qr[0, h, r, :], kb)

    def values(kb):
        return jnp.concatenate([kb[:, :KV_LORA], jnp.ones((kb.shape[0], LANES), BF16)], axis=1)

    def consume(c, s, v_ones, mask, first=False):
        if mask is not None:
            s = jnp.where(mask, s, NEG_INF)
        m_blk = jnp.broadcast_to(jnp.max(s, axis=1, keepdims=True), (c_rows, LANES))
        m_new = m_blk if first else jnp.maximum(m_scr[c], m_blk)
        p = jnp.exp2(s - jnp.concatenate([m_new] * (s.shape[1] // LANES), axis=1)).astype(BF16)
        pv = _dot(p, v_ones)
        if first:
            l_scr[c] = pv[:, KV_LORA:]
            acc_scr[c] = pv[:, :KV_LORA]
        else:
            alpha = jnp.exp2(m_scr[c] - m_new)
            l_scr[c] = alpha * l_scr[c] + pv[:, KV_LORA:]
            acc_scr[c] = alpha * acc_scr[c] + pv[:, :KV_LORA]
        m_scr[c] = m_new

    def key_block(j):
        return k_ref[0, pl.ds(pl.multiple_of(j * bk, bk), bk), :]

    @pl.when((b == 0) & (i == 0))
    def _():
        for c in range(chunks):
            smeta_scr[c] = scores(q_ref, c, km)
        rows_i = lax.broadcasted_iota(jnp.int32, (bq, bk), 0)
        cols_i = lax.broadcasted_iota(jnp.int32, (bq, bk), 1)
        tri_scr[...] = jnp.where(cols_i <= rows_i, 0.0, NEG_INF)

    n_full = (i * bq) // bk
    kb0 = key_block(0)
    v_meta = values(km)
    col = lax.broadcasted_iota(jnp.int32, (c_rows, BLOCK), 1)
    for c in range(chunks):
        s_scr[c] = scores(q_ref, c, kb0)
        consume(c, smeta_scr[c], v_meta, col >= PAD_FRONT, first=True)

    def full_block(j):
        v_ones = values(key_block(j))
        kb_next = key_block(j + 1)
        for c in range(chunks):
            s = s_scr[c]
            s_scr[c] = scores(q_ref, c, kb_next)
            consume(c, s, v_ones, None)

    def body(j, carry):
        full_block(j)
        return carry

    lax.fori_loop(0, n_full, body, 0)

    v_ones = values(key_block(n_full))
    for c in range(chunks):
        _, r = chunk_rows(c)
        n_keys = (c % row_splits + 1) * c_rows
        s = s_scr[c, :, :n_keys] + tri_scr[r, :n_keys]
        smeta_scr[c] = scores(qnext_ref, c, km)
        consume(c, s, v_ones[:n_keys], None)

    for c in range(chunks):
        h, r = chunk_rows(c)
        o_ref[0, r, h * KV_LORA:(h + 1) * KV_LORA] = (acc_scr[c] / l_scr[c]).astype(BF16)


def _attention(qp, kc, kmeta, bq, bk, chunks):
    bsz, _, seq, _ = qp.shape
    nq = seq // bq
    c_rows = N_HEADS * bq // chunks
    scratch = [pltpu.VMEM((chunks, c_rows, bk), F32), pltpu.VMEM((chunks, c_rows, BLOCK), F32),
               pltpu.VMEM((bq, bk), F32), pltpu.VMEM((chunks, c_rows, LANES), F32),
               pltpu.VMEM((chunks, c_rows, LANES), F32), pltpu.VMEM((chunks, c_rows, KV_LORA), F32)]

    def next_q(b, i):
        wrap = (i + 1) // nq
        return (jnp.minimum(b + wrap, bsz - 1), 0, i + 1 - wrap * nq, 0)

    return pl.pallas_call(
        functools.partial(_attention_kernel, bq=bq, bk=bk, chunks=chunks),
        grid=(bsz, nq),
        in_specs=[pl.BlockSpec((1, N_HEADS, bq, KEY_WIDTH), lambda b, i: (b, 0, i, 0)),
                  pl.BlockSpec((1, N_HEADS, bq, KEY_WIDTH), next_q),
                  pl.BlockSpec((1, seq, KEY_WIDTH), lambda b, i: (b, 0, 0)),
                  pl.BlockSpec((BLOCK, KEY_WIDTH), lambda b, i: (0, 0))],
        out_specs=pl.BlockSpec((1, bq, N_HEADS * KV_LORA), lambda b, i: (b, i, 0)),
        out_shape=jax.ShapeDtypeStruct((bsz, seq, N_HEADS * KV_LORA), BF16),
        scratch_shapes=scratch,
        compiler_params=pltpu.CompilerParams(dimension_semantics=("arbitrary", "arbitrary"),
                                             vmem_limit_bytes=VMEM_LIMIT),
        name="attention",
    )(qp, qp, kc, kmeta)


def _postmix_kernel(x_ref, olat_ref, convn_ref, wuv_ref, gattn_ref, wout_ref, gpost_ref, gffn_ref, wr_ref, br_ref,
                    h1_ref, trow_ref, oh1_ref, oh2_ref, wts_ref, *, sub_blocks):
    rows = x_ref.shape[0] // sub_blocks
    wout = wout_ref[...]
    lane = lax.broadcasted_iota(jnp.int32, (rows, LANES), 1)

    def first_argmax(v):
        mx = jnp.max(v, axis=1, keepdims=True)
        return mx, jnp.min(jnp.where(v == mx, lane, LANES), axis=1, keepdims=True)

    for sb in range(sub_blocks):
        r = pl.ds(sb * rows, rows)
        attn = _dot(olat_ref[r, :], wuv_ref[...])
        attn_n = _rms(attn, gattn_ref[...]).astype(BF16)
        mix = _dot(attn_n, wout[:ATTN_WIDTH]) + _dot(convn_ref[r, :], wout[ATTN_WIDTH:])
        h1 = x_ref[r, :] + _rms(mix, gpost_ref[...])
        h1_ref[r, :] = h1
        tf = _rms(h1, gffn_ref[...])
        t = tf.astype(BF16)
        _store_token_rows(trow_ref, tf, sb * rows)

        logits = _dot(t, wr_ref[...]) + br_ref[...]
        gl = jnp.where(lane < N_GROUPS, logits[:, :LANES], NEG_INF)
        gmax, gidx = first_argmax(gl)
        g_w = 1.0 / jnp.sum(jnp.exp(gl - gmax), axis=1, keepdims=True)
        el = jnp.where((lane >> 3) == gidx, logits[:, LANES:], NEG_INF)
        m1, i1 = first_argmax(el)
        m2, i2 = first_argmax(jnp.where(lane == i1, NEG_INF, el))
        e2 = jnp.exp(m2 - m1)
        den = 1.0 + e2
        oh1_ref[r, :] = jnp.where(lane == i1, 1.0, 0.0).astype(BF16)
        oh2_ref[r, :] = jnp.where(lane == i2, 1.0, 0.0).astype(BF16)
        wts_ref[r, :] = jnp.where(lane == 0, g_w / den, jnp.where(lane == 1, g_w * e2 / den, 0.0))


def _postmix(x2, olat2, convn2, wts, rows, sub_blocks):
    n = x2.shape[0]
    wuv, gattn, wout, gpost, gffn, wr, br = wts
    const = lambda a: pl.BlockSpec(a.shape, lambda i: (0,) * a.ndim)
    tile = lambda w: pl.BlockSpec((rows, w), lambda i: (i, 0))
    return pl.pallas_call(
        functools.partial(_postmix_kernel, sub_blocks=sub_blocks),
        grid=(n // rows,),
        in_specs=[tile(D_MODEL), tile(N_HEADS * KV_LORA), tile(CONV_DIM),
                  const(wuv), const(gattn), const(wout), const(gpost), const(gffn), const(wr), const(br)],
        out_specs=[tile(D_MODEL), pl.BlockSpec((rows * SUBLANES, LANES), lambda i: (i, 0)),
                   tile(LANES), tile(LANES), tile(LANES)],
        out_shape=[jax.ShapeDtypeStruct((n, D_MODEL), F32),
                   jax.ShapeDtypeStruct((n * SUBLANES, LANES), F32),
                   jax.ShapeDtypeStruct((n, LANES), BF16),
                   jax.ShapeDtypeStruct((n, LANES), BF16),
                   jax.ShapeDtypeStruct((n, LANES), F32)],
        compiler_params=pltpu.CompilerParams(dimension_semantics=("arbitrary",), vmem_limit_bytes=VMEM_LIMIT),
        name="postmix",
    )(x2, olat2, convn2, wuv, gattn, wout, gpost, gffn, wr, br)


def _silu(a):
    return a * (1.0 / (1.0 + jnp.exp(-a)))


def _plan_kernel(oh1_ref, oh2_ref, pos_ref, te_ref, misc_ref, *, chunk):
    n_tok = oh1_ref.shape[0]
    n_chunks = n_tok // chunk
    lane = lax.broadcasted_iota(jnp.int32, (SUBLANES, LANES), 1)

    def rows_of(ref, c):
        return ref[pl.ds(pl.multiple_of(c * chunk, chunk), chunk), :]

    def count_body(c, acc):
        both = rows_of(oh1_ref, c).astype(F32) + rows_of(oh2_ref, c).astype(F32)
        return acc + jnp.sum(both, axis=0, keepdims=True)

    counts = lax.fori_loop(0, n_chunks, count_body, jnp.zeros((1, LANES), F32))
    counts = jnp.broadcast_to(counts, (SUBLANES, LANES)).astype(jnp.int32)
    tile_shift = EXPERT_TILE.bit_length() - 1
    padded = ((counts + (EXPERT_TILE - 1)) >> tile_shift) << tile_shift
    ends = padded
    shift = 1
    while shift < N_EXPERTS:
        ends = ends + jnp.where(lane >= shift, pltpu.roll(ends, shift, axis=1), 0)
        shift *= 2
    base = (ends - padded)[0:1].astype(F32)

    r_iota = lax.broadcasted_iota(jnp.int32, (chunk, chunk), 0)
    c_iota = lax.broadcasted_iota(jnp.int32, (chunk, chunk), 1)
    earlier = jnp.where(c_iota < r_iota, 1.0, 0.0).astype(BF16)

    def rank_body(c, run):
        for slot, ref in enumerate((oh1_ref, oh2_ref)):
            oh = rows_of(ref, c)
            ohf = oh.astype(F32)
            before = _dot(earlier, oh) + run
            posv = jnp.sum(ohf * before, axis=1, keepdims=True)
            pos_row = jnp.broadcast_to(posv, (chunk, LANES)).T[0:1, :]
            pos_ref[pl.ds(slot * n_chunks + c, 1), :] = pos_row.astype(jnp.int32)
            run = run + jnp.sum(ohf, axis=0, keepdims=True)
        return run

    lax.fori_loop(0, n_chunks, rank_body, base)

    n_tile_lanes = te_ref.shape[1]
    tile_start = lax.broadcasted_iota(jnp.int32, (SUBLANES, n_tile_lanes), 1) * EXPERT_TILE
    te = jnp.zeros((SUBLANES, n_tile_lanes), jnp.int32)
    for e in range(N_EXPERTS):
        te = te + jnp.where(ends[:, e:e + 1] <= tile_start, 1, 0)
    te_ref[...] = jnp.minimum(te, N_EXPERTS - 1)
    used_tiles = ends[:, N_EXPERTS - 1:N_EXPERTS] >> tile_shift
    misc_ref[...] = jnp.where(lane < N_EXPERTS, jnp.where(padded > 0, ends - EXPERT_TILE, -1), used_tiles)


def _plan(oh1, oh2, n_tiles, chunk):
    n = oh1.shape[0]
    n_tile_lanes = -(-n_tiles // LANES) * LANES
    whole = lambda shape: pl.BlockSpec(shape, lambda i: (0, 0))
    return pl.pallas_call(
        functools.partial(_plan_kernel, chunk=chunk),
        grid=(1,),
        in_specs=[whole((n, LANES)), whole((n, LANES))],
        out_specs=[whole((TOP_K * n // chunk, chunk)), whole((SUBLANES, n_tile_lanes)), whole((SUBLANES, LANES))],
        out_shape=[jax.ShapeDtypeStruct((TOP_K * n // chunk, chunk), jnp.int32),
                   jax.ShapeDtypeStruct((SUBLANES, n_tile_lanes), jnp.int32),
                   jax.ShapeDtypeStruct((SUBLANES, LANES), jnp.int32)],
        compiler_params=pltpu.CompilerParams(dimension_semantics=("arbitrary",), vmem_limit_bytes=VMEM_LIMIT),
        name="moe_plan",
    )(oh1, oh2)


def _row_slice(ref, row, n_rows=1):
    start = row * SUBLANES if isinstance(row, int) else pl.multiple_of(row * SUBLANES, SUBLANES)
    return ref.at[pl.ds(start, n_rows * SUBLANES)]


def _dispatch_kernel(zstart_ref, pos_ref, trow_ref, wsg_ref, wsu_ref, wsd_ref, xs_hbm, shared_ref, zero_buf,
                     zero_sem, row_sem, *, tokens):
    i = pl.program_id(0)

    @pl.when(i == 0)
    def _():
        zero_buf[...] = jnp.zeros(zero_buf.shape, F32)
        n_tiles = xs_hbm.shape[0] // (EXPERT_TILE * SUBLANES)
        used = zstart_ref[N_EXPERTS]

        def zero_tile(first_row):
            return pltpu.make_async_copy(zero_buf, _row_slice(xs_hbm, first_row, EXPERT_TILE), zero_sem)

        for phase in ("start", "wait"):
            for e in range(N_EXPERTS):
                @pl.when(zstart_ref[e] >= 0)
                def _():
                    copy = zero_tile(zstart_ref[e])
                    copy.start() if phase == "start" else copy.wait()

            def tail(t, carry):
                copy = zero_tile(t * EXPERT_TILE)
                copy.start() if phase == "start" else copy.wait()
                return carry

            lax.fori_loop(used, n_tiles, tail, 0)

    unroll = 8

    def issue(g, carry):
        for u in range(unroll):
            n = g * unroll + u
            src = _row_slice(trow_ref, n)
            for k in range(TOP_K):
                pltpu.make_async_copy(src, _row_slice(xs_hbm, pos_ref[k, n]), row_sem).start(priority=(u + k) % 2)
        return carry

    lax.fori_loop(0, tokens // unroll, issue, 0)

    t = _load_token_rows(trow_ref, tokens).astype(BF16)
    hid = _silu(_dot(t, wsg_ref[...])) * _dot(t, wsu_ref[...])
    shared_ref[...] = _dot(hid.astype(BF16), wsd_ref[...])

    for k in range(TOP_K):
        pltpu.make_async_copy(trow_ref, _row_slice(xs_hbm, 0, tokens), row_sem).wait()


def _dispatch(zstart, pos, trow, shared_w, sorted_rows, tokens):
    n = pos.shape[1]
    const = lambda a: pl.BlockSpec(a.shape, lambda i: (0,) * a.ndim)
    return pl.pallas_call(
        functools.partial(_dispatch_kernel, tokens=tokens),
        grid=(n // tokens,),
        in_specs=[pl.BlockSpec(memory_space=pltpu.SMEM),
                  pl.BlockSpec((TOP_K, tokens), lambda i: (0, i), memory_space=pltpu.SMEM),
                  pl.BlockSpec((tokens * SUBLANES, LANES), lambda i: (i, 0))] + [const(w) for w in shared_w],
        out_specs=[pl.BlockSpec(memory_space=pl.ANY), pl.BlockSpec((tokens, D_MODEL), lambda i: (i, 0))],
        out_shape=[jax.ShapeDtypeStruct((sorted_rows * SUBLANES, LANES), F32),
                   jax.ShapeDtypeStruct((n, D_MODEL), F32)],
        scratch_shapes=[pltpu.VMEM((EXPERT_TILE * SUBLANES, LANES), F32), pltpu.SemaphoreType.DMA,
                        pltpu.SemaphoreType.DMA],
        compiler_params=pltpu.CompilerParams(dimension_semantics=("arbitrary",), vmem_limit_bytes=VMEM_LIMIT),
        name="moe_dispatch",
    )(zstart, pos, trow, *shared_w)


def _experts_kernel(te_ref, used_ref, xs_ref, wg_ref, wu_ref, wd_ref, ys_ref):
    t = pl.program_id(0)

    @pl.when(t < used_ref[0])
    def _():
        x = _load_token_rows(xs_ref, EXPERT_TILE).astype(BF16)
        hid = _silu(_dot(x, wg_ref[0].astype(BF16))) * _dot(x, wu_ref[0].astype(BF16))
        _store_token_rows(ys_ref, _dot(hid.astype(BF16), wd_ref[0].astype(BF16)))

    @pl.when(t >= used_ref[0])
    def _():
        ys_ref[...] = jnp.zeros(ys_ref.shape, F32)


def _experts(te, used, xs, wg, wu, wd, n_tiles):
    blk = EXPERT_TILE * SUBLANES
    return pl.pallas_call(
        _experts_kernel,
        grid_spec=pltpu.PrefetchScalarGridSpec(
            num_scalar_prefetch=2,
            grid=(n_tiles,),
            in_specs=[pl.BlockSpec((blk, LANES), lambda t, te, used: (jnp.minimum(t, used[0] - 1), 0)),
                      pl.BlockSpec((1, D_MODEL, D_FF), lambda t, te, used: (te[t], 0, 0)),
                      pl.BlockSpec((1, D_MODEL, D_FF), lambda t, te, used: (te[t], 0, 0)),
                      pl.BlockSpec((1, D_FF, D_MODEL), lambda t, te, used: (te[t], 0, 0))],
            out_specs=pl.BlockSpec((blk, LANES), lambda t, te, used: (t, 0))),
        out_shape=jax.ShapeDtypeStruct(xs.shape, F32),
        compiler_params=pltpu.CompilerParams(dimension_semantics=("arbitrary",), vmem_limit_bytes=VMEM_LIMIT),
        name="moe_experts",
    )(te, used, xs, wg, wu, wd)


def _combine_kernel(pos_ref, pos_next_ref, shared_ref, h1_ref, wts_ref, ys_hbm, gout_ref, out_ref, ybuf, sems, *,
                    tokens):
    i = pl.program_id(0)
    slot = lax.rem(i, 2)
    unroll = 8

    def start_gathers(p_ref, dst_slot):
        def issue(g, carry):
            for u in range(unroll):
                n = g * unroll + u
                for k in range(TOP_K):
                    pltpu.make_async_copy(_row_slice(ys_hbm, p_ref[k, n]), _row_slice(ybuf.at[dst_slot].at[k], n),
                                          sems.at[dst_slot]).start(priority=(u + k) % 2)
            return carry

        lax.fori_loop(0, tokens // unroll, issue, 0)

    @pl.when(i == 0)
    def _():
        start_gathers(pos_ref, slot)

    @pl.when(i + 1 < pl.num_programs(0))
    def _():
        start_gathers(pos_next_ref, 1 - slot)

    for k in range(TOP_K):
        pltpu.make_async_copy(_row_slice(ys_hbm, 0, tokens), ybuf.at[slot].at[k], sems.at[slot]).wait()
    wts = wts_ref[...]
    routed = (wts[:, 0:1] * _load_token_rows(ybuf.at[slot].at[0], tokens)
              + wts[:, 1:2] * _load_token_rows(ybuf.at[slot].at[1], tokens))
    out_ref[...] = h1_ref[...] + _rms(routed + shared_ref[...], gout_ref[...])


def _combine(pos, shared, h1, wts, ys, gout, tokens):
    n = h1.shape[0]
    const = lambda a: pl.BlockSpec(a.shape, lambda i: (0,) * a.ndim)
    tile = lambda w: pl.BlockSpec((tokens, w), lambda i: (i, 0))
    steps = n // tokens
    return pl.pallas_call(
        functools.partial(_combine_kernel, tokens=tokens),
        grid=(steps,),
        in_specs=[pl.BlockSpec((TOP_K, tokens), lambda i: (0, i), memory_space=pltpu.SMEM),
                  pl.BlockSpec((TOP_K, tokens), lambda i: (0, jnp.minimum(i + 1, steps - 1)),
                               memory_space=pltpu.SMEM),
                  tile(D_MODEL), tile(D_MODEL), tile(LANES), pl.BlockSpec(memory_space=pl.ANY), const(gout)],
        out_specs=tile(D_MODEL),
        out_shape=jax.ShapeDtypeStruct((n, D_MODEL), F32),
        scratch_shapes=[pltpu.VMEM((2, TOP_K, tokens * SUBLANES, LANES), F32), pltpu.SemaphoreType.DMA((2,))],
        compiler_params=pltpu.CompilerParams(dimension_semantics=("arbitrary",), vmem_limit_bytes=VMEM_LIMIT),
        name="moe_combine",
    )(pos, pos, shared, h1, wts, ys, gout)


def _swap_halves(w):
    half = w.shape[-1] // 2
    return jnp.concatenate([w[..., half:], w[..., :half]], axis=-1)


def _block_diag(blocks):
    h, r, c = blocks.shape
    eye = jnp.eye(h, dtype=blocks.dtype)
    return jnp.einsum('hrc,hg->hrgc', blocks, eye).reshape(h * r, h * c)


def kernel(x, positions, meta_tokens, pre_mix_norm, w_in, q_norm, w_uq, kv_norm, w_ukv, conv_w, attn_out_norm,
           conv_out_norm, w_out, post_mix_norm, pre_ffn_norm, w_group_router, b_group_router, w_expert_router,
           b_expert_router, w_gate, w_up, w_down, w_sh_gate, w_sh_up, w_sh_down, post_ffn_norm):
    bsz, seq, d = x.shape
    n = bsz * seq
    row = lambda v: v.reshape(1, -1).astype(F32)

    wi = w_in[0]
    o_kv, o_pe, o_b = Q_LORA, Q_LORA + KV_LORA, Q_LORA + KV_LORA + QK_ROPE
    k_pe = wi[:, o_pe:o_b]
    w1 = jnp.concatenate([wi[:, :o_pe], k_pe, _swap_halves(k_pe), jnp.zeros((d, LANES - 2 * QK_ROPE), F32),
                          wi[:, o_b:]], axis=1).astype(BF16)
    wq = w_uq[0].reshape(Q_LORA, N_HEADS, QK_NOPE + QK_ROPE)
    wqn = wq[:, :, :QK_NOPE].reshape(Q_LORA, N_HEADS * QK_NOPE).astype(F32)
    wq_rope = wq[:, :, QK_NOPE:]
    wqr = jnp.concatenate([wq_rope, _swap_halves(wq_rope),
                           jnp.zeros((Q_LORA, N_HEADS, LANES - 2 * QK_ROPE), F32)], axis=2)
    wqr = wqr.reshape(Q_LORA, N_HEADS * LANES).astype(F32)
    wkv = w_ukv[0].reshape(KV_LORA, N_HEADS, QK_NOPE + V_DIM)
    bduk = _block_diag(jnp.transpose(wkv[:, :, :QK_NOPE], (1, 2, 0))).astype(F32)
    wuv = _block_diag(jnp.transpose(wkv[:, :, QK_NOPE:], (1, 0, 2))).astype(BF16)
    premix_w = (row(pre_mix_norm[0]), w1, row(q_norm[0]), row(kv_norm[0]), wqn, bduk, wqr,
                conv_w[0].astype(F32), row(conv_out_norm[0]))
    wr = jnp.concatenate([w_group_router[0], jnp.zeros((d, LANES - N_GROUPS), F32),
                          w_expert_router[0], jnp.zeros((d, LANES - N_EXPERTS), F32)], axis=1).astype(BF16)
    br = jnp.concatenate([b_group_router[0], jnp.zeros((LANES - N_GROUPS,), F32),
                          b_expert_router[0], jnp.zeros((LANES - N_EXPERTS,), F32)]).reshape(1, -1).astype(F32)
    postmix_w = (wuv, row(attn_out_norm[0]), w_out[0].astype(BF16), row(post_mix_norm[0]), row(pre_ffn_norm[0]),
                 wr, br)
    expert_w = (w_gate[0], w_up[0], w_down[0])
    shared_w = (w_sh_gate[0].astype(BF16), w_sh_up[0].astype(BF16), w_sh_down[0].astype(BF16))

    pos_all = jnp.concatenate([positions.astype(jnp.int32).reshape(-1) + N_META,
                               jnp.zeros((PAD_FRONT,), jnp.int32), jnp.arange(N_META, dtype=jnp.int32)])
    inv_freq = 1.0 / (ROPE_THETA ** (jnp.arange(0, QK_ROPE, 2, dtype=F32) / QK_ROPE))
    n_tab = n + BLOCK
    tab = _rope_table(pos_all.reshape(1, n_tab), inv_freq.reshape(-1, 1), n_tab // 3)

    meta_blk = jnp.pad(meta_tokens.astype(F32), ((PAD_FRONT, 0), (0, 0)))[None]
    zero_carry = jnp.zeros((8, CONV_DIM), F32)
    _, kmeta, _, meta_tail = _premix(meta_blk, tab, n // BLOCK, zero_carry, premix_w, BLOCK, 1)
    qp, kc, convn, _ = _premix(x, tab, 0, meta_tail, premix_w, 1024, 2)

    olat = _attention(qp, kc, kmeta[0], 512, 512, 16)

    h1, trow, oh1, oh2, wts = _postmix(x.reshape(n, d), olat.reshape(n, -1), convn.reshape(n, -1), postmix_w, 1024, 2)

    sorted_rows = TOP_K * n + N_EXPERTS * EXPERT_TILE
    n_tiles = sorted_rows // EXPERT_TILE
    pos2, te8, misc = _plan(oh1, oh2, n_tiles, 512)
    pos = pos2.reshape(TOP_K, n)
    xs, shared = _dispatch(misc[0, :N_EXPERTS + 1], pos, trow, shared_w, sorted_rows, 1024)
    ys = _experts(te8[0, :n_tiles], misc[0, N_EXPERTS:N_EXPERTS + 1], xs, *expert_w, n_tiles)
    out = _combine(pos, shared, h1, wts, ys, row(post_ffn_norm[0]), 512)
    return out.reshape(bsz, seq, d)
```

```python
import functools
import math

import jax
import jax.numpy as jnp
from jax import lax
from jax.experimental import pallas as pl
from jax.experimental.pallas import tpu as pltpu

D_MODEL = 1024
N_META = 16
BLOCK = 128
PAD_FRONT = BLOCK - N_META
N_HEADS = 8
QK_NOPE = 64
QK_ROPE = 32
V_DIM = 64
Q_LORA = 256
KV_LORA = 128
ROPE_THETA = 10000.0
ATTN_SCALE = (QK_NOPE + QK_ROPE) ** -0.5
Q_SCALE = ATTN_SCALE * math.log2(math.e)
ATTN_WIDTH = N_HEADS * V_DIM
CONV_DIM = 512
CONV_W = 3
N_GROUPS = 4
EXPERTS_PER_GROUP = 8
N_EXPERTS = N_GROUPS * EXPERTS_PER_GROUP
D_FF = 256
EPS = 1e-6
NEG_INF = -1e30

LANES = 128
SUBLANES = 8
TOP_K = 2
EXPERT_TILE = 512
KEY_WIDTH = 2 * LANES
IN_COLS_PADDED = Q_LORA + KV_LORA + LANES + 3 * CONV_DIM

F32 = jnp.float32
BF16 = jnp.bfloat16
VMEM_LIMIT = 56 * 1024 * 1024


def _rms(x, g):
    return x * lax.rsqrt(jnp.mean(x * x, axis=-1, keepdims=True) + EPS) * g


def _dot(a, b):
    return jnp.dot(a, b, preferred_element_type=F32)


def _dot_nt(a, b):
    return lax.dot_general(a, b, (((1,), (1,)), ((), ())), preferred_element_type=F32)


def _store_token_rows(ref, value, first_token=0):
    rows = value.shape[0]
    for c in range(D_MODEL // LANES):
        ref[pl.ds(first_token * SUBLANES + c, rows, stride=SUBLANES), :] = value[:, c * LANES:(c + 1) * LANES]


def _load_token_rows(ref, rows, first_token=0):
    return jnp.concatenate([ref[pl.ds(first_token * SUBLANES + c, rows, stride=SUBLANES), :]
                            for c in range(D_MODEL // LANES)], axis=1)


def _rope_table_kernel(pos_ref, invf_ref, out_ref):
    pos = pos_ref[...].astype(F32)
    ang = invf_ref[...] * pos
    c = jnp.cos(ang)
    s = jnp.sin(ang)
    planes = jnp.concatenate([c, c, -s, s, jnp.zeros((LANES - 4 * (QK_ROPE // 2), ang.shape[1]), F32)], axis=0)
    out_ref[...] = planes.T


def _rope_table(pos_row, inv_freq_col, chunk):
    n = pos_row.shape[1]
    return pl.pallas_call(
        _rope_table_kernel,
        grid=(n // chunk,),
        in_specs=[pl.BlockSpec((1, chunk), lambda i: (0, i)),
                  pl.BlockSpec((QK_ROPE // 2, 1), lambda i: (0, 0))],
        out_specs=pl.BlockSpec((chunk, LANES), lambda i: (i, 0)),
        out_shape=jax.ShapeDtypeStruct((n, LANES), F32),
        name="rope_table",
    )(pos_row, inv_freq_col)


def _premix_kernel(x_ref, tab_ref, carry_in_ref, g_pre_ref, w1_ref, gq_ref, gkv_ref, wqn_ref, bduk_ref,
                   wqr_ref, convw_ref, gconv_ref, qp_ref, kc_ref, convn_ref, utail_ref, carry_ref, wq_scr, *,
                   sub_blocks):
    t = pl.program_id(1)
    rows = x_ref.shape[1]

    @pl.when(t == 0)
    def _():
        carry_ref[...] = carry_in_ref[...]

    @pl.when((pl.program_id(0) == 0) & (t == 0))
    def _():
        wabs = jnp.dot(wqn_ref[...], bduk_ref[...], preferred_element_type=F32,
                       precision=lax.Precision.HIGHEST) * Q_SCALE
        wrope = wqr_ref[...] * Q_SCALE
        for h in range(N_HEADS):
            sl = slice(h * LANES, (h + 1) * LANES)
            wq_scr[:, h * KEY_WIDTH:h * KEY_WIDTH + LANES] = wabs[:, sl].astype(BF16)
            wq_scr[:, h * KEY_WIDTH + LANES:(h + 1) * KEY_WIDTH] = wrope[:, sl].astype(BF16)

    rows = rows // sub_blocks
    lane = lax.broadcasted_iota(jnp.int32, (rows, LANES), 1)
    row8 = lax.broadcasted_iota(jnp.int32, (8, CONV_DIM), 0)
    cw = convw_ref[...]
    prev = carry_ref[...]
    for sb in range(sub_blocks):
        r = pl.ds(sb * rows, rows)
        xn = _rms(x_ref[0, r, :], g_pre_ref[...]).astype(BF16)
        z = _dot(xn, w1_ref[...])
        tab = tab_ref[r, :]

        def rope(pair):
            prod = pair * tab
            return jnp.where(lane < QK_ROPE, prod + pltpu.roll(prod, LANES - QK_ROPE, axis=1), 0.0)

        ckvn = _rms(z[:, Q_LORA:Q_LORA + KV_LORA], gkv_ref[...])
        krope = rope(z[:, Q_LORA + KV_LORA:Q_LORA + KV_LORA + LANES])
        kc_ref[0, r, :] = jnp.concatenate([ckvn, krope], axis=1).astype(BF16)

        cqn = _rms(z[:, :Q_LORA], gq_ref[...]).astype(BF16)
        qall = _dot(cqn, wq_scr[...])
        for h in range(N_HEADS):
            qp_ref[0, h, r, 0:LANES] = qall[:, h * KEY_WIDTH:h * KEY_WIDTH + LANES].astype(BF16)
            qp_ref[0, h, r, LANES:KEY_WIDTH] = rope(
                qall[:, h * KEY_WIDTH + LANES:(h + 1) * KEY_WIDTH]).astype(BF16)

        c0 = Q_LORA + KV_LORA + LANES
        gate_b = z[:, c0:c0 + CONV_DIM]
        u = z[:, c0 + CONV_DIM:c0 + 2 * CONV_DIM] * z[:, c0 + 2 * CONV_DIM:c0 + 3 * CONV_DIM]
        r1 = pltpu.roll(u, 1, axis=0)
        r2 = pltpu.roll(u, 2, axis=0)
        p1 = pltpu.roll(prev, 1, axis=0)
        p2 = pltpu.roll(prev, 2, axis=0)
        u1 = jnp.concatenate([jnp.where(row8 < 1, p1, r1[0:8]), r1[8:]], axis=0)
        u2 = jnp.concatenate([jnp.where(row8 < 2, p2, r2[0:8]), r2[8:]], axis=0)
        y = cw[0:1] * u2 + cw[1:2] * u1 + cw[2:3] * u
        convn_ref[0, r, :] = _rms(gate_b * y, gconv_ref[...]).astype(BF16)
        prev = u[rows - 8:rows]
    carry_ref[...] = prev
    utail_ref[...] = prev


def _premix(x3, tab, tab_block0, carry_in, wts, rows, sub_blocks):
    bsz, seq, _ = x3.shape
    nt = seq // rows
    g_pre, w1, gq, gkv, wqn, bduk, wqr, convw, gconv = wts
    const = lambda a: pl.BlockSpec(a.shape, lambda b, t: (0,) * a.ndim)
    return pl.pallas_call(
        functools.partial(_premix_kernel, sub_blocks=sub_blocks),
        grid=(bsz, nt),
        in_specs=[pl.BlockSpec((1, rows, D_MODEL), lambda b, t: (b, t, 0)),
                  pl.BlockSpec((rows, LANES), lambda b, t: (tab_block0 + b * nt + t, 0)),
                  const(carry_in), const(g_pre), const(w1), const(gq), const(gkv), const(wqn), const(bduk),
                  const(wqr), const(convw), const(gconv)],
        out_specs=[pl.BlockSpec((1, N_HEADS, rows, KEY_WIDTH), lambda b, t: (b, 0, t, 0)),
                   pl.BlockSpec((1, rows, KEY_WIDTH), lambda b, t: (b, t, 0)),
                   pl.BlockSpec((1, rows, CONV_DIM), lambda b, t: (b, t, 0)),
                   pl.BlockSpec((8, CONV_DIM), lambda b, t: (0, 0))],
        out_shape=[jax.ShapeDtypeStruct((bsz, N_HEADS, seq, KEY_WIDTH), BF16),
                   jax.ShapeDtypeStruct((bsz, seq, KEY_WIDTH), BF16),
                   jax.ShapeDtypeStruct((bsz, seq, CONV_DIM), BF16),
                   jax.ShapeDtypeStruct((8, CONV_DIM), F32)],
        scratch_shapes=[pltpu.VMEM((8, CONV_DIM), F32), pltpu.VMEM((Q_LORA, N_HEADS * KEY_WIDTH), BF16)],
        compiler_params=pltpu.CompilerParams(dimension_semantics=("arbitrary", "arbitrary"),
                                             vmem_limit_bytes=VMEM_LIMIT),
        name="premix",
    )(x3, tab, carry_in, g_pre, w1, gq, gkv, wqn, bduk, wqr, convw, gconv)


def _attention_kernel(q_ref, qnext_ref, k_ref, km_ref, o_ref, s_scr, smeta_scr, tri_scr, m_scr, l_scr, acc_scr, *,
                      bq, bk, chunks):
    assert bq == bk
    b = pl.program_id(0)
    i = pl.program_id(1)
    assert chunks % N_HEADS == 0
    row_splits = chunks // N_HEADS
    c_rows = bq // row_splits
    km = km_ref[...]

    def chunk_rows(c):
        return c // row_splits, pl.ds((c % row_splits) * c_rows, c_rows)

    def scores(qr, c, kb):
        h, r = chunk_rows(c)
        return _dot_nt(qr[0, h, r, :], kb)

    def values(kb):
        return jnp.concatenate([kb[:, :KV_LORA], jnp.ones((kb.shape[0], LANES), BF16)], axis=1)

    def consume(c, s, v_ones, mask, first=False):
        if mask is not None:
            s = jnp.where(mask, s, NEG_INF)
        m_blk = jnp.broadcast_to(jnp.max(s, axis=1, keepdims=True), (c_rows, LANES))
        m_new = m_blk if first else jnp.maximum(m_scr[c], m_blk)
        p = jnp.exp2(s - jnp.concatenate([m_new] * (s.shape[1] // LANES), axis=1)).astype(BF16)
        pv = _dot(p, v_ones)
        if first:
            l_scr[c] = pv[:, KV_LORA:]
            acc_scr[c] = pv[:, :KV_LORA]
        else:
            alpha = jnp.exp2(m_scr[c] - m_new)
            l_scr[c] = alpha * l_scr[c] + pv[:, KV_LORA:]
            acc_scr[c] = alpha * acc_scr[c] + pv[:, :KV_LORA]
        m_scr[c] = m_new

    def key_block(j):
        return k_ref[0, pl.ds(pl.multiple_of(j * bk, bk), bk), :]

    @pl.when((b == 0) & (i == 0))
    def _():
        for c in range(chunks):
            smeta_scr[c] = scores(q_ref, c, km)
        rows_i = lax.broadcasted_iota(jnp.int32, (bq, bk), 0)
        cols_i = lax.broadcasted_iota(jnp.int32, (bq, bk), 1)
        tri_scr[...] = jnp.where(cols_i <= rows_i, 0.0, NEG_INF)

    n_full = (i * bq) // bk
    kb0 = key_block(0)
    v_meta = values(km)
    col = lax.broadcasted_iota(jnp.int32, (c_rows, BLOCK), 1)
    for c in range(chunks):
        s_scr[c] = scores(q_ref, c, kb0)
        consume(c, smeta_scr[c], v_meta, col >= PAD_FRONT, first=True)

    def full_block(j):
        v_ones = values(key_block(j))
        kb_next = key_block(j + 1)
        for c in range(chunks):
            s = s_scr[c]
            s_scr[c] = scores(q_ref, c, kb_next)
            consume(c, s, v_ones, None)

    def body(j, carry):
        full_block(j)
        return carry

    lax.fori_loop(0, n_full, body, 0)

    v_ones = values(key_block(n_full))
    for c in range(chunks):
        _, r = chunk_rows(c)
        n_keys = (c % row_splits + 1) * c_rows
        s = s_scr[c, :, :n_keys] + tri_scr[r, :n_keys]
        smeta_scr[c] = scores(qnext_ref, c, km)
        consume(c, s, v_ones[:n_keys], None)

    for c in range(chunks):
        h, r = chunk_rows(c)
        o_ref[0, r, h * KV_LORA:(h + 1) * KV_LORA] = (acc_scr[c] / l_scr[c]).astype(BF16)


def _attention(qp, kc, kmeta, bq, bk, chunks):
    bsz, _, seq, _ = qp.shape
    nq = seq // bq
    c_rows = N_HEADS * bq // chunks
    scratch = [pltpu.VMEM((chunks, c_rows, bk), F32), pltpu.VMEM((chunks, c_rows, BLOCK), F32),
               pltpu.VMEM((bq, bk), F32), pltpu.VMEM((chunks, c_rows, LANES), F32),
               pltpu.VMEM((chunks, c_rows, LANES), F32), pltpu.VMEM((chunks, c_rows, KV_LORA), F32)]

    def next_q(b, i):
        wrap = (i + 1) // nq
        return (jnp.minimum(b + wrap, bsz - 1), 0, i + 1 - wrap * nq, 0)

    return pl.pallas_call(
        functools.partial(_attention_kernel, bq=bq, bk=bk, chunks=chunks),
        grid=(bsz, nq),
        in_specs=[pl.BlockSpec((1, N_HEADS, bq, KEY_WIDTH), lambda b, i: (b, 0, i, 0)),
                  pl.BlockSpec((1, N_HEADS, bq, KEY_WIDTH), next_q),
                  pl.BlockSpec((1, seq, KEY_WIDTH), lambda b, i: (b, 0, 0)),
                  pl.BlockSpec((BLOCK, KEY_WIDTH), lambda b, i: (0, 0))],
        out_specs=pl.BlockSpec((1, bq, N_HEADS * KV_LORA), lambda b, i: (b, i, 0)),
        out_shape=jax.ShapeDtypeStruct((bsz, seq, N_HEADS * KV_LORA), BF16),
        scratch_shapes=scratch,
        compiler_params=pltpu.CompilerParams(dimension_semantics=("arbitrary", "arbitrary"),
                                             vmem_limit_bytes=VMEM_LIMIT),
        name="attention",
    )(qp, qp, kc, kmeta)


def _postmix_kernel(x_ref, olat_ref, convn_ref, wuv_ref, gattn_ref, wout_ref, gpost_ref, gffn_ref, wr_ref, br_ref,
                    h1_ref, trow_ref, oh1_ref, oh2_ref, wts_ref, *, sub_blocks):
    rows = x_ref.shape[0] // sub_blocks
    wout = wout_ref[...]
    lane = lax.broadcasted_iota(jnp.int32, (rows, LANES), 1)

    def first_argmax(v):
        mx = jnp.max(v, axis=1, keepdims=True)
        return mx, jnp.min(jnp.where(v == mx, lane, LANES), axis=1, keepdims=True)

    for sb in range(sub_blocks):
        r = pl.ds(sb * rows, rows)
        attn = _dot(olat_ref[r, :], wuv_ref[...])
        attn_n = _rms(attn, gattn_ref[...]).astype(BF16)
        mix = _dot(attn_n, wout[:ATTN_WIDTH]) + _dot(convn_ref[r, :], wout[ATTN_WIDTH:])
        h1 = x_ref[r, :] + _rms(mix, gpost_ref[...])
        h1_ref[r, :] = h1
        tf = _rms(h1, gffn_ref[...])
        t = tf.astype(BF16)
        _store_token_rows(trow_ref, tf, sb * rows)

        logits = _dot(t, wr_ref[...]) + br_ref[...]
        gl = jnp.where(lane < N_GROUPS, logits[:, :LANES], NEG_INF)
        gmax, gidx = first_argmax(gl)
        g_w = 1.0 / jnp.sum(jnp.exp(gl - gmax), axis=1, keepdims=True)
        el = jnp.where((lane >> 3) == gidx, logits[:, LANES:], NEG_INF)
        m1, i1 = first_argmax(el)
        m2, i2 = first_argmax(jnp.where(lane == i1, NEG_INF, el))
        e2 = jnp.exp(m2 - m1)
        den = 1.0 + e2
        oh1_ref[r, :] = jnp.where(lane == i1, 1.0, 0.0).astype(BF16)
        oh2_ref[r, :] = jnp.where(lane == i2, 1.0, 0.0).astype(BF16)
        wts_ref[r, :] = jnp.where(lane == 0, g_w / den, jnp.where(lane == 1, g_w * e2 / den, 0.0))


def _postmix(x2, olat2, convn2, wts, rows, sub_blocks):
    n = x2.shape[0]
    wuv, gattn, wout, gpost, gffn, wr, br = wts
    const = lambda a: pl.BlockSpec(a.shape, lambda i: (0,) * a.ndim)
    tile = lambda w: pl.BlockSpec((rows, w), lambda i: (i, 0))
    return pl.pallas_call(
        functools.partial(_postmix_kernel, sub_blocks=sub_blocks),
        grid=(n // rows,),
        in_specs=[tile(D_MODEL), tile(N_HEADS * KV_LORA), tile(CONV_DIM),
                  const(wuv), const(gattn), const(wout), const(gpost), const(gffn), const(wr), const(br)],
        out_specs=[tile(D_MODEL), pl.BlockSpec((rows * SUBLANES, LANES), lambda i: (i, 0)),
                   tile(LANES), tile(LANES), tile(LANES)],
        out_shape=[jax.ShapeDtypeStruct((n, D_MODEL), F32),
                   jax.ShapeDtypeStruct((n * SUBLANES, LANES), F32),
                   jax.ShapeDtypeStruct((n, LANES), BF16),
                   jax.ShapeDtypeStruct((n, LANES), BF16),
                   jax.ShapeDtypeStruct((n, LANES), F32)],
        compiler_params=pltpu.CompilerParams(dimension_semantics=("arbitrary",), vmem_limit_bytes=VMEM_LIMIT),
        name="postmix",
    )(x2, olat2, convn2, wuv, gattn, wout, gpost, gffn, wr, br)


def _silu(a):
    return a * (1.0 / (1.0 + jnp.exp(-a)))


def _plan_kernel(oh1_ref, oh2_ref, pos_ref, te_ref, misc_ref, *, chunk):
    n_tok = oh1_ref.shape[0]
    n_chunks = n_tok // chunk
    lane = lax.broadcasted_iota(jnp.int32, (SUBLANES, LANES), 1)

    def rows_of(ref, c):
        return ref[pl.ds(pl.multiple_of(c * chunk, chunk), chunk), :]

    def count_body(c, acc):
        both = rows_of(oh1_ref, c).astype(F32) + rows_of(oh2_ref, c).astype(F32)
        return acc + jnp.sum(both, axis=0, keepdims=True)

    counts = lax.fori_loop(0, n_chunks, count_body, jnp.zeros((1, LANES), F32))
    counts = jnp.broadcast_to(counts, (SUBLANES, LANES)).astype(jnp.int32)
    tile_shift = EXPERT_TILE.bit_length() - 1
    padded = ((counts + (EXPERT_TILE - 1)) >> tile_shift) << tile_shift
    ends = padded
    shift = 1
    while shift < N_EXPERTS:
        ends = ends + jnp.where(lane >= shift, pltpu.roll(ends, shift, axis=1), 0)
        shift *= 2
    base = (ends - padded)[0:1].astype(F32)

    r_iota = lax.broadcasted_iota(jnp.int32, (chunk, chunk), 0)
    c_iota = lax.broadcasted_iota(jnp.int32, (chunk, chunk), 1)
    earlier = jnp.where(c_iota < r_iota, 1.0, 0.0).astype(BF16)

    def rank_body(c, run):
        for slot, ref in enumerate((oh1_ref, oh2_ref)):
            oh = rows_of(ref, c)
            ohf = oh.astype(F32)
            before = _dot(earlier, oh) + run
            posv = jnp.sum(ohf * before, axis=1, keepdims=True)
            pos_row = jnp.broadcast_to(posv, (chunk, LANES)).T[0:1, :]
            pos_ref[pl.ds(slot * n_chunks + c, 1), :] = pos_row.astype(jnp.int32)
            run = run + jnp.sum(ohf, axis=0, keepdims=True)
        return run

    lax.fori_loop(0, n_chunks, rank_body, base)

    n_tile_lanes = te_ref.shape[1]
    tile_start = lax.broadcasted_iota(jnp.int32, (SUBLANES, n_tile_lanes), 1) * EXPERT_TILE
    te = jnp.zeros((SUBLANES, n_tile_lanes), jnp.int32)
    for e in range(N_EXPERTS):
        te = te + jnp.where(ends[:, e:e + 1] <= tile_start, 1, 0)
    te_ref[...] = jnp.minimum(te, N_EXPERTS - 1)
    used_tiles = ends[:, N_EXPERTS - 1:N_EXPERTS] >> tile_shift
    misc_ref[...] = jnp.where(lane < N_EXPERTS, jnp.where(padded > 0, ends - EXPERT_TILE, -1), used_tiles)


def _plan(oh1, oh2, n_tiles, chunk):
    n = oh1.shape[0]
    n_tile_lanes = -(-n_tiles // LANES) * LANES
    whole = lambda shape: pl.BlockSpec(shape, lambda i: (0, 0))
    return pl.pallas_call(
        functools.partial(_plan_kernel, chunk=chunk),
        grid=(1,),
        in_specs=[whole((n, LANES)), whole((n, LANES))],
        out_specs=[whole((TOP_K * n // chunk, chunk)), whole((SUBLANES, n_tile_lanes)), whole((SUBLANES, LANES))],
        out_shape=[jax.ShapeDtypeStruct((TOP_K * n // chunk, chunk), jnp.int32),
                   jax.ShapeDtypeStruct((SUBLANES, n_tile_lanes), jnp.int32),
                   jax.ShapeDtypeStruct((SUBLANES, LANES), jnp.int32)],
        compiler_params=pltpu.CompilerParams(dimension_semantics=("arbitrary",), vmem_limit_bytes=VMEM_LIMIT),
        name="moe_plan",
    )(oh1, oh2)


def _row_slice(ref, row, n_rows=1):
    start = row * SUBLANES if isinstance(row, int) else pl.multiple_of(row * SUBLANES, SUBLANES)
    return ref.at[pl.ds(start, n_rows * SUBLANES)]


def _dispatch_kernel(zstart_ref, pos_ref, trow_ref, wsg_ref, wsu_ref, wsd_ref, xs_hbm, shared_ref, zero_buf,
                     zero_sem, row_sem, *, tokens):
    i = pl.program_id(0)

    @pl.when(i == 0)
    def _():
        zero_buf[...] = jnp.zeros(zero_buf.shape, F32)
        n_tiles = xs_hbm.shape[0] // (EXPERT_TILE * SUBLANES)
        used = zstart_ref[N_EXPERTS]

        def zero_tile(first_row):
            return pltpu.make_async_copy(zero_buf, _row_slice(xs_hbm, first_row, EXPERT_TILE), zero_sem)

        for phase in ("start", "wait"):
            for e in range(N_EXPERTS):
                @pl.when(zstart_ref[e] >= 0)
                def _():
                    copy = zero_tile(zstart_ref[e])
                    copy.start() if phase == "start" else copy.wait()

            def tail(t, carry):
                copy = zero_tile(t * EXPERT_TILE)
                copy.start() if phase == "start" else copy.wait()
                return carry

            lax.fori_loop(used, n_tiles, tail, 0)

    unroll = 8

    def issue(g, carry):
        for u in range(unroll):
            n = g * unroll + u
            src = _row_slice(trow_ref, n)
            for k in range(TOP_K):
                pltpu.make_async_copy(src, _row_slice(xs_hbm, pos_ref[k, n]), row_sem).start(priority=(u + k) % 2)
        return carry

    lax.fori_loop(0, tokens // unroll, issue, 0)

    t = _load_token_rows(trow_ref, tokens).astype(BF16)
    hid = _silu(_dot(t, wsg_ref[...])) * _dot(t, wsu_ref[...])
    shared_ref[...] = _dot(hid.astype(BF16), wsd_ref[...])

    for k in range(TOP_K):
        pltpu.make_async_copy(trow_ref, _row_slice(xs_hbm, 0, tokens), row_sem).wait()


def _dispatch(zstart, pos, trow, shared_w, sorted_rows, tokens):
    n = pos.shape[1]
    const = lambda a: pl.BlockSpec(a.shape, lambda i: (0,) * a.ndim)
    return pl.pallas_call(
        functools.partial(_dispatch_kernel, tokens=tokens),
        grid=(n // tokens,),
        in_specs=[pl.BlockSpec(memory_space=pltpu.SMEM),
                  pl.BlockSpec((TOP_K, tokens), lambda i: (0, i), memory_space=pltpu.SMEM),
                  pl.BlockSpec((tokens * SUBLANES, LANES), lambda i: (i, 0))] + [const(w) for w in shared_w],
        out_specs=[pl.BlockSpec(memory_space=pl.ANY), pl.BlockSpec((tokens, D_MODEL), lambda i: (i, 0))],
        out_shape=[jax.ShapeDtypeStruct((sorted_rows * SUBLANES, LANES), F32),
                   jax.ShapeDtypeStruct((n, D_MODEL), F32)],
        scratch_shapes=[pltpu.VMEM((EXPERT_TILE * SUBLANES, LANES), F32), pltpu.SemaphoreType.DMA,
                        pltpu.SemaphoreType.DMA],
        compiler_params=pltpu.CompilerParams(dimension_semantics=("arbitrary",), vmem_limit_bytes=VMEM_LIMIT),
        name="moe_dispatch",
    )(zstart, pos, trow, *shared_w)


def _experts_kernel(te_ref, used_ref, xs_ref, wg_ref, wu_ref, wd_ref, ys_ref):
    t = pl.program_id(0)

    @pl.when(t < used_ref[0])
    def _():
        x = _load_token_rows(xs_ref, EXPERT_TILE).astype(BF16)
        hid = _silu(_dot(x, wg_ref[0].astype(BF16))) * _dot(x, wu_ref[0].astype(BF16))
        _store_token_rows(ys_ref, _dot(hid.astype(BF16), wd_ref[0].astype(BF16)))

    @pl.when(t >= used_ref[0])
    def _():
        ys_ref[...] = jnp.zeros(ys_ref.shape, F32)


def _experts(te, used, xs, wg, wu, wd, n_tiles):
    blk = EXPERT_TILE * SUBLANES
    return pl.pallas_call(
        _experts_kernel,
        grid_spec=pltpu.PrefetchScalarGridSpec(
            num_scalar_prefetch=2,
            grid=(n_tiles,),
            in_specs=[pl.BlockSpec((blk, LANES), lambda t, te, used: (jnp.minimum(t, used[0] - 1), 0)),
                      pl.BlockSpec((1, D_MODEL, D_FF), lambda t, te, used: (te[t], 0, 0)),
                      pl.BlockSpec((1, D_MODEL, D_FF), lambda t, te, used: (te[t], 0, 0)),
                      pl.BlockSpec((1, D_FF, D_MODEL), lambda t, te, used: (te[t], 0, 0))],
            out_specs=pl.BlockSpec((blk, LANES), lambda t, te, used: (t, 0))),
        out_shape=jax.ShapeDtypeStruct(xs.shape, F32),
        compiler_params=pltpu.CompilerParams(dimension_semantics=("arbitrary",), vmem_limit_bytes=VMEM_LIMIT),
        name="moe_experts",
    )(te, used, xs, wg, wu, wd)


def _combine_kernel(pos_ref, pos_next_ref, shared_ref, h1_ref, wts_ref, ys_hbm, gout_ref, out_ref, ybuf, sems, *,
                    tokens):
    i = pl.program_id(0)
    slot = lax.rem(i, 2)
    other = 1 - slot
    group = 32

    def start_gathers(p_ref, dst_slot, first):
        for u in range(group):
            n = first + u
            for k in range(TOP_K):
                pltpu.make_async_copy(_row_slice(ys_hbm, p_ref[k, n]), _row_slice(ybuf.at[dst_slot].at[k], n),
                                      sems.at[dst_slot]).start(priority=(u + k) % 2)

    def wait_slot(s):
        for k in range(TOP_K):
            pltpu.make_async_copy(_row_slice(ys_hbm, 0, tokens), ybuf.at[s].at[k], sems.at[s]).wait()

    @pl.when(i == 0)
    def _():
        def first_step(g, carry):
            start_gathers(pos_ref, slot, g * group)
            return carry
        lax.fori_loop(0, tokens // group, first_step, 0)

    wait_slot(slot)

    def trip(g, carry):
        first = pl.multiple_of(g * group, group)
        rows = pl.ds(first, group)
        wts = wts_ref[rows, :]
        routed = (wts[:, 0:1] * _load_token_rows(ybuf.at[slot].at[0], group, first)
                  + wts[:, 1:2] * _load_token_rows(ybuf.at[slot].at[1], group, first))
        out_ref[rows, :] = h1_ref[rows, :] + _rms(routed + shared_ref[rows, :], gout_ref[...])
        start_gathers(pos_next_ref, other, first)
        return carry

    lax.fori_loop(0, tokens // group, trip, 0)

    @pl.when(i + 1 == pl.num_programs(0))
    def _():
        wait_slot(other)


def _combine(pos, shared, h1, wts, ys, gout, tokens):
    n = h1.shape[0]
    const = lambda a: pl.BlockSpec(a.shape, lambda i: (0,) * a.ndim)
    tile = lambda w: pl.BlockSpec((tokens, w), lambda i: (i, 0))
    steps = n // tokens
    return pl.pallas_call(
        functools.partial(_combine_kernel, tokens=tokens),
        grid=(steps,),
        in_specs=[pl.BlockSpec((TOP_K, tokens), lambda i: (0, i), memory_space=pltpu.SMEM),
                  pl.BlockSpec((TOP_K, tokens), lambda i: (0, jnp.minimum(i + 1, steps - 1)),
                               memory_space=pltpu.SMEM),
                  tile(D_MODEL), tile(D_MODEL), tile(LANES), pl.BlockSpec(memory_space=pl.ANY), const(gout)],
        out_specs=tile(D_MODEL),
        out_shape=jax.ShapeDtypeStruct((n, D_MODEL), F32),
        scratch_shapes=[pltpu.VMEM((2, TOP_K, tokens * SUBLANES, LANES), F32), pltpu.SemaphoreType.DMA((2,))],
        compiler_params=pltpu.CompilerParams(dimension_semantics=("arbitrary",), vmem_limit_bytes=VMEM_LIMIT),
        name="moe_combine",
    )(pos, pos, shared, h1, wts, ys, gout)


def _swap_halves(w):
    half = w.shape[-1] // 2
    return jnp.concatenate([w[..., half:], w[..., :half]], axis=-1)


def _block_diag(blocks):
    h, r, c = blocks.shape
    eye = jnp.eye(h, dtype=blocks.dtype)
    return jnp.einsum('hrc,hg->hrgc', blocks, eye).reshape(h * r, h * c)


def kernel(x, positions, meta_tokens, pre_mix_norm, w_in, q_norm, w_uq, kv_norm, w_ukv, conv_w, attn_out_norm,
           conv_out_norm, w_out, post_mix_norm, pre_ffn_norm, w_group_router, b_group_router, w_expert_router,
           b_expert_router, w_gate, w_up, w_down, w_sh_gate, w_sh_up, w_sh_down, post_ffn_norm):
    bsz, seq, d = x.shape
    n = bsz * seq
    row = lambda v: v.reshape(1, -1).astype(F32)

    wi = w_in[0]
    o_kv, o_pe, o_b = Q_LORA, Q_LORA + KV_LORA, Q_LORA + KV_LORA + QK_ROPE
    k_pe = wi[:, o_pe:o_b]
    w1 = jnp.concatenate([wi[:, :o_pe], k_pe, _swap_halves(k_pe), jnp.zeros((d, LANES - 2 * QK_ROPE), F32),
                          wi[:, o_b:]], axis=1).astype(BF16)
    wq = w_uq[0].reshape(Q_LORA, N_HEADS, QK_NOPE + QK_ROPE)
    wqn = wq[:, :, :QK_NOPE].reshape(Q_LORA, N_HEADS * QK_NOPE).astype(F32)
    wq_rope = wq[:, :, QK_NOPE:]
    wqr = jnp.concatenate([wq_rope, _swap_halves(wq_rope),
                           jnp.zeros((Q_LORA, N_HEADS, LANES - 2 * QK_ROPE), F32)], axis=2)
    wqr = wqr.reshape(Q_LORA, N_HEADS * LANES).astype(F32)
    wkv = w_ukv[0].reshape(KV_LORA, N_HEADS, QK_NOPE + V_DIM)
    bduk = _block_diag(jnp.transpose(wkv[:, :, :QK_NOPE], (1, 2, 0))).astype(F32)
    wuv = _block_diag(jnp.transpose(wkv[:, :, QK_NOPE:], (1, 0, 2))).astype(BF16)
    premix_w = (row(pre_mix_norm[0]), w1, row(q_norm[0]), row(kv_norm[0]), wqn, bduk, wqr,
                conv_w[0].astype(F32), row(conv_out_norm[0]))
    wr = jnp.concatenate([w_group_router[0], jnp.zeros((d, LANES - N_GROUPS), F32),
                          w_expert_router[0], jnp.zeros((d, LANES - N_EXPERTS), F32)], axis=1).astype(BF16)
    br = jnp.concatenate([b_group_router[0], jnp.zeros((LANES - N_GROUPS,), F32),
                          b_expert_router[0], jnp.zeros((LANES - N_EXPERTS,), F32)]).reshape(1, -1).astype(F32)
    postmix_w = (wuv, row(attn_out_norm[0]), w_out[0].astype(BF16), row(post_mix_norm[0]), row(pre_ffn_norm[0]),
                 wr, br)
    expert_w = (w_gate[0], w_up[0], w_down[0])
    shared_w = (w_sh_gate[0].astype(BF16), w_sh_up[0].astype(BF16), w_sh_down[0].astype(BF16))

    pos_all = jnp.concatenate([positions.astype(jnp.int32).reshape(-1) + N_META,
                               jnp.zeros((PAD_FRONT,), jnp.int32), jnp.arange(N_META, dtype=jnp.int32)])
    inv_freq = 1.0 / (ROPE_THETA ** (jnp.arange(0, QK_ROPE, 2, dtype=F32) / QK_ROPE))
    n_tab = n + BLOCK
    tab = _rope_table(pos_all.reshape(1, n_tab), inv_freq.reshape(-1, 1), n_tab // 3)

    meta_blk = jnp.pad(meta_tokens.astype(F32), ((PAD_FRONT, 0), (0, 0)))[None]
    zero_carry = jnp.zeros((8, CONV_DIM), F32)
    _, kmeta, _, meta_tail = _premix(meta_blk, tab, n // BLOCK, zero_carry, premix_w, BLOCK, 1)
    qp, kc, convn, _ = _premix(x, tab, 0, meta_tail, premix_w, 1024, 2)

    olat = _attention(qp, kc, kmeta[0], 512, 512, 16)

    h1, trow, oh1, oh2, wts = _postmix(x.reshape(n, d), olat.reshape(n, -1), convn.reshape(n, -1), postmix_w, 1024, 2)

    sorted_rows = TOP_K * n + N_EXPERTS * EXPERT_TILE
    n_tiles = sorted_rows // EXPERT_TILE
    pos2, te8, misc = _plan(oh1, oh2, n_tiles, 512)
    pos = pos2.reshape(TOP_K, n)
    xs, shared = _dispatch(misc[0, :N_EXPERTS + 1], pos, trow, shared_w, sorted_rows, 1024)
    ys = _experts(te8[0, :n_tiles], misc[0, N_EXPERTS:N_EXPERTS + 1], xs, *expert_w, n_tiles)
    out = _combine(pos, shared, h1, wts, ys, row(post_ffn_norm[0]), 512)
    return out.reshape(bsz, seq, d)
```

```python
import functools
import math

import jax
import jax.numpy as jnp
from jax import lax
from jax.experimental import pallas as pl
from jax.experimental.pallas import tpu as pltpu

D_MODEL = 1024
N_META = 16
BLOCK = 128
PAD_FRONT = BLOCK - N_META
N_HEADS = 8
QK_NOPE = 64
QK_ROPE = 32
V_DIM = 64
Q_LORA = 256
KV_LORA = 128
ROPE_THETA = 10000.0
ATTN_SCALE = (QK_NOPE + QK_ROPE) ** -0.5
Q_SCALE = ATTN_SCALE * math.log2(math.e)
ATTN_WIDTH = N_HEADS * V_DIM
CONV_DIM = 512
CONV_W = 3
N_GROUPS = 4
EXPERTS_PER_GROUP = 8
N_EXPERTS = N_GROUPS * EXPERTS_PER_GROUP
D_FF = 256
EPS = 1e-6
NEG_INF = -1e30

LANES = 128
SUBLANES = 8
TOP_K = 2
EXPERT_TILE = 256
KEY_WIDTH = 2 * LANES
IN_COLS_PADDED = Q_LORA + KV_LORA + LANES + 3 * CONV_DIM

F32 = jnp.float32
BF16 = jnp.bfloat16
VMEM_LIMIT = 56 * 1024 * 1024


def _rms(x, g):
    return x * lax.rsqrt(jnp.mean(x * x, axis=-1, keepdims=True) + EPS) * g


def _dot(a, b):
    return jnp.dot(a, b, preferred_element_type=F32)


def _dot_nt(a, b):
    return lax.dot_general(a, b, (((1,), (1,)), ((), ())), preferred_element_type=F32)


def _store_token_rows(ref, value, first_token=0):
    rows = value.shape[0]
    for c in range(D_MODEL // LANES):
        ref[pl.ds(first_token * SUBLANES + c, rows, stride=SUBLANES), :] = value[:, c * LANES:(c + 1) * LANES]


def _load_token_rows(ref, rows, first_token=0):
    return jnp.concatenate([ref[pl.ds(first_token * SUBLANES + c, rows, stride=SUBLANES), :]
                            for c in range(D_MODEL // LANES)], axis=1)


def _rope_table_kernel(pos_ref, invf_ref, out_ref):
    pos = pos_ref[...].astype(F32)
    ang = invf_ref[...] * pos
    c = jnp.cos(ang)
    s = jnp.sin(ang)
    planes = jnp.concatenate([c, c, -s, s, jnp.zeros((LANES - 4 * (QK_ROPE // 2), ang.shape[1]), F32)], axis=0)
    out_ref[...] = planes.T


def _rope_table(pos_row, inv_freq_col, chunk):
    n = pos_row.shape[1]
    return pl.pallas_call(
        _rope_table_kernel,
        grid=(n // chunk,),
        in_specs=[pl.BlockSpec((1, chunk), lambda i: (0, i)),
                  pl.BlockSpec((QK_ROPE // 2, 1), lambda i: (0, 0))],
        out_specs=pl.BlockSpec((chunk, LANES), lambda i: (i, 0)),
        out_shape=jax.ShapeDtypeStruct((n, LANES), F32),
        name="rope_table",
    )(pos_row, inv_freq_col)


def _premix_kernel(x_ref, tab_ref, carry_in_ref, g_pre_ref, w1_ref, gq_ref, gkv_ref, wqn_ref, bduk_ref,
                   wqr_ref, convw_ref, gconv_ref, qp_ref, kc_ref, convn_ref, utail_ref, carry_ref, wq_scr, *,
                   sub_blocks):
    t = pl.program_id(1)
    rows = x_ref.shape[1]

    @pl.when(t == 0)
    def _():
        carry_ref[...] = carry_in_ref[...]

    @pl.when((pl.program_id(0) == 0) & (t == 0))
    def _():
        wabs = jnp.dot(wqn_ref[...], bduk_ref[...], preferred_element_type=F32,
                       precision=lax.Precision.HIGHEST) * Q_SCALE
        wrope = wqr_ref[...] * Q_SCALE
        for h in range(N_HEADS):
            sl = slice(h * LANES, (h + 1) * LANES)
            wq_scr[:, h * KEY_WIDTH:h * KEY_WIDTH + LANES] = wabs[:, sl].astype(BF16)
            wq_scr[:, h * KEY_WIDTH + LANES:(h + 1) * KEY_WIDTH] = wrope[:, sl].astype(BF16)

    rows = rows // sub_blocks
    lane = lax.broadcasted_iota(jnp.int32, (rows, LANES), 1)
    row8 = lax.broadcasted_iota(jnp.int32, (8, CONV_DIM), 0)
    cw = convw_ref[...]
    prev = carry_ref[...]
    for sb in range(sub_blocks):
        r = pl.ds(sb * rows, rows)
        xn = _rms(x_ref[0, r, :], g_pre_ref[...]).astype(BF16)
        z = _dot(xn, w1_ref[...])
        tab = tab_ref[r, :]

        def rope(pair):
            prod = pair * tab
            return jnp.where(lane < QK_ROPE, prod + pltpu.roll(prod, LANES - QK_ROPE, axis=1), 0.0)

        ckvn = _rms(z[:, Q_LORA:Q_LORA + KV_LORA], gkv_ref[...])
        krope = rope(z[:, Q_LORA + KV_LORA:Q_LORA + KV_LORA + LANES])
        kc_ref[0, r, :] = jnp.concatenate([ckvn, krope], axis=1).astype(BF16)

        cqn = _rms(z[:, :Q_LORA], gq_ref[...]).astype(BF16)
        qall = _dot(cqn, wq_scr[...])
        for h in range(N_HEADS):
            qp_ref[0, h, r, 0:LANES] = qall[:, h * KEY_WIDTH:h * KEY_WIDTH + LANES].astype(BF16)
            qp_ref[0, h, r, LANES:KEY_WIDTH] = rope(
                qall[:, h * KEY_WIDTH + LANES:(h + 1) * KEY_WIDTH]).astype(BF16)

        c0 = Q_LORA + KV_LORA + LANES
        gate_b = z[:, c0:c0 + CONV_DIM]
        u = z[:, c0 + CONV_DIM:c0 + 2 * CONV_DIM] * z[:, c0 + 2 * CONV_DIM:c0 + 3 * CONV_DIM]
        r1 = pltpu.roll(u, 1, axis=0)
        r2 = pltpu.roll(u, 2, axis=0)
        p1 = pltpu.roll(prev, 1, axis=0)
        p2 = pltpu.roll(prev, 2, axis=0)
        u1 = jnp.concatenate([jnp.where(row8 < 1, p1, r1[0:8]), r1[8:]], axis=0)
        u2 = jnp.concatenate([jnp.where(row8 < 2, p2, r2[0:8]), r2[8:]], axis=0)
        y = cw[0:1] * u2 + cw[1:2] * u1 + cw[2:3] * u
        convn_ref[0, r, :] = _rms(gate_b * y, gconv_ref[...]).astype(BF16)
        prev = u[rows - 8:rows]
    carry_ref[...] = prev
    utail_ref[...] = prev


def _premix(x3, tab, tab_block0, carry_in, wts, rows, sub_blocks):
    bsz, seq, _ = x3.shape
    nt = seq // rows
    g_pre, w1, gq, gkv, wqn, bduk, wqr, convw, gconv = wts
    const = lambda a: pl.BlockSpec(a.shape, lambda b, t: (0,) * a.ndim)
    return pl.pallas_call(
        functools.partial(_premix_kernel, sub_blocks=sub_blocks),
        grid=(bsz, nt),
        in_specs=[pl.BlockSpec((1, rows, D_MODEL), lambda b, t: (b, t, 0)),
                  pl.BlockSpec((rows, LANES), lambda b, t: (tab_block0 + b * nt + t, 0)),
                  const(carry_in), const(g_pre), const(w1), const(gq), const(gkv), const(wqn), const(bduk),
                  const(wqr), const(convw), const(gconv)],
        out_specs=[pl.BlockSpec((1, N_HEADS, rows, KEY_WIDTH), lambda b, t: (b, 0, t, 0)),
                   pl.BlockSpec((1, rows, KEY_WIDTH), lambda b, t: (b, t, 0)),
                   pl.BlockSpec((1, rows, CONV_DIM), lambda b, t: (b, t, 0)),
                   pl.BlockSpec((8, CONV_DIM), lambda b, t: (0, 0))],
        out_shape=[jax.ShapeDtypeStruct((bsz, N_HEADS, seq, KEY_WIDTH), BF16),
                   jax.ShapeDtypeStruct((bsz, seq, KEY_WIDTH), BF16),
                   jax.ShapeDtypeStruct((bsz, seq, CONV_DIM), BF16),
                   jax.ShapeDtypeStruct((8, CONV_DIM), F32)],
        scratch_shapes=[pltpu.VMEM((8, CONV_DIM), F32), pltpu.VMEM((Q_LORA, N_HEADS * KEY_WIDTH), BF16)],
        compiler_params=pltpu.CompilerParams(dimension_semantics=("arbitrary", "arbitrary"),
                                             vmem_limit_bytes=VMEM_LIMIT),
        name="premix",
    )(x3, tab, carry_in, g_pre, w1, gq, gkv, wqn, bduk, wqr, convw, gconv)


def _attention_kernel(q_ref, qnext_ref, k_ref, km_ref, o_ref, s_scr, smeta_scr, tri_scr, m_scr, l_scr, acc_scr, *,
                      bq, bk, chunks):
    assert bq == bk
    b = pl.program_id(0)
    i = pl.program_id(1)
    assert chunks % N_HEADS == 0
    row_splits = chunks // N_HEADS
    c_rows = bq // row_splits
    km = km_ref[...]

    def chunk_rows(c):
        return c // row_splits, pl.ds((c % row_splits) * c_rows, c_rows)

    def scores(qr, c, kb):
        h, r = chunk_rows(c)
        return _dot_nt(qr[0, h, r, :], kb)

    def values(kb):
        return jnp.concatenate([kb[:, :KV_LORA], jnp.ones((kb.shape[0], LANES), BF16)], axis=1)

    def consume(c, s, v_ones, mask, first=False):
        if mask is not None:
            s = jnp.where(mask, s, NEG_INF)
        m_blk = jnp.broadcast_to(jnp.max(s, axis=1, keepdims=True), (c_rows, LANES))
        m_new = m_blk if first else jnp.maximum(m_scr[c], m_blk)
        p = jnp.exp2(s - jnp.concatenate([m_new] * (s.shape[1] // LANES), axis=1)).astype(BF16)
        pv = _dot(p, v_ones)
        if first:
            l_scr[c] = pv[:, KV_LORA:]
            acc_scr[c] = pv[:, :KV_LORA]
        else:
            alpha = jnp.exp2(m_scr[c] - m_new)
            l_scr[c] = alpha * l_scr[c] + pv[:, KV_LORA:]
            acc_scr[c] = alpha * acc_scr[c] + pv[:, :KV_LORA]
        m_scr[c] = m_new

    def key_block(j):
        return k_ref[0, pl.ds(pl.multiple_of(j * bk, bk), bk), :]

    @pl.when((b == 0) & (i == 0))
    def _():
        for c in range(chunks):
            smeta_scr[c] = scores(q_ref, c, km)
        rows_i = lax.broadcasted_iota(jnp.int32, (bq, bk), 0)
        cols_i = lax.broadcasted_iota(jnp.int32, (bq, bk), 1)
        tri_scr[...] = jnp.where(cols_i <= rows_i, 0.0, NEG_INF)

    n_full = (i * bq) // bk
    kb0 = key_block(0)
    v_meta = values(km)
    col = lax.broadcasted_iota(jnp.int32, (c_rows, BLOCK), 1)
    for c in range(chunks):
        s_scr[c] = scores(q_ref, c, kb0)
        consume(c, smeta_scr[c], v_meta, col >= PAD_FRONT, first=True)

    def full_block(j):
        v_ones = values(key_block(j))
        kb_next = key_block(j + 1)
        for c in range(chunks):
            s = s_scr[c]
            s_scr[c] = scores(q_ref, c, kb_next)
            consume(c, s, v_ones, None)

    def body(j, carry):
        full_block(j)
        return carry

    lax.fori_loop(0, n_full, body, 0)

    v_ones = values(key_block(n_full))
    for c in range(chunks):
        _, r = chunk_rows(c)
        n_keys = (c % row_splits + 1) * c_rows
        s = s_scr[c, :, :n_keys] + tri_scr[r, :n_keys]
        smeta_scr[c] = scores(qnext_ref, c, km)
        consume(c, s, v_ones[:n_keys], None)

    for c in range(chunks):
        h, r = chunk_rows(c)
        o_ref[0, r, h * KV_LORA:(h + 1) * KV_LORA] = (acc_scr[c] / l_scr[c]).astype(BF16)


def _attention(qp, kc, kmeta, bq, bk, chunks):
    bsz, _, seq, _ = qp.shape
    nq = seq // bq
    c_rows = N_HEADS * bq // chunks
    scratch = [pltpu.VMEM((chunks, c_rows, bk), F32), pltpu.VMEM((chunks, c_rows, BLOCK), F32),
               pltpu.VMEM((bq, bk), F32), pltpu.VMEM((chunks, c_rows, LANES), F32),
               pltpu.VMEM((chunks, c_rows, LANES), F32), pltpu.VMEM((chunks, c_rows, KV_LORA), F32)]

    def next_q(b, i):
        wrap = (i + 1) // nq
        return (jnp.minimum(b + wrap, bsz - 1), 0, i + 1 - wrap * nq, 0)

    return pl.pallas_call(
        functools.partial(_attention_kernel, bq=bq, bk=bk, chunks=chunks),
        grid=(bsz, nq),
        in_specs=[pl.BlockSpec((1, N_HEADS, bq, KEY_WIDTH), lambda b, i: (b, 0, i, 0)),
                  pl.BlockSpec((1, N_HEADS, bq, KEY_WIDTH), next_q),
                  pl.BlockSpec((1, seq, KEY_WIDTH), lambda b, i: (b, 0, 0)),
                  pl.BlockSpec((BLOCK, KEY_WIDTH), lambda b, i: (0, 0))],
        out_specs=pl.BlockSpec((1, bq, N_HEADS * KV_LORA), lambda b, i: (b, i, 0)),
        out_shape=jax.ShapeDtypeStruct((bsz, seq, N_HEADS * KV_LORA), BF16),
        scratch_shapes=scratch,
        compiler_params=pltpu.CompilerParams(dimension_semantics=("arbitrary", "arbitrary"),
                                             vmem_limit_bytes=VMEM_LIMIT),
        name="attention",
    )(qp, qp, kc, kmeta)


def _postmix_kernel(x_ref, olat_ref, convn_ref, wuv_ref, gattn_ref, wout_ref, gpost_ref, gffn_ref, wr_ref, br_ref,
                    h1_ref, trow_ref, oh1_ref, oh2_ref, wts_ref, *, sub_blocks):
    rows = x_ref.shape[0] // sub_blocks
    wout = wout_ref[...]
    lane = lax.broadcasted_iota(jnp.int32, (rows, LANES), 1)

    def first_argmax(v):
        mx = jnp.max(v, axis=1, keepdims=True)
        return mx, jnp.min(jnp.where(v == mx, lane, LANES), axis=1, keepdims=True)

    for sb in range(sub_blocks):
        r = pl.ds(sb * rows, rows)
        attn = _dot(olat_ref[r, :], wuv_ref[...])
        attn_n = _rms(attn, gattn_ref[...]).astype(BF16)
        mix = _dot(attn_n, wout[:ATTN_WIDTH]) + _dot(convn_ref[r, :], wout[ATTN_WIDTH:])
        h1 = x_ref[r, :] + _rms(mix, gpost_ref[...])
        h1_ref[r, :] = h1
        tf = _rms(h1, gffn_ref[...])
        t = tf.astype(BF16)
        _store_token_rows(trow_ref, tf, sb * rows)

        logits = _dot(t, wr_ref[...]) + br_ref[...]
        gl = jnp.where(lane < N_GROUPS, logits[:, :LANES], NEG_INF)
        gmax, gidx = first_argmax(gl)
        g_w = 1.0 / jnp.sum(jnp.exp(gl - gmax), axis=1, keepdims=True)
        el = jnp.where((lane >> 3) == gidx, logits[:, LANES:], NEG_INF)
        m1, i1 = first_argmax(el)
        m2, i2 = first_argmax(jnp.where(lane == i1, NEG_INF, el))
        e2 = jnp.exp(m2 - m1)
        den = 1.0 + e2
        oh1_ref[r, :] = jnp.where(lane == i1, 1.0, 0.0).astype(BF16)
        oh2_ref[r, :] = jnp.where(lane == i2, 1.0, 0.0).astype(BF16)
        wts_ref[r, :] = jnp.where(lane == 0, g_w / den, jnp.where(lane == 1, g_w * e2 / den, 0.0))


def _postmix(x2, olat2, convn2, wts, rows, sub_blocks):
    n = x2.shape[0]
    wuv, gattn, wout, gpost, gffn, wr, br = wts
    const = lambda a: pl.BlockSpec(a.shape, lambda i: (0,) * a.ndim)
    tile = lambda w: pl.BlockSpec((rows, w), lambda i: (i, 0))
    return pl.pallas_call(
        functools.partial(_postmix_kernel, sub_blocks=sub_blocks),
        grid=(n // rows,),
        in_specs=[tile(D_MODEL), tile(N_HEADS * KV_LORA), tile(CONV_DIM),
                  const(wuv), const(gattn), const(wout), const(gpost), const(gffn), const(wr), const(br)],
        out_specs=[tile(D_MODEL), pl.BlockSpec((rows * SUBLANES, LANES), lambda i: (i, 0)),
                   tile(LANES), tile(LANES), tile(LANES)],
        out_shape=[jax.ShapeDtypeStruct((n, D_MODEL), F32),
                   jax.ShapeDtypeStruct((n * SUBLANES, LANES), F32),
                   jax.ShapeDtypeStruct((n, LANES), BF16),
                   jax.ShapeDtypeStruct((n, LANES), BF16),
                   jax.ShapeDtypeStruct((n, LANES), F32)],
        compiler_params=pltpu.CompilerParams(dimension_semantics=("arbitrary",), vmem_limit_bytes=VMEM_LIMIT),
        name="postmix",
    )(x2, olat2, convn2, wuv, gattn, wout, gpost, gffn, wr, br)


def _silu(a):
    return a * (1.0 / (1.0 + jnp.exp(-a)))


def _plan_kernel(oh1_ref, oh2_ref, pos_ref, te_ref, misc_ref, *, chunk):
    n_tok = oh1_ref.shape[0]
    n_chunks = n_tok // chunk
    lane = lax.broadcasted_iota(jnp.int32, (SUBLANES, LANES), 1)

    def rows_of(ref, c):
        return ref[pl.ds(pl.multiple_of(c * chunk, chunk), chunk), :]

    def count_body(c, acc):
        both = rows_of(oh1_ref, c).astype(F32) + rows_of(oh2_ref, c).astype(F32)
        return acc + jnp.sum(both, axis=0, keepdims=True)

    counts = lax.fori_loop(0, n_chunks, count_body, jnp.zeros((1, LANES), F32))
    counts = jnp.broadcast_to(counts, (SUBLANES, LANES)).astype(jnp.int32)
    tile_shift = EXPERT_TILE.bit_length() - 1
    padded = ((counts + (EXPERT_TILE - 1)) >> tile_shift) << tile_shift
    ends = padded
    shift = 1
    while shift < N_EXPERTS:
        ends = ends + jnp.where(lane >= shift, pltpu.roll(ends, shift, axis=1), 0)
        shift *= 2
    base = (ends - padded)[0:1].astype(F32)

    r_iota = lax.broadcasted_iota(jnp.int32, (chunk, chunk), 0)
    c_iota = lax.broadcasted_iota(jnp.int32, (chunk, chunk), 1)
    earlier = jnp.where(c_iota < r_iota, 1.0, 0.0).astype(BF16)

    def rank_body(c, run):
        for slot, ref in enumerate((oh1_ref, oh2_ref)):
            oh = rows_of(ref, c)
            ohf = oh.astype(F32)
            before = _dot(earlier, oh) + run
            posv = jnp.sum(ohf * before, axis=1, keepdims=True)
            pos_row = jnp.broadcast_to(posv, (chunk, LANES)).T[0:1, :]
            pos_ref[pl.ds(slot * n_chunks + c, 1), :] = pos_row.astype(jnp.int32)
            run = run + jnp.sum(ohf, axis=0, keepdims=True)
        return run

    lax.fori_loop(0, n_chunks, rank_body, base)

    n_tile_lanes = te_ref.shape[1]
    tile_start = lax.broadcasted_iota(jnp.int32, (SUBLANES, n_tile_lanes), 1) * EXPERT_TILE
    te = jnp.zeros((SUBLANES, n_tile_lanes), jnp.int32)
    for e in range(N_EXPERTS):
        te = te + jnp.where(ends[:, e:e + 1] <= tile_start, 1, 0)
    te_ref[...] = jnp.minimum(te, N_EXPERTS - 1)
    used_tiles = ends[:, N_EXPERTS - 1:N_EXPERTS] >> tile_shift
    misc_ref[...] = jnp.where(lane < N_EXPERTS, jnp.where(padded > 0, ends - EXPERT_TILE, -1), used_tiles)


def _plan(oh1, oh2, n_tiles, chunk):
    n = oh1.shape[0]
    n_tile_lanes = -(-n_tiles // LANES) * LANES
    whole = lambda shape: pl.BlockSpec(shape, lambda i: (0, 0))
    return pl.pallas_call(
        functools.partial(_plan_kernel, chunk=chunk),
        grid=(1,),
        in_specs=[whole((n, LANES)), whole((n, LANES))],
        out_specs=[whole((TOP_K * n // chunk, chunk)), whole((SUBLANES, n_tile_lanes)), whole((SUBLANES, LANES))],
        out_shape=[jax.ShapeDtypeStruct((TOP_K * n // chunk, chunk), jnp.int32),
                   jax.ShapeDtypeStruct((SUBLANES, n_tile_lanes), jnp.int32),
                   jax.ShapeDtypeStruct((SUBLANES, LANES), jnp.int32)],
        compiler_params=pltpu.CompilerParams(dimension_semantics=("arbitrary",), vmem_limit_bytes=VMEM_LIMIT),
        name="moe_plan",
    )(oh1, oh2)


def _row_slice(ref, row, n_rows=1):
    start = row * SUBLANES if isinstance(row, int) else pl.multiple_of(row * SUBLANES, SUBLANES)
    return ref.at[pl.ds(start, n_rows * SUBLANES)]


def _dispatch_kernel(zstart_ref, pos_ref, trow_ref, wsg_ref, wsu_ref, wsd_ref, xs_hbm, shared_ref, zero_buf,
                     zero_sem, row_sem, *, tokens):
    i = pl.program_id(0)

    @pl.when(i == 0)
    def _():
        zero_buf[...] = jnp.zeros(zero_buf.shape, F32)
        n_tiles = xs_hbm.shape[0] // (EXPERT_TILE * SUBLANES)
        used = zstart_ref[N_EXPERTS]

        def zero_tile(first_row):
            return pltpu.make_async_copy(zero_buf, _row_slice(xs_hbm, first_row, EXPERT_TILE), zero_sem)

        for phase in ("start", "wait"):
            for e in range(N_EXPERTS):
                @pl.when(zstart_ref[e] >= 0)
                def _():
                    copy = zero_tile(zstart_ref[e])
                    copy.start() if phase == "start" else copy.wait()

            def tail(t, carry):
                copy = zero_tile(t * EXPERT_TILE)
                copy.start() if phase == "start" else copy.wait()
                return carry

            lax.fori_loop(used, n_tiles, tail, 0)

    unroll = 8

    def issue(g, carry):
        for u in range(unroll):
            n = g * unroll + u
            src = _row_slice(trow_ref, n)
            for k in range(TOP_K):
                pltpu.make_async_copy(src, _row_slice(xs_hbm, pos_ref[k, n]), row_sem).start(priority=(u + k) % 2)
        return carry

    lax.fori_loop(0, tokens // unroll, issue, 0)

    t = _load_token_rows(trow_ref, tokens).astype(BF16)
    hid = _silu(_dot(t, wsg_ref[...])) * _dot(t, wsu_ref[...])
    shared_ref[...] = _dot(hid.astype(BF16), wsd_ref[...])

    for k in range(TOP_K):
        pltpu.make_async_copy(trow_ref, _row_slice(xs_hbm, 0, tokens), row_sem).wait()


def _dispatch(zstart, pos, trow, shared_w, sorted_rows, tokens):
    n = pos.shape[1]
    const = lambda a: pl.BlockSpec(a.shape, lambda i: (0,) * a.ndim)
    return pl.pallas_call(
        functools.partial(_dispatch_kernel, tokens=tokens),
        grid=(n // tokens,),
        in_specs=[pl.BlockSpec(memory_space=pltpu.SMEM),
                  pl.BlockSpec((TOP_K, tokens), lambda i: (0, i), memory_space=pltpu.SMEM),
                  pl.BlockSpec((tokens * SUBLANES, LANES), lambda i: (i, 0))] + [const(w) for w in shared_w],
        out_specs=[pl.BlockSpec(memory_space=pl.ANY), pl.BlockSpec((tokens, D_MODEL), lambda i: (i, 0))],
        out_shape=[jax.ShapeDtypeStruct((sorted_rows * SUBLANES, LANES), F32),
                   jax.ShapeDtypeStruct((n, D_MODEL), F32)],
        scratch_shapes=[pltpu.VMEM((EXPERT_TILE * SUBLANES, LANES), F32), pltpu.SemaphoreType.DMA,
                        pltpu.SemaphoreType.DMA],
        compiler_params=pltpu.CompilerParams(dimension_semantics=("arbitrary",), vmem_limit_bytes=VMEM_LIMIT),
        name="moe_dispatch",
    )(zstart, pos, trow, *shared_w)


def _experts_kernel(te_ref, used_ref, xs_ref, wg_ref, wu_ref, wd_ref, ys_ref):
    t = pl.program_id(0)

    @pl.when(t < used_ref[0])
    def _():
        x = _load_token_rows(xs_ref, EXPERT_TILE).astype(BF16)
        hid = _silu(_dot(x, wg_ref[0].astype(BF16))) * _dot(x, wu_ref[0].astype(BF16))
        _store_token_rows(ys_ref, _dot(hid.astype(BF16), wd_ref[0].astype(BF16)))

    @pl.when(t >= used_ref[0])
    def _():
        ys_ref[...] = jnp.zeros(ys_ref.shape, F32)


def _experts(te, used, xs, wg, wu, wd, n_tiles):
    blk = EXPERT_TILE * SUBLANES
    return pl.pallas_call(
        _experts_kernel,
        grid_spec=pltpu.PrefetchScalarGridSpec(
            num_scalar_prefetch=2,
            grid=(n_tiles,),
            in_specs=[pl.BlockSpec((blk, LANES), lambda t, te, used: (jnp.minimum(t, used[0] - 1), 0)),
                      pl.BlockSpec((1, D_MODEL, D_FF), lambda t, te, used: (te[t], 0, 0)),
                      pl.BlockSpec((1, D_MODEL, D_FF), lambda t, te, used: (te[t], 0, 0)),
                      pl.BlockSpec((1, D_FF, D_MODEL), lambda t, te, used: (te[t], 0, 0))],
            out_specs=pl.BlockSpec((blk, LANES), lambda t, te, used: (t, 0))),
        out_shape=jax.ShapeDtypeStruct(xs.shape, F32),
        compiler_params=pltpu.CompilerParams(dimension_semantics=("arbitrary",), vmem_limit_bytes=VMEM_LIMIT),
        name="moe_experts",
    )(te, used, xs, wg, wu, wd)


def _combine_kernel(pos_ref, pos_next_ref, shared_ref, h1_ref, wts_ref, ys_hbm, gout_ref, out_ref, ybuf, sems, *,
                    tokens):
    i = pl.program_id(0)
    slot = lax.rem(i, 2)
    other = 1 - slot
    group = 32

    def start_gathers(p_ref, dst_slot, first):
        for u in range(group):
            n = first + u
            for k in range(TOP_K):
                pltpu.make_async_copy(_row_slice(ys_hbm, p_ref[k, n]), _row_slice(ybuf.at[dst_slot].at[k], n),
                                      sems.at[dst_slot]).start(priority=(u + k) % 2)

    def wait_slot(s):
        for k in range(TOP_K):
            pltpu.make_async_copy(_row_slice(ys_hbm, 0, tokens), ybuf.at[s].at[k], sems.at[s]).wait()

    @pl.when(i == 0)
    def _():
        def first_step(g, carry):
            start_gathers(pos_ref, slot, g * group)
            return carry
        lax.fori_loop(0, tokens // group, first_step, 0)

    wait_slot(slot)

    def trip(g, carry):
        first = pl.multiple_of(g * group, group)
        rows = pl.ds(first, group)
        wts = wts_ref[rows, :]
        routed = (wts[:, 0:1] * _load_token_rows(ybuf.at[slot].at[0], group, first)
                  + wts[:, 1:2] * _load_token_rows(ybuf.at[slot].at[1], group, first))
        out_ref[rows, :] = h1_ref[rows, :] + _rms(routed + shared_ref[rows, :], gout_ref[...])
        start_gathers(pos_next_ref, other, first)
        return carry

    lax.fori_loop(0, tokens // group, trip, 0)

    @pl.when(i + 1 == pl.num_programs(0))
    def _():
        wait_slot(other)


def _combine(pos, shared, h1, wts, ys, gout, tokens):
    n = h1.shape[0]
    const = lambda a: pl.BlockSpec(a.shape, lambda i: (0,) * a.ndim)
    tile = lambda w: pl.BlockSpec((tokens, w), lambda i: (i, 0))
    steps = n // tokens
    return pl.pallas_call(
        functools.partial(_combine_kernel, tokens=tokens),
        grid=(steps,),
        in_specs=[pl.BlockSpec((TOP_K, tokens), lambda i: (0, i), memory_space=pltpu.SMEM),
                  pl.BlockSpec((TOP_K, tokens), lambda i: (0, jnp.minimum(i + 1, steps - 1)),
                               memory_space=pltpu.SMEM),
                  tile(D_MODEL), tile(D_MODEL), tile(LANES), pl.BlockSpec(memory_space=pl.ANY), const(gout)],
        out_specs=tile(D_MODEL),
        out_shape=jax.ShapeDtypeStruct((n, D_MODEL), F32),
        scratch_shapes=[pltpu.VMEM((2, TOP_K, tokens * SUBLANES, LANES), F32), pltpu.SemaphoreType.DMA((2,))],
        compiler_params=pltpu.CompilerParams(dimension_semantics=("arbitrary",), vmem_limit_bytes=VMEM_LIMIT),
        name="moe_combine",
    )(pos, pos, shared, h1, wts, ys, gout)


def _swap_halves(w):
    half = w.shape[-1] // 2
    return jnp.concatenate([w[..., half:], w[..., :half]], axis=-1)


def _block_diag(blocks):
    h, r, c = blocks.shape
    eye = jnp.eye(h, dtype=blocks.dtype)
    return jnp.einsum('hrc,hg->hrgc', blocks, eye).reshape(h * r, h * c)


def kernel(x, positions, meta_tokens, pre_mix_norm, w_in, q_norm, w_uq, kv_norm, w_ukv, conv_w, attn_out_norm,
           conv_out_norm, w_out, post_mix_norm, pre_ffn_norm, w_group_router, b_group_router, w_expert_router,
           b_expert_router, w_gate, w_up, w_down, w_sh_gate, w_sh_up, w_sh_down, post_ffn_norm):
    bsz, seq, d = x.shape
    n = bsz * seq
    row = lambda v: v.reshape(1, -1).astype(F32)

    wi = w_in[0]
    o_kv, o_pe, o_b = Q_LORA, Q_LORA + KV_LORA, Q_LORA + KV_LORA + QK_ROPE
    k_pe = wi[:, o_pe:o_b]
    w1 = jnp.concatenate([wi[:, :o_pe], k_pe, _swap_halves(k_pe), jnp.zeros((d, LANES - 2 * QK_ROPE), F32),
                          wi[:, o_b:]], axis=1).astype(BF16)
    wq = w_uq[0].reshape(Q_LORA, N_HEADS, QK_NOPE + QK_ROPE)
    wqn = wq[:, :, :QK_NOPE].reshape(Q_LORA, N_HEADS * QK_NOPE).astype(F32)
    wq_rope = wq[:, :, QK_NOPE:]
    wqr = jnp.concatenate([wq_rope, _swap_halves(wq_rope),
                           jnp.zeros((Q_LORA, N_HEADS, LANES - 2 * QK_ROPE), F32)], axis=2)
    wqr = wqr.reshape(Q_LORA, N_HEADS * LANES).astype(F32)
    wkv = w_ukv[0].reshape(KV_LORA, N_HEADS, QK_NOPE + V_DIM)
    bduk = _block_diag(jnp.transpose(wkv[:, :, :QK_NOPE], (1, 2, 0))).astype(F32)
    wuv = _block_diag(jnp.transpose(wkv[:, :, QK_NOPE:], (1, 0, 2))).astype(BF16)
    premix_w = (row(pre_mix_norm[0]), w1, row(q_norm[0]), row(kv_norm[0]), wqn, bduk, wqr,
                conv_w[0].astype(F32), row(conv_out_norm[0]))
    wr = jnp.concatenate([w_group_router[0], jnp.zeros((d, LANES - N_GROUPS), F32),
                          w_expert_router[0], jnp.zeros((d, LANES - N_EXPERTS), F32)], axis=1).astype(BF16)
    br = jnp.concatenate([b_group_router[0], jnp.zeros((LANES - N_GROUPS,), F32),
                          b_expert_router[0], jnp.zeros((LANES - N_EXPERTS,), F32)]).reshape(1, -1).astype(F32)
    postmix_w = (wuv, row(attn_out_norm[0]), w_out[0].astype(BF16), row(post_mix_norm[0]), row(pre_ffn_norm[0]),
                 wr, br)
    expert_w = (w_gate[0], w_up[0], w_down[0])
    shared_w = (w_sh_gate[0].astype(BF16), w_sh_up[0].astype(BF16), w_sh_down[0].astype(BF16))

    pos_all = jnp.concatenate([positions.astype(jnp.int32).reshape(-1) + N_META,
                               jnp.zeros((PAD_FRONT,), jnp.int32), jnp.arange(N_META, dtype=jnp.int32)])
    inv_freq = 1.0 / (ROPE_THETA ** (jnp.arange(0, QK_ROPE, 2, dtype=F32) / QK_ROPE))
    n_tab = n + BLOCK
    tab = _rope_table(pos_all.reshape(1, n_tab), inv_freq.reshape(-1, 1), n_tab // 3)

    meta_blk = jnp.pad(meta_tokens.astype(F32), ((PAD_FRONT, 0), (0, 0)))[None]
    zero_carry = jnp.zeros((8, CONV_DIM), F32)
    _, kmeta, _, meta_tail = _premix(meta_blk, tab, n // BLOCK, zero_carry, premix_w, BLOCK, 1)
    qp, kc, convn, _ = _premix(x, tab, 0, meta_tail, premix_w, 1024, 2)

    olat = _attention(qp, kc, kmeta[0], 512, 512, 16)

    h1, trow, oh1, oh2, wts = _postmix(x.reshape(n, d), olat.reshape(n, -1), convn.reshape(n, -1), postmix_w, 1024, 2)

    sorted_rows = TOP_K * n + N_EXPERTS * EXPERT_TILE
    n_tiles = sorted_rows // EXPERT_TILE
    pos2, te8, misc = _plan(oh1, oh2, n_tiles, 512)
    pos = pos2.reshape(TOP_K, n)
    xs, shared = _dispatch(misc[0, :N_EXPERTS + 1], pos, trow, shared_w, sorted_rows, 1024)
    ys = _experts(te8[0, :n_tiles], misc[0, N_EXPERTS:N_EXPERTS + 1], xs, *expert_w, n_tiles)
    out = _combine(pos, shared, h1, wts, ys, row(post_ffn_norm[0]), 512)
    return out.reshape(bsz, seq, d)
```

```python
import functools
import math

import jax
import jax.numpy as jnp
from jax import lax
from jax.experimental import pallas as pl
from jax.experimental.pallas import tpu as pltpu

D_MODEL = 1024
N_META = 16
BLOCK = 128
PAD_FRONT = BLOCK - N_META
N_HEADS = 8
QK_NOPE = 64
QK_ROPE = 32
V_DIM = 64
Q_LORA = 256
KV_LORA = 128
ROPE_THETA = 10000.0
ATTN_SCALE = (QK_NOPE + QK_ROPE) ** -0.5
Q_SCALE = ATTN_SCALE * math.log2(math.e)
ATTN_WIDTH = N_HEADS * V_DIM
CONV_DIM = 512
CONV_W = 3
N_GROUPS = 4
EXPERTS_PER_GROUP = 8
N_EXPERTS = N_GROUPS * EXPERTS_PER_GROUP
D_FF = 256
EPS = 1e-6
NEG_INF = -1e30

LANES = 128
SUBLANES = 8
TOP_K = 2
EXPERT_TILE = 512
KEY_WIDTH = 2 * LANES
IN_COLS_PADDED = Q_LORA + KV_LORA + LANES + 3 * CONV_DIM

F32 = jnp.float32
BF16 = jnp.bfloat16
VMEM_LIMIT = 56 * 1024 * 1024


def _rms(x, g):
    return x * lax.rsqrt(jnp.mean(x * x, axis=-1, keepdims=True) + EPS) * g


def _dot(a, b):
    return jnp.dot(a, b, preferred_element_type=F32)


def _dot_nt(a, b):
    return lax.dot_general(a, b, (((1,), (1,)), ((), ())), preferred_element_type=F32)


def _store_token_rows(ref, value, first_token=0):
    rows = value.shape[0]
    for c in range(D_MODEL // LANES):
        ref[pl.ds(first_token * SUBLANES + c, rows, stride=SUBLANES), :] = value[:, c * LANES:(c + 1) * LANES]


def _load_token_rows(ref, rows, first_token=0):
    return jnp.concatenate([ref[pl.ds(first_token * SUBLANES + c, rows, stride=SUBLANES), :]
                            for c in range(D_MODEL // LANES)], axis=1)


def _rope_table_kernel(pos_ref, invf_ref, out_ref):
    pos = pos_ref[...].astype(F32)
    ang = invf_ref[...] * pos
    c = jnp.cos(ang)
    s = jnp.sin(ang)
    planes = jnp.concatenate([c, c, -s, s, jnp.zeros((LANES - 4 * (QK_ROPE // 2), ang.shape[1]), F32)], axis=0)
    out_ref[...] = planes.T


def _rope_table(pos_row, inv_freq_col, chunk):
    n = pos_row.shape[1]
    return pl.pallas_call(
        _rope_table_kernel,
        grid=(n // chunk,),
        in_specs=[pl.BlockSpec((1, chunk), lambda i: (0, i)),
                  pl.BlockSpec((QK_ROPE // 2, 1), lambda i: (0, 0))],
        out_specs=pl.BlockSpec((chunk, LANES), lambda i: (i, 0)),
        out_shape=jax.ShapeDtypeStruct((n, LANES), F32),
        name="rope_table",
    )(pos_row, inv_freq_col)


def _premix_kernel(x_ref, tab_ref, carry_in_ref, g_pre_ref, w1_ref, gq_ref, gkv_ref, wqn_ref, bduk_ref,
                   wqr_ref, convw_ref, gconv_ref, qp_ref, kc_ref, convn_ref, utail_ref, carry_ref, wq_scr, *,
                   sub_blocks):
    t = pl.program_id(1)
    rows = x_ref.shape[1]

    @pl.when(t == 0)
    def _():
        carry_ref[...] = carry_in_ref[...]

    @pl.when((pl.program_id(0) == 0) & (t == 0))
    def _():
        wabs = jnp.dot(wqn_ref[...], bduk_ref[...], preferred_element_type=F32,
                       precision=lax.Precision.HIGHEST) * Q_SCALE
        wrope = wqr_ref[...] * Q_SCALE
        for h in range(N_HEADS):
            sl = slice(h * LANES, (h + 1) * LANES)
            wq_scr[:, h * KEY_WIDTH:h * KEY_WIDTH + LANES] = wabs[:, sl].astype(BF16)
            wq_scr[:, h * KEY_WIDTH + LANES:(h + 1) * KEY_WIDTH] = wrope[:, sl].astype(BF16)

    rows = rows // sub_blocks
    lane = lax.broadcasted_iota(jnp.int32, (rows, LANES), 1)
    row8 = lax.broadcasted_iota(jnp.int32, (8, CONV_DIM), 0)
    cw = convw_ref[...]
    prev = carry_ref[...]
    for sb in range(sub_blocks):
        r = pl.ds(sb * rows, rows)
        xn = _rms(x_ref[0, r, :], g_pre_ref[...]).astype(BF16)
        z = _dot(xn, w1_ref[...])
        tab = tab_ref[r, :]

        def rope(pair):
            prod = pair * tab
            return jnp.where(lane < QK_ROPE, prod + pltpu.roll(prod, LANES - QK_ROPE, axis=1), 0.0)

        ckvn = _rms(z[:, Q_LORA:Q_LORA + KV_LORA], gkv_ref[...])
        krope = rope(z[:, Q_LORA + KV_LORA:Q_LORA + KV_LORA + LANES])
        kc_ref[0, r, :] = jnp.concatenate([ckvn, krope], axis=1).astype(BF16)

        cqn = _rms(z[:, :Q_LORA], gq_ref[...]).astype(BF16)
        qall = _dot(cqn, wq_scr[...])
        for h in range(N_HEADS):
            qp_ref[0, h, r, 0:LANES] = qall[:, h * KEY_WIDTH:h * KEY_WIDTH + LANES].astype(BF16)
            qp_ref[0, h, r, LANES:KEY_WIDTH] = rope(
                qall[:, h * KEY_WIDTH + LANES:(h + 1) * KEY_WIDTH]).astype(BF16)

        c0 = Q_LORA + KV_LORA + LANES
        gate_b = z[:, c0:c0 + CONV_DIM]
        u = z[:, c0 + CONV_DIM:c0 + 2 * CONV_DIM] * z[:, c0 + 2 * CONV_DIM:c0 + 3 * CONV_DIM]
        r1 = pltpu.roll(u, 1, axis=0)
        r2 = pltpu.roll(u, 2, axis=0)
        p1 = pltpu.roll(prev, 1, axis=0)
        p2 = pltpu.roll(prev, 2, axis=0)
        u1 = jnp.concatenate([jnp.where(row8 < 1, p1, r1[0:8]), r1[8:]], axis=0)
        u2 = jnp.concatenate([jnp.where(row8 < 2, p2, r2[0:8]), r2[8:]], axis=0)
        y = cw[0:1] * u2 + cw[1:2] * u1 + cw[2:3] * u
        convn_ref[0, r, :] = _rms(gate_b * y, gconv_ref[...]).astype(BF16)
        prev = u[rows - 8:rows]
    carry_ref[...] = prev
    utail_ref[...] = prev


def _premix(x3, tab, tab_block0, carry_in, wts, rows, sub_blocks):
    bsz, seq, _ = x3.shape
    nt = seq // rows
    g_pre, w1, gq, gkv, wqn, bduk, wqr, convw, gconv = wts
    const = lambda a: pl.BlockSpec(a.shape, lambda b, t: (0,) * a.ndim)
    return pl.pallas_call(
        functools.partial(_premix_kernel, sub_blocks=sub_blocks),
        grid=(bsz, nt),
        in_specs=[pl.BlockSpec((1, rows, D_MODEL), lambda b, t: (b, t, 0)),
                  pl.BlockSpec((rows, LANES), lambda b, t: (tab_block0 + b * nt + t, 0)),
                  const(carry_in), const(g_pre), const(w1), const(gq), const(gkv), const(wqn), const(bduk),
                  const(wqr), const(convw), const(gconv)],
        out_specs=[pl.BlockSpec((1, N_HEADS, rows, KEY_WIDTH), lambda b, t: (b, 0, t, 0)),
                   pl.BlockSpec((1, rows, KEY_WIDTH), lambda b, t: (b, t, 0)),
                   pl.BlockSpec((1, rows, CONV_DIM), lambda b, t: (b, t, 0)),
                   pl.BlockSpec((8, CONV_DIM), lambda b, t: (0, 0))],
        out_shape=[jax.ShapeDtypeStruct((bsz, N_HEADS, seq, KEY_WIDTH), BF16),
                   jax.ShapeDtypeStruct((bsz, seq, KEY_WIDTH), BF16),
                   jax.ShapeDtypeStruct((bsz, seq, CONV_DIM), BF16),
                   jax.ShapeDtypeStruct((8, CONV_DIM), F32)],
        scratch_shapes=[pltpu.VMEM((8, CONV_DIM), F32), pltpu.VMEM((Q_LORA, N_HEADS * KEY_WIDTH), BF16)],
        compiler_params=pltpu.CompilerParams(dimension_semantics=("arbitrary", "arbitrary"),
                                             vmem_limit_bytes=VMEM_LIMIT),
        name="premix",
    )(x3, tab, carry_in, g_pre, w1, gq, gkv, wqn, bduk, wqr, convw, gconv)


def _attention_kernel(q_ref, qnext_ref, k_ref, km_ref, o_ref, s_scr, smeta_scr, tri_scr, m_scr, l_scr, acc_scr, *,
                      bq, bk, chunks):
    assert bq == bk
    b = pl.program_id(0)
    i = pl.program_id(1)
    assert chunks % N_HEADS == 0
    row_splits = chunks // N_HEADS
    c_rows = bq // row_splits
    km = km_ref[...]

    def chunk_rows(c):
        return c // row_splits, pl.ds((c % row_splits) * c_rows, c_rows)

    def scores(qr, c, kb):
        h, r = chunk_rows(c)
        return _dot_nt(qr[0, h, r, :], kb)

    def values(kb):
        return jnp.concatenate([kb[:, :KV_LORA], jnp.ones((kb.shape[0], LANES), BF16)], axis=1)

    def consume(c, s, v_ones, mask, first=False):
        if mask is not None:
            s = jnp.where(mask, s, NEG_INF)
        m_blk = jnp.broadcast_to(jnp.max(s, axis=1, keepdims=True), (c_rows, LANES))
        m_new = m_blk if first else jnp.maximum(m_scr[c], m_blk)
        p = jnp.exp2(s - jnp.concatenate([m_new] * (s.shape[1] // LANES), axis=1)).astype(BF16)
        pv = _dot(p, v_ones)
        if first:
            l_scr[c] = pv[:, KV_LORA:]
            acc_scr[c] = pv[:, :KV_LORA]
        else:
            alpha = jnp.exp2(m_scr[c] - m_new)
            l_scr[c] = alpha * l_scr[c] + pv[:, KV_LORA:]
            acc_scr[c] = alpha * acc_scr[c] + pv[:, :KV_LORA]
        m_scr[c] = m_new

    def key_block(j):
        return k_ref[0, pl.ds(pl.multiple_of(j * bk, bk), bk), :]

    @pl.when((b == 0) & (i == 0))
    def _():
        for c in range(chunks):
            smeta_scr[c] = scores(q_ref, c, km)
        rows_i = lax.broadcasted_iota(jnp.int32, (bq, bk), 0)
        cols_i = lax.broadcasted_iota(jnp.int32, (bq, bk), 1)
        tri_scr[...] = jnp.where(cols_i <= rows_i, 0.0, NEG_INF)

    n_full = (i * bq) // bk
    kb0 = key_block(0)
    v_meta = values(km)
    col = lax.broadcasted_iota(jnp.int32, (c_rows, BLOCK), 1)
    for c in range(chunks):
        s_scr[c] = scores(q_ref, c, kb0)
        consume(c, smeta_scr[c], v_meta, col >= PAD_FRONT, first=True)

    def full_block(j):
        v_ones = values(key_block(j))
        kb_next = key_block(j + 1)
        for c in range(chunks):
            s = s_scr[c]
            s_scr[c] = scores(q_ref, c, kb_next)
            consume(c, s, v_ones, None)

    def body(j, carry):
        full_block(j)
        return carry

    lax.fori_loop(0, n_full, body, 0)

    v_ones = values(key_block(n_full))
    for c in range(chunks):
        _, r = chunk_rows(c)
        n_keys = (c % row_splits + 1) * c_rows
        s = s_scr[c, :, :n_keys] + tri_scr[r, :n_keys]
        smeta_scr[c] = scores(qnext_ref, c, km)
        consume(c, s, v_ones[:n_keys], None)

    for c in range(chunks):
        h, r = chunk_rows(c)
        o_ref[0, r, h * KV_LORA:(h + 1) * KV_LORA] = (acc_scr[c] / l_scr[c]).astype(BF16)


def _attention(qp, kc, kmeta, bq, bk, chunks):
    bsz, _, seq, _ = qp.shape
    nq = seq // bq
    c_rows = N_HEADS * bq // chunks
    scratch = [pltpu.VMEM((chunks, c_rows, bk), F32), pltpu.VMEM((chunks, c_rows, BLOCK), F32),
               pltpu.VMEM((bq, bk), F32), pltpu.VMEM((chunks, c_rows, LANES), F32),
               pltpu.VMEM((chunks, c_rows, LANES), F32), pltpu.VMEM((chunks, c_rows, KV_LORA), F32)]

    def next_q(b, i):
        wrap = (i + 1) // nq
        return (jnp.minimum(b + wrap, bsz - 1), 0, i + 1 - wrap * nq, 0)

    return pl.pallas_call(
        functools.partial(_attention_kernel, bq=bq, bk=bk, chunks=chunks),
        grid=(bsz, nq),
        in_specs=[pl.BlockSpec((1, N_HEADS, bq, KEY_WIDTH), lambda b, i: (b, 0, i, 0)),
                  pl.BlockSpec((1, N_HEADS, bq, KEY_WIDTH), next_q),
                  pl.BlockSpec((1, seq, KEY_WIDTH), lambda b, i: (b, 0, 0)),
                  pl.BlockSpec((BLOCK, KEY_WIDTH), lambda b, i: (0, 0))],
        out_specs=pl.BlockSpec((1, bq, N_HEADS * KV_LORA), lambda b, i: (b, i, 0)),
        out_shape=jax.ShapeDtypeStruct((bsz, seq, N_HEADS * KV_LORA), BF16),
        scratch_shapes=scratch,
        compiler_params=pltpu.CompilerParams(dimension_semantics=("arbitrary", "arbitrary"),
                                             vmem_limit_bytes=VMEM_LIMIT),
        name="attention",
    )(qp, qp, kc, kmeta)


def _postmix_kernel(x_ref, olat_ref, convn_ref, wuv_ref, gattn_ref, wout_ref, gpost_ref, gffn_ref, wr_ref, br_ref,
                    h1_ref, trow_ref, oh1_ref, oh2_ref, wts_ref, *, sub_blocks):
    rows = x_ref.shape[0] // sub_blocks
    wout = wout_ref[...]
    lane = lax.broadcasted_iota(jnp.int32, (rows, LANES), 1)

    def first_argmax(v):
        mx = jnp.max(v, axis=1, keepdims=True)
        return mx, jnp.min(jnp.where(v == mx, lane, LANES), axis=1, keepdims=True)

    for sb in range(sub_blocks):
        r = pl.ds(sb * rows, rows)
        attn = _dot(olat_ref[r, :], wuv_ref[...])
        attn_n = _rms(attn, gattn_ref[...]).astype(BF16)
        mix = _dot(attn_n, wout[:ATTN_WIDTH]) + _dot(convn_ref[r, :], wout[ATTN_WIDTH:])
        h1 = x_ref[r, :] + _rms(mix, gpost_ref[...])
        h1_ref[r, :] = h1
        tf = _rms(h1, gffn_ref[...])
        t = tf.astype(BF16)
        _store_token_rows(trow_ref, tf, sb * rows)

        logits = _dot(t, wr_ref[...]) + br_ref[...]
        gl = jnp.where(lane < N_GROUPS, logits[:, :LANES], NEG_INF)
        gmax, gidx = first_argmax(gl)
        g_w = 1.0 / jnp.sum(jnp.exp(gl - gmax), axis=1, keepdims=True)
        el = jnp.where((lane >> 3) == gidx, logits[:, LANES:], NEG_INF)
        m1, i1 = first_argmax(el)
        m2, i2 = first_argmax(jnp.where(lane == i1, NEG_INF, el))
        e2 = jnp.exp(m2 - m1)
        den = 1.0 + e2
        oh1_ref[r, :] = jnp.where(lane == i1, 1.0, 0.0).astype(BF16)
        oh2_ref[r, :] = jnp.where(lane == i2, 1.0, 0.0).astype(BF16)
        wts_ref[r, :] = jnp.where(lane == 0, g_w / den, jnp.where(lane == 1, g_w * e2 / den, 0.0))


def _postmix(x2, olat2, convn2, wts, rows, sub_blocks):
    n = x2.shape[0]
    wuv, gattn, wout, gpost, gffn, wr, br = wts
    const = lambda a: pl.BlockSpec(a.shape, lambda i: (0,) * a.ndim)
    tile = lambda w: pl.BlockSpec((rows, w), lambda i: (i, 0))
    return pl.pallas_call(
        functools.partial(_postmix_kernel, sub_blocks=sub_blocks),
        grid=(n // rows,),
        in_specs=[tile(D_MODEL), tile(N_HEADS * KV_LORA), tile(CONV_DIM),
                  const(wuv), const(gattn), const(wout), const(gpost), const(gffn), const(wr), const(br)],
        out_specs=[tile(D_MODEL), pl.BlockSpec((rows * SUBLANES, LANES), lambda i: (i, 0)),
                   tile(LANES), tile(LANES), tile(LANES)],
        out_shape=[jax.ShapeDtypeStruct((n, D_MODEL), F32),
                   jax.ShapeDtypeStruct((n * SUBLANES, LANES), F32),
                   jax.ShapeDtypeStruct((n, LANES), BF16),
                   jax.ShapeDtypeStruct((n, LANES), BF16),
                   jax.ShapeDtypeStruct((n, LANES), F32)],
        compiler_params=pltpu.CompilerParams(dimension_semantics=("arbitrary",), vmem_limit_bytes=VMEM_LIMIT),
        name="postmix",
    )(x2, olat2, convn2, wuv, gattn, wout, gpost, gffn, wr, br)


def _silu(a):
    return a * (1.0 / (1.0 + jnp.exp(-a)))


def _plan_kernel(oh1_ref, oh2_ref, pos_ref, te_ref, misc_ref, *, chunk):
    n_tok = oh1_ref.shape[0]
    n_chunks = n_tok // chunk
    lane = lax.broadcasted_iota(jnp.int32, (SUBLANES, LANES), 1)

    def rows_of(ref, c):
        return ref[pl.ds(pl.multiple_of(c * chunk, chunk), chunk), :]

    def count_body(c, acc):
        both = rows_of(oh1_ref, c).astype(F32) + rows_of(oh2_ref, c).astype(F32)
        return acc + jnp.sum(both, axis=0, keepdims=True)

    counts = lax.fori_loop(0, n_chunks, count_body, jnp.zeros((1, LANES), F32))
    counts = jnp.broadcast_to(counts, (SUBLANES, LANES)).astype(jnp.int32)
    tile_shift = EXPERT_TILE.bit_length() - 1
    padded = ((counts + (EXPERT_TILE - 1)) >> tile_shift) << tile_shift
    ends = padded
    shift = 1
    while shift < N_EXPERTS:
        ends = ends + jnp.where(lane >= shift, pltpu.roll(ends, shift, axis=1), 0)
        shift *= 2
    base = (ends - padded)[0:1].astype(F32)

    r_iota = lax.broadcasted_iota(jnp.int32, (chunk, chunk), 0)
    c_iota = lax.broadcasted_iota(jnp.int32, (chunk, chunk), 1)
    earlier = jnp.where(c_iota < r_iota, 1.0, 0.0).astype(BF16)

    def rank_body(c, run):
        for slot, ref in enumerate((oh1_ref, oh2_ref)):
            oh = rows_of(ref, c)
            ohf = oh.astype(F32)
            before = _dot(earlier, oh) + run
            posv = jnp.sum(ohf * before, axis=1, keepdims=True)
            pos_row = jnp.broadcast_to(posv, (chunk, LANES)).T[0:1, :]
            pos_ref[pl.ds(slot * n_chunks + c, 1), :] = pos_row.astype(jnp.int32)
            run = run + jnp.sum(ohf, axis=0, keepdims=True)
        return run

    lax.fori_loop(0, n_chunks, rank_body, base)

    n_tile_lanes = te_ref.shape[1]
    tile_start = lax.broadcasted_iota(jnp.int32, (SUBLANES, n_tile_lanes), 1) * EXPERT_TILE
    te = jnp.zeros((SUBLANES, n_tile_lanes), jnp.int32)
    for e in range(N_EXPERTS):
        te = te + jnp.where(ends[:, e:e + 1] <= tile_start, 1, 0)
    te_ref[...] = jnp.minimum(te, N_EXPERTS - 1)
    sub = lax.broadcasted_iota(jnp.int32, (SUBLANES, LANES), 0)
    used_tiles = ends[:, N_EXPERTS - 1:N_EXPERTS] >> tile_shift
    first_pad = jnp.where(lane < N_EXPERTS, ends - padded + counts, used_tiles)
    misc_ref[...] = jnp.where(sub == 0, first_pad, ends)


def _plan(oh1, oh2, n_tiles, chunk):
    n = oh1.shape[0]
    n_tile_lanes = -(-n_tiles // LANES) * LANES
    whole = lambda shape: pl.BlockSpec(shape, lambda i: (0, 0))
    return pl.pallas_call(
        functools.partial(_plan_kernel, chunk=chunk),
        grid=(1,),
        in_specs=[whole((n, LANES)), whole((n, LANES))],
        out_specs=[whole((TOP_K * n // chunk, chunk)), whole((SUBLANES, n_tile_lanes)), whole((SUBLANES, LANES))],
        out_shape=[jax.ShapeDtypeStruct((TOP_K * n // chunk, chunk), jnp.int32),
                   jax.ShapeDtypeStruct((SUBLANES, n_tile_lanes), jnp.int32),
                   jax.ShapeDtypeStruct((SUBLANES, LANES), jnp.int32)],
        compiler_params=pltpu.CompilerParams(dimension_semantics=("arbitrary",), vmem_limit_bytes=VMEM_LIMIT),
        name="moe_plan",
    )(oh1, oh2)


def _row_slice(ref, row, n_rows=1):
    start = row * SUBLANES if isinstance(row, int) else pl.multiple_of(row * SUBLANES, SUBLANES)
    return ref.at[pl.ds(start, n_rows * SUBLANES)]


def _dispatch_kernel(pad_ref, end_ref, pos_ref, trow_ref, wsg_ref, wsu_ref, wsd_ref, xs_hbm, shared_ref, zero_buf,
                     zero_sem, row_sem, *, tokens):
    i = pl.program_id(0)

    def zero_fill(phase):
        def run(first_row, n_rows):
            copy = pltpu.make_async_copy(_row_slice(zero_buf, 0, n_rows), _row_slice(xs_hbm, first_row, n_rows),
                                         zero_sem)
            copy.start() if phase == "start" else copy.wait()

        for e in range(N_EXPERTS):
            row = pad_ref[e]
            n_pad = end_ref[e] - row
            bit = EXPERT_TILE // 2
            while bit >= 1:
                @pl.when((n_pad & bit) != 0)
                def _():
                    run(row, bit)
                row = row + (n_pad & bit)
                bit //= 2

        def tail(t, carry):
            run(t * EXPERT_TILE, EXPERT_TILE)
            return carry

        lax.fori_loop(pad_ref[N_EXPERTS], xs_hbm.shape[0] // (EXPERT_TILE * SUBLANES), tail, 0)

    @pl.when(i == 0)
    def _():
        zero_buf[...] = jnp.zeros(zero_buf.shape, F32)
        zero_fill("start")

    unroll = 8

    def issue(g, carry):
        for u in range(unroll):
            n = g * unroll + u
            src = _row_slice(trow_ref, n)
            for k in range(TOP_K):
                pltpu.make_async_copy(src, _row_slice(xs_hbm, pos_ref[k, n]), row_sem).start(priority=(u + k) % 2)
        return carry

    lax.fori_loop(0, tokens // unroll, issue, 0)

    t = _load_token_rows(trow_ref, tokens).astype(BF16)
    hid = _silu(_dot(t, wsg_ref[...])) * _dot(t, wsu_ref[...])
    shared_ref[...] = _dot(hid.astype(BF16), wsd_ref[...])

    for k in range(TOP_K):
        pltpu.make_async_copy(trow_ref, _row_slice(xs_hbm, 0, tokens), row_sem).wait()

    @pl.when(i + 1 == pl.num_programs(0))
    def _():
        zero_fill("wait")


def _dispatch(first_pad, seg_end, pos, trow, shared_w, sorted_rows, tokens):
    n = pos.shape[1]
    const = lambda a: pl.BlockSpec(a.shape, lambda i: (0,) * a.ndim)
    return pl.pallas_call(
        functools.partial(_dispatch_kernel, tokens=tokens),
        grid=(n // tokens,),
        in_specs=[pl.BlockSpec(memory_space=pltpu.SMEM), pl.BlockSpec(memory_space=pltpu.SMEM),
                  pl.BlockSpec((TOP_K, tokens), lambda i: (0, i), memory_space=pltpu.SMEM),
                  pl.BlockSpec((tokens * SUBLANES, LANES), lambda i: (i, 0))] + [const(w) for w in shared_w],
        out_specs=[pl.BlockSpec(memory_space=pl.ANY), pl.BlockSpec((tokens, D_MODEL), lambda i: (i, 0))],
        out_shape=[jax.ShapeDtypeStruct((sorted_rows * SUBLANES, LANES), F32),
                   jax.ShapeDtypeStruct((n, D_MODEL), F32)],
        scratch_shapes=[pltpu.VMEM((EXPERT_TILE * SUBLANES, LANES), F32), pltpu.SemaphoreType.DMA,
                        pltpu.SemaphoreType.DMA],
        compiler_params=pltpu.CompilerParams(dimension_semantics=("arbitrary",), vmem_limit_bytes=VMEM_LIMIT),
        name="moe_dispatch",
    )(first_pad, seg_end, pos, trow, *shared_w)


def _experts_kernel(te_ref, used_ref, xs_ref, wg_ref, wu_ref, wd_ref, ys_ref):
    t = pl.program_id(0)

    @pl.when(t < used_ref[0])
    def _():
        x = _load_token_rows(xs_ref, EXPERT_TILE).astype(BF16)
        hid = _silu(_dot(x, wg_ref[0].astype(BF16))) * _dot(x, wu_ref[0].astype(BF16))
        _store_token_rows(ys_ref, _dot(hid.astype(BF16), wd_ref[0].astype(BF16)))

    @pl.when(t >= used_ref[0])
    def _():
        ys_ref[...] = jnp.zeros(ys_ref.shape, F32)


def _experts(te, used, xs, wg, wu, wd, n_tiles):
    blk = EXPERT_TILE * SUBLANES
    return pl.pallas_call(
        _experts_kernel,
        grid_spec=pltpu.PrefetchScalarGridSpec(
            num_scalar_prefetch=2,
            grid=(n_tiles,),
            in_specs=[pl.BlockSpec((blk, LANES), lambda t, te, used: (jnp.minimum(t, used[0] - 1), 0)),
                      pl.BlockSpec((1, D_MODEL, D_FF), lambda t, te, used: (te[t], 0, 0)),
                      pl.BlockSpec((1, D_MODEL, D_FF), lambda t, te, used: (te[t], 0, 0)),
                      pl.BlockSpec((1, D_FF, D_MODEL), lambda t, te, used: (te[t], 0, 0))],
            out_specs=pl.BlockSpec((blk, LANES), lambda t, te, used: (t, 0))),
        out_shape=jax.ShapeDtypeStruct(xs.shape, F32),
        compiler_params=pltpu.CompilerParams(dimension_semantics=("arbitrary",), vmem_limit_bytes=VMEM_LIMIT),
        name="moe_experts",
    )(te, used, xs, wg, wu, wd)


def _combine_kernel(pos_ref, pos_next_ref, shared_ref, h1_ref, wts_ref, ys_hbm, gout_ref, out_ref, ybuf, sems, *,
                    tokens):
    i = pl.program_id(0)
    slot = lax.rem(i, 2)
    other = 1 - slot
    group = 32

    def start_gathers(p_ref, dst_slot, first):
        for u in range(group):
            n = first + u
            for k in range(TOP_K):
                pltpu.make_async_copy(_row_slice(ys_hbm, p_ref[k, n]), _row_slice(ybuf.at[dst_slot].at[k], n),
                                      sems.at[dst_slot]).start(priority=(u + k) % 2)

    def wait_slot(s):
        for k in range(TOP_K):
            pltpu.make_async_copy(_row_slice(ys_hbm, 0, tokens), ybuf.at[s].at[k], sems.at[s]).wait()

    @pl.when(i == 0)
    def _():
        def first_step(g, carry):
            start_gathers(pos_ref, slot, g * group)
            return carry
        lax.fori_loop(0, tokens // group, first_step, 0)

    wait_slot(slot)

    def trip(g, carry):
        first = pl.multiple_of(g * group, group)
        rows = pl.ds(first, group)
        wts = wts_ref[rows, :]
        routed = (wts[:, 0:1] * _load_token_rows(ybuf.at[slot].at[0], group, first)
                  + wts[:, 1:2] * _load_token_rows(ybuf.at[slot].at[1], group, first))
        out_ref[rows, :] = h1_ref[rows, :] + _rms(routed + shared_ref[rows, :], gout_ref[...])
        start_gathers(pos_next_ref, other, first)
        return carry

    lax.fori_loop(0, tokens // group, trip, 0)

    @pl.when(i + 1 == pl.num_programs(0))
    def _():
        wait_slot(other)


def _combine(pos, shared, h1, wts, ys, gout, tokens):
    n = h1.shape[0]
    const = lambda a: pl.BlockSpec(a.shape, lambda i: (0,) * a.ndim)
    tile = lambda w: pl.BlockSpec((tokens, w), lambda i: (i, 0))
    steps = n // tokens
    return pl.pallas_call(
        functools.partial(_combine_kernel, tokens=tokens),
        grid=(steps,),
        in_specs=[pl.BlockSpec((TOP_K, tokens), lambda i: (0, i), memory_space=pltpu.SMEM),
                  pl.BlockSpec((TOP_K, tokens), lambda i: (0, jnp.minimum(i + 1, steps - 1)),
                               memory_space=pltpu.SMEM),
                  tile(D_MODEL), tile(D_MODEL), tile(LANES), pl.BlockSpec(memory_space=pl.ANY), const(gout)],
        out_specs=tile(D_MODEL),
        out_shape=jax.ShapeDtypeStruct((n, D_MODEL), F32),
        scratch_shapes=[pltpu.VMEM((2, TOP_K, tokens * SUBLANES, LANES), F32), pltpu.SemaphoreType.DMA((2,))],
        compiler_params=pltpu.CompilerParams(dimension_semantics=("arbitrary",), vmem_limit_bytes=VMEM_LIMIT),
        name="moe_combine",
    )(pos, pos, shared, h1, wts, ys, gout)


def _swap_halves(w):
    half = w.shape[-1] // 2
    return jnp.concatenate([w[..., half:], w[..., :half]], axis=-1)


def _block_diag(blocks):
    h, r, c = blocks.shape
    eye = jnp.eye(h, dtype=blocks.dtype)
    return jnp.einsum('hrc,hg->hrgc', blocks, eye).reshape(h * r, h * c)


def kernel(x, positions, meta_tokens, pre_mix_norm, w_in, q_norm, w_uq, kv_norm, w_ukv, conv_w, attn_out_norm,
           conv_out_norm, w_out, post_mix_norm, pre_ffn_norm, w_group_router, b_group_router, w_expert_router,
           b_expert_router, w_gate, w_up, w_down, w_sh_gate, w_sh_up, w_sh_down, post_ffn_norm):
    bsz, seq, d = x.shape
    n = bsz * seq
    row = lambda v: v.reshape(1, -1).astype(F32)

    wi = w_in[0]
    o_kv, o_pe, o_b = Q_LORA, Q_LORA + KV_LORA, Q_LORA + KV_LORA + QK_ROPE
    k_pe = wi[:, o_pe:o_b]
    w1 = jnp.concatenate([wi[:, :o_pe], k_pe, _swap_halves(k_pe), jnp.zeros((d, LANES - 2 * QK_ROPE), F32),
                          wi[:, o_b:]], axis=1).astype(BF16)
    wq = w_uq[0].reshape(Q_LORA, N_HEADS, QK_NOPE + QK_ROPE)
    wqn = wq[:, :, :QK_NOPE].reshape(Q_LORA, N_HEADS * QK_NOPE).astype(F32)
    wq_rope = wq[:, :, QK_NOPE:]
    wqr = jnp.concatenate([wq_rope, _swap_halves(wq_rope),
                           jnp.zeros((Q_LORA, N_HEADS, LANES - 2 * QK_ROPE), F32)], axis=2)
    wqr = wqr.reshape(Q_LORA, N_HEADS * LANES).astype(F32)
    wkv = w_ukv[0].reshape(KV_LORA, N_HEADS, QK_NOPE + V_DIM)
    bduk = _block_diag(jnp.transpose(wkv[:, :, :QK_NOPE], (1, 2, 0))).astype(F32)
    wuv = _block_diag(jnp.transpose(wkv[:, :, QK_NOPE:], (1, 0, 2))).astype(BF16)
    premix_w = (row(pre_mix_norm[0]), w1, row(q_norm[0]), row(kv_norm[0]), wqn, bduk, wqr,
                conv_w[0].astype(F32), row(conv_out_norm[0]))
    wr = jnp.concatenate([w_group_router[0], jnp.zeros((d, LANES - N_GROUPS), F32),
                          w_expert_router[0], jnp.zeros((d, LANES - N_EXPERTS), F32)], axis=1).astype(BF16)
    br = jnp.concatenate([b_group_router[0], jnp.zeros((LANES - N_GROUPS,), F32),
                          b_expert_router[0], jnp.zeros((LANES - N_EXPERTS,), F32)]).reshape(1, -1).astype(F32)
    postmix_w = (wuv, row(attn_out_norm[0]), w_out[0].astype(BF16), row(post_mix_norm[0]), row(pre_ffn_norm[0]),
                 wr, br)
    expert_w = (w_gate[0], w_up[0], w_down[0])
    shared_w = (w_sh_gate[0].astype(BF16), w_sh_up[0].astype(BF16), w_sh_down[0].astype(BF16))

    pos_all = jnp.concatenate([positions.astype(jnp.int32).reshape(-1) + N_META,
                               jnp.zeros((PAD_FRONT,), jnp.int32), jnp.arange(N_META, dtype=jnp.int32)])
    inv_freq = 1.0 / (ROPE_THETA ** (jnp.arange(0, QK_ROPE, 2, dtype=F32) / QK_ROPE))
    n_tab = n + BLOCK
    tab = _rope_table(pos_all.reshape(1, n_tab), inv_freq.reshape(-1, 1), n_tab // 3)

    meta_blk = jnp.pad(meta_tokens.astype(F32), ((PAD_FRONT, 0), (0, 0)))[None]
    zero_carry = jnp.zeros((8, CONV_DIM), F32)
    _, kmeta, _, meta_tail = _premix(meta_blk, tab, n // BLOCK, zero_carry, premix_w, BLOCK, 1)
    qp, kc, convn, _ = _premix(x, tab, 0, meta_tail, premix_w, 1024, 2)

    olat = _attention(qp, kc, kmeta[0], 512, 512, 16)

    h1, trow, oh1, oh2, wts = _postmix(x.reshape(n, d), olat.reshape(n, -1), convn.reshape(n, -1), postmix_w, 1024, 2)

    sorted_rows = TOP_K * n + N_EXPERTS * EXPERT_TILE
    n_tiles = sorted_rows // EXPERT_TILE
    pos2, te8, misc = _plan(oh1, oh2, n_tiles, 512)
    pos = pos2.reshape(TOP_K, n)
    xs, shared = _dispatch(misc[0, :N_EXPERTS + 1], misc[1, :N_EXPERTS], pos, trow, shared_w, sorted_rows, 1024)
    ys = _experts(te8[0, :n_tiles], misc[0, N_EXPERTS:N_EXPERTS + 1], xs, *expert_w, n_tiles)
    out = _combine(pos, shared, h1, wts, ys, row(post_ffn_norm[0]), 512)
    return out.reshape(bsz, seq, d)
```

```python
import functools
import math

import jax
import jax.numpy as jnp
from jax import lax
from jax.experimental import pallas as pl
from jax.experimental.pallas import tpu as pltpu

D_MODEL = 1024
N_META = 16
BLOCK = 128
PAD_FRONT = BLOCK - N_META
N_HEADS = 8
QK_NOPE = 64
QK_ROPE = 32
V_DIM = 64
Q_LORA = 256
KV_LORA = 128
ROPE_THETA = 10000.0
ATTN_SCALE = (QK_NOPE + QK_ROPE) ** -0.5
Q_SCALE = ATTN_SCALE * math.log2(math.e)
ATTN_WIDTH = N_HEADS * V_DIM
CONV_DIM = 512
CONV_W = 3
N_GROUPS = 4
EXPERTS_PER_GROUP = 8
N_EXPERTS = N_GROUPS * EXPERTS_PER_GROUP
D_FF = 256
EPS = 1e-6
NEG_INF = -1e30

LANES = 128
SUBLANES = 8
TOP_K = 2
EXPERT_TILE = 512
KEY_WIDTH = 2 * LANES
IN_COLS_PADDED = Q_LORA + KV_LORA + LANES + 3 * CONV_DIM

F32 = jnp.float32
BF16 = jnp.bfloat16
VMEM_LIMIT = 56 * 1024 * 1024

ROPE_STEPS = 3
MIX_ROWS = 1024
MIX_SUB_BLOCKS = 2
ATTN_BLOCK = 512
ATTN_CHUNKS = 16
PLAN_CHUNK = 512
DISPATCH_TOKENS = 1024
DISPATCH_PARTS = 4
COMBINE_TOKENS = 512


def _rms(x, g):
    return x * lax.rsqrt(jnp.mean(x * x, axis=-1, keepdims=True) + EPS) * g


def _dot(a, b):
    return jnp.dot(a, b, preferred_element_type=F32)


def _dot_nt(a, b):
    return lax.dot_general(a, b, (((1,), (1,)), ((), ())), preferred_element_type=F32)


def _store_token_rows(ref, value, first_token=0):
    rows = value.shape[0]
    for c in range(D_MODEL // LANES):
        ref[pl.ds(first_token * SUBLANES + c, rows, stride=SUBLANES), :] = value[:, c * LANES:(c + 1) * LANES]


def _load_token_rows(ref, rows, first_token=0):
    return jnp.concatenate([ref[pl.ds(first_token * SUBLANES + c, rows, stride=SUBLANES), :]
                            for c in range(D_MODEL // LANES)], axis=1)


def _rope_table_kernel(pos_ref, invf_ref, out_ref):
    pos = pos_ref[...].astype(F32)
    ang = invf_ref[...] * pos
    c = jnp.cos(ang)
    s = jnp.sin(ang)
    planes = jnp.concatenate([c, c, -s, s, jnp.zeros((LANES - 4 * (QK_ROPE // 2), ang.shape[1]), F32)], axis=0)
    out_ref[...] = planes.T


def _rope_table(pos_row, inv_freq_col, chunk):
    n = pos_row.shape[1]
    return pl.pallas_call(
        _rope_table_kernel,
        grid=(n // chunk,),
        in_specs=[pl.BlockSpec((1, chunk), lambda i: (0, i)),
                  pl.BlockSpec((QK_ROPE // 2, 1), lambda i: (0, 0))],
        out_specs=pl.BlockSpec((chunk, LANES), lambda i: (i, 0)),
        out_shape=jax.ShapeDtypeStruct((n, LANES), F32),
        name="rope_table",
    )(pos_row, inv_freq_col)


def _premix_kernel(x_ref, tab_ref, carry_in_ref, g_pre_ref, w1_ref, gq_ref, gkv_ref, wqn_ref, bduk_ref,
                   wqr_ref, convw_ref, gconv_ref, qp_ref, kc_ref, convn_ref, utail_ref, carry_ref, wq_scr, *,
                   sub_blocks):
    t = pl.program_id(1)
    rows = x_ref.shape[1]

    @pl.when(t == 0)
    def _():
        carry_ref[...] = carry_in_ref[...]

    @pl.when((pl.program_id(0) == 0) & (t == 0))
    def _():
        wabs = jnp.dot(wqn_ref[...], bduk_ref[...], preferred_element_type=F32,
                       precision=lax.Precision.HIGHEST) * Q_SCALE
        wrope = wqr_ref[...] * Q_SCALE
        for h in range(N_HEADS):
            sl = slice(h * LANES, (h + 1) * LANES)
            wq_scr[:, h * KEY_WIDTH:h * KEY_WIDTH + LANES] = wabs[:, sl].astype(BF16)
            wq_scr[:, h * KEY_WIDTH + LANES:(h + 1) * KEY_WIDTH] = wrope[:, sl].astype(BF16)

    rows = rows // sub_blocks
    lane = lax.broadcasted_iota(jnp.int32, (rows, LANES), 1)
    row8 = lax.broadcasted_iota(jnp.int32, (8, CONV_DIM), 0)
    cw = convw_ref[...]
    prev = carry_ref[...]
    for sb in range(sub_blocks):
        r = pl.ds(sb * rows, rows)
        xn = _rms(x_ref[0, r, :], g_pre_ref[...]).astype(BF16)
        z = _dot(xn, w1_ref[...])
        tab = tab_ref[r, :]

        def rope(pair):
            prod = pair * tab
            return jnp.where(lane < QK_ROPE, prod + pltpu.roll(prod, LANES - QK_ROPE, axis=1), 0.0)

        ckvn = _rms(z[:, Q_LORA:Q_LORA + KV_LORA], gkv_ref[...])
        krope = rope(z[:, Q_LORA + KV_LORA:Q_LORA + KV_LORA + LANES])
        kc_ref[0, r, :] = jnp.concatenate([ckvn, krope], axis=1).astype(BF16)

        cqn = _rms(z[:, :Q_LORA], gq_ref[...]).astype(BF16)
        qall = _dot(cqn, wq_scr[...])
        for h in range(N_HEADS):
            qp_ref[0, h, r, 0:LANES] = qall[:, h * KEY_WIDTH:h * KEY_WIDTH + LANES].astype(BF16)
            qp_ref[0, h, r, LANES:KEY_WIDTH] = rope(
                qall[:, h * KEY_WIDTH + LANES:(h + 1) * KEY_WIDTH]).astype(BF16)

        c0 = Q_LORA + KV_LORA + LANES
        gate_b = z[:, c0:c0 + CONV_DIM]
        u = z[:, c0 + CONV_DIM:c0 + 2 * CONV_DIM] * z[:, c0 + 2 * CONV_DIM:c0 + 3 * CONV_DIM]
        r1 = pltpu.roll(u, 1, axis=0)
        r2 = pltpu.roll(u, 2, axis=0)
        p1 = pltpu.roll(prev, 1, axis=0)
        p2 = pltpu.roll(prev, 2, axis=0)
        u1 = jnp.concatenate([jnp.where(row8 < 1, p1, r1[0:8]), r1[8:]], axis=0)
        u2 = jnp.concatenate([jnp.where(row8 < 2, p2, r2[0:8]), r2[8:]], axis=0)
        y = cw[0:1] * u2 + cw[1:2] * u1 + cw[2:3] * u
        convn_ref[0, r, :] = _rms(gate_b * y, gconv_ref[...]).astype(BF16)
        prev = u[rows - 8:rows]
    carry_ref[...] = prev
    utail_ref[...] = prev


def _premix(x3, tab, tab_block0, carry_in, wts, rows, sub_blocks):
    bsz, seq, _ = x3.shape
    nt = seq // rows
    g_pre, w1, gq, gkv, wqn, bduk, wqr, convw, gconv = wts
    const = lambda a: pl.BlockSpec(a.shape, lambda b, t: (0,) * a.ndim)
    return pl.pallas_call(
        functools.partial(_premix_kernel, sub_blocks=sub_blocks),
        grid=(bsz, nt),
        in_specs=[pl.BlockSpec((1, rows, D_MODEL), lambda b, t: (b, t, 0)),
                  pl.BlockSpec((rows, LANES), lambda b, t: (tab_block0 + b * nt + t, 0)),
                  const(carry_in), const(g_pre), const(w1), const(gq), const(gkv), const(wqn), const(bduk),
                  const(wqr), const(convw), const(gconv)],
        out_specs=[pl.BlockSpec((1, N_HEADS, rows, KEY_WIDTH), lambda b, t: (b, 0, t, 0)),
                   pl.BlockSpec((1, rows, KEY_WIDTH), lambda b, t: (b, t, 0)),
                   pl.BlockSpec((1, rows, CONV_DIM), lambda b, t: (b, t, 0)),
                   pl.BlockSpec((8, CONV_DIM), lambda b, t: (0, 0))],
        out_shape=[jax.ShapeDtypeStruct((bsz, N_HEADS, seq, KEY_WIDTH), BF16),
                   jax.ShapeDtypeStruct((bsz, seq, KEY_WIDTH), BF16),
                   jax.ShapeDtypeStruct((bsz, seq, CONV_DIM), BF16),
                   jax.ShapeDtypeStruct((8, CONV_DIM), F32)],
        scratch_shapes=[pltpu.VMEM((8, CONV_DIM), F32), pltpu.VMEM((Q_LORA, N_HEADS * KEY_WIDTH), BF16)],
        compiler_params=pltpu.CompilerParams(dimension_semantics=("arbitrary", "arbitrary"),
                                             vmem_limit_bytes=VMEM_LIMIT),
        name="premix",
    )(x3, tab, carry_in, g_pre, w1, gq, gkv, wqn, bduk, wqr, convw, gconv)


def _attention_kernel(q_ref, qnext_ref, k_ref, km_ref, o_ref, s_scr, smeta_scr, tri_scr, m_scr, l_scr, acc_scr, *,
                      bq, bk, chunks):
    assert bq == bk
    b = pl.program_id(0)
    i = pl.program_id(1)
    assert chunks % N_HEADS == 0
    row_splits = chunks // N_HEADS
    c_rows = bq // row_splits
    km = km_ref[...]

    def chunk_rows(c):
        return c // row_splits, pl.ds((c % row_splits) * c_rows, c_rows)

    def scores(qr, c, kb):
        h, r = chunk_rows(c)
        return _dot_nt(qr[0, h, r, :], kb)

    def values(kb):
        return jnp.concatenate([kb[:, :KV_LORA], jnp.ones((kb.shape[0], LANES), BF16)], axis=1)

    def consume(c, s, v_ones, mask, first=False):
        if mask is not None:
            s = jnp.where(mask, s, NEG_INF)
        m_blk = jnp.broadcast_to(jnp.max(s, axis=1, keepdims=True), (c_rows, LANES))
        m_new = m_blk if first else jnp.maximum(m_scr[c], m_blk)
        p = jnp.exp2(s - jnp.concatenate([m_new] * (s.shape[1] // LANES), axis=1)).astype(BF16)
        pv = _dot(p, v_ones)
        if first:
            l_scr[c] = pv[:, KV_LORA:]
            acc_scr[c] = pv[:, :KV_LORA]
        else:
            alpha = jnp.exp2(m_scr[c] - m_new)
            l_scr[c] = alpha * l_scr[c] + pv[:, KV_LORA:]
            acc_scr[c] = alpha * acc_scr[c] + pv[:, :KV_LORA]
        m_scr[c] = m_new

    def key_block(j):
        return k_ref[0, pl.ds(pl.multiple_of(j * bk, bk), bk), :]

    @pl.when((b == 0) & (i == 0))
    def _():
        for c in range(chunks):
            smeta_scr[c] = scores(q_ref, c, km)
        rows_i = lax.broadcasted_iota(jnp.int32, (bq, bk), 0)
        cols_i = lax.broadcasted_iota(jnp.int32, (bq, bk), 1)
        tri_scr[...] = jnp.where(cols_i <= rows_i, 0.0, NEG_INF)

    n_full = (i * bq) // bk
    kb0 = key_block(0)
    v_meta = values(km)
    col = lax.broadcasted_iota(jnp.int32, (c_rows, BLOCK), 1)
    for c in range(chunks):
        s_scr[c] = scores(q_ref, c, kb0)
        consume(c, smeta_scr[c], v_meta, col >= PAD_FRONT, first=True)

    def full_block(j):
        v_ones = values(key_block(j))
        kb_next = key_block(j + 1)
        for c in range(chunks):
            s = s_scr[c]
            s_scr[c] = scores(q_ref, c, kb_next)
            consume(c, s, v_ones, None)

    def body(j, carry):
        full_block(j)
        return carry

    lax.fori_loop(0, n_full, body, 0)

    v_ones = values(key_block(n_full))
    for c in range(chunks):
        _, r = chunk_rows(c)
        n_keys = (c % row_splits + 1) * c_rows
        s = s_scr[c, :, :n_keys] + tri_scr[r, :n_keys]
        smeta_scr[c] = scores(qnext_ref, c, km)
        consume(c, s, v_ones[:n_keys], None)

    for c in range(chunks):
        h, r = chunk_rows(c)
        o_ref[0, r, h * KV_LORA:(h + 1) * KV_LORA] = (acc_scr[c] / l_scr[c]).astype(BF16)


def _attention(qp, kc, kmeta, bq, bk, chunks):
    bsz, _, seq, _ = qp.shape
    nq = seq // bq
    c_rows = N_HEADS * bq // chunks
    scratch = [pltpu.VMEM((chunks, c_rows, bk), F32), pltpu.VMEM((chunks, c_rows, BLOCK), F32),
               pltpu.VMEM((bq, bk), F32), pltpu.VMEM((chunks, c_rows, LANES), F32),
               pltpu.VMEM((chunks, c_rows, LANES), F32), pltpu.VMEM((chunks, c_rows, KV_LORA), F32)]

    def next_q(b, i):
        wrap = (i + 1) // nq
        return (jnp.minimum(b + wrap, bsz - 1), 0, i + 1 - wrap * nq, 0)

    return pl.pallas_call(
        functools.partial(_attention_kernel, bq=bq, bk=bk, chunks=chunks),
        grid=(bsz, nq),
        in_specs=[pl.BlockSpec((1, N_HEADS, bq, KEY_WIDTH), lambda b, i: (b, 0, i, 0)),
                  pl.BlockSpec((1, N_HEADS, bq, KEY_WIDTH), next_q),
                  pl.BlockSpec((1, seq, KEY_WIDTH), lambda b, i: (b, 0, 0)),
                  pl.BlockSpec((BLOCK, KEY_WIDTH), lambda b, i: (0, 0))],
        out_specs=pl.BlockSpec((1, bq, N_HEADS * KV_LORA), lambda b, i: (b, i, 0)),
        out_shape=jax.ShapeDtypeStruct((bsz, seq, N_HEADS * KV_LORA), BF16),
        scratch_shapes=scratch,
        compiler_params=pltpu.CompilerParams(dimension_semantics=("arbitrary", "arbitrary"),
                                             vmem_limit_bytes=VMEM_LIMIT),
        name="attention",
    )(qp, qp, kc, kmeta)


def _postmix_kernel(x_ref, olat_ref, convn_ref, wuv_ref, gattn_ref, wout_ref, gpost_ref, gffn_ref, wr_ref, br_ref,
                    h1_ref, trow_ref, oh1_ref, oh2_ref, wts_ref, *, sub_blocks):
    rows = x_ref.shape[0] // sub_blocks
    wout = wout_ref[...]
    lane = lax.broadcasted_iota(jnp.int32, (rows, LANES), 1)

    def first_argmax(v):
        mx = jnp.max(v, axis=1, keepdims=True)
        return mx, jnp.min(jnp.where(v == mx, lane, LANES), axis=1, keepdims=True)

    for sb in range(sub_blocks):
        r = pl.ds(sb * rows, rows)
        attn = _dot(olat_ref[r, :], wuv_ref[...])
        attn_n = _rms(attn, gattn_ref[...]).astype(BF16)
        mix = _dot(attn_n, wout[:ATTN_WIDTH]) + _dot(convn_ref[r, :], wout[ATTN_WIDTH:])
        h1 = x_ref[r, :] + _rms(mix, gpost_ref[...])
        h1_ref[r, :] = h1
        tf = _rms(h1, gffn_ref[...])
        t = tf.astype(BF16)
        _store_token_rows(trow_ref, tf, sb * rows)

        logits = _dot(t, wr_ref[...]) + br_ref[...]
        gl = jnp.where(lane < N_GROUPS, logits[:, :LANES], NEG_INF)
        gmax, gidx = first_argmax(gl)
        g_w = 1.0 / jnp.sum(jnp.exp(gl - gmax), axis=1, keepdims=True)
        el = jnp.where((lane >> 3) == gidx, logits[:, LANES:], NEG_INF)
        m1, i1 = first_argmax(el)
        m2, i2 = first_argmax(jnp.where(lane == i1, NEG_INF, el))
        e2 = jnp.exp(m2 - m1)
        den = 1.0 + e2
        oh1_ref[r, :] = jnp.where(lane == i1, 1.0, 0.0).astype(BF16)
        oh2_ref[r, :] = jnp.where(lane == i2, 1.0, 0.0).astype(BF16)
        wts_ref[r, :] = jnp.where(lane == 0, g_w / den, jnp.where(lane == 1, g_w * e2 / den, 0.0))


def _postmix(x2, olat2, convn2, wts, rows, sub_blocks):
    n = x2.shape[0]
    wuv, gattn, wout, gpost, gffn, wr, br = wts
    const = lambda a: pl.BlockSpec(a.shape, lambda i: (0,) * a.ndim)
    tile = lambda w: pl.BlockSpec((rows, w), lambda i: (i, 0))
    return pl.pallas_call(
        functools.partial(_postmix_kernel, sub_blocks=sub_blocks),
        grid=(n // rows,),
        in_specs=[tile(D_MODEL), tile(N_HEADS * KV_LORA), tile(CONV_DIM),
                  const(wuv), const(gattn), const(wout), const(gpost), const(gffn), const(wr), const(br)],
        out_specs=[tile(D_MODEL), pl.BlockSpec((rows * SUBLANES, LANES), lambda i: (i, 0)),
                   tile(LANES), tile(LANES), tile(LANES)],
        out_shape=[jax.ShapeDtypeStruct((n, D_MODEL), F32),
                   jax.ShapeDtypeStruct((n * SUBLANES, LANES), F32),
                   jax.ShapeDtypeStruct((n, LANES), BF16),
                   jax.ShapeDtypeStruct((n, LANES), BF16),
                   jax.ShapeDtypeStruct((n, LANES), F32)],
        compiler_params=pltpu.CompilerParams(dimension_semantics=("arbitrary",), vmem_limit_bytes=VMEM_LIMIT),
        name="postmix",
    )(x2, olat2, convn2, wuv, gattn, wout, gpost, gffn, wr, br)


def _silu(a):
    return a * (1.0 / (1.0 + jnp.exp(-a)))


def _plan_kernel(oh1_ref, oh2_ref, pos_ref, te_ref, misc_ref, *, chunk):
    n_tok = oh1_ref.shape[0]
    n_chunks = n_tok // chunk
    lane = lax.broadcasted_iota(jnp.int32, (SUBLANES, LANES), 1)

    def rows_of(ref, c):
        return ref[pl.ds(pl.multiple_of(c * chunk, chunk), chunk), :]

    def count_body(c, acc):
        both = rows_of(oh1_ref, c).astype(F32) + rows_of(oh2_ref, c).astype(F32)
        return acc + jnp.sum(both, axis=0, keepdims=True)

    counts = lax.fori_loop(0, n_chunks, count_body, jnp.zeros((1, LANES), F32))
    counts = jnp.broadcast_to(counts, (SUBLANES, LANES)).astype(jnp.int32)
    tile_shift = EXPERT_TILE.bit_length() - 1
    padded = ((counts + (EXPERT_TILE - 1)) >> tile_shift) << tile_shift
    ends = padded
    shift = 1
    while shift < N_EXPERTS:
        ends = ends + jnp.where(lane >= shift, pltpu.roll(ends, shift, axis=1), 0)
        shift *= 2
    base = (ends - padded)[0:1].astype(F32)

    r_iota = lax.broadcasted_iota(jnp.int32, (chunk, chunk), 0)
    c_iota = lax.broadcasted_iota(jnp.int32, (chunk, chunk), 1)
    earlier = jnp.where(c_iota < r_iota, 1.0, 0.0).astype(BF16)

    def rank_body(c, run):
        for slot, ref in enumerate((oh1_ref, oh2_ref)):
            oh = rows_of(ref, c)
            ohf = oh.astype(F32)
            before = _dot(earlier, oh) + run
            posv = jnp.sum(ohf * before, axis=1, keepdims=True)
            pos_row = jnp.broadcast_to(posv, (chunk, LANES)).T[0:1, :]
            pos_ref[pl.ds(slot * n_chunks + c, 1), :] = pos_row.astype(jnp.int32)
            run = run + jnp.sum(ohf, axis=0, keepdims=True)
        return run

    lax.fori_loop(0, n_chunks, rank_body, base)

    n_tile_lanes = te_ref.shape[1]
    tile_start = lax.broadcasted_iota(jnp.int32, (SUBLANES, n_tile_lanes), 1) * EXPERT_TILE
    te = jnp.zeros((SUBLANES, n_tile_lanes), jnp.int32)
    for e in range(N_EXPERTS):
        te = te + jnp.where(ends[:, e:e + 1] <= tile_start, 1, 0)
    te_ref[...] = jnp.minimum(te, N_EXPERTS - 1)
    sub = lax.broadcasted_iota(jnp.int32, (SUBLANES, LANES), 0)
    used_tiles = ends[:, N_EXPERTS - 1:N_EXPERTS] >> tile_shift
    first_pad = jnp.where(lane < N_EXPERTS, ends - padded + counts, used_tiles)
    misc_ref[...] = jnp.where(sub == 0, first_pad, ends)


def _plan(oh1, oh2, n_tiles, chunk):
    n = oh1.shape[0]
    n_tile_lanes = -(-n_tiles // LANES) * LANES
    whole = lambda shape: pl.BlockSpec(shape, lambda i: (0, 0))
    return pl.pallas_call(
        functools.partial(_plan_kernel, chunk=chunk),
        grid=(1,),
        in_specs=[whole((n, LANES)), whole((n, LANES))],
        out_specs=[whole((TOP_K * n // chunk, chunk)), whole((SUBLANES, n_tile_lanes)), whole((SUBLANES, LANES))],
        out_shape=[jax.ShapeDtypeStruct((TOP_K * n // chunk, chunk), jnp.int32),
                   jax.ShapeDtypeStruct((SUBLANES, n_tile_lanes), jnp.int32),
                   jax.ShapeDtypeStruct((SUBLANES, LANES), jnp.int32)],
        compiler_params=pltpu.CompilerParams(dimension_semantics=("arbitrary",), vmem_limit_bytes=VMEM_LIMIT),
        name="moe_plan",
    )(oh1, oh2)


def _row_slice(ref, row, n_rows=1):
    start = row * SUBLANES if isinstance(row, int) else pl.multiple_of(row * SUBLANES, SUBLANES)
    return ref.at[pl.ds(start, n_rows * SUBLANES)]


def _dispatch_kernel(pad_ref, end_ref, pos_ref, trow_ref, wsg_ref, wsu_ref, wsd_ref, xs_hbm, shared_ref, zero_buf,
                     zero_sem, row_sem, *, tokens, parts):
    i = pl.program_id(0)

    def zero_fill(phase):
        def run(first_row, n_rows):
            copy = pltpu.make_async_copy(_row_slice(zero_buf, 0, n_rows), _row_slice(xs_hbm, first_row, n_rows),
                                         zero_sem)
            copy.start() if phase == "start" else copy.wait()

        for e in range(N_EXPERTS):
            row = pad_ref[e]
            n_pad = end_ref[e] - row
            bit = EXPERT_TILE // 2
            while bit >= 1:
                @pl.when((n_pad & bit) != 0)
                def _():
                    run(row, bit)
                row = row + (n_pad & bit)
                bit //= 2

        def tail(t, carry):
            run(t * EXPERT_TILE, EXPERT_TILE)
            return carry

        lax.fori_loop(pad_ref[N_EXPERTS], xs_hbm.shape[0] // (EXPERT_TILE * SUBLANES), tail, 0)

    @pl.when(i == 0)
    def _():
        zero_buf[...] = jnp.zeros(zero_buf.shape, F32)
        zero_fill("start")

    unroll = 8

    def issue(g, carry):
        for u in range(unroll):
            n = g * unroll + u
            src = _row_slice(trow_ref, n)
            for k in range(TOP_K):
                pltpu.make_async_copy(src, _row_slice(xs_hbm, pos_ref[k, n]), row_sem).start(priority=(u + k) % 2)
        return carry

    part = tokens // parts
    for p in range(parts):
        lax.fori_loop(p * part // unroll, (p + 1) * part // unroll, issue, 0)
        t = _load_token_rows(trow_ref, part, p * part).astype(BF16)
        hid = _silu(_dot(t, wsg_ref[...])) * _dot(t, wsu_ref[...])
        shared_ref[p * part:(p + 1) * part, :] = _dot(hid.astype(BF16), wsd_ref[...])

    for k in range(TOP_K):
        pltpu.make_async_copy(trow_ref, _row_slice(xs_hbm, 0, tokens), row_sem).wait()

    @pl.when(i + 1 == pl.num_programs(0))
    def _():
        zero_fill("wait")


def _dispatch(first_pad, seg_end, pos, trow, shared_w, sorted_rows, tokens):
    n = pos.shape[1]
    const = lambda a: pl.BlockSpec(a.shape, lambda i: (0,) * a.ndim)
    return pl.pallas_call(
        functools.partial(_dispatch_kernel, tokens=tokens, parts=DISPATCH_PARTS),
        grid=(n // tokens,),
        in_specs=[pl.BlockSpec(memory_space=pltpu.SMEM), pl.BlockSpec(memory_space=pltpu.SMEM),
                  pl.BlockSpec((TOP_K, tokens), lambda i: (0, i), memory_space=pltpu.SMEM),
                  pl.BlockSpec((tokens * SUBLANES, LANES), lambda i: (i, 0))] + [const(w) for w in shared_w],
        out_specs=[pl.BlockSpec(memory_space=pl.ANY), pl.BlockSpec((tokens, D_MODEL), lambda i: (i, 0))],
        out_shape=[jax.ShapeDtypeStruct((sorted_rows * SUBLANES, LANES), F32),
                   jax.ShapeDtypeStruct((n, D_MODEL), F32)],
        scratch_shapes=[pltpu.VMEM((EXPERT_TILE * SUBLANES, LANES), F32), pltpu.SemaphoreType.DMA,
                        pltpu.SemaphoreType.DMA],
        compiler_params=pltpu.CompilerParams(dimension_semantics=("arbitrary",), vmem_limit_bytes=VMEM_LIMIT),
        name="moe_dispatch",
    )(first_pad, seg_end, pos, trow, *shared_w)


def _experts_kernel(te_ref, used_ref, xs_ref, wg_ref, wu_ref, wd_ref, ys_ref):
    t = pl.program_id(0)

    @pl.when(t < used_ref[0])
    def _():
        x = _load_token_rows(xs_ref, EXPERT_TILE).astype(BF16)
        hid = _silu(_dot(x, wg_ref[0].astype(BF16))) * _dot(x, wu_ref[0].astype(BF16))
        _store_token_rows(ys_ref, _dot(hid.astype(BF16), wd_ref[0].astype(BF16)))

    @pl.when(t >= used_ref[0])
    def _():
        ys_ref[...] = jnp.zeros(ys_ref.shape, F32)


def _experts(te, used, xs, wg, wu, wd, n_tiles):
    blk = EXPERT_TILE * SUBLANES
    return pl.pallas_call(
        _experts_kernel,
        grid_spec=pltpu.PrefetchScalarGridSpec(
            num_scalar_prefetch=2,
            grid=(n_tiles,),
            in_specs=[pl.BlockSpec((blk, LANES), lambda t, te, used: (jnp.minimum(t, used[0] - 1), 0)),
                      pl.BlockSpec((1, D_MODEL, D_FF), lambda t, te, used: (te[t], 0, 0)),
                      pl.BlockSpec((1, D_MODEL, D_FF), lambda t, te, used: (te[t], 0, 0)),
                      pl.BlockSpec((1, D_FF, D_MODEL), lambda t, te, used: (te[t], 0, 0))],
            out_specs=pl.BlockSpec((blk, LANES), lambda t, te, used: (t, 0))),
        out_shape=jax.ShapeDtypeStruct(xs.shape, F32),
        compiler_params=pltpu.CompilerParams(dimension_semantics=("arbitrary",), vmem_limit_bytes=VMEM_LIMIT),
        name="moe_experts",
    )(te, used, xs, wg, wu, wd)


def _combine_kernel(pos_ref, pos_next_ref, shared_ref, h1_ref, wts_ref, ys_hbm, gout_ref, out_ref, ybuf, sems, *,
                    tokens):
    i = pl.program_id(0)
    slot = lax.rem(i, 2)
    other = 1 - slot
    group = 32

    def start_gathers(p_ref, dst_slot, first):
        for u in range(group):
            n = first + u
            for k in range(TOP_K):
                pltpu.make_async_copy(_row_slice(ys_hbm, p_ref[k, n]), _row_slice(ybuf.at[dst_slot].at[k], n),
                                      sems.at[dst_slot]).start(priority=(u + k) % 2)

    def wait_slot(s):
        for k in range(TOP_K):
            pltpu.make_async_copy(_row_slice(ys_hbm, 0, tokens), ybuf.at[s].at[k], sems.at[s]).wait()

    @pl.when(i == 0)
    def _():
        def first_step(g, carry):
            start_gathers(pos_ref, slot, g * group)
            return carry
        lax.fori_loop(0, tokens // group, first_step, 0)

    wait_slot(slot)

    def trip(g, carry):
        first = pl.multiple_of(g * group, group)
        rows = pl.ds(first, group)
        wts = wts_ref[rows, :]
        routed = (wts[:, 0:1] * _load_token_rows(ybuf.at[slot].at[0], group, first)
                  + wts[:, 1:2] * _load_token_rows(ybuf.at[slot].at[1], group, first))
        out_ref[rows, :] = h1_ref[rows, :] + _rms(routed + shared_ref[rows, :], gout_ref[...])
        start_gathers(pos_next_ref, other, first)
        return carry

    lax.fori_loop(0, tokens // group, trip, 0)

    @pl.when(i + 1 == pl.num_programs(0))
    def _():
        wait_slot(other)


def _combine(pos, shared, h1, wts, ys, gout, tokens):
    n = h1.shape[0]
    const = lambda a: pl.BlockSpec(a.shape, lambda i: (0,) * a.ndim)
    tile = lambda w: pl.BlockSpec((tokens, w), lambda i: (i, 0))
    steps = n // tokens
    return pl.pallas_call(
        functools.partial(_combine_kernel, tokens=tokens),
        grid=(steps,),
        in_specs=[pl.BlockSpec((TOP_K, tokens), lambda i: (0, i), memory_space=pltpu.SMEM),
                  pl.BlockSpec((TOP_K, tokens), lambda i: (0, jnp.minimum(i + 1, steps - 1)),
                               memory_space=pltpu.SMEM),
                  tile(D_MODEL), tile(D_MODEL), tile(LANES), pl.BlockSpec(memory_space=pl.ANY), const(gout)],
        out_specs=tile(D_MODEL),
        out_shape=jax.ShapeDtypeStruct((n, D_MODEL), F32),
        scratch_shapes=[pltpu.VMEM((2, TOP_K, tokens * SUBLANES, LANES), F32), pltpu.SemaphoreType.DMA((2,))],
        compiler_params=pltpu.CompilerParams(dimension_semantics=("arbitrary",), vmem_limit_bytes=VMEM_LIMIT),
        name="moe_combine",
    )(pos, pos, shared, h1, wts, ys, gout)


def _swap_halves(w):
    half = w.shape[-1] // 2
    return jnp.concatenate([w[..., half:], w[..., :half]], axis=-1)


def _block_diag(blocks):
    h, r, c = blocks.shape
    eye = jnp.eye(h, dtype=blocks.dtype)
    return jnp.einsum('hrc,hg->hrgc', blocks, eye).reshape(h * r, h * c)


def kernel(x, positions, meta_tokens, pre_mix_norm, w_in, q_norm, w_uq, kv_norm, w_ukv, conv_w, attn_out_norm,
           conv_out_norm, w_out, post_mix_norm, pre_ffn_norm, w_group_router, b_group_router, w_expert_router,
           b_expert_router, w_gate, w_up, w_down, w_sh_gate, w_sh_up, w_sh_down, post_ffn_norm):
    bsz, seq, d = x.shape
    n = bsz * seq
    row = lambda v: v.reshape(1, -1).astype(F32)
    assert d == D_MODEL and w_in.shape[0] == 1, "one layer of width 1024 is supported"
    assert seq % MIX_ROWS == 0 and seq % ATTN_BLOCK == 0 and n % DISPATCH_TOKENS == 0
    assert (n + BLOCK) % (ROPE_STEPS * LANES) == 0

    wi = w_in[0]
    o_kv, o_pe, o_b = Q_LORA, Q_LORA + KV_LORA, Q_LORA + KV_LORA + QK_ROPE
    k_pe = wi[:, o_pe:o_b]
    w1 = jnp.concatenate([wi[:, :o_pe], k_pe, _swap_halves(k_pe), jnp.zeros((d, LANES - 2 * QK_ROPE), F32),
                          wi[:, o_b:]], axis=1).astype(BF16)
    wq = w_uq[0].reshape(Q_LORA, N_HEADS, QK_NOPE + QK_ROPE)
    wqn = wq[:, :, :QK_NOPE].reshape(Q_LORA, N_HEADS * QK_NOPE).astype(F32)
    wq_rope = wq[:, :, QK_NOPE:]
    wqr = jnp.concatenate([wq_rope, _swap_halves(wq_rope),
                           jnp.zeros((Q_LORA, N_HEADS, LANES - 2 * QK_ROPE), F32)], axis=2)
    wqr = wqr.reshape(Q_LORA, N_HEADS * LANES).astype(F32)
    wkv = w_ukv[0].reshape(KV_LORA, N_HEADS, QK_NOPE + V_DIM)
    bduk = _block_diag(jnp.transpose(wkv[:, :, :QK_NOPE], (1, 2, 0))).astype(F32)
    wuv = _block_diag(jnp.transpose(wkv[:, :, QK_NOPE:], (1, 0, 2))).astype(BF16)
    premix_w = (row(pre_mix_norm[0]), w1, row(q_norm[0]), row(kv_norm[0]), wqn, bduk, wqr,
                conv_w[0].astype(F32), row(conv_out_norm[0]))
    wr = jnp.concatenate([w_group_router[0], jnp.zeros((d, LANES - N_GROUPS), F32),
                          w_expert_router[0], jnp.zeros((d, LANES - N_EXPERTS), F32)], axis=1).astype(BF16)
    br = jnp.concatenate([b_group_router[0], jnp.zeros((LANES - N_GROUPS,), F32),
                          b_expert_router[0], jnp.zeros((LANES - N_EXPERTS,), F32)]).reshape(1, -1).astype(F32)
    postmix_w = (wuv, row(attn_out_norm[0]), w_out[0].astype(BF16), row(post_mix_norm[0]), row(pre_ffn_norm[0]),
                 wr, br)
    expert_w = (w_gate[0], w_up[0], w_down[0])
    shared_w = (w_sh_gate[0].astype(BF16), w_sh_up[0].astype(BF16), w_sh_down[0].astype(BF16))

    pos_all = jnp.concatenate([positions.astype(jnp.int32).reshape(-1) + N_META,
                               jnp.zeros((PAD_FRONT,), jnp.int32), jnp.arange(N_META, dtype=jnp.int32)])
    inv_freq = 1.0 / (ROPE_THETA ** (jnp.arange(0, QK_ROPE, 2, dtype=F32) / QK_ROPE))
    n_tab = n + BLOCK
    tab = _rope_table(pos_all.reshape(1, n_tab), inv_freq.reshape(-1, 1), n_tab // ROPE_STEPS)

    meta_blk = jnp.pad(meta_tokens.astype(F32), ((PAD_FRONT, 0), (0, 0)))[None]
    zero_carry = jnp.zeros((8, CONV_DIM), F32)
    _, kmeta, _, meta_tail = _premix(meta_blk, tab, n // BLOCK, zero_carry, premix_w, BLOCK, 1)
    qp, kc, convn, _ = _premix(x, tab, 0, meta_tail, premix_w, MIX_ROWS, MIX_SUB_BLOCKS)

    olat = _attention(qp, kc, kmeta[0], ATTN_BLOCK, ATTN_BLOCK, ATTN_CHUNKS)

    h1, trow, oh1, oh2, wts = _postmix(x.reshape(n, d), olat.reshape(n, -1), convn.reshape(n, -1), postmix_w,
                                       MIX_ROWS, MIX_SUB_BLOCKS)

    sorted_rows = TOP_K * n + N_EXPERTS * EXPERT_TILE
    n_tiles = sorted_rows // EXPERT_TILE
    pos2, te8, misc = _plan(oh1, oh2, n_tiles, PLAN_CHUNK)
    pos = pos2.reshape(TOP_K, n)
    xs, shared = _dispatch(misc[0, :N_EXPERTS + 1], misc[1, :N_EXPERTS], pos, trow, shared_w, sorted_rows,
                           DISPATCH_TOKENS)
    ys = _experts(te8[0, :n_tiles], misc[0, N_EXPERTS:N_EXPERTS + 1], xs, *expert_w, n_tiles)
    out = _combine(pos, shared, h1, wts, ys, row(post_ffn_norm[0]), COMBINE_TOKENS)
    return out.reshape(bsz, seq, d)
```

```python
import functools
import math

import jax
import jax.numpy as jnp
from jax import lax
from jax.experimental import pallas as pl
from jax.experimental.pallas import tpu as pltpu

D_MODEL = 1024
N_META = 16
BLOCK = 128
PAD_FRONT = BLOCK - N_META
N_HEADS = 8
QK_NOPE = 64
QK_ROPE = 32
V_DIM = 64
Q_LORA = 256
KV_LORA = 128
ROPE_THETA = 10000.0
ATTN_SCALE = (QK_NOPE + QK_ROPE) ** -0.5
Q_SCALE = ATTN_SCALE * math.log2(math.e)
ATTN_WIDTH = N_HEADS * V_DIM
CONV_DIM = 512
CONV_W = 3
N_GROUPS = 4
EXPERTS_PER_GROUP = 8
N_EXPERTS = N_GROUPS * EXPERTS_PER_GROUP
D_FF = 256
EPS = 1e-6
NEG_INF = -1e30

LANES = 128
SUBLANES = 8
TOP_K = 2
EXPERT_TILE = 512
KEY_WIDTH = 2 * LANES

F32 = jnp.float32
BF16 = jnp.bfloat16
VMEM_LIMIT = 56 * 1024 * 1024

ROPE_STEPS = 3
MIX_ROWS = 1024
MIX_SUB_BLOCKS = 2
ATTN_BLOCK = 512
ATTN_CHUNKS = 16
PLAN_CHUNK = 512
DISPATCH_TOKENS = 1024
DISPATCH_PARTS = 1
COMBINE_TOKENS = 512
COMBINE_GROUP = 32


def _rms(x, g):
    return x * lax.rsqrt(jnp.mean(x * x, axis=-1, keepdims=True) + EPS) * g


def _dot(a, b):
    return jnp.dot(a, b, preferred_element_type=F32)


def _dot_nt(a, b):
    return lax.dot_general(a, b, (((1,), (1,)), ((), ())), preferred_element_type=F32)


def _store_token_rows(ref, value, first_token=0):
    rows = value.shape[0]
    for c in range(D_MODEL // LANES):
        ref[pl.ds(first_token * SUBLANES + c, rows, stride=SUBLANES), :] = value[:, c * LANES:(c + 1) * LANES]


def _load_token_rows(ref, rows, first_token=0):
    return jnp.concatenate([ref[pl.ds(first_token * SUBLANES + c, rows, stride=SUBLANES), :]
                            for c in range(D_MODEL // LANES)], axis=1)


def _rope_table_kernel(pos_ref, invf_ref, out_ref):
    pos = pos_ref[...].astype(F32)
    ang = invf_ref[...] * pos
    c = jnp.cos(ang)
    s = jnp.sin(ang)
    planes = jnp.concatenate([c, c, -s, s, jnp.zeros((LANES - 4 * (QK_ROPE // 2), ang.shape[1]), F32)], axis=0)
    out_ref[...] = planes.T


def _rope_table(pos_row, inv_freq_col, chunk):
    n = pos_row.shape[1]
    return pl.pallas_call(
        _rope_table_kernel,
        grid=(n // chunk,),
        in_specs=[pl.BlockSpec((1, chunk), lambda i: (0, i)),
                  pl.BlockSpec((QK_ROPE // 2, 1), lambda i: (0, 0))],
        out_specs=pl.BlockSpec((chunk, LANES), lambda i: (i, 0)),
        out_shape=jax.ShapeDtypeStruct((n, LANES), F32),
        name="rope_table",
    )(pos_row, inv_freq_col)


def _premix_kernel(x_ref, tab_ref, carry_in_ref, g_pre_ref, w1_ref, gq_ref, gkv_ref, wqn_ref, bduk_ref,
                   wqr_ref, convw_ref, gconv_ref, qp_ref, kc_ref, convn_ref, utail_ref, carry_ref, wq_scr, *,
                   sub_blocks):
    t = pl.program_id(1)
    rows = x_ref.shape[1]

    @pl.when(t == 0)
    def _():
        carry_ref[...] = carry_in_ref[...]

    @pl.when((pl.program_id(0) == 0) & (t == 0))
    def _():
        wabs = jnp.dot(wqn_ref[...], bduk_ref[...], preferred_element_type=F32,
                       precision=lax.Precision.HIGHEST) * Q_SCALE
        wrope = wqr_ref[...] * Q_SCALE
        for h in range(N_HEADS):
            sl = slice(h * LANES, (h + 1) * LANES)
            wq_scr[:, h * KEY_WIDTH:h * KEY_WIDTH + LANES] = wabs[:, sl].astype(BF16)
            wq_scr[:, h * KEY_WIDTH + LANES:(h + 1) * KEY_WIDTH] = wrope[:, sl].astype(BF16)

    rows = rows // sub_blocks
    lane = lax.broadcasted_iota(jnp.int32, (rows, LANES), 1)
    row8 = lax.broadcasted_iota(jnp.int32, (8, CONV_DIM), 0)
    cw = convw_ref[...]
    prev = carry_ref[...]
    for sb in range(sub_blocks):
        r = pl.ds(sb * rows, rows)
        xn = _rms(x_ref[0, r, :], g_pre_ref[...]).astype(BF16)
        z = _dot(xn, w1_ref[...])
        tab = tab_ref[r, :]

        def rope(pair):
            prod = pair * tab
            return jnp.where(lane < QK_ROPE, prod + pltpu.roll(prod, LANES - QK_ROPE, axis=1), 0.0)

        ckvn = _rms(z[:, Q_LORA:Q_LORA + KV_LORA], gkv_ref[...])
        krope = rope(z[:, Q_LORA + KV_LORA:Q_LORA + KV_LORA + LANES])
        kc_ref[0, r, :] = jnp.concatenate([ckvn, krope], axis=1).astype(BF16)

        cqn = _rms(z[:, :Q_LORA], gq_ref[...]).astype(BF16)
        qall = _dot(cqn, wq_scr[...])
        for h in range(N_HEADS):
            qp_ref[0, h, r, 0:LANES] = qall[:, h * KEY_WIDTH:h * KEY_WIDTH + LANES].astype(BF16)
            qp_ref[0, h, r, LANES:KEY_WIDTH] = rope(
                qall[:, h * KEY_WIDTH + LANES:(h + 1) * KEY_WIDTH]).astype(BF16)

        c0 = Q_LORA + KV_LORA + LANES
        gate_b = z[:, c0:c0 + CONV_DIM]
        u = z[:, c0 + CONV_DIM:c0 + 2 * CONV_DIM] * z[:, c0 + 2 * CONV_DIM:c0 + 3 * CONV_DIM]
        r1 = pltpu.roll(u, 1, axis=0)
        r2 = pltpu.roll(u, 2, axis=0)
        p1 = pltpu.roll(prev, 1, axis=0)
        p2 = pltpu.roll(prev, 2, axis=0)
        u1 = jnp.concatenate([jnp.where(row8 < 1, p1, r1[0:8]), r1[8:]], axis=0)
        u2 = jnp.concatenate([jnp.where(row8 < 2, p2, r2[0:8]), r2[8:]], axis=0)
        y = cw[0:1] * u2 + cw[1:2] * u1 + cw[2:3] * u
        convn_ref[0, r, :] = _rms(gate_b * y, gconv_ref[...]).astype(BF16)
        prev = u[rows - 8:rows]
    carry_ref[...] = prev
    utail_ref[...] = prev


def _premix(x3, tab, tab_block0, carry_in, wts, rows, sub_blocks):
    bsz, seq, _ = x3.shape
    nt = seq // rows
    g_pre, w1, gq, gkv, wqn, bduk, wqr, convw, gconv = wts
    const = lambda a: pl.BlockSpec(a.shape, lambda b, t: (0,) * a.ndim)
    return pl.pallas_call(
        functools.partial(_premix_kernel, sub_blocks=sub_blocks),
        grid=(bsz, nt),
        in_specs=[pl.BlockSpec((1, rows, D_MODEL), lambda b, t: (b, t, 0)),
                  pl.BlockSpec((rows, LANES), lambda b, t: (tab_block0 + b * nt + t, 0)),
                  const(carry_in), const(g_pre), const(w1), const(gq), const(gkv), const(wqn), const(bduk),
                  const(wqr), const(convw), const(gconv)],
        out_specs=[pl.BlockSpec((1, N_HEADS, rows, KEY_WIDTH), lambda b, t: (b, 0, t, 0)),
                   pl.BlockSpec((1, rows, KEY_WIDTH), lambda b, t: (b, t, 0)),
                   pl.BlockSpec((1, rows, CONV_DIM), lambda b, t: (b, t, 0)),
                   pl.BlockSpec((8, CONV_DIM), lambda b, t: (0, 0))],
        out_shape=[jax.ShapeDtypeStruct((bsz, N_HEADS, seq, KEY_WIDTH), BF16),
                   jax.ShapeDtypeStruct((bsz, seq, KEY_WIDTH), BF16),
                   jax.ShapeDtypeStruct((bsz, seq, CONV_DIM), BF16),
                   jax.ShapeDtypeStruct((8, CONV_DIM), F32)],
        scratch_shapes=[pltpu.VMEM((8, CONV_DIM), F32), pltpu.VMEM((Q_LORA, N_HEADS * KEY_WIDTH), BF16)],
        compiler_params=pltpu.CompilerParams(dimension_semantics=("arbitrary", "arbitrary"),
                                             vmem_limit_bytes=VMEM_LIMIT),
        name="premix",
    )(x3, tab, carry_in, g_pre, w1, gq, gkv, wqn, bduk, wqr, convw, gconv)


def _attention_kernel(q_ref, qnext_ref, k_ref, km_ref, o_ref, s_scr, smeta_scr, tri_scr, m_scr, l_scr, acc_scr, *,
                      bq, bk, chunks):
    assert bq == bk
    b = pl.program_id(0)
    i = pl.program_id(1)
    assert chunks % N_HEADS == 0
    row_splits = chunks // N_HEADS
    c_rows = bq // row_splits
    km = km_ref[...]

    def chunk_rows(c):
        return c // row_splits, pl.ds((c % row_splits) * c_rows, c_rows)

    def scores(qr, c, kb):
        h, r = chunk_rows(c)
        return _dot_nt(qr[0, h, r, :], kb)

    def values(kb):
        return jnp.concatenate([kb[:, :KV_LORA], jnp.ones((kb.shape[0], LANES), BF16)], axis=1)

    def consume(c, s, v_ones, mask, first=False):
        if mask is not None:
            s = jnp.where(mask, s, NEG_INF)
        m_blk = jnp.broadcast_to(jnp.max(s, axis=1, keepdims=True), (c_rows, LANES))
        m_new = m_blk if first else jnp.maximum(m_scr[c], m_blk)
        p = jnp.exp2(s - jnp.concatenate([m_new] * (s.shape[1] // LANES), axis=1)).astype(BF16)
        pv = _dot(p, v_ones)
        if first:
            l_scr[c] = pv[:, KV_LORA:]
            acc_scr[c] = pv[:, :KV_LORA]
        else:
            alpha = jnp.exp2(m_scr[c] - m_new)
            l_scr[c] = alpha * l_scr[c] + pv[:, KV_LORA:]
            acc_scr[c] = alpha * acc_scr[c] + pv[:, :KV_LORA]
        m_scr[c] = m_new

    def key_block(j):
        return k_ref[0, pl.ds(pl.multiple_of(j * bk, bk), bk), :]

    @pl.when((b == 0) & (i == 0))
    def _():
        for c in range(chunks):
            smeta_scr[c] = scores(q_ref, c, km)
        rows_i = lax.broadcasted_iota(jnp.int32, (bq, bk), 0)
        cols_i = lax.broadcasted_iota(jnp.int32, (bq, bk), 1)
        tri_scr[...] = jnp.where(cols_i <= rows_i, 0.0, NEG_INF)

    n_full = (i * bq) // bk
    kb0 = key_block(0)
    v_meta = values(km)
    col = lax.broadcasted_iota(jnp.int32, (c_rows, BLOCK), 1)
    for c in range(chunks):
        s_scr[c] = scores(q_ref, c, kb0)
        consume(c, smeta_scr[c], v_meta, col >= PAD_FRONT, first=True)

    def full_block(j):
        v_ones = values(key_block(j))
        kb_next = key_block(j + 1)
        for c in range(chunks):
            s = s_scr[c]
            s_scr[c] = scores(q_ref, c, kb_next)
            consume(c, s, v_ones, None)

    def body(j, carry):
        full_block(j)
        return carry

    lax.fori_loop(0, n_full, body, 0)

    v_ones = values(key_block(n_full))
    for c in range(chunks):
        _, r = chunk_rows(c)
        n_keys = (c % row_splits + 1) * c_rows
        s = s_scr[c, :, :n_keys] + tri_scr[r, :n_keys]
        smeta_scr[c] = scores(qnext_ref, c, km)
        consume(c, s, v_ones[:n_keys], None)

    for c in range(chunks):
        h, r = chunk_rows(c)
        o_ref[0, r, h * KV_LORA:(h + 1) * KV_LORA] = (acc_scr[c] / l_scr[c]).astype(BF16)


def _attention(qp, kc, kmeta, bq, bk, chunks):
    bsz, _, seq, _ = qp.shape
    nq = seq // bq
    c_rows = N_HEADS * bq // chunks
    scratch = [pltpu.VMEM((chunks, c_rows, bk), F32), pltpu.VMEM((chunks, c_rows, BLOCK), F32),
               pltpu.VMEM((bq, bk), F32), pltpu.VMEM((chunks, c_rows, LANES), F32),
               pltpu.VMEM((chunks, c_rows, LANES), F32), pltpu.VMEM((chunks, c_rows, KV_LORA), F32)]

    def next_q(b, i):
        wrap = (i + 1) // nq
        return (jnp.minimum(b + wrap, bsz - 1), 0, i + 1 - wrap * nq, 0)

    return pl.pallas_call(
        functools.partial(_attention_kernel, bq=bq, bk=bk, chunks=chunks),
        grid=(bsz, nq),
        in_specs=[pl.BlockSpec((1, N_HEADS, bq, KEY_WIDTH), lambda b, i: (b, 0, i, 0)),
                  pl.BlockSpec((1, N_HEADS, bq, KEY_WIDTH), next_q),
                  pl.BlockSpec((1, seq, KEY_WIDTH), lambda b, i: (b, 0, 0)),
                  pl.BlockSpec((BLOCK, KEY_WIDTH), lambda b, i: (0, 0))],
        out_specs=pl.BlockSpec((1, bq, N_HEADS * KV_LORA), lambda b, i: (b, i, 0)),
        out_shape=jax.ShapeDtypeStruct((bsz, seq, N_HEADS * KV_LORA), BF16),
        scratch_shapes=scratch,
        compiler_params=pltpu.CompilerParams(dimension_semantics=("arbitrary", "arbitrary"),
                                             vmem_limit_bytes=VMEM_LIMIT),
        name="attention",
    )(qp, qp, kc, kmeta)


def _postmix_kernel(x_ref, olat_ref, convn_ref, wuv_ref, gattn_ref, wout_ref, gpost_ref, gffn_ref, wr_ref, br_ref,
                    h1_ref, trow_ref, oh1_ref, oh2_ref, wts_ref, *, sub_blocks):
    rows = x_ref.shape[0] // sub_blocks
    wout = wout_ref[...]
    lane = lax.broadcasted_iota(jnp.int32, (rows, LANES), 1)

    def first_argmax(v):
        mx = jnp.max(v, axis=1, keepdims=True)
        return mx, jnp.min(jnp.where(v == mx, lane, LANES), axis=1, keepdims=True)

    for sb in range(sub_blocks):
        r = pl.ds(sb * rows, rows)
        attn = _dot(olat_ref[r, :], wuv_ref[...])
        attn_n = _rms(attn, gattn_ref[...]).astype(BF16)
        mix = _dot(attn_n, wout[:ATTN_WIDTH]) + _dot(convn_ref[r, :], wout[ATTN_WIDTH:])
        h1 = x_ref[r, :] + _rms(mix, gpost_ref[...])
        h1_ref[r, :] = h1
        tf = _rms(h1, gffn_ref[...])
        t = tf.astype(BF16)
        _store_token_rows(trow_ref, tf, sb * rows)

        logits = _dot(t, wr_ref[...]) + br_ref[...]
        gl = jnp.where(lane < N_GROUPS, logits[:, :LANES], NEG_INF)
        gmax, gidx = first_argmax(gl)
        g_w = 1.0 / jnp.sum(jnp.exp(gl - gmax), axis=1, keepdims=True)
        el = jnp.where((lane >> 3) == gidx, logits[:, LANES:], NEG_INF)
        m1, i1 = first_argmax(el)
        m2, i2 = first_argmax(jnp.where(lane == i1, NEG_INF, el))
        e2 = jnp.exp(m2 - m1)
        den = 1.0 + e2
        oh1_ref[r, :] = jnp.where(lane == i1, 1.0, 0.0).astype(BF16)
        oh2_ref[r, :] = jnp.where(lane == i2, 1.0, 0.0).astype(BF16)
        wts_ref[r, :] = jnp.where(lane == 0, g_w / den, jnp.where(lane == 1, g_w * e2 / den, 0.0))


def _postmix(x2, olat2, convn2, wts, rows, sub_blocks):
    n = x2.shape[0]
    wuv, gattn, wout, gpost, gffn, wr, br = wts
    const = lambda a: pl.BlockSpec(a.shape, lambda i: (0,) * a.ndim)
    tile = lambda w: pl.BlockSpec((rows, w), lambda i: (i, 0))
    return pl.pallas_call(
        functools.partial(_postmix_kernel, sub_blocks=sub_blocks),
        grid=(n // rows,),
        in_specs=[tile(D_MODEL), tile(N_HEADS * KV_LORA), tile(CONV_DIM),
                  const(wuv), const(gattn), const(wout), const(gpost), const(gffn), const(wr), const(br)],
        out_specs=[tile(D_MODEL), pl.BlockSpec((rows * SUBLANES, LANES), lambda i: (i, 0)),
                   tile(LANES), tile(LANES), tile(LANES)],
        out_shape=[jax.ShapeDtypeStruct((n, D_MODEL), F32),
                   jax.ShapeDtypeStruct((n * SUBLANES, LANES), F32),
                   jax.ShapeDtypeStruct((n, LANES), BF16),
                   jax.ShapeDtypeStruct((n, LANES), BF16),
                   jax.ShapeDtypeStruct((n, LANES), F32)],
        compiler_params=pltpu.CompilerParams(dimension_semantics=("arbitrary",), vmem_limit_bytes=VMEM_LIMIT),
        name="postmix",
    )(x2, olat2, convn2, wuv, gattn, wout, gpost, gffn, wr, br)


def _silu(a):
    return a * (1.0 / (1.0 + jnp.exp(-a)))


def _plan_kernel(oh1_ref, oh2_ref, pos_ref, te_ref, misc_ref, *, chunk):
    n_tok = oh1_ref.shape[0]
    n_chunks = n_tok // chunk
    lane = lax.broadcasted_iota(jnp.int32, (SUBLANES, LANES), 1)

    def rows_of(ref, c):
        return ref[pl.ds(pl.multiple_of(c * chunk, chunk), chunk), :]

    def count_body(c, acc):
        both = rows_of(oh1_ref, c).astype(F32) + rows_of(oh2_ref, c).astype(F32)
        return acc + jnp.sum(both, axis=0, keepdims=True)

    counts = lax.fori_loop(0, n_chunks, count_body, jnp.zeros((1, LANES), F32))
    counts = jnp.broadcast_to(counts, (SUBLANES, LANES)).astype(jnp.int32)
    tile_shift = EXPERT_TILE.bit_length() - 1
    padded = ((counts + (EXPERT_TILE - 1)) >> tile_shift) << tile_shift
    ends = padded
    shift = 1
    while shift < N_EXPERTS:
        ends = ends + jnp.where(lane >= shift, pltpu.roll(ends, shift, axis=1), 0)
        shift *= 2
    base = (ends - padded)[0:1].astype(F32)

    r_iota = lax.broadcasted_iota(jnp.int32, (chunk, chunk), 0)
    c_iota = lax.broadcasted_iota(jnp.int32, (chunk, chunk), 1)
    earlier = jnp.where(c_iota < r_iota, 1.0, 0.0).astype(BF16)

    def rank_body(c, run):
        for slot, ref in enumerate((oh1_ref, oh2_ref)):
            oh = rows_of(ref, c)
            ohf = oh.astype(F32)
            before = _dot(earlier, oh) + run
            posv = jnp.sum(ohf * before, axis=1, keepdims=True)
            pos_row = jnp.broadcast_to(posv, (chunk, LANES)).T[0:1, :]
            pos_ref[pl.ds(slot * n_chunks + c, 1), :] = pos_row.astype(jnp.int32)
            run = run + jnp.sum(ohf, axis=0, keepdims=True)
        return run

    lax.fori_loop(0, n_chunks, rank_body, base)

    n_tile_lanes = te_ref.shape[1]
    tile_start = lax.broadcasted_iota(jnp.int32, (SUBLANES, n_tile_lanes), 1) * EXPERT_TILE
    zeros = jnp.zeros((SUBLANES, n_tile_lanes), jnp.int32)
    total = ends[:, N_EXPERTS - 1:N_EXPERTS]
    te, seg_index, prev_end, seg_end = zeros, zeros, zeros, zeros + total
    for e in range(N_EXPERTS):
        end_e = ends[:, e:e + 1]
        done = end_e <= tile_start
        te = te + jnp.where(done, 1, 0)
        seg_index = seg_index + jnp.where(done & (padded[:, e:e + 1] > 0), 1, 0)
        prev_end = jnp.maximum(prev_end, jnp.where(done, end_e, 0))
        seg_end = jnp.minimum(seg_end, jnp.where(done, total, end_e))
    nxt = zeros
    for e in range(N_EXPERTS):
        nxt = nxt + jnp.where(ends[:, e:e + 1] <= seg_end, 1, 0)
    sub_t = lax.broadcasted_iota(jnp.int32, (SUBLANES, n_tile_lanes), 0)
    rows = [jnp.minimum(te, N_EXPERTS - 1), jnp.where(prev_end == tile_start, 1, 0),
            jnp.minimum(nxt, N_EXPERTS - 1), jnp.where(seg_end < total, 1, 0), seg_index & 1]
    info = zeros
    for r, v in enumerate(rows):
        info = jnp.where(sub_t == r, v, info)
    te_ref[...] = info
    sub = lax.broadcasted_iota(jnp.int32, (SUBLANES, LANES), 0)
    used_tiles = ends[:, N_EXPERTS - 1:N_EXPERTS] >> tile_shift
    first_pad = jnp.where(lane < N_EXPERTS, ends - padded + counts, used_tiles)
    misc_ref[...] = jnp.where(sub == 0, first_pad, ends)


def _plan(oh1, oh2, n_tiles, chunk):
    n = oh1.shape[0]
    n_tile_lanes = -(-n_tiles // LANES) * LANES
    whole = lambda shape: pl.BlockSpec(shape, lambda i: (0, 0))
    return pl.pallas_call(
        functools.partial(_plan_kernel, chunk=chunk),
        grid=(1,),
        in_specs=[whole((n, LANES)), whole((n, LANES))],
        out_specs=[whole((TOP_K * n // chunk, chunk)), whole((SUBLANES, n_tile_lanes)), whole((SUBLANES, LANES))],
        out_shape=[jax.ShapeDtypeStruct((TOP_K * n // chunk, chunk), jnp.int32),
                   jax.ShapeDtypeStruct((SUBLANES, n_tile_lanes), jnp.int32),
                   jax.ShapeDtypeStruct((SUBLANES, LANES), jnp.int32)],
        compiler_params=pltpu.CompilerParams(dimension_semantics=("arbitrary",), vmem_limit_bytes=VMEM_LIMIT),
        name="moe_plan",
    )(oh1, oh2)


def _row_slice(ref, row, n_rows=1):
    start = row * SUBLANES if isinstance(row, int) else pl.multiple_of(row * SUBLANES, SUBLANES)
    return ref.at[pl.ds(start, n_rows * SUBLANES)]


def _dispatch_kernel(pad_ref, end_ref, pos_ref, trow_ref, wsg_ref, wsu_ref, wsd_ref, xs_hbm, shared_ref, zero_buf,
                     zero_sem, row_sem, *, tokens, parts):
    i = pl.program_id(0)

    def zero_fill(phase):
        def run(first_row, n_rows):
            copy = pltpu.make_async_copy(_row_slice(zero_buf, 0, n_rows), _row_slice(xs_hbm, first_row, n_rows),
                                         zero_sem)
            copy.start() if phase == "start" else copy.wait()

        for e in range(N_EXPERTS):
            row = pad_ref[e]
            n_pad = end_ref[e] - row
            bit = EXPERT_TILE // 2
            while bit >= 1:
                @pl.when((n_pad & bit) != 0)
                def _():
                    run(row, bit)
                row = row + (n_pad & bit)
                bit //= 2

        def tail(t, carry):
            run(t * EXPERT_TILE, EXPERT_TILE)
            return carry

        lax.fori_loop(pad_ref[N_EXPERTS], xs_hbm.shape[0] // (EXPERT_TILE * SUBLANES), tail, 0)

    @pl.when(i == 0)
    def _():
        zero_buf[...] = jnp.zeros(zero_buf.shape, F32)
        zero_fill("start")

    unroll = 8

    def issue(g, carry):
        for u in range(unroll):
            n = g * unroll + u
            src = _row_slice(trow_ref, n)
            for k in range(TOP_K):
                pltpu.make_async_copy(src, _row_slice(xs_hbm, pos_ref[k, n]), row_sem).start(priority=(u + k) % 2)
        return carry

    part = tokens // parts
    for p in range(parts):
        lax.fori_loop(p * part // unroll, (p + 1) * part // unroll, issue, 0)
        t = _load_token_rows(trow_ref, part, p * part).astype(BF16)
        hid = _silu(_dot(t, wsg_ref[...])) * _dot(t, wsu_ref[...])
        shared_ref[p * part:(p + 1) * part, :] = _dot(hid.astype(BF16), wsd_ref[...])

    for k in range(TOP_K):
        pltpu.make_async_copy(trow_ref, _row_slice(xs_hbm, 0, tokens), row_sem).wait()

    @pl.when(i + 1 == pl.num_programs(0))
    def _():
        zero_fill("wait")


def _dispatch(first_pad, seg_end, pos, trow, shared_w, sorted_rows, tokens):
    n = pos.shape[1]
    const = lambda a: pl.BlockSpec(a.shape, lambda i: (0,) * a.ndim)
    return pl.pallas_call(
        functools.partial(_dispatch_kernel, tokens=tokens, parts=DISPATCH_PARTS),
        grid=(n // tokens,),
        in_specs=[pl.BlockSpec(memory_space=pltpu.SMEM), pl.BlockSpec(memory_space=pltpu.SMEM),
                  pl.BlockSpec((TOP_K, tokens), lambda i: (0, i), memory_space=pltpu.SMEM),
                  pl.BlockSpec((tokens * SUBLANES, LANES), lambda i: (i, 0))] + [const(w) for w in shared_w],
        out_specs=[pl.BlockSpec(memory_space=pl.ANY), pl.BlockSpec((tokens, D_MODEL), lambda i: (i, 0))],
        out_shape=[jax.ShapeDtypeStruct((sorted_rows * SUBLANES, LANES), F32),
                   jax.ShapeDtypeStruct((n, D_MODEL), F32)],
        scratch_shapes=[pltpu.VMEM((EXPERT_TILE * SUBLANES, LANES), F32), pltpu.SemaphoreType.DMA,
                        pltpu.SemaphoreType.DMA],
        compiler_params=pltpu.CompilerParams(dimension_semantics=("arbitrary",), vmem_limit_bytes=VMEM_LIMIT),
        name="moe_dispatch",
    )(first_pad, seg_end, pos, trow, *shared_w)


def _experts_kernel(te_ref, first_ref, next_ref, has_next_ref, slot_ref, used_ref, xs_ref, wg_hbm, wu_hbm, wd_hbm,
                    ys_ref, wg_buf, wu_buf, wd_buf, sems):
    t = pl.program_id(0)
    active = t < used_ref[0]

    def weights(e, slot, action):
        for j, (src, buf) in enumerate(((wg_hbm, wg_buf), (wu_hbm, wu_buf), (wd_hbm, wd_buf))):
            copy = pltpu.make_async_copy(src.at[e], buf.at[slot], sems.at[slot, j])
            copy.start() if action == "start" else copy.wait()

    @pl.when(t == 0)
    def _():
        weights(te_ref[0], slot_ref[0], "start")

    @pl.when(active & (first_ref[t] == 1))
    def _():
        @pl.when(has_next_ref[t] == 1)
        def _():
            weights(next_ref[t], 1 - slot_ref[t], "start")
        weights(te_ref[t], slot_ref[t], "wait")

    @pl.when(active)
    def _():
        slot = slot_ref[t]
        x = _load_token_rows(xs_ref, EXPERT_TILE).astype(BF16)
        hid = _silu(_dot(x, wg_buf[slot].astype(BF16))) * _dot(x, wu_buf[slot].astype(BF16))
        _store_token_rows(ys_ref, _dot(hid.astype(BF16), wd_buf[slot].astype(BF16)))

    @pl.when(jnp.logical_not(active))
    def _():
        ys_ref[...] = jnp.zeros(ys_ref.shape, F32)


def _experts(tile_info, used, xs, wg, wu, wd, n_tiles):
    blk = EXPERT_TILE * SUBLANES
    n_prefetch = len(tile_info) + 1
    tile_map = lambda t, te, first, nxt, has_next, slot, used: (t, 0)
    in_map = lambda t, te, first, nxt, has_next, slot, used: (jnp.minimum(t, used[0] - 1), 0)
    return pl.pallas_call(
        _experts_kernel,
        grid_spec=pltpu.PrefetchScalarGridSpec(
            num_scalar_prefetch=n_prefetch,
            grid=(n_tiles,),
            in_specs=[pl.BlockSpec((blk, LANES), in_map), pl.BlockSpec(memory_space=pl.ANY),
                      pl.BlockSpec(memory_space=pl.ANY), pl.BlockSpec(memory_space=pl.ANY)],
            out_specs=pl.BlockSpec((blk, LANES), tile_map),
            scratch_shapes=[pltpu.VMEM((2, D_MODEL, D_FF), F32), pltpu.VMEM((2, D_MODEL, D_FF), F32),
                            pltpu.VMEM((2, D_FF, D_MODEL), F32), pltpu.SemaphoreType.DMA((2, 3))]),
        out_shape=jax.ShapeDtypeStruct(xs.shape, F32),
        compiler_params=pltpu.CompilerParams(dimension_semantics=("arbitrary",), vmem_limit_bytes=VMEM_LIMIT),
        name="moe_experts",
    )(*tile_info, used, xs, wg, wu, wd)


def _combine_kernel(pos_ref, pos_next_ref, shared_ref, h1_ref, wts_ref, ys_hbm, gout_ref, out_ref, ybuf, sems, *,
                    tokens):
    i = pl.program_id(0)
    slot = lax.rem(i, 2)
    other = 1 - slot
    group = COMBINE_GROUP

    def start_gathers(p_ref, dst_slot, first):
        for u in range(group):
            n = first + u
            for k in range(TOP_K):
                pltpu.make_async_copy(_row_slice(ys_hbm, p_ref[k, n]), _row_slice(ybuf.at[dst_slot].at[k], n),
                                      sems.at[dst_slot]).start(priority=(u + k) % 2)

    def wait_slot(s):
        for k in range(TOP_K):
            pltpu.make_async_copy(_row_slice(ys_hbm, 0, tokens), ybuf.at[s].at[k], sems.at[s]).wait()

    @pl.when(i == 0)
    def _():
        def first_step(g, carry):
            start_gathers(pos_ref, slot, g * group)
            return carry
        lax.fori_loop(0, tokens // group, first_step, 0)

    wait_slot(slot)

    def trip(g, carry):
        first = pl.multiple_of(g * group, group)
        rows = pl.ds(first, group)
        wts = wts_ref[rows, :]
        routed = (wts[:, 0:1] * _load_token_rows(ybuf.at[slot].at[0], group, first)
                  + wts[:, 1:2] * _load_token_rows(ybuf.at[slot].at[1], group, first))
        out_ref[rows, :] = h1_ref[rows, :] + _rms(routed + shared_ref[rows, :], gout_ref[...])
        start_gathers(pos_next_ref, other, first)
        return carry

    lax.fori_loop(0, tokens // group, trip, 0)

    @pl.when(i + 1 == pl.num_programs(0))
    def _():
        wait_slot(other)


def _combine(pos, shared, h1, wts, ys, gout, tokens):
    n = h1.shape[0]
    const = lambda a: pl.BlockSpec(a.shape, lambda i: (0,) * a.ndim)
    tile = lambda w: pl.BlockSpec((tokens, w), lambda i: (i, 0))
    steps = n // tokens
    return pl.pallas_call(
        functools.partial(_combine_kernel, tokens=tokens),
        grid=(steps,),
        in_specs=[pl.BlockSpec((TOP_K, tokens), lambda i: (0, i), memory_space=pltpu.SMEM),
                  pl.BlockSpec((TOP_K, tokens), lambda i: (0, jnp.minimum(i + 1, steps - 1)),
                               memory_space=pltpu.SMEM),
                  tile(D_MODEL), tile(D_MODEL), tile(LANES), pl.BlockSpec(memory_space=pl.ANY), const(gout)],
        out_specs=tile(D_MODEL),
        out_shape=jax.ShapeDtypeStruct((n, D_MODEL), F32),
        scratch_shapes=[pltpu.VMEM((2, TOP_K, tokens * SUBLANES, LANES), F32), pltpu.SemaphoreType.DMA((2,))],
        compiler_params=pltpu.CompilerParams(dimension_semantics=("arbitrary",), vmem_limit_bytes=VMEM_LIMIT),
        name="moe_combine",
    )(pos, pos, shared, h1, wts, ys, gout)


def _swap_halves(w):
    half = w.shape[-1] // 2
    return jnp.concatenate([w[..., half:], w[..., :half]], axis=-1)


def _block_diag(blocks):
    h, r, c = blocks.shape
    eye = jnp.eye(h, dtype=blocks.dtype)
    return jnp.einsum('hrc,hg->hrgc', blocks, eye).reshape(h * r, h * c)


def kernel(x, positions, meta_tokens, pre_mix_norm, w_in, q_norm, w_uq, kv_norm, w_ukv, conv_w, attn_out_norm,
           conv_out_norm, w_out, post_mix_norm, pre_ffn_norm, w_group_router, b_group_router, w_expert_router,
           b_expert_router, w_gate, w_up, w_down, w_sh_gate, w_sh_up, w_sh_down, post_ffn_norm):
    bsz, seq, d = x.shape
    n = bsz * seq
    row = lambda v: v.reshape(1, -1).astype(F32)
    assert d == D_MODEL and w_in.shape[0] == 1, "one layer of width 1024 is supported"
    assert seq % MIX_ROWS == 0 and seq % ATTN_BLOCK == 0 and n % DISPATCH_TOKENS == 0
    assert (n + BLOCK) % (ROPE_STEPS * LANES) == 0

    wi = w_in[0]
    o_kv, o_pe, o_b = Q_LORA, Q_LORA + KV_LORA, Q_LORA + KV_LORA + QK_ROPE
    k_pe = wi[:, o_pe:o_b]
    w1 = jnp.concatenate([wi[:, :o_pe], k_pe, _swap_halves(k_pe), jnp.zeros((d, LANES - 2 * QK_ROPE), F32),
                          wi[:, o_b:]], axis=1).astype(BF16)
    wq = w_uq[0].reshape(Q_LORA, N_HEADS, QK_NOPE + QK_ROPE)
    wqn = wq[:, :, :QK_NOPE].reshape(Q_LORA, N_HEADS * QK_NOPE).astype(F32)
    wq_rope = wq[:, :, QK_NOPE:]
    wqr = jnp.concatenate([wq_rope, _swap_halves(wq_rope),
                           jnp.zeros((Q_LORA, N_HEADS, LANES - 2 * QK_ROPE), F32)], axis=2)
    wqr = wqr.reshape(Q_LORA, N_HEADS * LANES).astype(F32)
    wkv = w_ukv[0].reshape(KV_LORA, N_HEADS, QK_NOPE + V_DIM)
    bduk = _block_diag(jnp.transpose(wkv[:, :, :QK_NOPE], (1, 2, 0))).astype(F32)
    wuv = _block_diag(jnp.transpose(wkv[:, :, QK_NOPE:], (1, 0, 2))).astype(BF16)
    premix_w = (row(pre_mix_norm[0]), w1, row(q_norm[0]), row(kv_norm[0]), wqn, bduk, wqr,
                conv_w[0].astype(F32), row(conv_out_norm[0]))
    wr = jnp.concatenate([w_group_router[0], jnp.zeros((d, LANES - N_GROUPS), F32),
                          w_expert_router[0], jnp.zeros((d, LANES - N_EXPERTS), F32)], axis=1).astype(BF16)
    br = jnp.concatenate([b_group_router[0], jnp.zeros((LANES - N_GROUPS,), F32),
                          b_expert_router[0], jnp.zeros((LANES - N_EXPERTS,), F32)]).reshape(1, -1).astype(F32)
    postmix_w = (wuv, row(attn_out_norm[0]), w_out[0].astype(BF16), row(post_mix_norm[0]), row(pre_ffn_norm[0]),
                 wr, br)
    expert_w = (w_gate[0], w_up[0], w_down[0])
    shared_w = (w_sh_gate[0].astype(BF16), w_sh_up[0].astype(BF16), w_sh_down[0].astype(BF16))

    pos_all = jnp.concatenate([positions.astype(jnp.int32).reshape(-1) + N_META,
                               jnp.zeros((PAD_FRONT,), jnp.int32), jnp.arange(N_META, dtype=jnp.int32)])
    inv_freq = 1.0 / (ROPE_THETA ** (jnp.arange(0, QK_ROPE, 2, dtype=F32) / QK_ROPE))
    n_tab = n + BLOCK
    tab = _rope_table(pos_all.reshape(1, n_tab), inv_freq.reshape(-1, 1), n_tab // ROPE_STEPS)

    meta_blk = jnp.pad(meta_tokens.astype(F32), ((PAD_FRONT, 0), (0, 0)))[None]
    zero_carry = jnp.zeros((8, CONV_DIM), F32)
    _, kmeta, _, meta_tail = _premix(meta_blk, tab, n // BLOCK, zero_carry, premix_w, BLOCK, 1)
    qp, kc, convn, _ = _premix(x, tab, 0, meta_tail, premix_w, MIX_ROWS, MIX_SUB_BLOCKS)

    olat = _attention(qp, kc, kmeta[0], ATTN_BLOCK, ATTN_BLOCK, ATTN_CHUNKS)

    h1, trow, oh1, oh2, wts = _postmix(x.reshape(n, d), olat.reshape(n, -1), convn.reshape(n, -1), postmix_w,
                                       MIX_ROWS, MIX_SUB_BLOCKS)

    sorted_rows = TOP_K * n + N_EXPERTS * EXPERT_TILE
    n_tiles = sorted_rows // EXPERT_TILE
    pos2, te8, misc = _plan(oh1, oh2, n_tiles, PLAN_CHUNK)
    pos = pos2.reshape(TOP_K, n)
    xs, shared = _dispatch(misc[0, :N_EXPERTS + 1], misc[1, :N_EXPERTS], pos, trow, shared_w, sorted_rows,
                           DISPATCH_TOKENS)
    tile_info = [te8[r, :n_tiles] for r in range(5)]
    ys = _experts(tile_info, misc[0, N_EXPERTS:N_EXPERTS + 1], xs, *expert_w, n_tiles)
    out = _combine(pos, shared, h1, wts, ys, row(post_ffn_norm[0]), COMBINE_TOKENS)
    return out.reshape(bsz, seq, d)
```

```python
import functools
import math

import jax
import jax.numpy as jnp
from jax import lax
from jax.experimental import pallas as pl
from jax.experimental.pallas import tpu as pltpu

D_MODEL = 1024
N_META = 16
BLOCK = 128
PAD_FRONT = BLOCK - N_META
N_HEADS = 8
QK_NOPE = 64
QK_ROPE = 32
V_DIM = 64
Q_LORA = 256
KV_LORA = 128
ROPE_THETA = 10000.0
ATTN_SCALE = (QK_NOPE + QK_ROPE) ** -0.5
Q_SCALE = ATTN_SCALE * math.log2(math.e)
ATTN_WIDTH = N_HEADS * V_DIM
CONV_DIM = 512
CONV_W = 3
N_GROUPS = 4
EXPERTS_PER_GROUP = 8
N_EXPERTS = N_GROUPS * EXPERTS_PER_GROUP
D_FF = 256
EPS = 1e-6
NEG_INF = -1e30

LANES = 128
SUBLANES = 8
TOP_K = 2
EXPERT_TILE = 512
KEY_WIDTH = 2 * LANES

F32 = jnp.float32
BF16 = jnp.bfloat16
VMEM_LIMIT = 56 * 1024 * 1024

ROPE_STEPS = 3
MIX_ROWS = 1024
MIX_SUB_BLOCKS = 2
ATTN_BLOCK = 512
ATTN_CHUNKS = 16
PLAN_CHUNK = 512
DISPATCH_TOKENS = 2048
DISPATCH_PARTS = 1
COMBINE_TOKENS = 1024
COMBINE_GROUP = 32


def _rms(x, g):
    return x * lax.rsqrt(jnp.mean(x * x, axis=-1, keepdims=True) + EPS) * g


def _dot(a, b):
    return jnp.dot(a, b, preferred_element_type=F32)


def _dot_nt(a, b):
    return lax.dot_general(a, b, (((1,), (1,)), ((), ())), preferred_element_type=F32)


def _store_token_rows(ref, value, first_token=0):
    rows = value.shape[0]
    for c in range(D_MODEL // LANES):
        ref[pl.ds(first_token * SUBLANES + c, rows, stride=SUBLANES), :] = value[:, c * LANES:(c + 1) * LANES]


def _load_token_rows(ref, rows, first_token=0):
    return jnp.concatenate([ref[pl.ds(first_token * SUBLANES + c, rows, stride=SUBLANES), :]
                            for c in range(D_MODEL // LANES)], axis=1)


def _rope_table_kernel(pos_ref, invf_ref, out_ref):
    pos = pos_ref[...].astype(F32)
    ang = invf_ref[...] * pos
    c = jnp.cos(ang)
    s = jnp.sin(ang)
    planes = jnp.concatenate([c, c, -s, s, jnp.zeros((LANES - 4 * (QK_ROPE // 2), ang.shape[1]), F32)], axis=0)
    out_ref[...] = planes.T


def _rope_table(pos_row, inv_freq_col, chunk):
    n = pos_row.shape[1]
    return pl.pallas_call(
        _rope_table_kernel,
        grid=(n // chunk,),
        in_specs=[pl.BlockSpec((1, chunk), lambda i: (0, i)),
                  pl.BlockSpec((QK_ROPE // 2, 1), lambda i: (0, 0))],
        out_specs=pl.BlockSpec((chunk, LANES), lambda i: (i, 0)),
        out_shape=jax.ShapeDtypeStruct((n, LANES), F32),
        name="rope_table",
    )(pos_row, inv_freq_col)


def _premix_kernel(x_ref, tab_ref, carry_in_ref, g_pre_ref, w1_ref, gq_ref, gkv_ref, wqn_ref, bduk_ref,
                   wqr_ref, convw_ref, gconv_ref, qp_ref, kc_ref, convn_ref, utail_ref, carry_ref, wq_scr, *,
                   sub_blocks):
    t = pl.program_id(1)
    rows = x_ref.shape[1]

    @pl.when(t == 0)
    def _():
        carry_ref[...] = carry_in_ref[...]

    @pl.when((pl.program_id(0) == 0) & (t == 0))
    def _():
        wabs = jnp.dot(wqn_ref[...], bduk_ref[...], preferred_element_type=F32,
                       precision=lax.Precision.HIGHEST) * Q_SCALE
        wrope = wqr_ref[...] * Q_SCALE
        for h in range(N_HEADS):
            sl = slice(h * LANES, (h + 1) * LANES)
            wq_scr[:, h * KEY_WIDTH:h * KEY_WIDTH + LANES] = wabs[:, sl].astype(BF16)
            wq_scr[:, h * KEY_WIDTH + LANES:(h + 1) * KEY_WIDTH] = wrope[:, sl].astype(BF16)

    rows = rows // sub_blocks
    lane = lax.broadcasted_iota(jnp.int32, (rows, LANES), 1)
    row8 = lax.broadcasted_iota(jnp.int32, (8, CONV_DIM), 0)
    cw = convw_ref[...]
    prev = carry_ref[...]
    for sb in range(sub_blocks):
        r = pl.ds(sb * rows, rows)
        xn = _rms(x_ref[0, r, :], g_pre_ref[...]).astype(BF16)
        z = _dot(xn, w1_ref[...])
        tab = tab_ref[r, :]

        def rope(pair):
            prod = pair * tab
            return jnp.where(lane < QK_ROPE, prod + pltpu.roll(prod, LANES - QK_ROPE, axis=1), 0.0)

        ckvn = _rms(z[:, Q_LORA:Q_LORA + KV_LORA], gkv_ref[...])
        krope = rope(z[:, Q_LORA + KV_LORA:Q_LORA + KV_LORA + LANES])
        kc_ref[0, r, :] = jnp.concatenate([ckvn, krope], axis=1).astype(BF16)

        cqn = _rms(z[:, :Q_LORA], gq_ref[...]).astype(BF16)
        qall = _dot(cqn, wq_scr[...])
        for h in range(N_HEADS):
            qp_ref[0, h, r, 0:LANES] = qall[:, h * KEY_WIDTH:h * KEY_WIDTH + LANES].astype(BF16)
            qp_ref[0, h, r, LANES:KEY_WIDTH] = rope(
                qall[:, h * KEY_WIDTH + LANES:(h + 1) * KEY_WIDTH]).astype(BF16)

        c0 = Q_LORA + KV_LORA + LANES
        gate_b = z[:, c0:c0 + CONV_DIM]
        u = z[:, c0 + CONV_DIM:c0 + 2 * CONV_DIM] * z[:, c0 + 2 * CONV_DIM:c0 + 3 * CONV_DIM]
        r1 = pltpu.roll(u, 1, axis=0)
        r2 = pltpu.roll(u, 2, axis=0)
        p1 = pltpu.roll(prev, 1, axis=0)
        p2 = pltpu.roll(prev, 2, axis=0)
        u1 = jnp.concatenate([jnp.where(row8 < 1, p1, r1[0:8]), r1[8:]], axis=0)
        u2 = jnp.concatenate([jnp.where(row8 < 2, p2, r2[0:8]), r2[8:]], axis=0)
        y = cw[0:1] * u2 + cw[1:2] * u1 + cw[2:3] * u
        convn_ref[0, r, :] = _rms(gate_b * y, gconv_ref[...]).astype(BF16)
        prev = u[rows - 8:rows]
    carry_ref[...] = prev
    utail_ref[...] = prev


def _premix(x3, tab, tab_block0, carry_in, wts, rows, sub_blocks):
    bsz, seq, _ = x3.shape
    nt = seq // rows
    g_pre, w1, gq, gkv, wqn, bduk, wqr, convw, gconv = wts
    const = lambda a: pl.BlockSpec(a.shape, lambda b, t: (0,) * a.ndim)
    return pl.pallas_call(
        functools.partial(_premix_kernel, sub_blocks=sub_blocks),
        grid=(bsz, nt),
        in_specs=[pl.BlockSpec((1, rows, D_MODEL), lambda b, t: (b, t, 0)),
                  pl.BlockSpec((rows, LANES), lambda b, t: (tab_block0 + b * nt + t, 0)),
                  const(carry_in), const(g_pre), const(w1), const(gq), const(gkv), const(wqn), const(bduk),
                  const(wqr), const(convw), const(gconv)],
        out_specs=[pl.BlockSpec((1, N_HEADS, rows, KEY_WIDTH), lambda b, t: (b, 0, t, 0)),
                   pl.BlockSpec((1, rows, KEY_WIDTH), lambda b, t: (b, t, 0)),
                   pl.BlockSpec((1, rows, CONV_DIM), lambda b, t: (b, t, 0)),
                   pl.BlockSpec((8, CONV_DIM), lambda b, t: (0, 0))],
        out_shape=[jax.ShapeDtypeStruct((bsz, N_HEADS, seq, KEY_WIDTH), BF16),
                   jax.ShapeDtypeStruct((bsz, seq, KEY_WIDTH), BF16),
                   jax.ShapeDtypeStruct((bsz, seq, CONV_DIM), BF16),
                   jax.ShapeDtypeStruct((8, CONV_DIM), F32)],
        scratch_shapes=[pltpu.VMEM((8, CONV_DIM), F32), pltpu.VMEM((Q_LORA, N_HEADS * KEY_WIDTH), BF16)],
        compiler_params=pltpu.CompilerParams(dimension_semantics=("arbitrary", "arbitrary"),
                                             vmem_limit_bytes=VMEM_LIMIT),
        name="premix",
    )(x3, tab, carry_in, g_pre, w1, gq, gkv, wqn, bduk, wqr, convw, gconv)


def _attention_kernel(q_ref, qnext_ref, k_ref, km_ref, o_ref, s_scr, smeta_scr, tri_scr, m_scr, l_scr, acc_scr, *,
                      bq, bk, chunks):
    assert bq == bk
    b = pl.program_id(0)
    i = pl.program_id(1)
    assert chunks % N_HEADS == 0
    row_splits = chunks // N_HEADS
    c_rows = bq // row_splits
    km = km_ref[...]

    def chunk_rows(c):
        return c // row_splits, pl.ds((c % row_splits) * c_rows, c_rows)

    def scores(qr, c, kb):
        h, r = chunk_rows(c)
        return _dot_nt(qr[0, h, r, :], kb)

    def values(kb):
        return jnp.concatenate([kb[:, :KV_LORA], jnp.ones((kb.shape[0], LANES), BF16)], axis=1)

    def consume(c, s, v_ones, mask, first=False):
        if mask is not None:
            s = jnp.where(mask, s, NEG_INF)
        m_blk = jnp.broadcast_to(jnp.max(s, axis=1, keepdims=True), (c_rows, LANES))
        m_new = m_blk if first else jnp.maximum(m_scr[c], m_blk)
        p = jnp.exp2(s - jnp.concatenate([m_new] * (s.shape[1] // LANES), axis=1)).astype(BF16)
        pv = _dot(p, v_ones)
        if first:
            l_scr[c] = pv[:, KV_LORA:]
            acc_scr[c] = pv[:, :KV_LORA]
        else:
            alpha = jnp.exp2(m_scr[c] - m_new)
            l_scr[c] = alpha * l_scr[c] + pv[:, KV_LORA:]
            acc_scr[c] = alpha * acc_scr[c] + pv[:, :KV_LORA]
        m_scr[c] = m_new

    def key_block(j):
        return k_ref[0, pl.ds(pl.multiple_of(j * bk, bk), bk), :]

    @pl.when((b == 0) & (i == 0))
    def _():
        for c in range(chunks):
            smeta_scr[c] = scores(q_ref, c, km)
        rows_i = lax.broadcasted_iota(jnp.int32, (bq, bk), 0)
        cols_i = lax.broadcasted_iota(jnp.int32, (bq, bk), 1)
        tri_scr[...] = jnp.where(cols_i <= rows_i, 0.0, NEG_INF)

    n_full = (i * bq) // bk
    kb0 = key_block(0)
    v_meta = values(km)
    col = lax.broadcasted_iota(jnp.int32, (c_rows, BLOCK), 1)
    for c in range(chunks):
        s_scr[c] = scores(q_ref, c, kb0)
        consume(c, smeta_scr[c], v_meta, col >= PAD_FRONT, first=True)

    def full_block(j):
        v_ones = values(key_block(j))
        kb_next = key_block(j + 1)
        for c in range(chunks):
            s = s_scr[c]
            s_scr[c] = scores(q_ref, c, kb_next)
            consume(c, s, v_ones, None)

    def body(j, carry):
        full_block(j)
        return carry

    lax.fori_loop(0, n_full, body, 0)

    v_ones = values(key_block(n_full))
    for c in range(chunks):
        _, r = chunk_rows(c)
        n_keys = (c % row_splits + 1) * c_rows
        s = s_scr[c, :, :n_keys] + tri_scr[r, :n_keys]
        smeta_scr[c] = scores(qnext_ref, c, km)
        consume(c, s, v_ones[:n_keys], None)

    for c in range(chunks):
        h, r = chunk_rows(c)
        o_ref[0, r, h * KV_LORA:(h + 1) * KV_LORA] = (acc_scr[c] / l_scr[c]).astype(BF16)


def _attention(qp, kc, kmeta, bq, bk, chunks):
    bsz, _, seq, _ = qp.shape
    nq = seq // bq
    c_rows = N_HEADS * bq // chunks
    scratch = [pltpu.VMEM((chunks, c_rows, bk), F32), pltpu.VMEM((chunks, c_rows, BLOCK), F32),
               pltpu.VMEM((bq, bk), F32), pltpu.VMEM((chunks, c_rows, LANES), F32),
               pltpu.VMEM((chunks, c_rows, LANES), F32), pltpu.VMEM((chunks, c_rows, KV_LORA), F32)]

    def next_q(b, i):
        wrap = (i + 1) // nq
        return (jnp.minimum(b + wrap, bsz - 1), 0, i + 1 - wrap * nq, 0)

    return pl.pallas_call(
        functools.partial(_attention_kernel, bq=bq, bk=bk, chunks=chunks),
        grid=(bsz, nq),
        in_specs=[pl.BlockSpec((1, N_HEADS, bq, KEY_WIDTH), lambda b, i: (b, 0, i, 0)),
                  pl.BlockSpec((1, N_HEADS, bq, KEY_WIDTH), next_q),
                  pl.BlockSpec((1, seq, KEY_WIDTH), lambda b, i: (b, 0, 0)),
                  pl.BlockSpec((BLOCK, KEY_WIDTH), lambda b, i: (0, 0))],
        out_specs=pl.BlockSpec((1, bq, N_HEADS * KV_LORA), lambda b, i: (b, i, 0)),
        out_shape=jax.ShapeDtypeStruct((bsz, seq, N_HEADS * KV_LORA), BF16),
        scratch_shapes=scratch,
        compiler_params=pltpu.CompilerParams(dimension_semantics=("arbitrary", "arbitrary"),
                                             vmem_limit_bytes=VMEM_LIMIT),
        name="attention",
    )(qp, qp, kc, kmeta)


def _postmix_kernel(x_ref, olat_ref, convn_ref, wuv_ref, gattn_ref, wout_ref, gpost_ref, gffn_ref, wr_ref, br_ref,
                    h1_ref, trow_ref, oh1_ref, oh2_ref, wts_ref, *, sub_blocks):
    rows = x_ref.shape[0] // sub_blocks
    wout = wout_ref[...]
    lane = lax.broadcasted_iota(jnp.int32, (rows, LANES), 1)

    def first_argmax(v):
        mx = jnp.max(v, axis=1, keepdims=True)
        return mx, jnp.min(jnp.where(v == mx, lane, LANES), axis=1, keepdims=True)

    for sb in range(sub_blocks):
        r = pl.ds(sb * rows, rows)
        attn = _dot(olat_ref[r, :], wuv_ref[...])
        attn_n = _rms(attn, gattn_ref[...]).astype(BF16)
        mix = _dot(attn_n, wout[:ATTN_WIDTH]) + _dot(convn_ref[r, :], wout[ATTN_WIDTH:])
        h1 = x_ref[r, :] + _rms(mix, gpost_ref[...])
        h1_ref[r, :] = h1
        tf = _rms(h1, gffn_ref[...])
        t = tf.astype(BF16)
        _store_token_rows(trow_ref, tf, sb * rows)

        logits = _dot(t, wr_ref[...]) + br_ref[...]
        gl = jnp.where(lane < N_GROUPS, logits[:, :LANES], NEG_INF)
        gmax, gidx = first_argmax(gl)
        g_w = 1.0 / jnp.sum(jnp.exp(gl - gmax), axis=1, keepdims=True)
        el = jnp.where((lane >> 3) == gidx, logits[:, LANES:], NEG_INF)
        m1, i1 = first_argmax(el)
        m2, i2 = first_argmax(jnp.where(lane == i1, NEG_INF, el))
        e2 = jnp.exp(m2 - m1)
        den = 1.0 + e2
        oh1_ref[r, :] = jnp.where(lane == i1, 1.0, 0.0).astype(BF16)
        oh2_ref[r, :] = jnp.where(lane == i2, 1.0, 0.0).astype(BF16)
        wts_ref[r, :] = jnp.where(lane == 0, g_w / den, jnp.where(lane == 1, g_w * e2 / den, 0.0))


def _postmix(x2, olat2, convn2, wts, rows, sub_blocks):
    n = x2.shape[0]
    wuv, gattn, wout, gpost, gffn, wr, br = wts
    const = lambda a: pl.BlockSpec(a.shape, lambda i: (0,) * a.ndim)
    tile = lambda w: pl.BlockSpec((rows, w), lambda i: (i, 0))
    return pl.pallas_call(
        functools.partial(_postmix_kernel, sub_blocks=sub_blocks),
        grid=(n // rows,),
        in_specs=[tile(D_MODEL), tile(N_HEADS * KV_LORA), tile(CONV_DIM),
                  const(wuv), const(gattn), const(wout), const(gpost), const(gffn), const(wr), const(br)],
        out_specs=[tile(D_MODEL), pl.BlockSpec((rows * SUBLANES, LANES), lambda i: (i, 0)),
                   tile(LANES), tile(LANES), tile(LANES)],
        out_shape=[jax.ShapeDtypeStruct((n, D_MODEL), F32),
                   jax.ShapeDtypeStruct((n * SUBLANES, LANES), F32),
                   jax.ShapeDtypeStruct((n, LANES), BF16),
                   jax.ShapeDtypeStruct((n, LANES), BF16),
                   jax.ShapeDtypeStruct((n, LANES), F32)],
        compiler_params=pltpu.CompilerParams(dimension_semantics=("arbitrary",), vmem_limit_bytes=VMEM_LIMIT),
        name="postmix",
    )(x2, olat2, convn2, wuv, gattn, wout, gpost, gffn, wr, br)


def _silu(a):
    return a * (1.0 / (1.0 + jnp.exp(-a)))


def _plan_kernel(oh1_ref, oh2_ref, pos_ref, te_ref, misc_ref, *, chunk):
    n_tok = oh1_ref.shape[0]
    n_chunks = n_tok // chunk
    lane = lax.broadcasted_iota(jnp.int32, (SUBLANES, LANES), 1)

    def rows_of(ref, c):
        return ref[pl.ds(pl.multiple_of(c * chunk, chunk), chunk), :]

    def count_body(c, acc):
        both = rows_of(oh1_ref, c).astype(F32) + rows_of(oh2_ref, c).astype(F32)
        return acc + jnp.sum(both, axis=0, keepdims=True)

    counts = lax.fori_loop(0, n_chunks, count_body, jnp.zeros((1, LANES), F32))
    counts = jnp.broadcast_to(counts, (SUBLANES, LANES)).astype(jnp.int32)
    tile_shift = EXPERT_TILE.bit_length() - 1
    padded = ((counts + (EXPERT_TILE - 1)) >> tile_shift) << tile_shift
    ends = padded
    shift = 1
    while shift < N_EXPERTS:
        ends = ends + jnp.where(lane >= shift, pltpu.roll(ends, shift, axis=1), 0)
        shift *= 2
    base = (ends - padded)[0:1].astype(F32)

    r_iota = lax.broadcasted_iota(jnp.int32, (chunk, chunk), 0)
    c_iota = lax.broadcasted_iota(jnp.int32, (chunk, chunk), 1)
    earlier = jnp.where(c_iota < r_iota, 1.0, 0.0).astype(BF16)

    def rank_body(c, run):
        for slot, ref in enumerate((oh1_ref, oh2_ref)):
            oh = rows_of(ref, c)
            ohf = oh.astype(F32)
            before = _dot(earlier, oh) + run
            posv = jnp.sum(ohf * before, axis=1, keepdims=True)
            pos_row = jnp.broadcast_to(posv, (chunk, LANES)).T[0:1, :]
            pos_ref[pl.ds(slot * n_chunks + c, 1), :] = pos_row.astype(jnp.int32)
            run = run + jnp.sum(ohf, axis=0, keepdims=True)
        return run

    lax.fori_loop(0, n_chunks, rank_body, base)

    n_tile_lanes = te_ref.shape[1]
    tile_start = lax.broadcasted_iota(jnp.int32, (SUBLANES, n_tile_lanes), 1) * EXPERT_TILE
    zeros = jnp.zeros((SUBLANES, n_tile_lanes), jnp.int32)
    total = ends[:, N_EXPERTS - 1:N_EXPERTS]
    te, seg_index, prev_end, seg_end = zeros, zeros, zeros, zeros + total
    for e in range(N_EXPERTS):
        end_e = ends[:, e:e + 1]
        done = end_e <= tile_start
        te = te + jnp.where(done, 1, 0)
        seg_index = seg_index + jnp.where(done & (padded[:, e:e + 1] > 0), 1, 0)
        prev_end = jnp.maximum(prev_end, jnp.where(done, end_e, 0))
        seg_end = jnp.minimum(seg_end, jnp.where(done, total, end_e))
    nxt = zeros
    for e in range(N_EXPERTS):
        nxt = nxt + jnp.where(ends[:, e:e + 1] <= seg_end, 1, 0)
    sub_t = lax.broadcasted_iota(jnp.int32, (SUBLANES, n_tile_lanes), 0)
    rows = [jnp.minimum(te, N_EXPERTS - 1), jnp.where(prev_end == tile_start, 1, 0),
            jnp.minimum(nxt, N_EXPERTS - 1), jnp.where(seg_end < total, 1, 0), seg_index & 1]
    info = zeros
    for r, v in enumerate(rows):
        info = jnp.where(sub_t == r, v, info)
    te_ref[...] = info
    sub = lax.broadcasted_iota(jnp.int32, (SUBLANES, LANES), 0)
    used_tiles = ends[:, N_EXPERTS - 1:N_EXPERTS] >> tile_shift
    first_pad = jnp.where(lane < N_EXPERTS, ends - padded + counts, used_tiles)
    misc_ref[...] = jnp.where(sub == 0, first_pad, ends)


def _plan(oh1, oh2, n_tiles, chunk):
    n = oh1.shape[0]
    n_tile_lanes = -(-n_tiles // LANES) * LANES
    whole = lambda shape: pl.BlockSpec(shape, lambda i: (0, 0))
    return pl.pallas_call(
        functools.partial(_plan_kernel, chunk=chunk),
        grid=(1,),
        in_specs=[whole((n, LANES)), whole((n, LANES))],
        out_specs=[whole((TOP_K * n // chunk, chunk)), whole((SUBLANES, n_tile_lanes)), whole((SUBLANES, LANES))],
        out_shape=[jax.ShapeDtypeStruct((TOP_K * n // chunk, chunk), jnp.int32),
                   jax.ShapeDtypeStruct((SUBLANES, n_tile_lanes), jnp.int32),
                   jax.ShapeDtypeStruct((SUBLANES, LANES), jnp.int32)],
        compiler_params=pltpu.CompilerParams(dimension_semantics=("arbitrary",), vmem_limit_bytes=VMEM_LIMIT),
        name="moe_plan",
    )(oh1, oh2)


def _row_slice(ref, row, n_rows=1):
    start = row * SUBLANES if isinstance(row, int) else pl.multiple_of(row * SUBLANES, SUBLANES)
    return ref.at[pl.ds(start, n_rows * SUBLANES)]


def _dispatch_kernel(pad_ref, end_ref, pos_ref, trow_ref, wsg_ref, wsu_ref, wsd_ref, xs_hbm, shared_ref, zero_buf,
                     zero_sem, row_sem, *, tokens, parts):
    i = pl.program_id(0)

    def zero_fill(phase):
        def run(first_row, n_rows):
            copy = pltpu.make_async_copy(_row_slice(zero_buf, 0, n_rows), _row_slice(xs_hbm, first_row, n_rows),
                                         zero_sem)
            copy.start() if phase == "start" else copy.wait()

        for e in range(N_EXPERTS):
            row = pad_ref[e]
            n_pad = end_ref[e] - row
            bit = EXPERT_TILE // 2
            while bit >= 1:
                @pl.when((n_pad & bit) != 0)
                def _():
                    run(row, bit)
                row = row + (n_pad & bit)
                bit //= 2

        def tail(t, carry):
            run(t * EXPERT_TILE, EXPERT_TILE)
            return carry

        lax.fori_loop(pad_ref[N_EXPERTS], xs_hbm.shape[0] // (EXPERT_TILE * SUBLANES), tail, 0)

    @pl.when(i == 0)
    def _():
        zero_buf[...] = jnp.zeros(zero_buf.shape, F32)
        zero_fill("start")

    unroll = 8

    def issue(g, carry):
        for u in range(unroll):
            n = g * unroll + u
            src = _row_slice(trow_ref, n)
            for k in range(TOP_K):
                pltpu.make_async_copy(src, _row_slice(xs_hbm, pos_ref[k, n]), row_sem).start(priority=(u + k) % 2)
        return carry

    part = tokens // parts
    for p in range(parts):
        lax.fori_loop(p * part // unroll, (p + 1) * part // unroll, issue, 0)
        t = _load_token_rows(trow_ref, part, p * part).astype(BF16)
        hid = _silu(_dot(t, wsg_ref[...])) * _dot(t, wsu_ref[...])
        shared_ref[p * part:(p + 1) * part, :] = _dot(hid.astype(BF16), wsd_ref[...])

    for k in range(TOP_K):
        pltpu.make_async_copy(trow_ref, _row_slice(xs_hbm, 0, tokens), row_sem).wait()

    @pl.when(i + 1 == pl.num_programs(0))
    def _():
        zero_fill("wait")


def _dispatch(first_pad, seg_end, pos, trow, shared_w, sorted_rows, tokens):
    n = pos.shape[1]
    const = lambda a: pl.BlockSpec(a.shape, lambda i: (0,) * a.ndim)
    return pl.pallas_call(
        functools.partial(_dispatch_kernel, tokens=tokens, parts=DISPATCH_PARTS),
        grid=(n // tokens,),
        in_specs=[pl.BlockSpec(memory_space=pltpu.SMEM), pl.BlockSpec(memory_space=pltpu.SMEM),
                  pl.BlockSpec((TOP_K, tokens), lambda i: (0, i), memory_space=pltpu.SMEM),
                  pl.BlockSpec((tokens * SUBLANES, LANES), lambda i: (i, 0))] + [const(w) for w in shared_w],
        out_specs=[pl.BlockSpec(memory_space=pl.ANY), pl.BlockSpec((tokens, D_MODEL), lambda i: (i, 0))],
        out_shape=[jax.ShapeDtypeStruct((sorted_rows * SUBLANES, LANES), F32),
                   jax.ShapeDtypeStruct((n, D_MODEL), F32)],
        scratch_shapes=[pltpu.VMEM((EXPERT_TILE * SUBLANES, LANES), F32), pltpu.SemaphoreType.DMA,
                        pltpu.SemaphoreType.DMA],
        compiler_params=pltpu.CompilerParams(dimension_semantics=("arbitrary",), vmem_limit_bytes=VMEM_LIMIT),
        name="moe_dispatch",
    )(first_pad, seg_end, pos, trow, *shared_w)


def _experts_kernel(te_ref, first_ref, next_ref, has_next_ref, slot_ref, used_ref, xs_ref, wg_hbm, wu_hbm, wd_hbm,
                    ys_ref, wg_buf, wu_buf, wd_buf, sems):
    t = pl.program_id(0)
    active = t < used_ref[0]

    def weights(e, slot, action):
        for j, (src, buf) in enumerate(((wg_hbm, wg_buf), (wu_hbm, wu_buf), (wd_hbm, wd_buf))):
            copy = pltpu.make_async_copy(src.at[e], buf.at[slot], sems.at[slot, j])
            copy.start() if action == "start" else copy.wait()

    @pl.when(t == 0)
    def _():
        weights(te_ref[0], slot_ref[0], "start")

    @pl.when(active & (first_ref[t] == 1))
    def _():
        @pl.when(has_next_ref[t] == 1)
        def _():
            weights(next_ref[t], 1 - slot_ref[t], "start")
        weights(te_ref[t], slot_ref[t], "wait")

    @pl.when(active)
    def _():
        slot = slot_ref[t]
        x = _load_token_rows(xs_ref, EXPERT_TILE).astype(BF16)
        hid = _silu(_dot(x, wg_buf[slot].astype(BF16))) * _dot(x, wu_buf[slot].astype(BF16))
        _store_token_rows(ys_ref, _dot(hid.astype(BF16), wd_buf[slot].astype(BF16)))

    @pl.when(jnp.logical_not(active))
    def _():
        ys_ref[...] = jnp.zeros(ys_ref.shape, F32)


def _experts(tile_info, used, xs, wg, wu, wd, n_tiles):
    blk = EXPERT_TILE * SUBLANES
    n_prefetch = len(tile_info) + 1
    tile_map = lambda t, te, first, nxt, has_next, slot, used: (t, 0)
    in_map = lambda t, te, first, nxt, has_next, slot, used: (jnp.minimum(t, used[0] - 1), 0)
    return pl.pallas_call(
        _experts_kernel,
        grid_spec=pltpu.PrefetchScalarGridSpec(
            num_scalar_prefetch=n_prefetch,
            grid=(n_tiles,),
            in_specs=[pl.BlockSpec((blk, LANES), in_map), pl.BlockSpec(memory_space=pl.ANY),
                      pl.BlockSpec(memory_space=pl.ANY), pl.BlockSpec(memory_space=pl.ANY)],
            out_specs=pl.BlockSpec((blk, LANES), tile_map),
            scratch_shapes=[pltpu.VMEM((2, D_MODEL, D_FF), F32), pltpu.VMEM((2, D_MODEL, D_FF), F32),
                            pltpu.VMEM((2, D_FF, D_MODEL), F32), pltpu.SemaphoreType.DMA((2, 3))]),
        out_shape=jax.ShapeDtypeStruct(xs.shape, F32),
        compiler_params=pltpu.CompilerParams(dimension_semantics=("arbitrary",), vmem_limit_bytes=VMEM_LIMIT),
        name="moe_experts",
    )(*tile_info, used, xs, wg, wu, wd)


def _combine_kernel(pos_ref, pos_next_ref, shared_ref, h1_ref, wts_ref, ys_hbm, gout_ref, out_ref, ybuf, sems, *,
                    tokens):
    i = pl.program_id(0)
    slot = lax.rem(i, 2)
    other = 1 - slot
    group = COMBINE_GROUP

    def start_gathers(p_ref, dst_slot, first):
        for u in range(group):
            n = first + u
            for k in range(TOP_K):
                pltpu.make_async_copy(_row_slice(ys_hbm, p_ref[k, n]), _row_slice(ybuf.at[dst_slot].at[k], n),
                                      sems.at[dst_slot]).start(priority=(u + k) % 2)

    def wait_slot(s):
        for k in range(TOP_K):
            pltpu.make_async_copy(_row_slice(ys_hbm, 0, tokens), ybuf.at[s].at[k], sems.at[s]).wait()

    @pl.when(i == 0)
    def _():
        def first_step(g, carry):
            start_gathers(pos_ref, slot, g * group)
            return carry
        lax.fori_loop(0, tokens // group, first_step, 0)

    wait_slot(slot)

    def trip(g, carry):
        first = pl.multiple_of(g * group, group)
        rows = pl.ds(first, group)
        wts = wts_ref[rows, :]
        routed = (wts[:, 0:1] * _load_token_rows(ybuf.at[slot].at[0], group, first)
                  + wts[:, 1:2] * _load_token_rows(ybuf.at[slot].at[1], group, first))
        out_ref[rows, :] = h1_ref[rows, :] + _rms(routed + shared_ref[rows, :], gout_ref[...])
        start_gathers(pos_next_ref, other, first)
        return carry

    lax.fori_loop(0, tokens // group, trip, 0)

    @pl.when(i + 1 == pl.num_programs(0))
    def _():
        wait_slot(other)


def _combine(pos, shared, h1, wts, ys, gout, tokens):
    n = h1.shape[0]
    const = lambda a: pl.BlockSpec(a.shape, lambda i: (0,) * a.ndim)
    tile = lambda w: pl.BlockSpec((tokens, w), lambda i: (i, 0))
    steps = n // tokens
    return pl.pallas_call(
        functools.partial(_combine_kernel, tokens=tokens),
        grid=(steps,),
        in_specs=[pl.BlockSpec((TOP_K, tokens), lambda i: (0, i), memory_space=pltpu.SMEM),
                  pl.BlockSpec((TOP_K, tokens), lambda i: (0, jnp.minimum(i + 1, steps - 1)),
                               memory_space=pltpu.SMEM),
                  tile(D_MODEL), tile(D_MODEL), tile(LANES), pl.BlockSpec(memory_space=pl.ANY), const(gout)],
        out_specs=tile(D_MODEL),
        out_shape=jax.ShapeDtypeStruct((n, D_MODEL), F32),
        scratch_shapes=[pltpu.VMEM((2, TOP_K, tokens * SUBLANES, LANES), F32), pltpu.SemaphoreType.DMA((2,))],
        compiler_params=pltpu.CompilerParams(dimension_semantics=("arbitrary",), vmem_limit_bytes=VMEM_LIMIT),
        name="moe_combine",
    )(pos, pos, shared, h1, wts, ys, gout)


def _swap_halves(w):
    half = w.shape[-1] // 2
    return jnp.concatenate([w[..., half:], w[..., :half]], axis=-1)


def _block_diag(blocks):
    h, r, c = blocks.shape
    eye = jnp.eye(h, dtype=blocks.dtype)
    return jnp.einsum('hrc,hg->hrgc', blocks, eye).reshape(h * r, h * c)


def kernel(x, positions, meta_tokens, pre_mix_norm, w_in, q_norm, w_uq, kv_norm, w_ukv, conv_w, attn_out_norm,
           conv_out_norm, w_out, post_mix_norm, pre_ffn_norm, w_group_router, b_group_router, w_expert_router,
           b_expert_router, w_gate, w_up, w_down, w_sh_gate, w_sh_up, w_sh_down, post_ffn_norm):
    bsz, seq, d = x.shape
    n = bsz * seq
    row = lambda v: v.reshape(1, -1).astype(F32)
    assert d == D_MODEL and w_in.shape[0] == 1, "one layer of width 1024 is supported"
    assert seq % MIX_ROWS == 0 and seq % ATTN_BLOCK == 0 and n % DISPATCH_TOKENS == 0
    assert (n + BLOCK) % (ROPE_STEPS * LANES) == 0

    wi = w_in[0]
    o_kv, o_pe, o_b = Q_LORA, Q_LORA + KV_LORA, Q_LORA + KV_LORA + QK_ROPE
    k_pe = wi[:, o_pe:o_b]
    w1 = jnp.concatenate([wi[:, :o_pe], k_pe, _swap_halves(k_pe), jnp.zeros((d, LANES - 2 * QK_ROPE), F32),
                          wi[:, o_b:]], axis=1).astype(BF16)
    wq = w_uq[0].reshape(Q_LORA, N_HEADS, QK_NOPE + QK_ROPE)
    wqn = wq[:, :, :QK_NOPE].reshape(Q_LORA, N_HEADS * QK_NOPE).astype(F32)
    wq_rope = wq[:, :, QK_NOPE:]
    wqr = jnp.concatenate([wq_rope, _swap_halves(wq_rope),
                           jnp.zeros((Q_LORA, N_HEADS, LANES - 2 * QK_ROPE), F32)], axis=2)
    wqr = wqr.reshape(Q_LORA, N_HEADS * LANES).astype(F32)
    wkv = w_ukv[0].reshape(KV_LORA, N_HEADS, QK_NOPE + V_DIM)
    bduk = _block_diag(jnp.transpose(wkv[:, :, :QK_NOPE], (1, 2, 0))).astype(F32)
    wuv = _block_diag(jnp.transpose(wkv[:, :, QK_NOPE:], (1, 0, 2))).astype(BF16)
    premix_w = (row(pre_mix_norm[0]), w1, row(q_norm[0]), row(kv_norm[0]), wqn, bduk, wqr,
                conv_w[0].astype(F32), row(conv_out_norm[0]))
    wr = jnp.concatenate([w_group_router[0], jnp.zeros((d, LANES - N_GROUPS), F32),
                          w_expert_router[0], jnp.zeros((d, LANES - N_EXPERTS), F32)], axis=1).astype(BF16)
    br = jnp.concatenate([b_group_router[0], jnp.zeros((LANES - N_GROUPS,), F32),
                          b_expert_router[0], jnp.zeros((LANES - N_EXPERTS,), F32)]).reshape(1, -1).astype(F32)
    postmix_w = (wuv, row(attn_out_norm[0]), w_out[0].astype(BF16), row(post_mix_norm[0]), row(pre_ffn_norm[0]),
                 wr, br)
    expert_w = (w_gate[0], w_up[0], w_down[0])
    shared_w = (w_sh_gate[0].astype(BF16), w_sh_up[0].astype(BF16), w_sh_down[0].astype(BF16))

    pos_all = jnp.concatenate([positions.astype(jnp.int32).reshape(-1) + N_META,
                               jnp.zeros((PAD_FRONT,), jnp.int32), jnp.arange(N_META, dtype=jnp.int32)])
    inv_freq = 1.0 / (ROPE_THETA ** (jnp.arange(0, QK_ROPE, 2, dtype=F32) / QK_ROPE))
    n_tab = n + BLOCK
    tab = _rope_table(pos_all.reshape(1, n_tab), inv_freq.reshape(-1, 1), n_tab // ROPE_STEPS)

    meta_blk = jnp.pad(meta_tokens.astype(F32), ((PAD_FRONT, 0), (0, 0)))[None]
    zero_carry = jnp.zeros((8, CONV_DIM), F32)
    _, kmeta, _, meta_tail = _premix(meta_blk, tab, n // BLOCK, zero_carry, premix_w, BLOCK, 1)
    qp, kc, convn, _ = _premix(x, tab, 0, meta_tail, premix_w, MIX_ROWS, MIX_SUB_BLOCKS)

    olat = _attention(qp, kc, kmeta[0], ATTN_BLOCK, ATTN_BLOCK, ATTN_CHUNKS)

    h1, trow, oh1, oh2, wts = _postmix(x.reshape(n, d), olat.reshape(n, -1), convn.reshape(n, -1), postmix_w,
                                       MIX_ROWS, MIX_SUB_BLOCKS)

    sorted_rows = TOP_K * n + N_EXPERTS * EXPERT_TILE
    n_tiles = sorted_rows // EXPERT_TILE
    pos2, te8, misc = _plan(oh1, oh2, n_tiles, PLAN_CHUNK)
    pos = pos2.reshape(TOP_K, n)
    xs, shared = _dispatch(misc[0, :N_EXPERTS + 1], misc[1, :N_EXPERTS], pos, trow, shared_w, sorted_rows,
                           DISPATCH_TOKENS)
    tile_info = [te8[r, :n_tiles] for r in range(5)]
    ys = _experts(tile_info, misc[0, N_EXPERTS:N_EXPERTS + 1], xs, *expert_w, n_tiles)
    out = _combine(pos, shared, h1, wts, ys, row(post_ffn_norm[0]), COMBINE_TOKENS)
    return out.reshape(bsz, seq, d)
```

```python
import functools
import math

import jax
import jax.numpy as jnp
from jax import lax
from jax.experimental import pallas as pl
from jax.experimental.pallas import tpu as pltpu

D_MODEL = 1024
N_META = 16
BLOCK = 128
PAD_FRONT = BLOCK - N_META
N_HEADS = 8
QK_NOPE = 64
QK_ROPE = 32
V_DIM = 64
Q_LORA = 256
KV_LORA = 128
ROPE_THETA = 10000.0
ATTN_SCALE = (QK_NOPE + QK_ROPE) ** -0.5
Q_SCALE = ATTN_SCALE * math.log2(math.e)
ATTN_WIDTH = N_HEADS * V_DIM
CONV_DIM = 512
CONV_W = 3
N_GROUPS = 4
EXPERTS_PER_GROUP = 8
N_EXPERTS = N_GROUPS * EXPERTS_PER_GROUP
D_FF = 256
EPS = 1e-6
NEG_INF = -1e30

LANES = 128
SUBLANES = 8
TOP_K = 2
EXPERT_TILE = 512
KEY_WIDTH = 2 * LANES

F32 = jnp.float32
BF16 = jnp.bfloat16
VMEM_LIMIT = 56 * 1024 * 1024

ROPE_STEPS = 3
MIX_ROWS = 1024
MIX_SUB_BLOCKS = 2
ATTN_BLOCK = 512
ATTN_CHUNKS = 16
PLAN_CHUNK = 512
DISPATCH_TOKENS = 512
DISPATCH_PARTS = 1
COMBINE_TOKENS = 256
COMBINE_GROUP = 32


def _rms(x, g):
    return x * lax.rsqrt(jnp.mean(x * x, axis=-1, keepdims=True) + EPS) * g


def _dot(a, b):
    return jnp.dot(a, b, preferred_element_type=F32)


def _dot_nt(a, b):
    return lax.dot_general(a, b, (((1,), (1,)), ((), ())), preferred_element_type=F32)


def _store_token_rows(ref, value, first_token=0):
    rows = value.shape[0]
    for c in range(D_MODEL // LANES):
        ref[pl.ds(first_token * SUBLANES + c, rows, stride=SUBLANES), :] = value[:, c * LANES:(c + 1) * LANES]


def _load_token_rows(ref, rows, first_token=0):
    return jnp.concatenate([ref[pl.ds(first_token * SUBLANES + c, rows, stride=SUBLANES), :]
                            for c in range(D_MODEL // LANES)], axis=1)


def _rope_table_kernel(pos_ref, invf_ref, out_ref):
    pos = pos_ref[...].astype(F32)
    ang = invf_ref[...] * pos
    c = jnp.cos(ang)
    s = jnp.sin(ang)
    planes = jnp.concatenate([c, c, -s, s, jnp.zeros((LANES - 4 * (QK_ROPE // 2), ang.shape[1]), F32)], axis=0)
    out_ref[...] = planes.T


def _rope_table(pos_row, inv_freq_col, chunk):
    n = pos_row.shape[1]
    return pl.pallas_call(
        _rope_table_kernel,
        grid=(n // chunk,),
        in_specs=[pl.BlockSpec((1, chunk), lambda i: (0, i)),
                  pl.BlockSpec((QK_ROPE // 2, 1), lambda i: (0, 0))],
        out_specs=pl.BlockSpec((chunk, LANES), lambda i: (i, 0)),
        out_shape=jax.ShapeDtypeStruct((n, LANES), F32),
        name="rope_table",
    )(pos_row, inv_freq_col)


def _premix_kernel(x_ref, tab_ref, carry_in_ref, g_pre_ref, w1_ref, gq_ref, gkv_ref, wqn_ref, bduk_ref,
                   wqr_ref, convw_ref, gconv_ref, qp_ref, kc_ref, convn_ref, utail_ref, carry_ref, wq_scr, *,
                   sub_blocks):
    t = pl.program_id(1)
    rows = x_ref.shape[1]

    @pl.when(t == 0)
    def _():
        carry_ref[...] = carry_in_ref[...]

    @pl.when((pl.program_id(0) == 0) & (t == 0))
    def _():
        wabs = jnp.dot(wqn_ref[...], bduk_ref[...], preferred_element_type=F32,
                       precision=lax.Precision.HIGHEST) * Q_SCALE
        wrope = wqr_ref[...] * Q_SCALE
        for h in range(N_HEADS):
            sl = slice(h * LANES, (h + 1) * LANES)
            wq_scr[:, h * KEY_WIDTH:h * KEY_WIDTH + LANES] = wabs[:, sl].astype(BF16)
            wq_scr[:, h * KEY_WIDTH + LANES:(h + 1) * KEY_WIDTH] = wrope[:, sl].astype(BF16)

    rows = rows // sub_blocks
    lane = lax.broadcasted_iota(jnp.int32, (rows, LANES), 1)
    row8 = lax.broadcasted_iota(jnp.int32, (8, CONV_DIM), 0)
    cw = convw_ref[...]
    prev = carry_ref[...]
    for sb in range(sub_blocks):
        r = pl.ds(sb * rows, rows)
        xn = _rms(x_ref[0, r, :], g_pre_ref[...]).astype(BF16)
        z = _dot(xn, w1_ref[...])
        tab = tab_ref[r, :]

        def rope(pair):
            prod = pair * tab
            return jnp.where(lane < QK_ROPE, prod + pltpu.roll(prod, LANES - QK_ROPE, axis=1), 0.0)

        ckvn = _rms(z[:, Q_LORA:Q_LORA + KV_LORA], gkv_ref[...])
        krope = rope(z[:, Q_LORA + KV_LORA:Q_LORA + KV_LORA + LANES])
        kc_ref[0, r, :] = jnp.concatenate([ckvn, krope], axis=1).astype(BF16)

        cqn = _rms(z[:, :Q_LORA], gq_ref[...]).astype(BF16)
        qall = _dot(cqn, wq_scr[...])
        for h in range(N_HEADS):
            qp_ref[0, h, r, 0:LANES] = qall[:, h * KEY_WIDTH:h * KEY_WIDTH + LANES].astype(BF16)
            qp_ref[0, h, r, LANES:KEY_WIDTH] = rope(
                qall[:, h * KEY_WIDTH + LANES:(h + 1) * KEY_WIDTH]).astype(BF16)

        c0 = Q_LORA + KV_LORA + LANES
        gate_b = z[:, c0:c0 + CONV_DIM]
        u = z[:, c0 + CONV_DIM:c0 + 2 * CONV_DIM] * z[:, c0 + 2 * CONV_DIM:c0 + 3 * CONV_DIM]
        r1 = pltpu.roll(u, 1, axis=0)
        r2 = pltpu.roll(u, 2, axis=0)
        p1 = pltpu.roll(prev, 1, axis=0)
        p2 = pltpu.roll(prev, 2, axis=0)
        u1 = jnp.concatenate([jnp.where(row8 < 1, p1, r1[0:8]), r1[8:]], axis=0)
        u2 = jnp.concatenate([jnp.where(row8 < 2, p2, r2[0:8]), r2[8:]], axis=0)
        y = cw[0:1] * u2 + cw[1:2] * u1 + cw[2:3] * u
        convn_ref[0, r, :] = _rms(gate_b * y, gconv_ref[...]).astype(BF16)
        prev = u[rows - 8:rows]
    carry_ref[...] = prev
    utail_ref[...] = prev


def _premix(x3, tab, tab_block0, carry_in, wts, rows, sub_blocks):
    bsz, seq, _ = x3.shape
    nt = seq // rows
    g_pre, w1, gq, gkv, wqn, bduk, wqr, convw, gconv = wts
    const = lambda a: pl.BlockSpec(a.shape, lambda b, t: (0,) * a.ndim)
    return pl.pallas_call(
        functools.partial(_premix_kernel, sub_blocks=sub_blocks),
        grid=(bsz, nt),
        in_specs=[pl.BlockSpec((1, rows, D_MODEL), lambda b, t: (b, t, 0)),
                  pl.BlockSpec((rows, LANES), lambda b, t: (tab_block0 + b * nt + t, 0)),
                  const(carry_in), const(g_pre), const(w1), const(gq), const(gkv), const(wqn), const(bduk),
                  const(wqr), const(convw), const(gconv)],
        out_specs=[pl.BlockSpec((1, N_HEADS, rows, KEY_WIDTH), lambda b, t: (b, 0, t, 0)),
                   pl.BlockSpec((1, rows, KEY_WIDTH), lambda b, t: (b, t, 0)),
                   pl.BlockSpec((1, rows, CONV_DIM), lambda b, t: (b, t, 0)),
                   pl.BlockSpec((8, CONV_DIM), lambda b, t: (0, 0))],
        out_shape=[jax.ShapeDtypeStruct((bsz, N_HEADS, seq, KEY_WIDTH), BF16),
                   jax.ShapeDtypeStruct((bsz, seq, KEY_WIDTH), BF16),
                   jax.ShapeDtypeStruct((bsz, seq, CONV_DIM), BF16),
                   jax.ShapeDtypeStruct((8, CONV_DIM), F32)],
        scratch_shapes=[pltpu.VMEM((8, CONV_DIM), F32), pltpu.VMEM((Q_LORA, N_HEADS * KEY_WIDTH), BF16)],
        compiler_params=pltpu.CompilerParams(dimension_semantics=("arbitrary", "arbitrary"),
                                             vmem_limit_bytes=VMEM_LIMIT),
        name="premix",
    )(x3, tab, carry_in, g_pre, w1, gq, gkv, wqn, bduk, wqr, convw, gconv)


def _attention_kernel(q_ref, qnext_ref, k_ref, km_ref, o_ref, s_scr, smeta_scr, tri_scr, m_scr, l_scr, acc_scr, *,
                      bq, bk, chunks):
    assert bq == bk
    b = pl.program_id(0)
    i = pl.program_id(1)
    assert chunks % N_HEADS == 0
    row_splits = chunks // N_HEADS
    c_rows = bq // row_splits
    km = km_ref[...]

    def chunk_rows(c):
        return c // row_splits, pl.ds((c % row_splits) * c_rows, c_rows)

    def scores(qr, c, kb):
        h, r = chunk_rows(c)
        return _dot_nt(qr[0, h, r, :], kb)

    def values(kb):
        return jnp.concatenate([kb[:, :KV_LORA], jnp.ones((kb.shape[0], LANES), BF16)], axis=1)

    def consume(c, s, v_ones, mask, first=False):
        if mask is not None:
            s = jnp.where(mask, s, NEG_INF)
        m_blk = jnp.broadcast_to(jnp.max(s, axis=1, keepdims=True), (c_rows, LANES))
        m_new = m_blk if first else jnp.maximum(m_scr[c], m_blk)
        p = jnp.exp2(s - jnp.concatenate([m_new] * (s.shape[1] // LANES), axis=1)).astype(BF16)
        pv = _dot(p, v_ones)
        if first:
            l_scr[c] = pv[:, KV_LORA:]
            acc_scr[c] = pv[:, :KV_LORA]
        else:
            alpha = jnp.exp2(m_scr[c] - m_new)
            l_scr[c] = alpha * l_scr[c] + pv[:, KV_LORA:]
            acc_scr[c] = alpha * acc_scr[c] + pv[:, :KV_LORA]
        m_scr[c] = m_new

    def key_block(j):
        return k_ref[0, pl.ds(pl.multiple_of(j * bk, bk), bk), :]

    @pl.when((b == 0) & (i == 0))
    def _():
        for c in range(chunks):
            smeta_scr[c] = scores(q_ref, c, km)
        rows_i = lax.broadcasted_iota(jnp.int32, (bq, bk), 0)
        cols_i = lax.broadcasted_iota(jnp.int32, (bq, bk), 1)
        tri_scr[...] = jnp.where(cols_i <= rows_i, 0.0, NEG_INF)

    n_full = (i * bq) // bk
    kb0 = key_block(0)
    v_meta = values(km)
    col = lax.broadcasted_iota(jnp.int32, (c_rows, BLOCK), 1)
    for c in range(chunks):
        s_scr[c] = scores(q_ref, c, kb0)
        consume(c, smeta_scr[c], v_meta, col >= PAD_FRONT, first=True)

    def full_block(j):
        v_ones = values(key_block(j))
        kb_next = key_block(j + 1)
        for c in range(chunks):
            s = s_scr[c]
            s_scr[c] = scores(q_ref, c, kb_next)
            consume(c, s, v_ones, None)

    def body(j, carry):
        full_block(j)
        return carry

    lax.fori_loop(0, n_full, body, 0)

    v_ones = values(key_block(n_full))
    for c in range(chunks):
        _, r = chunk_rows(c)
        n_keys = (c % row_splits + 1) * c_rows
        s = s_scr[c, :, :n_keys] + tri_scr[r, :n_keys]
        smeta_scr[c] = scores(qnext_ref, c, km)
        consume(c, s, v_ones[:n_keys], None)

    for c in range(chunks):
        h, r = chunk_rows(c)
        o_ref[0, r, h * KV_LORA:(h + 1) * KV_LORA] = (acc_scr[c] / l_scr[c]).astype(BF16)


def _attention(qp, kc, kmeta, bq, bk, chunks):
    bsz, _, seq, _ = qp.shape
    nq = seq // bq
    c_rows = N_HEADS * bq // chunks
    scratch = [pltpu.VMEM((chunks, c_rows, bk), F32), pltpu.VMEM((chunks, c_rows, BLOCK), F32),
               pltpu.VMEM((bq, bk), F32), pltpu.VMEM((chunks, c_rows, LANES), F32),
               pltpu.VMEM((chunks, c_rows, LANES), F32), pltpu.VMEM((chunks, c_rows, KV_LORA), F32)]

    def next_q(b, i):
        wrap = (i + 1) // nq
        return (jnp.minimum(b + wrap, bsz - 1), 0, i + 1 - wrap * nq, 0)

    return pl.pallas_call(
        functools.partial(_attention_kernel, bq=bq, bk=bk, chunks=chunks),
        grid=(bsz, nq),
        in_specs=[pl.BlockSpec((1, N_HEADS, bq, KEY_WIDTH), lambda b, i: (b, 0, i, 0)),
                  pl.BlockSpec((1, N_HEADS, bq, KEY_WIDTH), next_q),
                  pl.BlockSpec((1, seq, KEY_WIDTH), lambda b, i: (b, 0, 0)),
                  pl.BlockSpec((BLOCK, KEY_WIDTH), lambda b, i: (0, 0))],
        out_specs=pl.BlockSpec((1, bq, N_HEADS * KV_LORA), lambda b, i: (b, i, 0)),
        out_shape=jax.ShapeDtypeStruct((bsz, seq, N_HEADS * KV_LORA), BF16),
        scratch_shapes=scratch,
        compiler_params=pltpu.CompilerParams(dimension_semantics=("arbitrary", "arbitrary"),
                                             vmem_limit_bytes=VMEM_LIMIT),
        name="attention",
    )(qp, qp, kc, kmeta)


def _postmix_kernel(x_ref, olat_ref, convn_ref, wuv_ref, gattn_ref, wout_ref, gpost_ref, gffn_ref, wr_ref, br_ref,
                    h1_ref, trow_ref, oh1_ref, oh2_ref, wts_ref, *, sub_blocks):
    rows = x_ref.shape[0] // sub_blocks
    wout = wout_ref[...]
    lane = lax.broadcasted_iota(jnp.int32, (rows, LANES), 1)

    def first_argmax(v):
        mx = jnp.max(v, axis=1, keepdims=True)
        return mx, jnp.min(jnp.where(v == mx, lane, LANES), axis=1, keepdims=True)

    for sb in range(sub_blocks):
        r = pl.ds(sb * rows, rows)
        attn = _dot(olat_ref[r, :], wuv_ref[...])
        attn_n = _rms(attn, gattn_ref[...]).astype(BF16)
        mix = _dot(attn_n, wout[:ATTN_WIDTH]) + _dot(convn_ref[r, :], wout[ATTN_WIDTH:])
        h1 = x_ref[r, :] + _rms(mix, gpost_ref[...])
        h1_ref[r, :] = h1
        tf = _rms(h1, gffn_ref[...])
        t = tf.astype(BF16)
        _store_token_rows(trow_ref, tf, sb * rows)

        logits = _dot(t, wr_ref[...]) + br_ref[...]
        gl = jnp.where(lane < N_GROUPS, logits[:, :LANES], NEG_INF)
        gmax, gidx = first_argmax(gl)
        g_w = 1.0 / jnp.sum(jnp.exp(gl - gmax), axis=1, keepdims=True)
        el = jnp.where((lane >> 3) == gidx, logits[:, LANES:], NEG_INF)
        m1, i1 = first_argmax(el)
        m2, i2 = first_argmax(jnp.where(lane == i1, NEG_INF, el))
        e2 = jnp.exp(m2 - m1)
        den = 1.0 + e2
        oh1_ref[r, :] = jnp.where(lane == i1, 1.0, 0.0).astype(BF16)
        oh2_ref[r, :] = jnp.where(lane == i2, 1.0, 0.0).astype(BF16)
        wts_ref[r, :] = jnp.where(lane == 0, g_w / den, jnp.where(lane == 1, g_w * e2 / den, 0.0))


def _postmix(x2, olat2, convn2, wts, rows, sub_blocks):
    n = x2.shape[0]
    wuv, gattn, wout, gpost, gffn, wr, br = wts
    const = lambda a: pl.BlockSpec(a.shape, lambda i: (0,) * a.ndim)
    tile = lambda w: pl.BlockSpec((rows, w), lambda i: (i, 0))
    return pl.pallas_call(
        functools.partial(_postmix_kernel, sub_blocks=sub_blocks),
        grid=(n // rows,),
        in_specs=[tile(D_MODEL), tile(N_HEADS * KV_LORA), tile(CONV_DIM),
                  const(wuv), const(gattn), const(wout), const(gpost), const(gffn), const(wr), const(br)],
        out_specs=[tile(D_MODEL), pl.BlockSpec((rows * SUBLANES, LANES), lambda i: (i, 0)),
                   tile(LANES), tile(LANES), tile(LANES)],
        out_shape=[jax.ShapeDtypeStruct((n, D_MODEL), F32),
                   jax.ShapeDtypeStruct((n * SUBLANES, LANES), F32),
                   jax.ShapeDtypeStruct((n, LANES), BF16),
                   jax.ShapeDtypeStruct((n, LANES), BF16),
                   jax.ShapeDtypeStruct((n, LANES), F32)],
        compiler_params=pltpu.CompilerParams(dimension_semantics=("arbitrary",), vmem_limit_bytes=VMEM_LIMIT),
        name="postmix",
    )(x2, olat2, convn2, wuv, gattn, wout, gpost, gffn, wr, br)


def _silu(a):
    return a * (1.0 / (1.0 + jnp.exp(-a)))


def _plan_kernel(oh1_ref, oh2_ref, pos_ref, te_ref, misc_ref, *, chunk):
    n_tok = oh1_ref.shape[0]
    n_chunks = n_tok // chunk
    lane = lax.broadcasted_iota(jnp.int32, (SUBLANES, LANES), 1)

    def rows_of(ref, c):
        return ref[pl.ds(pl.multiple_of(c * chunk, chunk), chunk), :]

    def count_body(c, acc):
        both = rows_of(oh1_ref, c).astype(F32) + rows_of(oh2_ref, c).astype(F32)
        return acc + jnp.sum(both, axis=0, keepdims=True)

    counts = lax.fori_loop(0, n_chunks, count_body, jnp.zeros((1, LANES), F32))
    counts = jnp.broadcast_to(counts, (SUBLANES, LANES)).astype(jnp.int32)
    tile_shift = EXPERT_TILE.bit_length() - 1
    padded = ((counts + (EXPERT_TILE - 1)) >> tile_shift) << tile_shift
    ends = padded
    shift = 1
    while shift < N_EXPERTS:
        ends = ends + jnp.where(lane >= shift, pltpu.roll(ends, shift, axis=1), 0)
        shift *= 2
    base = (ends - padded)[0:1].astype(F32)

    r_iota = lax.broadcasted_iota(jnp.int32, (chunk, chunk), 0)
    c_iota = lax.broadcasted_iota(jnp.int32, (chunk, chunk), 1)
    earlier = jnp.where(c_iota < r_iota, 1.0, 0.0).astype(BF16)

    def rank_body(c, run):
        for slot, ref in enumerate((oh1_ref, oh2_ref)):
            oh = rows_of(ref, c)
            ohf = oh.astype(F32)
            before = _dot(earlier, oh) + run
            posv = jnp.sum(ohf * before, axis=1, keepdims=True)
            pos_row = jnp.broadcast_to(posv, (chunk, LANES)).T[0:1, :]
            pos_ref[pl.ds(slot * n_chunks + c, 1), :] = pos_row.astype(jnp.int32)
            run = run + jnp.sum(ohf, axis=0, keepdims=True)
        return run

    lax.fori_loop(0, n_chunks, rank_body, base)

    n_tile_lanes = te_ref.shape[1]
    tile_start = lax.broadcasted_iota(jnp.int32, (SUBLANES, n_tile_lanes), 1) * EXPERT_TILE
    zeros = jnp.zeros((SUBLANES, n_tile_lanes), jnp.int32)
    total = ends[:, N_EXPERTS - 1:N_EXPERTS]
    te, seg_index, prev_end, seg_end = zeros, zeros, zeros, zeros + total
    for e in range(N_EXPERTS):
        end_e = ends[:, e:e + 1]
        done = end_e <= tile_start
        te = te + jnp.where(done, 1, 0)
        seg_index = seg_index + jnp.where(done & (padded[:, e:e + 1] > 0), 1, 0)
        prev_end = jnp.maximum(prev_end, jnp.where(done, end_e, 0))
        seg_end = jnp.minimum(seg_end, jnp.where(done, total, end_e))
    nxt = zeros
    for e in range(N_EXPERTS):
        nxt = nxt + jnp.where(ends[:, e:e + 1] <= seg_end, 1, 0)
    sub_t = lax.broadcasted_iota(jnp.int32, (SUBLANES, n_tile_lanes), 0)
    rows = [jnp.minimum(te, N_EXPERTS - 1), jnp.where(prev_end == tile_start, 1, 0),
            jnp.minimum(nxt, N_EXPERTS - 1), jnp.where(seg_end < total, 1, 0), seg_index & 1]
    info = zeros
    for r, v in enumerate(rows):
        info = jnp.where(sub_t == r, v, info)
    te_ref[...] = info
    sub = lax.broadcasted_iota(jnp.int32, (SUBLANES, LANES), 0)
    used_tiles = ends[:, N_EXPERTS - 1:N_EXPERTS] >> tile_shift
    first_pad = jnp.where(lane < N_EXPERTS, ends - padded + counts, used_tiles)
    misc_ref[...] = jnp.where(sub == 0, first_pad, ends)


def _plan(oh1, oh2, n_tiles, chunk):
    n = oh1.shape[0]
    n_tile_lanes = -(-n_tiles // LANES) * LANES
    whole = lambda shape: pl.BlockSpec(shape, lambda i: (0, 0))
    return pl.pallas_call(
        functools.partial(_plan_kernel, chunk=chunk),
        grid=(1,),
        in_specs=[whole((n, LANES)), whole((n, LANES))],
        out_specs=[whole((TOP_K * n // chunk, chunk)), whole((SUBLANES, n_tile_lanes)), whole((SUBLANES, LANES))],
        out_shape=[jax.ShapeDtypeStruct((TOP_K * n // chunk, chunk), jnp.int32),
                   jax.ShapeDtypeStruct((SUBLANES, n_tile_lanes), jnp.int32),
                   jax.ShapeDtypeStruct((SUBLANES, LANES), jnp.int32)],
        compiler_params=pltpu.CompilerParams(dimension_semantics=("arbitrary",), vmem_limit_bytes=VMEM_LIMIT),
        name="moe_plan",
    )(oh1, oh2)


def _row_slice(ref, row, n_rows=1):
    start = row * SUBLANES if isinstance(row, int) else pl.multiple_of(row * SUBLANES, SUBLANES)
    return ref.at[pl.ds(start, n_rows * SUBLANES)]


def _dispatch_kernel(pad_ref, end_ref, pos_ref, trow_ref, wsg_ref, wsu_ref, wsd_ref, xs_hbm, shared_ref, zero_buf,
                     zero_sem, row_sem, *, tokens, parts):
    i = pl.program_id(0)

    def zero_fill(phase):
        def run(first_row, n_rows):
            copy = pltpu.make_async_copy(_row_slice(zero_buf, 0, n_rows), _row_slice(xs_hbm, first_row, n_rows),
                                         zero_sem)
            copy.start() if phase == "start" else copy.wait()

        for e in range(N_EXPERTS):
            row = pad_ref[e]
            n_pad = end_ref[e] - row
            bit = EXPERT_TILE // 2
            while bit >= 1:
                @pl.when((n_pad & bit) != 0)
                def _():
                    run(row, bit)
                row = row + (n_pad & bit)
                bit //= 2

        def tail(t, carry):
            run(t * EXPERT_TILE, EXPERT_TILE)
            return carry

        lax.fori_loop(pad_ref[N_EXPERTS], xs_hbm.shape[0] // (EXPERT_TILE * SUBLANES), tail, 0)

    @pl.when(i == 0)
    def _():
        zero_buf[...] = jnp.zeros(zero_buf.shape, F32)
        zero_fill("start")

    unroll = 8

    def issue(g, carry):
        for u in range(unroll):
            n = g * unroll + u
            src = _row_slice(trow_ref, n)
            for k in range(TOP_K):
                pltpu.make_async_copy(src, _row_slice(xs_hbm, pos_ref[k, n]), row_sem).start(priority=(u + k) % 2)
        return carry

    part = tokens // parts
    for p in range(parts):
        lax.fori_loop(p * part // unroll, (p + 1) * part // unroll, issue, 0)
        t = _load_token_rows(trow_ref, part, p * part).astype(BF16)
        hid = _silu(_dot(t, wsg_ref[...])) * _dot(t, wsu_ref[...])
        shared_ref[p * part:(p + 1) * part, :] = _dot(hid.astype(BF16), wsd_ref[...])

    for k in range(TOP_K):
        pltpu.make_async_copy(trow_ref, _row_slice(xs_hbm, 0, tokens), row_sem).wait()

    @pl.when(i + 1 == pl.num_programs(0))
    def _():
        zero_fill("wait")


def _dispatch(first_pad, seg_end, pos, trow, shared_w, sorted_rows, tokens):
    n = pos.shape[1]
    const = lambda a: pl.BlockSpec(a.shape, lambda i: (0,) * a.ndim)
    return pl.pallas_call(
        functools.partial(_dispatch_kernel, tokens=tokens, parts=DISPATCH_PARTS),
        grid=(n // tokens,),
        in_specs=[pl.BlockSpec(memory_space=pltpu.SMEM), pl.BlockSpec(memory_space=pltpu.SMEM),
                  pl.BlockSpec((TOP_K, tokens), lambda i: (0, i), memory_space=pltpu.SMEM),
                  pl.BlockSpec((tokens * SUBLANES, LANES), lambda i: (i, 0))] + [const(w) for w in shared_w],
        out_specs=[pl.BlockSpec(memory_space=pl.ANY), pl.BlockSpec((tokens, D_MODEL), lambda i: (i, 0))],
        out_shape=[jax.ShapeDtypeStruct((sorted_rows * SUBLANES, LANES), F32),
                   jax.ShapeDtypeStruct((n, D_MODEL), F32)],
        scratch_shapes=[pltpu.VMEM((EXPERT_TILE * SUBLANES, LANES), F32), pltpu.SemaphoreType.DMA,
                        pltpu.SemaphoreType.DMA],
        compiler_params=pltpu.CompilerParams(dimension_semantics=("arbitrary",), vmem_limit_bytes=VMEM_LIMIT),
        name="moe_dispatch",
    )(first_pad, seg_end, pos, trow, *shared_w)


def _experts_kernel(te_ref, first_ref, next_ref, has_next_ref, slot_ref, used_ref, xs_ref, wg_hbm, wu_hbm, wd_hbm,
                    ys_ref, wg_buf, wu_buf, wd_buf, sems):
    t = pl.program_id(0)
    active = t < used_ref[0]

    def weights(e, slot, action):
        for j, (src, buf) in enumerate(((wg_hbm, wg_buf), (wu_hbm, wu_buf), (wd_hbm, wd_buf))):
            copy = pltpu.make_async_copy(src.at[e], buf.at[slot], sems.at[slot, j])
            copy.start() if action == "start" else copy.wait()

    @pl.when(t == 0)
    def _():
        weights(te_ref[0], slot_ref[0], "start")

    @pl.when(active & (first_ref[t] == 1))
    def _():
        @pl.when(has_next_ref[t] == 1)
        def _():
            weights(next_ref[t], 1 - slot_ref[t], "start")
        weights(te_ref[t], slot_ref[t], "wait")

    @pl.when(active)
    def _():
        slot = slot_ref[t]
        x = _load_token_rows(xs_ref, EXPERT_TILE).astype(BF16)
        hid = _silu(_dot(x, wg_buf[slot].astype(BF16))) * _dot(x, wu_buf[slot].astype(BF16))
        _store_token_rows(ys_ref, _dot(hid.astype(BF16), wd_buf[slot].astype(BF16)))

    @pl.when(jnp.logical_not(active))
    def _():
        ys_ref[...] = jnp.zeros(ys_ref.shape, F32)


def _experts(tile_info, used, xs, wg, wu, wd, n_tiles):
    blk = EXPERT_TILE * SUBLANES
    n_prefetch = len(tile_info) + 1
    tile_map = lambda t, te, first, nxt, has_next, slot, used: (t, 0)
    in_map = lambda t, te, first, nxt, has_next, slot, used: (jnp.minimum(t, used[0] - 1), 0)
    return pl.pallas_call(
        _experts_kernel,
        grid_spec=pltpu.PrefetchScalarGridSpec(
            num_scalar_prefetch=n_prefetch,
            grid=(n_tiles,),
            in_specs=[pl.BlockSpec((blk, LANES), in_map), pl.BlockSpec(memory_space=pl.ANY),
                      pl.BlockSpec(memory_space=pl.ANY), pl.BlockSpec(memory_space=pl.ANY)],
            out_specs=pl.BlockSpec((blk, LANES), tile_map),
            scratch_shapes=[pltpu.VMEM((2, D_MODEL, D_FF), F32), pltpu.VMEM((2, D_MODEL, D_FF), F32),
                            pltpu.VMEM((2, D_FF, D_MODEL), F32), pltpu.SemaphoreType.DMA((2, 3))]),
        out_shape=jax.ShapeDtypeStruct(xs.shape, F32),
        compiler_params=pltpu.CompilerParams(dimension_semantics=("arbitrary",), vmem_limit_bytes=VMEM_LIMIT),
        name="moe_experts",
    )(*tile_info, used, xs, wg, wu, wd)


def _combine_kernel(pos_ref, pos_next_ref, shared_ref, h1_ref, wts_ref, ys_hbm, gout_ref, out_ref, ybuf, sems, *,
                    tokens):
    i = pl.program_id(0)
    slot = lax.rem(i, 2)
    other = 1 - slot
    group = COMBINE_GROUP

    def start_gathers(p_ref, dst_slot, first):
        for u in range(group):
            n = first + u
            for k in range(TOP_K):
                pltpu.make_async_copy(_row_slice(ys_hbm, p_ref[k, n]), _row_slice(ybuf.at[dst_slot].at[k], n),
                                      sems.at[dst_slot]).start(priority=(u + k) % 2)

    def wait_slot(s):
        for k in range(TOP_K):
            pltpu.make_async_copy(_row_slice(ys_hbm, 0, tokens), ybuf.at[s].at[k], sems.at[s]).wait()

    @pl.when(i == 0)
    def _():
        def first_step(g, carry):
            start_gathers(pos_ref, slot, g * group)
            return carry
        lax.fori_loop(0, tokens // group, first_step, 0)

    wait_slot(slot)

    def trip(g, carry):
        first = pl.multiple_of(g * group, group)
        rows = pl.ds(first, group)
        wts = wts_ref[rows, :]
        routed = (wts[:, 0:1] * _load_token_rows(ybuf.at[slot].at[0], group, first)
                  + wts[:, 1:2] * _load_token_rows(ybuf.at[slot].at[1], group, first))
        out_ref[rows, :] = h1_ref[rows, :] + _rms(routed + shared_ref[rows, :], gout_ref[...])
        start_gathers(pos_next_ref, other, first)
        return carry

    lax.fori_loop(0, tokens // group, trip, 0)

    @pl.when(i + 1 == pl.num_programs(0))
    def _():
        wait_slot(other)


def _combine(pos, shared, h1, wts, ys, gout, tokens):
    n = h1.shape[0]
    const = lambda a: pl.BlockSpec(a.shape, lambda i: (0,) * a.ndim)
    tile = lambda w: pl.BlockSpec((tokens, w), lambda i: (i, 0))
    steps = n // tokens
    return pl.pallas_call(
        functools.partial(_combine_kernel, tokens=tokens),
        grid=(steps,),
        in_specs=[pl.BlockSpec((TOP_K, tokens), lambda i: (0, i), memory_space=pltpu.SMEM),
                  pl.BlockSpec((TOP_K, tokens), lambda i: (0, jnp.minimum(i + 1, steps - 1)),
                               memory_space=pltpu.SMEM),
                  tile(D_MODEL), tile(D_MODEL), tile(LANES), pl.BlockSpec(memory_space=pl.ANY), const(gout)],
        out_specs=tile(D_MODEL),
        out_shape=jax.ShapeDtypeStruct((n, D_MODEL), F32),
        scratch_shapes=[pltpu.VMEM((2, TOP_K, tokens * SUBLANES, LANES), F32), pltpu.SemaphoreType.DMA((2,))],
        compiler_params=pltpu.CompilerParams(dimension_semantics=("arbitrary",), vmem_limit_bytes=VMEM_LIMIT),
        name="moe_combine",
    )(pos, pos, shared, h1, wts, ys, gout)


def _swap_halves(w):
    half = w.shape[-1] // 2
    return jnp.concatenate([w[..., half:], w[..., :half]], axis=-1)


def _block_diag(blocks):
    h, r, c = blocks.shape
    eye = jnp.eye(h, dtype=blocks.dtype)
    return jnp.einsum('hrc,hg->hrgc', blocks, eye).reshape(h * r, h * c)


def kernel(x, positions, meta_tokens, pre_mix_norm, w_in, q_norm, w_uq, kv_norm, w_ukv, conv_w, attn_out_norm,
           conv_out_norm, w_out, post_mix_norm, pre_ffn_norm, w_group_router, b_group_router, w_expert_router,
           b_expert_router, w_gate, w_up, w_down, w_sh_gate, w_sh_up, w_sh_down, post_ffn_norm):
    bsz, seq, d = x.shape
    n = bsz * seq
    row = lambda v: v.reshape(1, -1).astype(F32)
    assert d == D_MODEL and w_in.shape[0] == 1, "one layer of width 1024 is supported"
    assert seq % MIX_ROWS == 0 and seq % ATTN_BLOCK == 0 and n % DISPATCH_TOKENS == 0
    assert (n + BLOCK) % (ROPE_STEPS * LANES) == 0

    wi = w_in[0]
    o_kv, o_pe, o_b = Q_LORA, Q_LORA + KV_LORA, Q_LORA + KV_LORA + QK_ROPE
    k_pe = wi[:, o_pe:o_b]
    w1 = jnp.concatenate([wi[:, :o_pe], k_pe, _swap_halves(k_pe), jnp.zeros((d, LANES - 2 * QK_ROPE), F32),
                          wi[:, o_b:]], axis=1).astype(BF16)
    wq = w_uq[0].reshape(Q_LORA, N_HEADS, QK_NOPE + QK_ROPE)
    wqn = wq[:, :, :QK_NOPE].reshape(Q_LORA, N_HEADS * QK_NOPE).astype(F32)
    wq_rope = wq[:, :, QK_NOPE:]
    wqr = jnp.concatenate([wq_rope, _swap_halves(wq_rope),
                           jnp.zeros((Q_LORA, N_HEADS, LANES - 2 * QK_ROPE), F32)], axis=2)
    wqr = wqr.reshape(Q_LORA, N_HEADS * LANES).astype(F32)
    wkv = w_ukv[0].reshape(KV_LORA, N_HEADS, QK_NOPE + V_DIM)
    bduk = _block_diag(jnp.transpose(wkv[:, :, :QK_NOPE], (1, 2, 0))).astype(F32)
    wuv = _block_diag(jnp.transpose(wkv[:, :, QK_NOPE:], (1, 0, 2))).astype(BF16)
    premix_w = (row(pre_mix_norm[0]), w1, row(q_norm[0]), row(kv_norm[0]), wqn, bduk, wqr,
                conv_w[0].astype(F32), row(conv_out_norm[0]))
    wr = jnp.concatenate([w_group_router[0], jnp.zeros((d, LANES - N_GROUPS), F32),
                          w_expert_router[0], jnp.zeros((d, LANES - N_EXPERTS), F32)], axis=1).astype(BF16)
    br = jnp.concatenate([b_group_router[0], jnp.zeros((LANES - N_GROUPS,), F32),
                          b_expert_router[0], jnp.zeros((LANES - N_EXPERTS,), F32)]).reshape(1, -1).astype(F32)
    postmix_w = (wuv, row(attn_out_norm[0]), w_out[0].astype(BF16), row(post_mix_norm[0]), row(pre_ffn_norm[0]),
                 wr, br)
    expert_w = (w_gate[0], w_up[0], w_down[0])
    shared_w = (w_sh_gate[0].astype(BF16), w_sh_up[0].astype(BF16), w_sh_down[0].astype(BF16))

    pos_all = jnp.concatenate([positions.astype(jnp.int32).reshape(-1) + N_META,
                               jnp.zeros((PAD_FRONT,), jnp.int32), jnp.arange(N_META, dtype=jnp.int32)])
    inv_freq = 1.0 / (ROPE_THETA ** (jnp.arange(0, QK_ROPE, 2, dtype=F32) / QK_ROPE))
    n_tab = n + BLOCK
    tab = _rope_table(pos_all.reshape(1, n_tab), inv_freq.reshape(-1, 1), n_tab // ROPE_STEPS)

    meta_blk = jnp.pad(meta_tokens.astype(F32), ((PAD_FRONT, 0), (0, 0)))[None]
    zero_carry = jnp.zeros((8, CONV_DIM), F32)
    _, kmeta, _, meta_tail = _premix(meta_blk, tab, n // BLOCK, zero_carry, premix_w, BLOCK, 1)
    qp, kc, convn, _ = _premix(x, tab, 0, meta_tail, premix_w, MIX_ROWS, MIX_SUB_BLOCKS)

    olat = _attention(qp, kc, kmeta[0], ATTN_BLOCK, ATTN_BLOCK, ATTN_CHUNKS)

    h1, trow, oh1, oh2, wts = _postmix(x.reshape(n, d), olat.reshape(n, -1), convn.reshape(n, -1), postmix_w,
                                       MIX_ROWS, MIX_SUB_BLOCKS)

    sorted_rows = TOP_K * n + N_EXPERTS * EXPERT_TILE
    n_tiles = sorted_rows // EXPERT_TILE
    pos2, te8, misc = _plan(oh1, oh2, n_tiles, PLAN_CHUNK)
    pos = pos2.reshape(TOP_K, n)
    xs, shared = _dispatch(misc[0, :N_EXPERTS + 1], misc[1, :N_EXPERTS], pos, trow, shared_w, sorted_rows,
                           DISPATCH_TOKENS)
    tile_info = [te8[r, :n_tiles] for r in range(5)]
    ys = _experts(tile_info, misc[0, N_EXPERTS:N_EXPERTS + 1], xs, *expert_w, n_tiles)
    out = _combine(pos, shared, h1, wts, ys, row(post_ffn_norm[0]), COMBINE_TOKENS)
    return out.reshape(bsz, seq, d)
```

```python
import functools
import math

import jax
import jax.numpy as jnp
from jax import lax
from jax.experimental import pallas as pl
from jax.experimental.pallas import tpu as pltpu

D_MODEL = 1024
N_META = 16
BLOCK = 128
PAD_FRONT = BLOCK - N_META
N_HEADS = 8
QK_NOPE = 64
QK_ROPE = 32
V_DIM = 64
Q_LORA = 256
KV_LORA = 128
ROPE_THETA = 10000.0
ATTN_SCALE = (QK_NOPE + QK_ROPE) ** -0.5
Q_SCALE = ATTN_SCALE * math.log2(math.e)
ATTN_WIDTH = N_HEADS * V_DIM
CONV_DIM = 512
CONV_W = 3
N_GROUPS = 4
EXPERTS_PER_GROUP = 8
N_EXPERTS = N_GROUPS * EXPERTS_PER_GROUP
GROUP_SHIFT = EXPERTS_PER_GROUP.bit_length() - 1
assert 1 << GROUP_SHIFT == EXPERTS_PER_GROUP
D_FF = 256
EPS = 1e-6
NEG_INF = -1e30

LANES = 128
SUBLANES = 8
TOP_K = 2
EXPERT_TILE = 512
KEY_WIDTH = 2 * LANES

F32 = jnp.float32
BF16 = jnp.bfloat16
VMEM_LIMIT = 56 * 1024 * 1024

ROPE_STEPS = 3
MIX_ROWS = 1024
MIX_SUB_BLOCKS = 2
ATTN_BLOCK = 512
ATTN_CHUNKS = 16
PLAN_CHUNK = 512
RANK_UNROLL = 4
DISPATCH_TOKENS = 1024
DISPATCH_PARTS = 1
COMBINE_TOKENS = 512
COMBINE_GROUP = 32


def _rms(x, g):
    return x * lax.rsqrt(jnp.mean(x * x, axis=-1, keepdims=True) + EPS) * g


def _dot(a, b):
    return jnp.dot(a, b, preferred_element_type=F32)


def _dot_nt(a, b):
    return lax.dot_general(a, b, (((1,), (1,)), ((), ())), preferred_element_type=F32)


def _store_token_rows(ref, value, first_token=0):
    rows = value.shape[0]
    for c in range(D_MODEL // LANES):
        ref[pl.ds(first_token * SUBLANES + c, rows, stride=SUBLANES), :] = value[:, c * LANES:(c + 1) * LANES]


def _load_token_rows(ref, rows, first_token=0):
    return jnp.concatenate([ref[pl.ds(first_token * SUBLANES + c, rows, stride=SUBLANES), :]
                            for c in range(D_MODEL // LANES)], axis=1)


def _rope_table_kernel(pos_ref, invf_ref, out_ref):
    pos = pos_ref[...].astype(F32)
    ang = invf_ref[...] * pos
    c = jnp.cos(ang)
    s = jnp.sin(ang)
    planes = jnp.concatenate([c, c, -s, s, jnp.zeros((LANES - 4 * (QK_ROPE // 2), ang.shape[1]), F32)], axis=0)
    out_ref[...] = planes.T


def _rope_table(pos_row, inv_freq_col, chunk):
    n = pos_row.shape[1]
    return pl.pallas_call(
        _rope_table_kernel,
        grid=(n // chunk,),
        in_specs=[pl.BlockSpec((1, chunk), lambda i: (0, i)),
                  pl.BlockSpec((QK_ROPE // 2, 1), lambda i: (0, 0))],
        out_specs=pl.BlockSpec((chunk, LANES), lambda i: (i, 0)),
        out_shape=jax.ShapeDtypeStruct((n, LANES), F32),
        name="rope_table",
    )(pos_row, inv_freq_col)


def _premix_kernel(x_ref, tab_ref, carry_in_ref, g_pre_ref, w1_ref, gq_ref, gkv_ref, wqn_ref, bduk_ref,
                   wqr_ref, convw_ref, gconv_ref, qp_ref, kc_ref, convn_ref, utail_ref, carry_ref, wq_scr, *,
                   sub_blocks):
    t = pl.program_id(1)
    rows = x_ref.shape[1]

    @pl.when(t == 0)
    def _():
        carry_ref[...] = carry_in_ref[...]

    @pl.when((pl.program_id(0) == 0) & (t == 0))
    def _():
        wabs = jnp.dot(wqn_ref[...], bduk_ref[...], preferred_element_type=F32,
                       precision=lax.Precision.HIGHEST) * Q_SCALE
        wrope = wqr_ref[...] * Q_SCALE
        for h in range(N_HEADS):
            sl = slice(h * LANES, (h + 1) * LANES)
            wq_scr[:, h * KEY_WIDTH:h * KEY_WIDTH + LANES] = wabs[:, sl].astype(BF16)
            wq_scr[:, h * KEY_WIDTH + LANES:(h + 1) * KEY_WIDTH] = wrope[:, sl].astype(BF16)

    rows = rows // sub_blocks
    lane = lax.broadcasted_iota(jnp.int32, (rows, LANES), 1)
    row8 = lax.broadcasted_iota(jnp.int32, (8, CONV_DIM), 0)
    assert CONV_W == 3
    cw = convw_ref[...]
    prev = carry_ref[...]
    for sb in range(sub_blocks):
        r = pl.ds(sb * rows, rows)
        xn = _rms(x_ref[0, r, :], g_pre_ref[...]).astype(BF16)
        z = _dot(xn, w1_ref[...])
        tab = tab_ref[r, :]

        def rope(pair):
            prod = pair * tab
            return jnp.where(lane < QK_ROPE, prod + pltpu.roll(prod, LANES - QK_ROPE, axis=1), 0.0)

        ckvn = _rms(z[:, Q_LORA:Q_LORA + KV_LORA], gkv_ref[...])
        krope = rope(z[:, Q_LORA + KV_LORA:Q_LORA + KV_LORA + LANES])
        kc_ref[0, r, :] = jnp.concatenate([ckvn, krope], axis=1).astype(BF16)

        cqn = _rms(z[:, :Q_LORA], gq_ref[...]).astype(BF16)
        qall = _dot(cqn, wq_scr[...])
        for h in range(N_HEADS):
            qp_ref[0, h, r, 0:LANES] = qall[:, h * KEY_WIDTH:h * KEY_WIDTH + LANES].astype(BF16)
            qp_ref[0, h, r, LANES:KEY_WIDTH] = rope(
                qall[:, h * KEY_WIDTH + LANES:(h + 1) * KEY_WIDTH]).astype(BF16)

        c0 = Q_LORA + KV_LORA + LANES
        gate_b = z[:, c0:c0 + CONV_DIM]
        u = z[:, c0 + CONV_DIM:c0 + 2 * CONV_DIM] * z[:, c0 + 2 * CONV_DIM:c0 + 3 * CONV_DIM]
        r1 = pltpu.roll(u, 1, axis=0)
        r2 = pltpu.roll(u, 2, axis=0)
        p1 = pltpu.roll(prev, 1, axis=0)
        p2 = pltpu.roll(prev, 2, axis=0)
        u1 = jnp.concatenate([jnp.where(row8 < 1, p1, r1[0:8]), r1[8:]], axis=0)
        u2 = jnp.concatenate([jnp.where(row8 < 2, p2, r2[0:8]), r2[8:]], axis=0)
        y = cw[0:1] * u2 + cw[1:2] * u1 + cw[2:3] * u
        convn_ref[0, r, :] = _rms(gate_b * y, gconv_ref[...]).astype(BF16)
        prev = u[rows - 8:rows]
    carry_ref[...] = prev
    utail_ref[...] = prev


def _premix(x3, tab, tab_block0, carry_in, wts, rows, sub_blocks):
    bsz, seq, _ = x3.shape
    nt = seq // rows
    g_pre, w1, gq, gkv, wqn, bduk, wqr, convw, gconv = wts
    const = lambda a: pl.BlockSpec(a.shape, lambda b, t: (0,) * a.ndim)
    return pl.pallas_call(
        functools.partial(_premix_kernel, sub_blocks=sub_blocks),
        grid=(bsz, nt),
        in_specs=[pl.BlockSpec((1, rows, D_MODEL), lambda b, t: (b, t, 0)),
                  pl.BlockSpec((rows, LANES), lambda b, t: (tab_block0 + b * nt + t, 0)),
                  const(carry_in), const(g_pre), const(w1), const(gq), const(gkv), const(wqn), const(bduk),
                  const(wqr), const(convw), const(gconv)],
        out_specs=[pl.BlockSpec((1, N_HEADS, rows, KEY_WIDTH), lambda b, t: (b, 0, t, 0)),
                   pl.BlockSpec((1, rows, KEY_WIDTH), lambda b, t: (b, t, 0)),
                   pl.BlockSpec((1, rows, CONV_DIM), lambda b, t: (b, t, 0)),
                   pl.BlockSpec((8, CONV_DIM), lambda b, t: (0, 0))],
        out_shape=[jax.ShapeDtypeStruct((bsz, N_HEADS, seq, KEY_WIDTH), BF16),
                   jax.ShapeDtypeStruct((bsz, seq, KEY_WIDTH), BF16),
                   jax.ShapeDtypeStruct((bsz, seq, CONV_DIM), BF16),
                   jax.ShapeDtypeStruct((8, CONV_DIM), F32)],
        scratch_shapes=[pltpu.VMEM((8, CONV_DIM), F32), pltpu.VMEM((Q_LORA, N_HEADS * KEY_WIDTH), BF16)],
        compiler_params=pltpu.CompilerParams(dimension_semantics=("arbitrary", "arbitrary"),
                                             vmem_limit_bytes=VMEM_LIMIT),
        name="premix",
    )(x3, tab, carry_in, g_pre, w1, gq, gkv, wqn, bduk, wqr, convw, gconv)


def _attention_kernel(q_ref, qnext_ref, k_ref, km_ref, o_ref, s_scr, smeta_scr, tri_scr, m_scr, l_scr, acc_scr, *,
                      bq, bk, chunks):
    assert bq == bk
    b = pl.program_id(0)
    i = pl.program_id(1)
    assert chunks % N_HEADS == 0
    row_splits = chunks // N_HEADS
    c_rows = bq // row_splits
    km = km_ref[...]

    def chunk_rows(c):
        return c // row_splits, pl.ds((c % row_splits) * c_rows, c_rows)

    def scores(qr, c, kb):
        h, r = chunk_rows(c)
        return _dot_nt(qr[0, h, r, :], kb)

    def values(kb):
        return jnp.concatenate([kb[:, :KV_LORA], jnp.ones((kb.shape[0], LANES), BF16)], axis=1)

    def consume(c, s, v_ones, mask, first=False):
        if mask is not None:
            s = jnp.where(mask, s, NEG_INF)
        m_blk = jnp.broadcast_to(jnp.max(s, axis=1, keepdims=True), (c_rows, LANES))
        m_new = m_blk if first else jnp.maximum(m_scr[c], m_blk)
        p = jnp.exp2(s - jnp.concatenate([m_new] * (s.shape[1] // LANES), axis=1)).astype(BF16)
        pv = _dot(p, v_ones)
        if first:
            l_scr[c] = pv[:, KV_LORA:]
            acc_scr[c] = pv[:, :KV_LORA]
        else:
            alpha = jnp.exp2(m_scr[c] - m_new)
            l_scr[c] = alpha * l_scr[c] + pv[:, KV_LORA:]
            acc_scr[c] = alpha * acc_scr[c] + pv[:, :KV_LORA]
        m_scr[c] = m_new

    def key_block(j):
        return k_ref[0, pl.ds(pl.multiple_of(j * bk, bk), bk), :]

    @pl.when((b == 0) & (i == 0))
    def _():
        for c in range(chunks):
            smeta_scr[c] = scores(q_ref, c, km)
        rows_i = lax.broadcasted_iota(jnp.int32, (bq, bk), 0)
        cols_i = lax.broadcasted_iota(jnp.int32, (bq, bk), 1)
        tri_scr[...] = jnp.where(cols_i <= rows_i, 0.0, NEG_INF)

    n_full = (i * bq) // bk
    kb0 = key_block(0)
    v_meta = values(km)
    col = lax.broadcasted_iota(jnp.int32, (c_rows, BLOCK), 1)
    for c in range(chunks):
        s_scr[c] = scores(q_ref, c, kb0)
        consume(c, smeta_scr[c], v_meta, col >= PAD_FRONT, first=True)

    def full_block(j):
        v_ones = values(key_block(j))
        kb_next = key_block(j + 1)
        for c in range(chunks):
            s = s_scr[c]
            s_scr[c] = scores(q_ref, c, kb_next)
            consume(c, s, v_ones, None)

    def body(j, carry):
        full_block(j)
        return carry

    lax.fori_loop(0, n_full, body, 0)

    v_ones = values(key_block(n_full))
    for c in range(chunks):
        _, r = chunk_rows(c)
        n_keys = (c % row_splits + 1) * c_rows
        s = s_scr[c, :, :n_keys] + tri_scr[r, :n_keys]
        smeta_scr[c] = scores(qnext_ref, c, km)
        consume(c, s, v_ones[:n_keys], None)

    for c in range(chunks):
        h, r = chunk_rows(c)
        o_ref[0, r, h * KV_LORA:(h + 1) * KV_LORA] = (acc_scr[c] / l_scr[c]).astype(BF16)


def _attention(qp, kc, kmeta, bq, bk, chunks):
    bsz, _, seq, _ = qp.shape
    nq = seq // bq
    c_rows = N_HEADS * bq // chunks
    scratch = [pltpu.VMEM((chunks, c_rows, bk), F32), pltpu.VMEM((chunks, c_rows, BLOCK), F32),
               pltpu.VMEM((bq, bk), F32), pltpu.VMEM((chunks, c_rows, LANES), F32),
               pltpu.VMEM((chunks, c_rows, LANES), F32), pltpu.VMEM((chunks, c_rows, KV_LORA), F32)]

    def next_q(b, i):
        wrap = (i + 1) // nq
        return (jnp.minimum(b + wrap, bsz - 1), 0, i + 1 - wrap * nq, 0)

    return pl.pallas_call(
        functools.partial(_attention_kernel, bq=bq, bk=bk, chunks=chunks),
        grid=(bsz, nq),
        in_specs=[pl.BlockSpec((1, N_HEADS, bq, KEY_WIDTH), lambda b, i: (b, 0, i, 0)),
                  pl.BlockSpec((1, N_HEADS, bq, KEY_WIDTH), next_q),
                  pl.BlockSpec((1, seq, KEY_WIDTH), lambda b, i: (b, 0, 0)),
                  pl.BlockSpec((BLOCK, KEY_WIDTH), lambda b, i: (0, 0))],
        out_specs=pl.BlockSpec((1, bq, N_HEADS * KV_LORA), lambda b, i: (b, i, 0)),
        out_shape=jax.ShapeDtypeStruct((bsz, seq, N_HEADS * KV_LORA), BF16),
        scratch_shapes=scratch,
        compiler_params=pltpu.CompilerParams(dimension_semantics=("arbitrary", "arbitrary"),
                                             vmem_limit_bytes=VMEM_LIMIT),
        name="attention",
    )(qp, qp, kc, kmeta)


def _postmix_kernel(x_ref, olat_ref, convn_ref, wuv_ref, gattn_ref, wout_ref, gpost_ref, gffn_ref, wr_ref, br_ref,
                    h1_ref, trow_ref, oh1_ref, oh2_ref, wts_ref, *, sub_blocks):
    rows = x_ref.shape[0] // sub_blocks
    wout = wout_ref[...]
    lane = lax.broadcasted_iota(jnp.int32, (rows, LANES), 1)

    def first_argmax(v):
        mx = jnp.max(v, axis=1, keepdims=True)
        return mx, jnp.min(jnp.where(v == mx, lane, LANES), axis=1, keepdims=True)

    for sb in range(sub_blocks):
        r = pl.ds(sb * rows, rows)
        attn = _dot(olat_ref[r, :], wuv_ref[...])
        attn_n = _rms(attn, gattn_ref[...]).astype(BF16)
        mix = _dot(attn_n, wout[:ATTN_WIDTH]) + _dot(convn_ref[r, :], wout[ATTN_WIDTH:])
        h1 = x_ref[r, :] + _rms(mix, gpost_ref[...])
        h1_ref[r, :] = h1
        tf = _rms(h1, gffn_ref[...])
        t = tf.astype(BF16)
        _store_token_rows(trow_ref, tf, sb * rows)

        logits = _dot(t, wr_ref[...]) + br_ref[...]
        gl = jnp.where(lane < N_GROUPS, logits[:, :LANES], NEG_INF)
        gmax, gidx = first_argmax(gl)
        g_w = 1.0 / jnp.sum(jnp.exp(gl - gmax), axis=1, keepdims=True)
        el = jnp.where((lane >> GROUP_SHIFT) == gidx, logits[:, LANES:], NEG_INF)
        m1, i1 = first_argmax(el)
        m2, i2 = first_argmax(jnp.where(lane == i1, NEG_INF, el))
        e2 = jnp.exp(m2 - m1)
        den = 1.0 + e2
        oh1_ref[r, :] = jnp.where(lane == i1, 1.0, 0.0).astype(BF16)
        oh2_ref[r, :] = jnp.where(lane == i2, 1.0, 0.0).astype(BF16)
        wts_ref[r, :] = jnp.where(lane == 0, g_w / den, jnp.where(lane == 1, g_w * e2 / den, 0.0))


def _postmix(x2, olat2, convn2, wts, rows, sub_blocks):
    n = x2.shape[0]
    wuv, gattn, wout, gpost, gffn, wr, br = wts
    const = lambda a: pl.BlockSpec(a.shape, lambda i: (0,) * a.ndim)
    tile = lambda w: pl.BlockSpec((rows, w), lambda i: (i, 0))
    return pl.pallas_call(
        functools.partial(_postmix_kernel, sub_blocks=sub_blocks),
        grid=(n // rows,),
        in_specs=[tile(D_MODEL), tile(N_HEADS * KV_LORA), tile(CONV_DIM),
                  const(wuv), const(gattn), const(wout), const(gpost), const(gffn), const(wr), const(br)],
        out_specs=[tile(D_MODEL), pl.BlockSpec((rows * SUBLANES, LANES), lambda i: (i, 0)),
                   tile(LANES), tile(LANES), tile(LANES)],
        out_shape=[jax.ShapeDtypeStruct((n, D_MODEL), F32),
                   jax.ShapeDtypeStruct((n * SUBLANES, LANES), F32),
                   jax.ShapeDtypeStruct((n, LANES), BF16),
                   jax.ShapeDtypeStruct((n, LANES), BF16),
                   jax.ShapeDtypeStruct((n, LANES), F32)],
        compiler_params=pltpu.CompilerParams(dimension_semantics=("arbitrary",), vmem_limit_bytes=VMEM_LIMIT),
        name="postmix",
    )(x2, olat2, convn2, wuv, gattn, wout, gpost, gffn, wr, br)


def _silu(a):
    return a * (1.0 / (1.0 + jnp.exp(-a)))


def _plan_kernel(oh1_ref, oh2_ref, pos_ref, te_ref, misc_ref, *, chunk):
    n_tok = oh1_ref.shape[0]
    n_chunks = n_tok // chunk
    lane = lax.broadcasted_iota(jnp.int32, (SUBLANES, LANES), 1)

    def rows_of(ref, c):
        return ref[pl.ds(pl.multiple_of(c * chunk, chunk), chunk), :]

    def count_body(c, acc):
        both = rows_of(oh1_ref, c).astype(F32) + rows_of(oh2_ref, c).astype(F32)
        return acc + jnp.sum(both, axis=0, keepdims=True)

    counts = lax.fori_loop(0, n_chunks, count_body, jnp.zeros((1, LANES), F32))
    counts = jnp.broadcast_to(counts, (SUBLANES, LANES)).astype(jnp.int32)
    tile_shift = EXPERT_TILE.bit_length() - 1
    padded = ((counts + (EXPERT_TILE - 1)) >> tile_shift) << tile_shift
    ends = padded
    shift = 1
    while shift < N_EXPERTS:
        ends = ends + jnp.where(lane >= shift, pltpu.roll(ends, shift, axis=1), 0)
        shift *= 2
    base = (ends - padded)[0:1].astype(F32)

    r_iota = lax.broadcasted_iota(jnp.int32, (chunk, chunk), 0)
    c_iota = lax.broadcasted_iota(jnp.int32, (chunk, chunk), 1)
    earlier = jnp.where(c_iota < r_iota, 1.0, 0.0).astype(BF16)

    def rank_body(cc, run):
        for c in [RANK_UNROLL * cc + u for u in range(RANK_UNROLL)]:
            for slot, ref in enumerate((oh1_ref, oh2_ref)):
                oh = rows_of(ref, c)
                ohf = oh.astype(F32)
                before = _dot(earlier, oh) + run
                posv = jnp.sum(ohf * before, axis=1, keepdims=True)
                pos_row = jnp.broadcast_to(posv, (chunk, LANES)).T[0:1, :]
                pos_ref[pl.ds(slot * n_chunks + c, 1), :] = pos_row.astype(jnp.int32)
                run = run + jnp.sum(ohf, axis=0, keepdims=True)
        return run

    assert n_chunks % RANK_UNROLL == 0
    lax.fori_loop(0, n_chunks // RANK_UNROLL, rank_body, base)

    n_tile_lanes = te_ref.shape[1]
    tile_start = lax.broadcasted_iota(jnp.int32, (SUBLANES, n_tile_lanes), 1) * EXPERT_TILE
    zeros = jnp.zeros((SUBLANES, n_tile_lanes), jnp.int32)
    total = ends[:, N_EXPERTS - 1:N_EXPERTS]
    te, seg_index, prev_end, seg_end = zeros, zeros, zeros, zeros + total
    for e in range(N_EXPERTS):
        end_e = ends[:, e:e + 1]
        done = end_e <= tile_start
        te = te + jnp.where(done, 1, 0)
        seg_index = seg_index + jnp.where(done & (padded[:, e:e + 1] > 0), 1, 0)
        prev_end = jnp.maximum(prev_end, jnp.where(done, end_e, 0))
        seg_end = jnp.minimum(seg_end, jnp.where(done, total, end_e))
    nxt = zeros
    for e in range(N_EXPERTS):
        nxt = nxt + jnp.where(ends[:, e:e + 1] <= seg_end, 1, 0)
    sub_t = lax.broadcasted_iota(jnp.int32, (SUBLANES, n_tile_lanes), 0)
    rows = [jnp.minimum(te, N_EXPERTS - 1), jnp.where(prev_end == tile_start, 1, 0),
            jnp.minimum(nxt, N_EXPERTS - 1), jnp.where(seg_end < total, 1, 0), seg_index & 1]
    info = zeros
    for r, v in enumerate(rows):
        info = jnp.where(sub_t == r, v, info)
    te_ref[...] = info
    sub = lax.broadcasted_iota(jnp.int32, (SUBLANES, LANES), 0)
    used_tiles = ends[:, N_EXPERTS - 1:N_EXPERTS] >> tile_shift
    first_pad = jnp.where(lane < N_EXPERTS, ends - padded + counts, used_tiles)
    misc_ref[...] = jnp.where(sub == 0, first_pad, ends)


def _plan(oh1, oh2, n_tiles, chunk):
    n = oh1.shape[0]
    n_tile_lanes = -(-n_tiles // LANES) * LANES
    whole = lambda shape: pl.BlockSpec(shape, lambda i: (0, 0))
    return pl.pallas_call(
        functools.partial(_plan_kernel, chunk=chunk),
        grid=(1,),
        in_specs=[whole((n, LANES)), whole((n, LANES))],
        out_specs=[whole((TOP_K * n // chunk, chunk)), whole((SUBLANES, n_tile_lanes)), whole((SUBLANES, LANES))],
        out_shape=[jax.ShapeDtypeStruct((TOP_K * n // chunk, chunk), jnp.int32),
                   jax.ShapeDtypeStruct((SUBLANES, n_tile_lanes), jnp.int32),
                   jax.ShapeDtypeStruct((SUBLANES, LANES), jnp.int32)],
        compiler_params=pltpu.CompilerParams(dimension_semantics=("arbitrary",), vmem_limit_bytes=VMEM_LIMIT),
        name="moe_plan",
    )(oh1, oh2)


def _row_slice(ref, row, n_rows=1):
    start = row * SUBLANES if isinstance(row, int) else pl.multiple_of(row * SUBLANES, SUBLANES)
    return ref.at[pl.ds(start, n_rows * SUBLANES)]


def _dispatch_kernel(pad_ref, end_ref, pos_ref, trow_ref, wsg_ref, wsu_ref, wsd_ref, xs_hbm, shared_ref, zero_buf,
                     zero_sem, row_sem, *, tokens, parts):
    i = pl.program_id(0)

    def zero_fill(phase):
        def run(first_row, n_rows):
            copy = pltpu.make_async_copy(_row_slice(zero_buf, 0, n_rows), _row_slice(xs_hbm, first_row, n_rows),
                                         zero_sem)
            copy.start() if phase == "start" else copy.wait()

        for e in range(N_EXPERTS):
            row = pad_ref[e]
            n_pad = end_ref[e] - row
            bit = EXPERT_TILE // 2
            while bit >= 1:
                @pl.when((n_pad & bit) != 0)
                def _():
                    run(row, bit)
                row = row + (n_pad & bit)
                bit //= 2

        def tail(t, carry):
            run(t * EXPERT_TILE, EXPERT_TILE)
            return carry

        lax.fori_loop(pad_ref[N_EXPERTS], xs_hbm.shape[0] // (EXPERT_TILE * SUBLANES), tail, 0)

    @pl.when(i == 0)
    def _():
        zero_buf[...] = jnp.zeros(zero_buf.shape, F32)
        zero_fill("start")

    unroll = 8

    def issue(g, carry):
        for u in range(unroll):
            n = g * unroll + u
            src = _row_slice(trow_ref, n)
            for k in range(TOP_K):
                pltpu.make_async_copy(src, _row_slice(xs_hbm, pos_ref[k, n]), row_sem).start(priority=(u + k) % 2)
        return carry

    part = tokens // parts
    for p in range(parts):
        lax.fori_loop(p * part // unroll, (p + 1) * part // unroll, issue, 0)
        t = _load_token_rows(trow_ref, part, p * part).astype(BF16)
        hid = _silu(_dot(t, wsg_ref[...])) * _dot(t, wsu_ref[...])
        shared_ref[p * part:(p + 1) * part, :] = _dot(hid.astype(BF16), wsd_ref[...]).astype(shared_ref.dtype)

    for k in range(TOP_K):
        pltpu.make_async_copy(trow_ref, _row_slice(xs_hbm, 0, tokens), row_sem).wait()

    @pl.when(i + 1 == pl.num_programs(0))
    def _():
        zero_fill("wait")


def _dispatch(first_pad, seg_end, pos, trow, shared_w, sorted_rows, tokens):
    n = pos.shape[1]
    const = lambda a: pl.BlockSpec(a.shape, lambda i: (0,) * a.ndim)
    return pl.pallas_call(
        functools.partial(_dispatch_kernel, tokens=tokens, parts=DISPATCH_PARTS),
        grid=(n // tokens,),
        in_specs=[pl.BlockSpec(memory_space=pltpu.SMEM), pl.BlockSpec(memory_space=pltpu.SMEM),
                  pl.BlockSpec((TOP_K, tokens), lambda i: (0, i), memory_space=pltpu.SMEM),
                  pl.BlockSpec((tokens * SUBLANES, LANES), lambda i: (i, 0))] + [const(w) for w in shared_w],
        out_specs=[pl.BlockSpec(memory_space=pl.ANY), pl.BlockSpec((tokens, D_MODEL), lambda i: (i, 0))],
        out_shape=[jax.ShapeDtypeStruct((sorted_rows * SUBLANES, LANES), F32),
                   jax.ShapeDtypeStruct((n, D_MODEL), BF16)],
        scratch_shapes=[pltpu.VMEM((EXPERT_TILE * SUBLANES, LANES), F32), pltpu.SemaphoreType.DMA,
                        pltpu.SemaphoreType.DMA],
        compiler_params=pltpu.CompilerParams(dimension_semantics=("arbitrary",), vmem_limit_bytes=VMEM_LIMIT),
        name="moe_dispatch",
    )(first_pad, seg_end, pos, trow, *shared_w)


def _experts_kernel(te_ref, first_ref, next_ref, has_next_ref, slot_ref, used_ref, xs_ref, wg_hbm, wu_hbm, wd_hbm,
                    ys_ref, wg_buf, wu_buf, wd_buf, sems):
    t = pl.program_id(0)
    active = t < used_ref[0]

    def weights(e, slot, action):
        for j, (src, buf) in enumerate(((wg_hbm, wg_buf), (wu_hbm, wu_buf), (wd_hbm, wd_buf))):
            copy = pltpu.make_async_copy(src.at[e], buf.at[slot], sems.at[slot, j])
            copy.start() if action == "start" else copy.wait()

    @pl.when(t == 0)
    def _():
        weights(te_ref[0], slot_ref[0], "start")

    @pl.when(active & (first_ref[t] == 1))
    def _():
        @pl.when(has_next_ref[t] == 1)
        def _():
            weights(next_ref[t], 1 - slot_ref[t], "start")
        weights(te_ref[t], slot_ref[t], "wait")

    @pl.when(active)
    def _():
        slot = slot_ref[t]
        x = _load_token_rows(xs_ref, EXPERT_TILE).astype(BF16)
        hid = _silu(_dot(x, wg_buf[slot].astype(BF16))) * _dot(x, wu_buf[slot].astype(BF16))
        _store_token_rows(ys_ref, _dot(hid.astype(BF16), wd_buf[slot].astype(BF16)))

    @pl.when(jnp.logical_not(active))
    def _():
        ys_ref[...] = jnp.zeros(ys_ref.shape, F32)


def _experts(tile_info, used, xs, wg, wu, wd, n_tiles):
    blk = EXPERT_TILE * SUBLANES
    n_prefetch = len(tile_info) + 1
    tile_map = lambda t, te, first, nxt, has_next, slot, used: (t, 0)
    in_map = lambda t, te, first, nxt, has_next, slot, used: (jnp.minimum(t, used[0] - 1), 0)
    return pl.pallas_call(
        _experts_kernel,
        grid_spec=pltpu.PrefetchScalarGridSpec(
            num_scalar_prefetch=n_prefetch,
            grid=(n_tiles,),
            in_specs=[pl.BlockSpec((blk, LANES), in_map), pl.BlockSpec(memory_space=pl.ANY),
                      pl.BlockSpec(memory_space=pl.ANY), pl.BlockSpec(memory_space=pl.ANY)],
            out_specs=pl.BlockSpec((blk, LANES), tile_map),
            scratch_shapes=[pltpu.VMEM((2, D_MODEL, D_FF), F32), pltpu.VMEM((2, D_MODEL, D_FF), F32),
                            pltpu.VMEM((2, D_FF, D_MODEL), F32), pltpu.SemaphoreType.DMA((2, 3))]),
        out_shape=jax.ShapeDtypeStruct(xs.shape, F32),
        compiler_params=pltpu.CompilerParams(dimension_semantics=("arbitrary",), vmem_limit_bytes=VMEM_LIMIT),
        name="moe_experts",
    )(*tile_info, used, xs, wg, wu, wd)


def _combine_kernel(pos_ref, pos_next_ref, shared_ref, h1_ref, wts_ref, ys_hbm, gout_ref, out_ref, ybuf, sems, *,
                    tokens):
    i = pl.program_id(0)
    slot = lax.rem(i, 2)
    other = 1 - slot
    group = COMBINE_GROUP

    def start_gathers(p_ref, dst_slot, first):
        for u in range(group):
            n = first + u
            for k in range(TOP_K):
                pltpu.make_async_copy(_row_slice(ys_hbm, p_ref[k, n]), _row_slice(ybuf.at[dst_slot].at[k], n),
                                      sems.at[dst_slot]).start(priority=(u + k) % 2)

    def wait_slot(s):
        for k in range(TOP_K):
            pltpu.make_async_copy(_row_slice(ys_hbm, 0, tokens), ybuf.at[s].at[k], sems.at[s]).wait()

    @pl.when(i == 0)
    def _():
        def first_step(g, carry):
            start_gathers(pos_ref, slot, g * group)
            return carry
        lax.fori_loop(0, tokens // group, first_step, 0)

    wait_slot(slot)

    def trip(g, carry):
        first = pl.multiple_of(g * group, group)
        rows = pl.ds(first, group)
        wts = wts_ref[rows, :]
        routed = (wts[:, 0:1] * _load_token_rows(ybuf.at[slot].at[0], group, first)
                  + wts[:, 1:2] * _load_token_rows(ybuf.at[slot].at[1], group, first))
        out_ref[rows, :] = h1_ref[rows, :] + _rms(routed + shared_ref[rows, :].astype(F32), gout_ref[...])
        start_gathers(pos_next_ref, other, first)
        return carry

    lax.fori_loop(0, tokens // group, trip, 0)

    @pl.when(i + 1 == pl.num_programs(0))
    def _():
        wait_slot(other)


def _combine(pos, shared, h1, wts, ys, gout, tokens):
    n = h1.shape[0]
    const = lambda a: pl.BlockSpec(a.shape, lambda i: (0,) * a.ndim)
    tile = lambda w: pl.BlockSpec((tokens, w), lambda i: (i, 0))
    steps = n // tokens
    return pl.pallas_call(
        functools.partial(_combine_kernel, tokens=tokens),
        grid=(steps,),
        in_specs=[pl.BlockSpec((TOP_K, tokens), lambda i: (0, i), memory_space=pltpu.SMEM),
                  pl.BlockSpec((TOP_K, tokens), lambda i: (0, jnp.minimum(i + 1, steps - 1)),
                               memory_space=pltpu.SMEM),
                  tile(D_MODEL), tile(D_MODEL), tile(LANES), pl.BlockSpec(memory_space=pl.ANY), const(gout)],
        out_specs=tile(D_MODEL),
        out_shape=jax.ShapeDtypeStruct((n, D_MODEL), F32),
        scratch_shapes=[pltpu.VMEM((2, TOP_K, tokens * SUBLANES, LANES), F32), pltpu.SemaphoreType.DMA((2,))],
        compiler_params=pltpu.CompilerParams(dimension_semantics=("arbitrary",), vmem_limit_bytes=VMEM_LIMIT),
        name="moe_combine",
    )(pos, pos, shared, h1, wts, ys, gout)


def _swap_halves(w):
    half = w.shape[-1] // 2
    return jnp.concatenate([w[..., half:], w[..., :half]], axis=-1)


def _block_diag(blocks):
    h, r, c = blocks.shape
    eye = jnp.eye(h, dtype=blocks.dtype)
    return jnp.einsum('hrc,hg->hrgc', blocks, eye).reshape(h * r, h * c)


def kernel(x, positions, meta_tokens, pre_mix_norm, w_in, q_norm, w_uq, kv_norm, w_ukv, conv_w, attn_out_norm,
           conv_out_norm, w_out, post_mix_norm, pre_ffn_norm, w_group_router, b_group_router, w_expert_router,
           b_expert_router, w_gate, w_up, w_down, w_sh_gate, w_sh_up, w_sh_down, post_ffn_norm):
    bsz, seq, d = x.shape
    n = bsz * seq
    row = lambda v: v.reshape(1, -1).astype(F32)
    assert d == D_MODEL and w_in.shape[0] == 1, "one layer of width 1024 is supported"
    assert seq % MIX_ROWS == 0 and seq % ATTN_BLOCK == 0 and n % DISPATCH_TOKENS == 0
    assert (n + BLOCK) % (ROPE_STEPS * LANES) == 0

    wi = w_in[0]
    o_kv, o_pe, o_b = Q_LORA, Q_LORA + KV_LORA, Q_LORA + KV_LORA + QK_ROPE
    k_pe = wi[:, o_pe:o_b]
    w1 = jnp.concatenate([wi[:, :o_pe], k_pe, _swap_halves(k_pe), jnp.zeros((d, LANES - 2 * QK_ROPE), F32),
                          wi[:, o_b:]], axis=1).astype(BF16)
    wq = w_uq[0].reshape(Q_LORA, N_HEADS, QK_NOPE + QK_ROPE)
    wqn = wq[:, :, :QK_NOPE].reshape(Q_LORA, N_HEADS * QK_NOPE).astype(F32)
    wq_rope = wq[:, :, QK_NOPE:]
    wqr = jnp.concatenate([wq_rope, _swap_halves(wq_rope),
                           jnp.zeros((Q_LORA, N_HEADS, LANES - 2 * QK_ROPE), F32)], axis=2)
    wqr = wqr.reshape(Q_LORA, N_HEADS * LANES).astype(F32)
    wkv = w_ukv[0].reshape(KV_LORA, N_HEADS, QK_NOPE + V_DIM)
    bduk = _block_diag(jnp.transpose(wkv[:, :, :QK_NOPE], (1, 2, 0))).astype(F32)
    wuv = _block_diag(jnp.transpose(wkv[:, :, QK_NOPE:], (1, 0, 2))).astype(BF16)
    premix_w = (row(pre_mix_norm[0]), w1, row(q_norm[0]), row(kv_norm[0]), wqn, bduk, wqr,
                conv_w[0].astype(F32), row(conv_out_norm[0]))
    wr = jnp.concatenate([w_group_router[0], jnp.zeros((d, LANES - N_GROUPS), F32),
                          w_expert_router[0], jnp.zeros((d, LANES - N_EXPERTS), F32)], axis=1).astype(BF16)
    br = jnp.concatenate([b_group_router[0], jnp.zeros((LANES - N_GROUPS,), F32),
                          b_expert_router[0], jnp.zeros((LANES - N_EXPERTS,), F32)]).reshape(1, -1).astype(F32)
    postmix_w = (wuv, row(attn_out_norm[0]), w_out[0].astype(BF16), row(post_mix_norm[0]), row(pre_ffn_norm[0]),
                 wr, br)
    expert_w = (w_gate[0], w_up[0], w_down[0])
    shared_w = (w_sh_gate[0].astype(BF16), w_sh_up[0].astype(BF16), w_sh_down[0].astype(BF16))

    pos_all = jnp.concatenate([positions.astype(jnp.int32).reshape(-1) + N_META,
                               jnp.zeros((PAD_FRONT,), jnp.int32), jnp.arange(N_META, dtype=jnp.int32)])
    inv_freq = 1.0 / (ROPE_THETA ** (jnp.arange(0, QK_ROPE, 2, dtype=F32) / QK_ROPE))
    n_tab = n + BLOCK
    tab = _rope_table(pos_all.reshape(1, n_tab), inv_freq.reshape(-1, 1), n_tab // ROPE_STEPS)

    meta_blk = jnp.pad(meta_tokens.astype(F32), ((PAD_FRONT, 0), (0, 0)))[None]
    zero_carry = jnp.zeros((8, CONV_DIM), F32)
    _, kmeta, _, meta_tail = _premix(meta_blk, tab, n // BLOCK, zero_carry, premix_w, BLOCK, 1)
    qp, kc, convn, _ = _premix(x, tab, 0, meta_tail, premix_w, MIX_ROWS, MIX_SUB_BLOCKS)

    olat = _attention(qp, kc, kmeta[0], ATTN_BLOCK, ATTN_BLOCK, ATTN_CHUNKS)

    h1, trow, oh1, oh2, wts = _postmix(x.reshape(n, d), olat.reshape(n, -1), convn.reshape(n, -1), postmix_w,
                                       MIX_ROWS, MIX_SUB_BLOCKS)

    sorted_rows = TOP_K * n + N_EXPERTS * EXPERT_TILE
    n_tiles = sorted_rows // EXPERT_TILE
    pos2, te8, misc = _plan(oh1, oh2, n_tiles, PLAN_CHUNK)
    pos = pos2.reshape(TOP_K, n)
    xs, shared = _dispatch(misc[0, :N_EXPERTS + 1], misc[1, :N_EXPERTS], pos, trow, shared_w, sorted_rows,
                           DISPATCH_TOKENS)
    tile_info = [te8[r, :n_tiles] for r in range(5)]
    ys = _experts(tile_info, misc[0, N_EXPERTS:N_EXPERTS + 1], xs, *expert_w, n_tiles)
    out = _combine(pos, shared, h1, wts, ys, row(post_ffn_norm[0]), COMBINE_TOKENS)
    return out.reshape(bsz, seq, d)
```

```python
import functools
import math

import jax
import jax.numpy as jnp
from jax import lax
from jax.experimental import pallas as pl
from jax.experimental.pallas import tpu as pltpu

D_MODEL = 1024
N_META = 16
BLOCK = 128
PAD_FRONT = BLOCK - N_META
N_HEADS = 8
QK_NOPE = 64
QK_ROPE = 32
V_DIM = 64
Q_LORA = 256
KV_LORA = 128
ROPE_THETA = 10000.0
ATTN_SCALE = (QK_NOPE + QK_ROPE) ** -0.5
Q_SCALE = ATTN_SCALE * math.log2(math.e)
ATTN_WIDTH = N_HEADS * V_DIM
CONV_DIM = 512
CONV_W = 3
N_GROUPS = 4
EXPERTS_PER_GROUP = 8
N_EXPERTS = N_GROUPS * EXPERTS_PER_GROUP
GROUP_SHIFT = EXPERTS_PER_GROUP.bit_length() - 1
assert 1 << GROUP_SHIFT == EXPERTS_PER_GROUP
D_FF = 256
EPS = 1e-6
NEG_INF = -1e30

LANES = 128
SUBLANES = 8
TOP_K = 2
EXPERT_TILE = 512
KEY_WIDTH = 2 * LANES

F32 = jnp.float32
BF16 = jnp.bfloat16
VMEM_LIMIT = 56 * 1024 * 1024

ROPE_STEPS = 3
MIX_ROWS = 1024
MIX_SUB_BLOCKS = 2
ATTN_BLOCK = 512
ATTN_CHUNKS = 16
PLAN_CHUNK = 512
RANK_UNROLL = 4
DISPATCH_TOKENS = 1024
DISPATCH_PARTS = 1
X_RING = 3
COMBINE_TOKENS = 512
COMBINE_GROUP = 32


def _rms(x, g):
    return x * lax.rsqrt(jnp.mean(x * x, axis=-1, keepdims=True) + EPS) * g


def _dot(a, b):
    return jnp.dot(a, b, preferred_element_type=F32)


def _dot_nt(a, b):
    return lax.dot_general(a, b, (((1,), (1,)), ((), ())), preferred_element_type=F32)


def _store_token_rows(ref, value, first_token=0):
    rows = value.shape[0]
    for c in range(D_MODEL // LANES):
        ref[pl.ds(first_token * SUBLANES + c, rows, stride=SUBLANES), :] = value[:, c * LANES:(c + 1) * LANES]


def _load_token_rows(ref, rows, first_token=0):
    return jnp.concatenate([ref[pl.ds(first_token * SUBLANES + c, rows, stride=SUBLANES), :]
                            for c in range(D_MODEL // LANES)], axis=1)


def _rope_table_kernel(pos_ref, invf_ref, out_ref):
    pos = pos_ref[...].astype(F32)
    ang = invf_ref[...] * pos
    c = jnp.cos(ang)
    s = jnp.sin(ang)
    planes = jnp.concatenate([c, c, -s, s, jnp.zeros((LANES - 4 * (QK_ROPE // 2), ang.shape[1]), F32)], axis=0)
    out_ref[...] = planes.T


def _rope_table(pos_row, inv_freq_col, chunk):
    n = pos_row.shape[1]
    return pl.pallas_call(
        _rope_table_kernel,
        grid=(n // chunk,),
        in_specs=[pl.BlockSpec((1, chunk), lambda i: (0, i)),
                  pl.BlockSpec((QK_ROPE // 2, 1), lambda i: (0, 0))],
        out_specs=pl.BlockSpec((chunk, LANES), lambda i: (i, 0)),
        out_shape=jax.ShapeDtypeStruct((n, LANES), F32),
        name="rope_table",
    )(pos_row, inv_freq_col)


def _premix_kernel(x_ref, tab_ref, carry_in_ref, g_pre_ref, w1_ref, gq_ref, gkv_ref, wqn_ref, bduk_ref,
                   wqr_ref, convw_ref, gconv_ref, qp_ref, kc_ref, convn_ref, utail_ref, carry_ref, wq_scr, *,
                   sub_blocks):
    t = pl.program_id(1)
    rows = x_ref.shape[1]

    @pl.when(t == 0)
    def _():
        carry_ref[...] = carry_in_ref[...]

    @pl.when((pl.program_id(0) == 0) & (t == 0))
    def _():
        wabs = jnp.dot(wqn_ref[...], bduk_ref[...], preferred_element_type=F32,
                       precision=lax.Precision.HIGHEST) * Q_SCALE
        wrope = wqr_ref[...] * Q_SCALE
        for h in range(N_HEADS):
            sl = slice(h * LANES, (h + 1) * LANES)
            wq_scr[:, h * KEY_WIDTH:h * KEY_WIDTH + LANES] = wabs[:, sl].astype(BF16)
            wq_scr[:, h * KEY_WIDTH + LANES:(h + 1) * KEY_WIDTH] = wrope[:, sl].astype(BF16)

    rows = rows // sub_blocks
    lane = lax.broadcasted_iota(jnp.int32, (rows, LANES), 1)
    row8 = lax.broadcasted_iota(jnp.int32, (8, CONV_DIM), 0)
    assert CONV_W == 3
    cw = convw_ref[...]
    prev = carry_ref[...]
    for sb in range(sub_blocks):
        r = pl.ds(sb * rows, rows)
        xn = _rms(x_ref[0, r, :], g_pre_ref[...]).astype(BF16)
        z = _dot(xn, w1_ref[...])
        tab = tab_ref[r, :]

        def rope(pair):
            prod = pair * tab
            return jnp.where(lane < QK_ROPE, prod + pltpu.roll(prod, LANES - QK_ROPE, axis=1), 0.0)

        ckvn = _rms(z[:, Q_LORA:Q_LORA + KV_LORA], gkv_ref[...])
        krope = rope(z[:, Q_LORA + KV_LORA:Q_LORA + KV_LORA + LANES])
        kc_ref[0, r, :] = jnp.concatenate([ckvn, krope], axis=1).astype(BF16)

        cqn = _rms(z[:, :Q_LORA], gq_ref[...]).astype(BF16)
        qall = _dot(cqn, wq_scr[...])
        for h in range(N_HEADS):
            qp_ref[0, h, r, 0:LANES] = qall[:, h * KEY_WIDTH:h * KEY_WIDTH + LANES].astype(BF16)
            qp_ref[0, h, r, LANES:KEY_WIDTH] = rope(
                qall[:, h * KEY_WIDTH + LANES:(h + 1) * KEY_WIDTH]).astype(BF16)

        c0 = Q_LORA + KV_LORA + LANES
        gate_b = z[:, c0:c0 + CONV_DIM]
        u = z[:, c0 + CONV_DIM:c0 + 2 * CONV_DIM] * z[:, c0 + 2 * CONV_DIM:c0 + 3 * CONV_DIM]
        r1 = pltpu.roll(u, 1, axis=0)
        r2 = pltpu.roll(u, 2, axis=0)
        p1 = pltpu.roll(prev, 1, axis=0)
        p2 = pltpu.roll(prev, 2, axis=0)
        u1 = jnp.concatenate([jnp.where(row8 < 1, p1, r1[0:8]), r1[8:]], axis=0)
        u2 = jnp.concatenate([jnp.where(row8 < 2, p2, r2[0:8]), r2[8:]], axis=0)
        y = cw[0:1] * u2 + cw[1:2] * u1 + cw[2:3] * u
        convn_ref[0, r, :] = _rms(gate_b * y, gconv_ref[...]).astype(BF16)
        prev = u[rows - 8:rows]
    carry_ref[...] = prev
    utail_ref[...] = prev


def _premix(x3, tab, tab_block0, carry_in, wts, rows, sub_blocks):
    bsz, seq, _ = x3.shape
    nt = seq // rows
    g_pre, w1, gq, gkv, wqn, bduk, wqr, convw, gconv = wts
    const = lambda a: pl.BlockSpec(a.shape, lambda b, t: (0,) * a.ndim)
    return pl.pallas_call(
        functools.partial(_premix_kernel, sub_blocks=sub_blocks),
        grid=(bsz, nt),
        in_specs=[pl.BlockSpec((1, rows, D_MODEL), lambda b, t: (b, t, 0)),
                  pl.BlockSpec((rows, LANES), lambda b, t: (tab_block0 + b * nt + t, 0)),
                  const(carry_in), const(g_pre), const(w1), const(gq), const(gkv), const(wqn), const(bduk),
                  const(wqr), const(convw), const(gconv)],
        out_specs=[pl.BlockSpec((1, N_HEADS, rows, KEY_WIDTH), lambda b, t: (b, 0, t, 0)),
                   pl.BlockSpec((1, rows, KEY_WIDTH), lambda b, t: (b, t, 0)),
                   pl.BlockSpec((1, rows, CONV_DIM), lambda b, t: (b, t, 0)),
                   pl.BlockSpec((8, CONV_DIM), lambda b, t: (0, 0))],
        out_shape=[jax.ShapeDtypeStruct((bsz, N_HEADS, seq, KEY_WIDTH), BF16),
                   jax.ShapeDtypeStruct((bsz, seq, KEY_WIDTH), BF16),
                   jax.ShapeDtypeStruct((bsz, seq, CONV_DIM), BF16),
                   jax.ShapeDtypeStruct((8, CONV_DIM), F32)],
        scratch_shapes=[pltpu.VMEM((8, CONV_DIM), F32), pltpu.VMEM((Q_LORA, N_HEADS * KEY_WIDTH), BF16)],
        compiler_params=pltpu.CompilerParams(dimension_semantics=("arbitrary", "arbitrary"),
                                             vmem_limit_bytes=VMEM_LIMIT),
        name="premix",
    )(x3, tab, carry_in, g_pre, w1, gq, gkv, wqn, bduk, wqr, convw, gconv)


def _attention_kernel(q_ref, qnext_ref, k_ref, km_ref, o_ref, s_scr, smeta_scr, tri_scr, m_scr, l_scr, acc_scr, *,
                      bq, bk, chunks):
    assert bq == bk
    b = pl.program_id(0)
    i = pl.program_id(1)
    assert chunks % N_HEADS == 0
    row_splits = chunks // N_HEADS
    c_rows = bq // row_splits
    km = km_ref[...]

    def chunk_rows(c):
        return c // row_splits, pl.ds((c % row_splits) * c_rows, c_rows)

    def scores(qr, c, kb):
        h, r = chunk_rows(c)
        return _dot_nt(qr[0, h, r, :], kb)

    def values(kb):
        return jnp.concatenate([kb[:, :KV_LORA], jnp.ones((kb.shape[0], LANES), BF16)], axis=1)

    def consume(c, s, v_ones, mask, first=False):
        if mask is not None:
            s = jnp.where(mask, s, NEG_INF)
        m_blk = jnp.broadcast_to(jnp.max(s, axis=1, keepdims=True), (c_rows, LANES))
        m_new = m_blk if first else jnp.maximum(m_scr[c], m_blk)
        p = jnp.exp2(s - jnp.concatenate([m_new] * (s.shape[1] // LANES), axis=1)).astype(BF16)
        pv = _dot(p, v_ones)
        if first:
            l_scr[c] = pv[:, KV_LORA:]
            acc_scr[c] = pv[:, :KV_LORA]
        else:
            alpha = jnp.exp2(m_scr[c] - m_new)
            l_scr[c] = alpha * l_scr[c] + pv[:, KV_LORA:]
            acc_scr[c] = alpha * acc_scr[c] + pv[:, :KV_LORA]
        m_scr[c] = m_new

    def key_block(j):
        return k_ref[0, pl.ds(pl.multiple_of(j * bk, bk), bk), :]

    @pl.when((b == 0) & (i == 0))
    def _():
        for c in range(chunks):
            smeta_scr[c] = scores(q_ref, c, km)
        rows_i = lax.broadcasted_iota(jnp.int32, (bq, bk), 0)
        cols_i = lax.broadcasted_iota(jnp.int32, (bq, bk), 1)
        tri_scr[...] = jnp.where(cols_i <= rows_i, 0.0, NEG_INF)

    n_full = (i * bq) // bk
    kb0 = key_block(0)
    v_meta = values(km)
    col = lax.broadcasted_iota(jnp.int32, (c_rows, BLOCK), 1)
    for c in range(chunks):
        s_scr[c] = scores(q_ref, c, kb0)
        consume(c, smeta_scr[c], v_meta, col >= PAD_FRONT, first=True)

    def full_block(j):
        v_ones = values(key_block(j))
        kb_next = key_block(j + 1)
        for c in range(chunks):
            s = s_scr[c]
            s_scr[c] = scores(q_ref, c, kb_next)
            consume(c, s, v_ones, None)

    def body(j, carry):
        full_block(j)
        return carry

    lax.fori_loop(0, n_full, body, 0)

    v_ones = values(key_block(n_full))
    for c in range(chunks):
        _, r = chunk_rows(c)
        n_keys = (c % row_splits + 1) * c_rows
        s = s_scr[c, :, :n_keys] + tri_scr[r, :n_keys]
        smeta_scr[c] = scores(qnext_ref, c, km)
        consume(c, s, v_ones[:n_keys], None)

    for c in range(chunks):
        h, r = chunk_rows(c)
        o_ref[0, r, h * KV_LORA:(h + 1) * KV_LORA] = (acc_scr[c] / l_scr[c]).astype(BF16)


def _attention(qp, kc, kmeta, bq, bk, chunks):
    bsz, _, seq, _ = qp.shape
    nq = seq // bq
    c_rows = N_HEADS * bq // chunks
    scratch = [pltpu.VMEM((chunks, c_rows, bk), F32), pltpu.VMEM((chunks, c_rows, BLOCK), F32),
               pltpu.VMEM((bq, bk), F32), pltpu.VMEM((chunks, c_rows, LANES), F32),
               pltpu.VMEM((chunks, c_rows, LANES), F32), pltpu.VMEM((chunks, c_rows, KV_LORA), F32)]

    def next_q(b, i):
        wrap = (i + 1) // nq
        return (jnp.minimum(b + wrap, bsz - 1), 0, i + 1 - wrap * nq, 0)

    return pl.pallas_call(
        functools.partial(_attention_kernel, bq=bq, bk=bk, chunks=chunks),
        grid=(bsz, nq),
        in_specs=[pl.BlockSpec((1, N_HEADS, bq, KEY_WIDTH), lambda b, i: (b, 0, i, 0)),
                  pl.BlockSpec((1, N_HEADS, bq, KEY_WIDTH), next_q),
                  pl.BlockSpec((1, seq, KEY_WIDTH), lambda b, i: (b, 0, 0)),
                  pl.BlockSpec((BLOCK, KEY_WIDTH), lambda b, i: (0, 0))],
        out_specs=pl.BlockSpec((1, bq, N_HEADS * KV_LORA), lambda b, i: (b, i, 0)),
        out_shape=jax.ShapeDtypeStruct((bsz, seq, N_HEADS * KV_LORA), BF16),
        scratch_shapes=scratch,
        compiler_params=pltpu.CompilerParams(dimension_semantics=("arbitrary", "arbitrary"),
                                             vmem_limit_bytes=VMEM_LIMIT),
        name="attention",
    )(qp, qp, kc, kmeta)


def _postmix_kernel(x_ref, olat_ref, convn_ref, wuv_ref, gattn_ref, wout_ref, gpost_ref, gffn_ref, wr_ref, br_ref,
                    h1_ref, trow_ref, oh1_ref, oh2_ref, wts_ref, *, sub_blocks):
    rows = x_ref.shape[0] // sub_blocks
    wout = wout_ref[...]
    lane = lax.broadcasted_iota(jnp.int32, (rows, LANES), 1)

    def first_argmax(v):
        mx = jnp.max(v, axis=1, keepdims=True)
        return mx, jnp.min(jnp.where(v == mx, lane, LANES), axis=1, keepdims=True)

    for sb in range(sub_blocks):
        r = pl.ds(sb * rows, rows)
        attn = _dot(olat_ref[r, :], wuv_ref[...])
        attn_n = _rms(attn, gattn_ref[...]).astype(BF16)
        mix = _dot(attn_n, wout[:ATTN_WIDTH]) + _dot(convn_ref[r, :], wout[ATTN_WIDTH:])
        h1 = x_ref[r, :] + _rms(mix, gpost_ref[...])
        h1_ref[r, :] = h1
        tf = _rms(h1, gffn_ref[...])
        t = tf.astype(BF16)
        _store_token_rows(trow_ref, tf, sb * rows)

        logits = _dot(t, wr_ref[...]) + br_ref[...]
        gl = jnp.where(lane < N_GROUPS, logits[:, :LANES], NEG_INF)
        gmax, gidx = first_argmax(gl)
        g_w = 1.0 / jnp.sum(jnp.exp(gl - gmax), axis=1, keepdims=True)
        el = jnp.where((lane >> GROUP_SHIFT) == gidx, logits[:, LANES:], NEG_INF)
        m1, i1 = first_argmax(el)
        m2, i2 = first_argmax(jnp.where(lane == i1, NEG_INF, el))
        e2 = jnp.exp(m2 - m1)
        den = 1.0 + e2
        oh1_ref[r, :] = jnp.where(lane == i1, 1.0, 0.0).astype(BF16)
        oh2_ref[r, :] = jnp.where(lane == i2, 1.0, 0.0).astype(BF16)
        wts_ref[r, :] = jnp.where(lane == 0, g_w / den, jnp.where(lane == 1, g_w * e2 / den, 0.0))


def _postmix(x2, olat2, convn2, wts, rows, sub_blocks):
    n = x2.shape[0]
    wuv, gattn, wout, gpost, gffn, wr, br = wts
    const = lambda a: pl.BlockSpec(a.shape, lambda i: (0,) * a.ndim)
    tile = lambda w: pl.BlockSpec((rows, w), lambda i: (i, 0))
    return pl.pallas_call(
        functools.partial(_postmix_kernel, sub_blocks=sub_blocks),
        grid=(n // rows,),
        in_specs=[tile(D_MODEL), tile(N_HEADS * KV_LORA), tile(CONV_DIM),
                  const(wuv), const(gattn), const(wout), const(gpost), const(gffn), const(wr), const(br)],
        out_specs=[tile(D_MODEL), pl.BlockSpec((rows * SUBLANES, LANES), lambda i: (i, 0)),
                   tile(LANES), tile(LANES), tile(LANES)],
        out_shape=[jax.ShapeDtypeStruct((n, D_MODEL), F32),
                   jax.ShapeDtypeStruct((n * SUBLANES, LANES), F32),
                   jax.ShapeDtypeStruct((n, LANES), BF16),
                   jax.ShapeDtypeStruct((n, LANES), BF16),
                   jax.ShapeDtypeStruct((n, LANES), F32)],
        compiler_params=pltpu.CompilerParams(dimension_semantics=("arbitrary",), vmem_limit_bytes=VMEM_LIMIT),
        name="postmix",
    )(x2, olat2, convn2, wuv, gattn, wout, gpost, gffn, wr, br)


def _silu(a):
    return a * (1.0 / (1.0 + jnp.exp(-a)))


def _plan_kernel(oh1_ref, oh2_ref, pos_ref, te_ref, misc_ref, *, chunk):
    n_tok = oh1_ref.shape[0]
    n_chunks = n_tok // chunk
    lane = lax.broadcasted_iota(jnp.int32, (SUBLANES, LANES), 1)

    def rows_of(ref, c):
        return ref[pl.ds(pl.multiple_of(c * chunk, chunk), chunk), :]

    def count_body(c, acc):
        both = rows_of(oh1_ref, c).astype(F32) + rows_of(oh2_ref, c).astype(F32)
        return acc + jnp.sum(both, axis=0, keepdims=True)

    counts = lax.fori_loop(0, n_chunks, count_body, jnp.zeros((1, LANES), F32))
    counts = jnp.broadcast_to(counts, (SUBLANES, LANES)).astype(jnp.int32)
    tile_shift = EXPERT_TILE.bit_length() - 1
    padded = ((counts + (EXPERT_TILE - 1)) >> tile_shift) << tile_shift
    ends = padded
    shift = 1
    while shift < N_EXPERTS:
        ends = ends + jnp.where(lane >= shift, pltpu.roll(ends, shift, axis=1), 0)
        shift *= 2
    base = (ends - padded)[0:1].astype(F32)

    r_iota = lax.broadcasted_iota(jnp.int32, (chunk, chunk), 0)
    c_iota = lax.broadcasted_iota(jnp.int32, (chunk, chunk), 1)
    earlier = jnp.where(c_iota < r_iota, 1.0, 0.0).astype(BF16)

    def rank_body(cc, run):
        for c in [RANK_UNROLL * cc + u for u in range(RANK_UNROLL)]:
            for slot, ref in enumerate((oh1_ref, oh2_ref)):
                oh = rows_of(ref, c)
                ohf = oh.astype(F32)
                before = _dot(earlier, oh) + run
                posv = jnp.sum(ohf * before, axis=1, keepdims=True)
                pos_row = jnp.broadcast_to(posv, (chunk, LANES)).T[0:1, :]
                pos_ref[pl.ds(slot * n_chunks + c, 1), :] = pos_row.astype(jnp.int32)
                run = run + jnp.sum(ohf, axis=0, keepdims=True)
        return run

    assert n_chunks % RANK_UNROLL == 0
    lax.fori_loop(0, n_chunks // RANK_UNROLL, rank_body, base)

    n_tile_lanes = te_ref.shape[1]
    tile_start = lax.broadcasted_iota(jnp.int32, (SUBLANES, n_tile_lanes), 1) * EXPERT_TILE
    zeros = jnp.zeros((SUBLANES, n_tile_lanes), jnp.int32)
    total = ends[:, N_EXPERTS - 1:N_EXPERTS]
    te, seg_index, prev_end, seg_end = zeros, zeros, zeros, zeros + total
    for e in range(N_EXPERTS):
        end_e = ends[:, e:e + 1]
        done = end_e <= tile_start
        te = te + jnp.where(done, 1, 0)
        seg_index = seg_index + jnp.where(done & (padded[:, e:e + 1] > 0), 1, 0)
        prev_end = jnp.maximum(prev_end, jnp.where(done, end_e, 0))
        seg_end = jnp.minimum(seg_end, jnp.where(done, total, end_e))
    nxt = zeros
    for e in range(N_EXPERTS):
        nxt = nxt + jnp.where(ends[:, e:e + 1] <= seg_end, 1, 0)
    sub_t = lax.broadcasted_iota(jnp.int32, (SUBLANES, n_tile_lanes), 0)
    rows = [jnp.minimum(te, N_EXPERTS - 1), jnp.where(prev_end == tile_start, 1, 0),
            jnp.minimum(nxt, N_EXPERTS - 1), jnp.where(seg_end < total, 1, 0), seg_index & 1]
    info = zeros
    for r, v in enumerate(rows):
        info = jnp.where(sub_t == r, v, info)
    te_ref[...] = info
    sub = lax.broadcasted_iota(jnp.int32, (SUBLANES, LANES), 0)
    used_tiles = ends[:, N_EXPERTS - 1:N_EXPERTS] >> tile_shift
    first_pad = jnp.where(lane < N_EXPERTS, ends - padded + counts, used_tiles)
    misc_ref[...] = jnp.where(sub == 0, first_pad, ends)


def _plan(oh1, oh2, n_tiles, chunk):
    n = oh1.shape[0]
    n_tile_lanes = -(-n_tiles // LANES) * LANES
    whole = lambda shape: pl.BlockSpec(shape, lambda i: (0, 0))
    return pl.pallas_call(
        functools.partial(_plan_kernel, chunk=chunk),
        grid=(1,),
        in_specs=[whole((n, LANES)), whole((n, LANES))],
        out_specs=[whole((TOP_K * n // chunk, chunk)), whole((SUBLANES, n_tile_lanes)), whole((SUBLANES, LANES))],
        out_shape=[jax.ShapeDtypeStruct((TOP_K * n // chunk, chunk), jnp.int32),
                   jax.ShapeDtypeStruct((SUBLANES, n_tile_lanes), jnp.int32),
                   jax.ShapeDtypeStruct((SUBLANES, LANES), jnp.int32)],
        compiler_params=pltpu.CompilerParams(dimension_semantics=("arbitrary",), vmem_limit_bytes=VMEM_LIMIT),
        name="moe_plan",
    )(oh1, oh2)


def _row_slice(ref, row, n_rows=1):
    start = row * SUBLANES if isinstance(row, int) else pl.multiple_of(row * SUBLANES, SUBLANES)
    return ref.at[pl.ds(start, n_rows * SUBLANES)]


def _dispatch_kernel(pad_ref, end_ref, pos_ref, trow_ref, wsg_ref, wsu_ref, wsd_ref, xs_hbm, shared_ref, zero_buf,
                     zero_sem, row_sem, *, tokens, parts):
    i = pl.program_id(0)

    def zero_fill(phase):
        def run(first_row, n_rows):
            copy = pltpu.make_async_copy(_row_slice(zero_buf, 0, n_rows), _row_slice(xs_hbm, first_row, n_rows),
                                         zero_sem)
            copy.start() if phase == "start" else copy.wait()

        for e in range(N_EXPERTS):
            row = pad_ref[e]
            n_pad = end_ref[e] - row
            bit = EXPERT_TILE // 2
            while bit >= 1:
                @pl.when((n_pad & bit) != 0)
                def _():
                    run(row, bit)
                row = row + (n_pad & bit)
                bit //= 2

        def tail(t, carry):
            run(t * EXPERT_TILE, EXPERT_TILE)
            return carry

        lax.fori_loop(pad_ref[N_EXPERTS], xs_hbm.shape[0] // (EXPERT_TILE * SUBLANES), tail, 0)

    @pl.when(i == 0)
    def _():
        zero_buf[...] = jnp.zeros(zero_buf.shape, F32)
        zero_fill("start")

    unroll = 8

    def issue(g, carry):
        for u in range(unroll):
            n = g * unroll + u
            src = _row_slice(trow_ref, n)
            for k in range(TOP_K):
                pltpu.make_async_copy(src, _row_slice(xs_hbm, pos_ref[k, n]), row_sem).start(priority=(u + k) % 2)
        return carry

    part = tokens // parts
    for p in range(parts):
        lax.fori_loop(p * part // unroll, (p + 1) * part // unroll, issue, 0)
        t = _load_token_rows(trow_ref, part, p * part).astype(BF16)
        hid = _silu(_dot(t, wsg_ref[...])) * _dot(t, wsu_ref[...])
        shared_ref[p * part:(p + 1) * part, :] = _dot(hid.astype(BF16), wsd_ref[...]).astype(shared_ref.dtype)

    for k in range(TOP_K):
        pltpu.make_async_copy(trow_ref, _row_slice(xs_hbm, 0, tokens), row_sem).wait()

    @pl.when(i + 1 == pl.num_programs(0))
    def _():
        zero_fill("wait")


def _dispatch(first_pad, seg_end, pos, trow, shared_w, sorted_rows, tokens):
    n = pos.shape[1]
    const = lambda a: pl.BlockSpec(a.shape, lambda i: (0,) * a.ndim)
    return pl.pallas_call(
        functools.partial(_dispatch_kernel, tokens=tokens, parts=DISPATCH_PARTS),
        grid=(n // tokens,),
        in_specs=[pl.BlockSpec(memory_space=pltpu.SMEM), pl.BlockSpec(memory_space=pltpu.SMEM),
                  pl.BlockSpec((TOP_K, tokens), lambda i: (0, i), memory_space=pltpu.SMEM),
                  pl.BlockSpec((tokens * SUBLANES, LANES), lambda i: (i, 0))] + [const(w) for w in shared_w],
        out_specs=[pl.BlockSpec(memory_space=pl.ANY), pl.BlockSpec((tokens, D_MODEL), lambda i: (i, 0))],
        out_shape=[jax.ShapeDtypeStruct((sorted_rows * SUBLANES, LANES), F32),
                   jax.ShapeDtypeStruct((n, D_MODEL), BF16)],
        scratch_shapes=[pltpu.VMEM((EXPERT_TILE * SUBLANES, LANES), F32), pltpu.SemaphoreType.DMA,
                        pltpu.SemaphoreType.DMA],
        compiler_params=pltpu.CompilerParams(dimension_semantics=("arbitrary",), vmem_limit_bytes=VMEM_LIMIT),
        name="moe_dispatch",
    )(first_pad, seg_end, pos, trow, *shared_w)


def _experts_kernel(te_ref, first_ref, next_ref, has_next_ref, slot_ref, used_ref, xs_hbm, wg_hbm, wu_hbm, wd_hbm,
                    ys_ref, wg_buf, wu_buf, wd_buf, sems, x_buf, x_sems):
    t = pl.program_id(0)
    used = used_ref[0]
    active = t < used

    def x_tile(tile, action):
        s = lax.rem(tile, X_RING)
        copy = pltpu.make_async_copy(_row_slice(xs_hbm, tile * EXPERT_TILE, EXPERT_TILE), x_buf.at[s], x_sems.at[s])
        copy.start() if action == "start" else copy.wait()

    @pl.when(t == 0)
    def _():
        for ahead in range(X_RING - 1):
            @pl.when(ahead < used)
            def _():
                x_tile(ahead, "start")

    @pl.when(t + (X_RING - 1) < used)
    def _():
        x_tile(t + (X_RING - 1), "start")

    def weights(e, slot, action):
        for j, (src, buf) in enumerate(((wg_hbm, wg_buf), (wu_hbm, wu_buf), (wd_hbm, wd_buf))):
            copy = pltpu.make_async_copy(src.at[e], buf.at[slot], sems.at[slot, j])
            copy.start() if action == "start" else copy.wait()

    @pl.when(t == 0)
    def _():
        weights(te_ref[0], slot_ref[0], "start")

    @pl.when(active & (first_ref[t] == 1))
    def _():
        @pl.when(has_next_ref[t] == 1)
        def _():
            weights(next_ref[t], 1 - slot_ref[t], "start")
        weights(te_ref[t], slot_ref[t], "wait")

    @pl.when(active)
    def _():
        slot = slot_ref[t]
        x_tile(t, "wait")
        x = _load_token_rows(x_buf.at[lax.rem(t, X_RING)], EXPERT_TILE).astype(BF16)
        hid = _silu(_dot(x, wg_buf[slot].astype(BF16))) * _dot(x, wu_buf[slot].astype(BF16))
        _store_token_rows(ys_ref, _dot(hid.astype(BF16), wd_buf[slot].astype(BF16)))

    @pl.when(jnp.logical_not(active))
    def _():
        ys_ref[...] = jnp.zeros(ys_ref.shape, F32)


def _experts(tile_info, used, xs, wg, wu, wd, n_tiles):
    blk = EXPERT_TILE * SUBLANES
    n_prefetch = len(tile_info) + 1
    tile_map = lambda t, te, first, nxt, has_next, slot, used: (t, 0)
    return pl.pallas_call(
        _experts_kernel,
        grid_spec=pltpu.PrefetchScalarGridSpec(
            num_scalar_prefetch=n_prefetch,
            grid=(n_tiles,),
            in_specs=[pl.BlockSpec(memory_space=pl.ANY), pl.BlockSpec(memory_space=pl.ANY),
                      pl.BlockSpec(memory_space=pl.ANY), pl.BlockSpec(memory_space=pl.ANY)],
            out_specs=pl.BlockSpec((blk, LANES), tile_map),
            scratch_shapes=[pltpu.VMEM((2, D_MODEL, D_FF), F32), pltpu.VMEM((2, D_MODEL, D_FF), F32),
                            pltpu.VMEM((2, D_FF, D_MODEL), F32), pltpu.SemaphoreType.DMA((2, 3)),
                            pltpu.VMEM((X_RING, blk, LANES), F32), pltpu.SemaphoreType.DMA((X_RING,))]),
        out_shape=jax.ShapeDtypeStruct(xs.shape, F32),
        compiler_params=pltpu.CompilerParams(dimension_semantics=("arbitrary",), vmem_limit_bytes=VMEM_LIMIT),
        name="moe_experts",
    )(*tile_info, used, xs, wg, wu, wd)


def _combine_kernel(pos_ref, pos_next_ref, shared_ref, h1_ref, wts_ref, ys_hbm, gout_ref, out_ref, ybuf, sems, *,
                    tokens):
    i = pl.program_id(0)
    slot = lax.rem(i, 2)
    other = 1 - slot
    group = COMBINE_GROUP

    def start_gathers(p_ref, dst_slot, first):
        for u in range(group):
            n = first + u
            for k in range(TOP_K):
                pltpu.make_async_copy(_row_slice(ys_hbm, p_ref[k, n]), _row_slice(ybuf.at[dst_slot].at[k], n),
                                      sems.at[dst_slot]).start(priority=(u + k) % 2)

    def wait_slot(s):
        for k in range(TOP_K):
            pltpu.make_async_copy(_row_slice(ys_hbm, 0, tokens), ybuf.at[s].at[k], sems.at[s]).wait()

    @pl.when(i == 0)
    def _():
        def first_step(g, carry):
            start_gathers(pos_ref, slot, g * group)
            return carry
        lax.fori_loop(0, tokens // group, first_step, 0)

    wait_slot(slot)

    def trip(g, carry):
        first = pl.multiple_of(g * group, group)
        rows = pl.ds(first, group)
        wts = wts_ref[rows, :]
        routed = (wts[:, 0:1] * _load_token_rows(ybuf.at[slot].at[0], group, first)
                  + wts[:, 1:2] * _load_token_rows(ybuf.at[slot].at[1], group, first))
        out_ref[rows, :] = h1_ref[rows, :] + _rms(routed + shared_ref[rows, :].astype(F32), gout_ref[...])
        start_gathers(pos_next_ref, other, first)
        return carry

    lax.fori_loop(0, tokens // group, trip, 0)

    @pl.when(i + 1 == pl.num_programs(0))
    def _():
        wait_slot(other)


def _combine(pos, shared, h1, wts, ys, gout, tokens):
    n = h1.shape[0]
    const = lambda a: pl.BlockSpec(a.shape, lambda i: (0,) * a.ndim)
    tile = lambda w: pl.BlockSpec((tokens, w), lambda i: (i, 0))
    steps = n // tokens
    return pl.pallas_call(
        functools.partial(_combine_kernel, tokens=tokens),
        grid=(steps,),
        in_specs=[pl.BlockSpec((TOP_K, tokens), lambda i: (0, i), memory_space=pltpu.SMEM),
                  pl.BlockSpec((TOP_K, tokens), lambda i: (0, jnp.minimum(i + 1, steps - 1)),
                               memory_space=pltpu.SMEM),
                  tile(D_MODEL), tile(D_MODEL), tile(LANES), pl.BlockSpec(memory_space=pl.ANY), const(gout)],
        out_specs=tile(D_MODEL),
        out_shape=jax.ShapeDtypeStruct((n, D_MODEL), F32),
        scratch_shapes=[pltpu.VMEM((2, TOP_K, tokens * SUBLANES, LANES), F32), pltpu.SemaphoreType.DMA((2,))],
        compiler_params=pltpu.CompilerParams(dimension_semantics=("arbitrary",), vmem_limit_bytes=VMEM_LIMIT),
        name="moe_combine",
    )(pos, pos, shared, h1, wts, ys, gout)


def _swap_halves(w):
    half = w.shape[-1] // 2
    return jnp.concatenate([w[..., half:], w[..., :half]], axis=-1)


def _block_diag(blocks):
    h, r, c = blocks.shape
    eye = jnp.eye(h, dtype=blocks.dtype)
    return jnp.einsum('hrc,hg->hrgc', blocks, eye).reshape(h * r, h * c)


def kernel(x, positions, meta_tokens, pre_mix_norm, w_in, q_norm, w_uq, kv_norm, w_ukv, conv_w, attn_out_norm,
           conv_out_norm, w_out, post_mix_norm, pre_ffn_norm, w_group_router, b_group_router, w_expert_router,
           b_expert_router, w_gate, w_up, w_down, w_sh_gate, w_sh_up, w_sh_down, post_ffn_norm):
    bsz, seq, d = x.shape
    n = bsz * seq
    row = lambda v: v.reshape(1, -1).astype(F32)
    assert d == D_MODEL and w_in.shape[0] == 1, "one layer of width 1024 is supported"
    assert seq % MIX_ROWS == 0 and seq % ATTN_BLOCK == 0 and n % DISPATCH_TOKENS == 0
    assert (n + BLOCK) % (ROPE_STEPS * LANES) == 0

    wi = w_in[0]
    o_kv, o_pe, o_b = Q_LORA, Q_LORA + KV_LORA, Q_LORA + KV_LORA + QK_ROPE
    k_pe = wi[:, o_pe:o_b]
    w1 = jnp.concatenate([wi[:, :o_pe], k_pe, _swap_halves(k_pe), jnp.zeros((d, LANES - 2 * QK_ROPE), F32),
                          wi[:, o_b:]], axis=1).astype(BF16)
    wq = w_uq[0].reshape(Q_LORA, N_HEADS, QK_NOPE + QK_ROPE)
    wqn = wq[:, :, :QK_NOPE].reshape(Q_LORA, N_HEADS * QK_NOPE).astype(F32)
    wq_rope = wq[:, :, QK_NOPE:]
    wqr = jnp.concatenate([wq_rope, _swap_halves(wq_rope),
                           jnp.zeros((Q_LORA, N_HEADS, LANES - 2 * QK_ROPE), F32)], axis=2)
    wqr = wqr.reshape(Q_LORA, N_HEADS * LANES).astype(F32)
    wkv = w_ukv[0].reshape(KV_LORA, N_HEADS, QK_NOPE + V_DIM)
    bduk = _block_diag(jnp.transpose(wkv[:, :, :QK_NOPE], (1, 2, 0))).astype(F32)
    wuv = _block_diag(jnp.transpose(wkv[:, :, QK_NOPE:], (1, 0, 2))).astype(BF16)
    premix_w = (row(pre_mix_norm[0]), w1, row(q_norm[0]), row(kv_norm[0]), wqn, bduk, wqr,
                conv_w[0].astype(F32), row(conv_out_norm[0]))
    wr = jnp.concatenate([w_group_router[0], jnp.zeros((d, LANES - N_GROUPS), F32),
                          w_expert_router[0], jnp.zeros((d, LANES - N_EXPERTS), F32)], axis=1).astype(BF16)
    br = jnp.concatenate([b_group_router[0], jnp.zeros((LANES - N_GROUPS,), F32),
                          b_expert_router[0], jnp.zeros((LANES - N_EXPERTS,), F32)]).reshape(1, -1).astype(F32)
    postmix_w = (wuv, row(attn_out_norm[0]), w_out[0].astype(BF16), row(post_mix_norm[0]), row(pre_ffn_norm[0]),
                 wr, br)
    expert_w = (w_gate[0], w_up[0], w_down[0])
    shared_w = (w_sh_gate[0].astype(BF16), w_sh_up[0].astype(BF16), w_sh_down[0].astype(BF16))

    pos_all = jnp.concatenate([positions.astype(jnp.int32).reshape(-1) + N_META,
                               jnp.zeros((PAD_FRONT,), jnp.int32), jnp.arange(N_META, dtype=jnp.int32)])
    inv_freq = 1.0 / (ROPE_THETA ** (jnp.arange(0, QK_ROPE, 2, dtype=F32) / QK_ROPE))
    n_tab = n + BLOCK
    tab = _rope_table(pos_all.reshape(1, n_tab), inv_freq.reshape(-1, 1), n_tab // ROPE_STEPS)

    meta_blk = jnp.pad(meta_tokens.astype(F32), ((PAD_FRONT, 0), (0, 0)))[None]
    zero_carry = jnp.zeros((8, CONV_DIM), F32)
    _, kmeta, _, meta_tail = _premix(meta_blk, tab, n // BLOCK, zero_carry, premix_w, BLOCK, 1)
    qp, kc, convn, _ = _premix(x, tab, 0, meta_tail, premix_w, MIX_ROWS, MIX_SUB_BLOCKS)

    olat = _attention(qp, kc, kmeta[0], ATTN_BLOCK, ATTN_BLOCK, ATTN_CHUNKS)

    h1, trow, oh1, oh2, wts = _postmix(x.reshape(n, d), olat.reshape(n, -1), convn.reshape(n, -1), postmix_w,
                                       MIX_ROWS, MIX_SUB_BLOCKS)

    sorted_rows = TOP_K * n + N_EXPERTS * EXPERT_TILE
    n_tiles = sorted_rows // EXPERT_TILE
    pos2, te8, misc = _plan(oh1, oh2, n_tiles, PLAN_CHUNK)
    pos = pos2.reshape(TOP_K, n)
    xs, shared = _dispatch(misc[0, :N_EXPERTS + 1], misc[1, :N_EXPERTS], pos, trow, shared_w, sorted_rows,
                           DISPATCH_TOKENS)
    tile_info = [te8[r, :n_tiles] for r in range(5)]
    ys = _experts(tile_info, misc[0, N_EXPERTS:N_EXPERTS + 1], xs, *expert_w, n_tiles)
    out = _combine(pos, shared, h1, wts, ys, row(post_ffn_norm[0]), COMBINE_TOKENS)
    return out.reshape(bsz, seq, d)
```

```python
import functools
import math

import jax
import jax.numpy as jnp
from jax import lax
from jax.experimental import pallas as pl
from jax.experimental.pallas import tpu as pltpu

D_MODEL = 1024
N_META = 16
BLOCK = 128
PAD_FRONT = BLOCK - N_META
N_HEADS = 8
QK_NOPE = 64
QK_ROPE = 32
V_DIM = 64
Q_LORA = 256
KV_LORA = 128
ROPE_THETA = 10000.0
ATTN_SCALE = (QK_NOPE + QK_ROPE) ** -0.5
Q_SCALE = ATTN_SCALE * math.log2(math.e)
ATTN_WIDTH = N_HEADS * V_DIM
CONV_DIM = 512
CONV_W = 3
N_GROUPS = 4
EXPERTS_PER_GROUP = 8
N_EXPERTS = N_GROUPS * EXPERTS_PER_GROUP
GROUP_SHIFT = EXPERTS_PER_GROUP.bit_length() - 1
assert 1 << GROUP_SHIFT == EXPERTS_PER_GROUP
D_FF = 256
EPS = 1e-6
NEG_INF = -1e30

LANES = 128
SUBLANES = 8
TOP_K = 2
EXPERT_TILE = 512
KEY_WIDTH = 2 * LANES

F32 = jnp.float32
BF16 = jnp.bfloat16
VMEM_LIMIT = 56 * 1024 * 1024

ROPE_STEPS = 3
MIX_ROWS = 1024
MIX_SUB_BLOCKS = 2
ATTN_BLOCK = 512
ATTN_CHUNKS = 16
PLAN_CHUNK = 512
RANK_UNROLL = 4
DISPATCH_TOKENS = 1024
DISPATCH_PARTS = 1
X_RING = 4
COMBINE_TOKENS = 512
COMBINE_GROUP = 32


def _rms(x, g):
    return x * lax.rsqrt(jnp.mean(x * x, axis=-1, keepdims=True) + EPS) * g


def _dot(a, b):
    return jnp.dot(a, b, preferred_element_type=F32)


def _dot_nt(a, b):
    return lax.dot_general(a, b, (((1,), (1,)), ((), ())), preferred_element_type=F32)


def _store_token_rows(ref, value, first_token=0):
    rows = value.shape[0]
    for c in range(D_MODEL // LANES):
        ref[pl.ds(first_token * SUBLANES + c, rows, stride=SUBLANES), :] = value[:, c * LANES:(c + 1) * LANES]


def _load_token_rows(ref, rows, first_token=0):
    return jnp.concatenate([ref[pl.ds(first_token * SUBLANES + c, rows, stride=SUBLANES), :]
                            for c in range(D_MODEL // LANES)], axis=1)


def _rope_table_kernel(pos_ref, invf_ref, out_ref):
    pos = pos_ref[...].astype(F32)
    ang = invf_ref[...] * pos
    c = jnp.cos(ang)
    s = jnp.sin(ang)
    planes = jnp.concatenate([c, c, -s, s, jnp.zeros((LANES - 4 * (QK_ROPE // 2), ang.shape[1]), F32)], axis=0)
    out_ref[...] = planes.T


def _rope_table(pos_row, inv_freq_col, chunk):
    n = pos_row.shape[1]
    return pl.pallas_call(
        _rope_table_kernel,
        grid=(n // chunk,),
        in_specs=[pl.BlockSpec((1, chunk), lambda i: (0, i)),
                  pl.BlockSpec((QK_ROPE // 2, 1), lambda i: (0, 0))],
        out_specs=pl.BlockSpec((chunk, LANES), lambda i: (i, 0)),
        out_shape=jax.ShapeDtypeStruct((n, LANES), F32),
        name="rope_table",
    )(pos_row, inv_freq_col)


def _premix_kernel(x_ref, tab_ref, carry_in_ref, g_pre_ref, w1_ref, gq_ref, gkv_ref, wqn_ref, bduk_ref,
                   wqr_ref, convw_ref, gconv_ref, qp_ref, kc_ref, convn_ref, utail_ref, carry_ref, wq_scr, *,
                   sub_blocks):
    t = pl.program_id(1)
    rows = x_ref.shape[1]

    @pl.when(t == 0)
    def _():
        carry_ref[...] = carry_in_ref[...]

    @pl.when((pl.program_id(0) == 0) & (t == 0))
    def _():
        wabs = jnp.dot(wqn_ref[...], bduk_ref[...], preferred_element_type=F32,
                       precision=lax.Precision.HIGHEST) * Q_SCALE
        wrope = wqr_ref[...] * Q_SCALE
        for h in range(N_HEADS):
            sl = slice(h * LANES, (h + 1) * LANES)
            wq_scr[:, h * KEY_WIDTH:h * KEY_WIDTH + LANES] = wabs[:, sl].astype(BF16)
            wq_scr[:, h * KEY_WIDTH + LANES:(h + 1) * KEY_WIDTH] = wrope[:, sl].astype(BF16)

    rows = rows // sub_blocks
    lane = lax.broadcasted_iota(jnp.int32, (rows, LANES), 1)
    row8 = lax.broadcasted_iota(jnp.int32, (8, CONV_DIM), 0)
    assert CONV_W == 3
    cw = convw_ref[...]
    prev = carry_ref[...]
    for sb in range(sub_blocks):
        r = pl.ds(sb * rows, rows)
        xn = _rms(x_ref[0, r, :], g_pre_ref[...]).astype(BF16)
        z = _dot(xn, w1_ref[...])
        tab = tab_ref[r, :]

        def rope(pair):
            prod = pair * tab
            return jnp.where(lane < QK_ROPE, prod + pltpu.roll(prod, LANES - QK_ROPE, axis=1), 0.0)

        ckvn = _rms(z[:, Q_LORA:Q_LORA + KV_LORA], gkv_ref[...])
        krope = rope(z[:, Q_LORA + KV_LORA:Q_LORA + KV_LORA + LANES])
        kc_ref[0, r, :] = jnp.concatenate([ckvn, krope], axis=1).astype(BF16)

        cqn = _rms(z[:, :Q_LORA], gq_ref[...]).astype(BF16)
        qall = _dot(cqn, wq_scr[...])
        for h in range(N_HEADS):
            qp_ref[0, h, r, 0:LANES] = qall[:, h * KEY_WIDTH:h * KEY_WIDTH + LANES].astype(BF16)
            qp_ref[0, h, r, LANES:KEY_WIDTH] = rope(
                qall[:, h * KEY_WIDTH + LANES:(h + 1) * KEY_WIDTH]).astype(BF16)

        c0 = Q_LORA + KV_LORA + LANES
        gate_b = z[:, c0:c0 + CONV_DIM]
        u = z[:, c0 + CONV_DIM:c0 + 2 * CONV_DIM] * z[:, c0 + 2 * CONV_DIM:c0 + 3 * CONV_DIM]
        r1 = pltpu.roll(u, 1, axis=0)
        r2 = pltpu.roll(u, 2, axis=0)
        p1 = pltpu.roll(prev, 1, axis=0)
        p2 = pltpu.roll(prev, 2, axis=0)
        u1 = jnp.concatenate([jnp.where(row8 < 1, p1, r1[0:8]), r1[8:]], axis=0)
        u2 = jnp.concatenate([jnp.where(row8 < 2, p2, r2[0:8]), r2[8:]], axis=0)
        y = cw[0:1] * u2 + cw[1:2] * u1 + cw[2:3] * u
        convn_ref[0, r, :] = _rms(gate_b * y, gconv_ref[...]).astype(BF16)
        prev = u[rows - 8:rows]
    carry_ref[...] = prev
    utail_ref[...] = prev


def _premix(x3, tab, tab_block0, carry_in, wts, rows, sub_blocks):
    bsz, seq, _ = x3.shape
    nt = seq // rows
    g_pre, w1, gq, gkv, wqn, bduk, wqr, convw, gconv = wts
    const = lambda a: pl.BlockSpec(a.shape, lambda b, t: (0,) * a.ndim)
    return pl.pallas_call(
        functools.partial(_premix_kernel, sub_blocks=sub_blocks),
        grid=(bsz, nt),
        in_specs=[pl.BlockSpec((1, rows, D_MODEL), lambda b, t: (b, t, 0)),
                  pl.BlockSpec((rows, LANES), lambda b, t: (tab_block0 + b * nt + t, 0)),
                  const(carry_in), const(g_pre), const(w1), const(gq), const(gkv), const(wqn), const(bduk),
                  const(wqr), const(convw), const(gconv)],
        out_specs=[pl.BlockSpec((1, N_HEADS, rows, KEY_WIDTH), lambda b, t: (b, 0, t, 0)),
                   pl.BlockSpec((1, rows, KEY_WIDTH), lambda b, t: (b, t, 0)),
                   pl.BlockSpec((1, rows, CONV_DIM), lambda b, t: (b, t, 0)),
                   pl.BlockSpec((8, CONV_DIM), lambda b, t: (0, 0))],
        out_shape=[jax.ShapeDtypeStruct((bsz, N_HEADS, seq, KEY_WIDTH), BF16),
                   jax.ShapeDtypeStruct((bsz, seq, KEY_WIDTH), BF16),
                   jax.ShapeDtypeStruct((bsz, seq, CONV_DIM), BF16),
                   jax.ShapeDtypeStruct((8, CONV_DIM), F32)],
        scratch_shapes=[pltpu.VMEM((8, CONV_DIM), F32), pltpu.VMEM((Q_LORA, N_HEADS * KEY_WIDTH), BF16)],
        compiler_params=pltpu.CompilerParams(dimension_semantics=("arbitrary", "arbitrary"),
                                             vmem_limit_bytes=VMEM_LIMIT),
        name="premix",
    )(x3, tab, carry_in, g_pre, w1, gq, gkv, wqn, bduk, wqr, convw, gconv)


def _attention_kernel(q_ref, qnext_ref, k_ref, km_ref, o_ref, s_scr, smeta_scr, tri_scr, m_scr, l_scr, acc_scr, *,
                      bq, bk, chunks):
    assert bq == bk
    b = pl.program_id(0)
    i = pl.program_id(1)
    assert chunks % N_HEADS == 0
    row_splits = chunks // N_HEADS
    c_rows = bq // row_splits
    km = km_ref[...]

    def chunk_rows(c):
        return c // row_splits, pl.ds((c % row_splits) * c_rows, c_rows)

    def scores(qr, c, kb):
        h, r = chunk_rows(c)
        return _dot_nt(qr[0, h, r, :], kb)

    def values(kb):
        return jnp.concatenate([kb[:, :KV_LORA], jnp.ones((kb.shape[0], LANES), BF16)], axis=1)

    def consume(c, s, v_ones, mask, first=False):
        if mask is not None:
            s = jnp.where(mask, s, NEG_INF)
        m_blk = jnp.broadcast_to(jnp.max(s, axis=1, keepdims=True), (c_rows, LANES))
        m_new = m_blk if first else jnp.maximum(m_scr[c], m_blk)
        p = jnp.exp2(s - jnp.concatenate([m_new] * (s.shape[1] // LANES), axis=1)).astype(BF16)
        pv = _dot(p, v_ones)
        if first:
            l_scr[c] = pv[:, KV_LORA:]
            acc_scr[c] = pv[:, :KV_LORA]
        else:
            alpha = jnp.exp2(m_scr[c] - m_new)
            l_scr[c] = alpha * l_scr[c] + pv[:, KV_LORA:]
            acc_scr[c] = alpha * acc_scr[c] + pv[:, :KV_LORA]
        m_scr[c] = m_new

    def key_block(j):
        return k_ref[0, pl.ds(pl.multiple_of(j * bk, bk), bk), :]

    @pl.when((b == 0) & (i == 0))
    def _():
        for c in range(chunks):
            smeta_scr[c] = scores(q_ref, c, km)
        rows_i = lax.broadcasted_iota(jnp.int32, (bq, bk), 0)
        cols_i = lax.broadcasted_iota(jnp.int32, (bq, bk), 1)
        tri_scr[...] = jnp.where(cols_i <= rows_i, 0.0, NEG_INF)

    n_full = (i * bq) // bk
    kb0 = key_block(0)
    v_meta = values(km)
    col = lax.broadcasted_iota(jnp.int32, (c_rows, BLOCK), 1)
    for c in range(chunks):
        s_scr[c] = scores(q_ref, c, kb0)
        consume(c, smeta_scr[c], v_meta, col >= PAD_FRONT, first=True)

    def full_block(j):
        v_ones = values(key_block(j))
        kb_next = key_block(j + 1)
        for c in range(chunks):
            s = s_scr[c]
            s_scr[c] = scores(q_ref, c, kb_next)
            consume(c, s, v_ones, None)

    def body(j, carry):
        full_block(j)
        return carry

    lax.fori_loop(0, n_full, body, 0)

    v_ones = values(key_block(n_full))
    for c in range(chunks):
        _, r = chunk_rows(c)
        n_keys = (c % row_splits + 1) * c_rows
        s = s_scr[c, :, :n_keys] + tri_scr[r, :n_keys]
        smeta_scr[c] = scores(qnext_ref, c, km)
        consume(c, s, v_ones[:n_keys], None)

    for c in range(chunks):
        h, r = chunk_rows(c)
        o_ref[0, r, h * KV_LORA:(h + 1) * KV_LORA] = (acc_scr[c] / l_scr[c]).astype(BF16)


def _attention(qp, kc, kmeta, bq, bk, chunks):
    bsz, _, seq, _ = qp.shape
    nq = seq // bq
    c_rows = N_HEADS * bq // chunks
    scratch = [pltpu.VMEM((chunks, c_rows, bk), F32), pltpu.VMEM((chunks, c_rows, BLOCK), F32),
               pltpu.VMEM((bq, bk), F32), pltpu.VMEM((chunks, c_rows, LANES), F32),
               pltpu.VMEM((chunks, c_rows, LANES), F32), pltpu.VMEM((chunks, c_rows, KV_LORA), F32)]

    def next_q(b, i):
        wrap = (i + 1) // nq
        return (jnp.minimum(b + wrap, bsz - 1), 0, i + 1 - wrap * nq, 0)

    return pl.pallas_call(
        functools.partial(_attention_kernel, bq=bq, bk=bk, chunks=chunks),
        grid=(bsz, nq),
        in_specs=[pl.BlockSpec((1, N_HEADS, bq, KEY_WIDTH), lambda b, i: (b, 0, i, 0)),
                  pl.BlockSpec((1, N_HEADS, bq, KEY_WIDTH), next_q),
                  pl.BlockSpec((1, seq, KEY_WIDTH), lambda b, i: (b, 0, 0)),
                  pl.BlockSpec((BLOCK, KEY_WIDTH), lambda b, i: (0, 0))],
        out_specs=pl.BlockSpec((1, bq, N_HEADS * KV_LORA), lambda b, i: (b, i, 0)),
        out_shape=jax.ShapeDtypeStruct((bsz, seq, N_HEADS * KV_LORA), BF16),
        scratch_shapes=scratch,
        compiler_params=pltpu.CompilerParams(dimension_semantics=("arbitrary", "arbitrary"),
                                             vmem_limit_bytes=VMEM_LIMIT),
        name="attention",
    )(qp, qp, kc, kmeta)


def _postmix_kernel(x_ref, olat_ref, convn_ref, wuv_ref, gattn_ref, wout_ref, gpost_ref, gffn_ref, wr_ref, br_ref,
                    h1_ref, trow_ref, oh1_ref, oh2_ref, wts_ref, *, sub_blocks):
    rows = x_ref.shape[0] // sub_blocks
    wout = wout_ref[...]
    lane = lax.broadcasted_iota(jnp.int32, (rows, LANES), 1)

    def first_argmax(v):
        mx = jnp.max(v, axis=1, keepdims=True)
        return mx, jnp.min(jnp.where(v == mx, lane, LANES), axis=1, keepdims=True)

    for sb in range(sub_blocks):
        r = pl.ds(sb * rows, rows)
        attn = _dot(olat_ref[r, :], wuv_ref[...])
        attn_n = _rms(attn, gattn_ref[...]).astype(BF16)
        mix = _dot(attn_n, wout[:ATTN_WIDTH]) + _dot(convn_ref[r, :], wout[ATTN_WIDTH:])
        h1 = x_ref[r, :] + _rms(mix, gpost_ref[...])
        h1_ref[r, :] = h1
        tf = _rms(h1, gffn_ref[...])
        t = tf.astype(BF16)
        _store_token_rows(trow_ref, tf, sb * rows)

        logits = _dot(t, wr_ref[...]) + br_ref[...]
        gl = jnp.where(lane < N_GROUPS, logits[:, :LANES], NEG_INF)
        gmax, gidx = first_argmax(gl)
        g_w = 1.0 / jnp.sum(jnp.exp(gl - gmax), axis=1, keepdims=True)
        el = jnp.where((lane >> GROUP_SHIFT) == gidx, logits[:, LANES:], NEG_INF)
        m1, i1 = first_argmax(el)
        m2, i2 = first_argmax(jnp.where(lane == i1, NEG_INF, el))
        e2 = jnp.exp(m2 - m1)
        den = 1.0 + e2
        oh1_ref[r, :] = jnp.where(lane == i1, 1.0, 0.0).astype(BF16)
        oh2_ref[r, :] = jnp.where(lane == i2, 1.0, 0.0).astype(BF16)
        wts_ref[r, :] = jnp.where(lane == 0, g_w / den, jnp.where(lane == 1, g_w * e2 / den, 0.0))


def _postmix(x2, olat2, convn2, wts, rows, sub_blocks):
    n = x2.shape[0]
    wuv, gattn, wout, gpost, gffn, wr, br = wts
    const = lambda a: pl.BlockSpec(a.shape, lambda i: (0,) * a.ndim)
    tile = lambda w: pl.BlockSpec((rows, w), lambda i: (i, 0))
    return pl.pallas_call(
        functools.partial(_postmix_kernel, sub_blocks=sub_blocks),
        grid=(n // rows,),
        in_specs=[tile(D_MODEL), tile(N_HEADS * KV_LORA), tile(CONV_DIM),
                  const(wuv), const(gattn), const(wout), const(gpost), const(gffn), const(wr), const(br)],
        out_specs=[tile(D_MODEL), pl.BlockSpec((rows * SUBLANES, LANES), lambda i: (i, 0)),
                   tile(LANES), tile(LANES), tile(LANES)],
        out_shape=[jax.ShapeDtypeStruct((n, D_MODEL), F32),
                   jax.ShapeDtypeStruct((n * SUBLANES, LANES), F32),
                   jax.ShapeDtypeStruct((n, LANES), BF16),
                   jax.ShapeDtypeStruct((n, LANES), BF16),
                   jax.ShapeDtypeStruct((n, LANES), F32)],
        compiler_params=pltpu.CompilerParams(dimension_semantics=("arbitrary",), vmem_limit_bytes=VMEM_LIMIT),
        name="postmix",
    )(x2, olat2, convn2, wuv, gattn, wout, gpost, gffn, wr, br)


def _silu(a):
    return a * (1.0 / (1.0 + jnp.exp(-a)))


def _plan_kernel(oh1_ref, oh2_ref, pos_ref, te_ref, misc_ref, *, chunk):
    n_tok = oh1_ref.shape[0]
    n_chunks = n_tok // chunk
    lane = lax.broadcasted_iota(jnp.int32, (SUBLANES, LANES), 1)

    def rows_of(ref, c):
        return ref[pl.ds(pl.multiple_of(c * chunk, chunk), chunk), :]

    def count_body(c, acc):
        both = rows_of(oh1_ref, c).astype(F32) + rows_of(oh2_ref, c).astype(F32)
        return acc + jnp.sum(both, axis=0, keepdims=True)

    counts = lax.fori_loop(0, n_chunks, count_body, jnp.zeros((1, LANES), F32))
    counts = jnp.broadcast_to(counts, (SUBLANES, LANES)).astype(jnp.int32)
    tile_shift = EXPERT_TILE.bit_length() - 1
    padded = ((counts + (EXPERT_TILE - 1)) >> tile_shift) << tile_shift
    ends = padded
    shift = 1
    while shift < N_EXPERTS:
        ends = ends + jnp.where(lane >= shift, pltpu.roll(ends, shift, axis=1), 0)
        shift *= 2
    base = (ends - padded)[0:1].astype(F32)

    r_iota = lax.broadcasted_iota(jnp.int32, (chunk, chunk), 0)
    c_iota = lax.broadcasted_iota(jnp.int32, (chunk, chunk), 1)
    earlier = jnp.where(c_iota < r_iota, 1.0, 0.0).astype(BF16)

    def rank_body(cc, run):
        for c in [RANK_UNROLL * cc + u for u in range(RANK_UNROLL)]:
            for slot, ref in enumerate((oh1_ref, oh2_ref)):
                oh = rows_of(ref, c)
                ohf = oh.astype(F32)
                before = _dot(earlier, oh) + run
                posv = jnp.sum(ohf * before, axis=1, keepdims=True)
                pos_row = jnp.broadcast_to(posv, (chunk, LANES)).T[0:1, :]
                pos_ref[pl.ds(slot * n_chunks + c, 1), :] = pos_row.astype(jnp.int32)
                run = run + jnp.sum(ohf, axis=0, keepdims=True)
        return run

    assert n_chunks % RANK_UNROLL == 0
    lax.fori_loop(0, n_chunks // RANK_UNROLL, rank_body, base)

    n_tile_lanes = te_ref.shape[1]
    tile_start = lax.broadcasted_iota(jnp.int32, (SUBLANES, n_tile_lanes), 1) * EXPERT_TILE
    zeros = jnp.zeros((SUBLANES, n_tile_lanes), jnp.int32)
    total = ends[:, N_EXPERTS - 1:N_EXPERTS]
    te, seg_index, prev_end, seg_end = zeros, zeros, zeros, zeros + total
    for e in range(N_EXPERTS):
        end_e = ends[:, e:e + 1]
        done = end_e <= tile_start
        te = te + jnp.where(done, 1, 0)
        seg_index = seg_index + jnp.where(done & (padded[:, e:e + 1] > 0), 1, 0)
        prev_end = jnp.maximum(prev_end, jnp.where(done, end_e, 0))
        seg_end = jnp.minimum(seg_end, jnp.where(done, total, end_e))
    nxt = zeros
    for e in range(N_EXPERTS):
        nxt = nxt + jnp.where(ends[:, e:e + 1] <= seg_end, 1, 0)
    sub_t = lax.broadcasted_iota(jnp.int32, (SUBLANES, n_tile_lanes), 0)
    rows = [jnp.minimum(te, N_EXPERTS - 1), jnp.where(prev_end == tile_start, 1, 0),
            jnp.minimum(nxt, N_EXPERTS - 1), jnp.where(seg_end < total, 1, 0), seg_index & 1]
    info = zeros
    for r, v in enumerate(rows):
        info = jnp.where(sub_t == r, v, info)
    te_ref[...] = info
    sub = lax.broadcasted_iota(jnp.int32, (SUBLANES, LANES), 0)
    used_tiles = ends[:, N_EXPERTS - 1:N_EXPERTS] >> tile_shift
    first_pad = jnp.where(lane < N_EXPERTS, ends - padded + counts, used_tiles)
    misc_ref[...] = jnp.where(sub == 0, first_pad, ends)


def _plan(oh1, oh2, n_tiles, chunk):
    n = oh1.shape[0]
    n_tile_lanes = -(-n_tiles // LANES) * LANES
    whole = lambda shape: pl.BlockSpec(shape, lambda i: (0, 0))
    return pl.pallas_call(
        functools.partial(_plan_kernel, chunk=chunk),
        grid=(1,),
        in_specs=[whole((n, LANES)), whole((n, LANES))],
        out_specs=[whole((TOP_K * n // chunk, chunk)), whole((SUBLANES, n_tile_lanes)), whole((SUBLANES, LANES))],
        out_shape=[jax.ShapeDtypeStruct((TOP_K * n // chunk, chunk), jnp.int32),
                   jax.ShapeDtypeStruct((SUBLANES, n_tile_lanes), jnp.int32),
                   jax.ShapeDtypeStruct((SUBLANES, LANES), jnp.int32)],
        compiler_params=pltpu.CompilerParams(dimension_semantics=("arbitrary",), vmem_limit_bytes=VMEM_LIMIT),
        name="moe_plan",
    )(oh1, oh2)


def _row_slice(ref, row, n_rows=1):
    start = row * SUBLANES if isinstance(row, int) else pl.multiple_of(row * SUBLANES, SUBLANES)
    return ref.at[pl.ds(start, n_rows * SUBLANES)]


def _dispatch_kernel(pad_ref, end_ref, pos_ref, trow_ref, wsg_ref, wsu_ref, wsd_ref, xs_hbm, shared_ref, zero_buf,
                     zero_sem, row_sem, *, tokens, parts):
    i = pl.program_id(0)

    def zero_fill(phase):
        def run(first_row, n_rows):
            copy = pltpu.make_async_copy(_row_slice(zero_buf, 0, n_rows), _row_slice(xs_hbm, first_row, n_rows),
                                         zero_sem)
            copy.start() if phase == "start" else copy.wait()

        for e in range(N_EXPERTS):
            row = pad_ref[e]
            n_pad = end_ref[e] - row
            bit = EXPERT_TILE // 2
            while bit >= 1:
                @pl.when((n_pad & bit) != 0)
                def _():
                    run(row, bit)
                row = row + (n_pad & bit)
                bit //= 2

        def tail(t, carry):
            run(t * EXPERT_TILE, EXPERT_TILE)
            return carry

        lax.fori_loop(pad_ref[N_EXPERTS], xs_hbm.shape[0] // (EXPERT_TILE * SUBLANES), tail, 0)

    @pl.when(i == 0)
    def _():
        zero_buf[...] = jnp.zeros(zero_buf.shape, F32)
        zero_fill("start")

    unroll = 8

    def issue(g, carry):
        for u in range(unroll):
            n = g * unroll + u
            src = _row_slice(trow_ref, n)
            for k in range(TOP_K):
                pltpu.make_async_copy(src, _row_slice(xs_hbm, pos_ref[k, n]), row_sem).start(priority=(u + k) % 2)
        return carry

    part = tokens // parts
    for p in range(parts):
        lax.fori_loop(p * part // unroll, (p + 1) * part // unroll, issue, 0)
        t = _load_token_rows(trow_ref, part, p * part).astype(BF16)
        hid = _silu(_dot(t, wsg_ref[...])) * _dot(t, wsu_ref[...])
        shared_ref[p * part:(p + 1) * part, :] = _dot(hid.astype(BF16), wsd_ref[...]).astype(shared_ref.dtype)

    for k in range(TOP_K):
        pltpu.make_async_copy(trow_ref, _row_slice(xs_hbm, 0, tokens), row_sem).wait()

    @pl.when(i + 1 == pl.num_programs(0))
    def _():
        zero_fill("wait")


def _dispatch(first_pad, seg_end, pos, trow, shared_w, sorted_rows, tokens):
    n = pos.shape[1]
    const = lambda a: pl.BlockSpec(a.shape, lambda i: (0,) * a.ndim)
    return pl.pallas_call(
        functools.partial(_dispatch_kernel, tokens=tokens, parts=DISPATCH_PARTS),
        grid=(n // tokens,),
        in_specs=[pl.BlockSpec(memory_space=pltpu.SMEM), pl.BlockSpec(memory_space=pltpu.SMEM),
                  pl.BlockSpec((TOP_K, tokens), lambda i: (0, i), memory_space=pltpu.SMEM),
                  pl.BlockSpec((tokens * SUBLANES, LANES), lambda i: (i, 0))] + [const(w) for w in shared_w],
        out_specs=[pl.BlockSpec(memory_space=pl.ANY), pl.BlockSpec((tokens, D_MODEL), lambda i: (i, 0))],
        out_shape=[jax.ShapeDtypeStruct((sorted_rows * SUBLANES, LANES), F32),
                   jax.ShapeDtypeStruct((n, D_MODEL), BF16)],
        scratch_shapes=[pltpu.VMEM((EXPERT_TILE * SUBLANES, LANES), F32), pltpu.SemaphoreType.DMA,
                        pltpu.SemaphoreType.DMA],
        compiler_params=pltpu.CompilerParams(dimension_semantics=("arbitrary",), vmem_limit_bytes=VMEM_LIMIT),
        name="moe_dispatch",
    )(first_pad, seg_end, pos, trow, *shared_w)


def _experts_kernel(te_ref, first_ref, next_ref, has_next_ref, slot_ref, used_ref, xs_hbm, wg_hbm, wu_hbm, wd_hbm,
                    ys_ref, wg_buf, wu_buf, wd_buf, sems, x_buf, x_sems):
    t = pl.program_id(0)
    used = used_ref[0]
    active = t < used

    def x_tile(tile, action):
        s = lax.rem(tile, X_RING)
        copy = pltpu.make_async_copy(_row_slice(xs_hbm, tile * EXPERT_TILE, EXPERT_TILE), x_buf.at[s], x_sems.at[s])
        copy.start() if action == "start" else copy.wait()

    @pl.when(t == 0)
    def _():
        for ahead in range(X_RING - 1):
            @pl.when(ahead < used)
            def _():
                x_tile(ahead, "start")

    @pl.when(t + (X_RING - 1) < used)
    def _():
        x_tile(t + (X_RING - 1), "start")

    def weights(e, slot, action):
        for j, (src, buf) in enumerate(((wg_hbm, wg_buf), (wu_hbm, wu_buf), (wd_hbm, wd_buf))):
            copy = pltpu.make_async_copy(src.at[e], buf.at[slot], sems.at[slot, j])
            copy.start() if action == "start" else copy.wait()

    @pl.when(t == 0)
    def _():
        weights(te_ref[0], slot_ref[0], "start")

    @pl.when(active & (first_ref[t] == 1))
    def _():
        @pl.when(has_next_ref[t] == 1)
        def _():
            weights(next_ref[t], 1 - slot_ref[t], "start")
        weights(te_ref[t], slot_ref[t], "wait")

    @pl.when(active)
    def _():
        slot = slot_ref[t]
        x_tile(t, "wait")
        x = _load_token_rows(x_buf.at[lax.rem(t, X_RING)], EXPERT_TILE).astype(BF16)
        hid = _silu(_dot(x, wg_buf[slot].astype(BF16))) * _dot(x, wu_buf[slot].astype(BF16))
        _store_token_rows(ys_ref, _dot(hid.astype(BF16), wd_buf[slot].astype(BF16)))

    @pl.when(jnp.logical_not(active))
    def _():
        ys_ref[...] = jnp.zeros(ys_ref.shape, F32)


def _experts(tile_info, used, xs, wg, wu, wd, n_tiles):
    blk = EXPERT_TILE * SUBLANES
    n_prefetch = len(tile_info) + 1
    tile_map = lambda t, te, first, nxt, has_next, slot, used: (t, 0)
    return pl.pallas_call(
        _experts_kernel,
        grid_spec=pltpu.PrefetchScalarGridSpec(
            num_scalar_prefetch=n_prefetch,
            grid=(n_tiles,),
            in_specs=[pl.BlockSpec(memory_space=pl.ANY), pl.BlockSpec(memory_space=pl.ANY),
                      pl.BlockSpec(memory_space=pl.ANY), pl.BlockSpec(memory_space=pl.ANY)],
            out_specs=pl.BlockSpec((blk, LANES), tile_map),
            scratch_shapes=[pltpu.VMEM((2, D_MODEL, D_FF), F32), pltpu.VMEM((2, D_MODEL, D_FF), F32),
                            pltpu.VMEM((2, D_FF, D_MODEL), F32), pltpu.SemaphoreType.DMA((2, 3)),
                            pltpu.VMEM((X_RING, blk, LANES), F32), pltpu.SemaphoreType.DMA((X_RING,))]),
        out_shape=jax.ShapeDtypeStruct(xs.shape, F32),
        compiler_params=pltpu.CompilerParams(dimension_semantics=("arbitrary",), vmem_limit_bytes=VMEM_LIMIT),
        name="moe_experts",
    )(*tile_info, used, xs, wg, wu, wd)


def _combine_kernel(pos_ref, pos_next_ref, shared_ref, h1_ref, wts_ref, ys_hbm, gout_ref, out_ref, ybuf, sems, *,
                    tokens):
    i = pl.program_id(0)
    slot = lax.rem(i, 2)
    other = 1 - slot
    group = COMBINE_GROUP

    def start_gathers(p_ref, dst_slot, first):
        for u in range(group):
            n = first + u
            for k in range(TOP_K):
                pltpu.make_async_copy(_row_slice(ys_hbm, p_ref[k, n]), _row_slice(ybuf.at[dst_slot].at[k], n),
                                      sems.at[dst_slot]).start(priority=(u + k) % 2)

    def wait_slot(s):
        for k in range(TOP_K):
            pltpu.make_async_copy(_row_slice(ys_hbm, 0, tokens), ybuf.at[s].at[k], sems.at[s]).wait()

    @pl.when(i == 0)
    def _():
        def first_step(g, carry):
            start_gathers(pos_ref, slot, g * group)
            return carry
        lax.fori_loop(0, tokens // group, first_step, 0)

    wait_slot(slot)

    def trip(g, carry):
        first = pl.multiple_of(g * group, group)
        rows = pl.ds(first, group)
        wts = wts_ref[rows, :]
        routed = (wts[:, 0:1] * _load_token_rows(ybuf.at[slot].at[0], group, first)
                  + wts[:, 1:2] * _load_token_rows(ybuf.at[slot].at[1], group, first))
        out_ref[rows, :] = h1_ref[rows, :] + _rms(routed + shared_ref[rows, :].astype(F32), gout_ref[...])
        start_gathers(pos_next_ref, other, first)
        return carry

    lax.fori_loop(0, tokens // group, trip, 0)

    @pl.when(i + 1 == pl.num_programs(0))
    def _():
        wait_slot(other)


def _combine(pos, shared, h1, wts, ys, gout, tokens):
    n = h1.shape[0]
    const = lambda a: pl.BlockSpec(a.shape, lambda i: (0,) * a.ndim)
    tile = lambda w: pl.BlockSpec((tokens, w), lambda i: (i, 0))
    steps = n // tokens
    return pl.pallas_call(
        functools.partial(_combine_kernel, tokens=tokens),
        grid=(steps,),
        in_specs=[pl.BlockSpec((TOP_K, tokens), lambda i: (0, i), memory_space=pltpu.SMEM),
                  pl.BlockSpec((TOP_K, tokens), lambda i: (0, jnp.minimum(i + 1, steps - 1)),
                               memory_space=pltpu.SMEM),
                  tile(D_MODEL), tile(D_MODEL), tile(LANES), pl.BlockSpec(memory_space=pl.ANY), const(gout)],
        out_specs=tile(D_MODEL),
        out_shape=jax.ShapeDtypeStruct((n, D_MODEL), F32),
        scratch_shapes=[pltpu.VMEM((2, TOP_K, tokens * SUBLANES, LANES), F32), pltpu.SemaphoreType.DMA((2,))],
        compiler_params=pltpu.CompilerParams(dimension_semantics=("arbitrary",), vmem_limit_bytes=VMEM_LIMIT),
        name="moe_combine",
    )(pos, pos, shared, h1, wts, ys, gout)


def _swap_halves(w):
    half = w.shape[-1] // 2
    return jnp.concatenate([w[..., half:], w[..., :half]], axis=-1)


def _block_diag(blocks):
    h, r, c = blocks.shape
    eye = jnp.eye(h, dtype=blocks.dtype)
    return jnp.einsum('hrc,hg->hrgc', blocks, eye).reshape(h * r, h * c)


def kernel(x, positions, meta_tokens, pre_mix_norm, w_in, q_norm, w_uq, kv_norm, w_ukv, conv_w, attn_out_norm,
           conv_out_norm, w_out, post_mix_norm, pre_ffn_norm, w_group_router, b_group_router, w_expert_router,
           b_expert_router, w_gate, w_up, w_down, w_sh_gate, w_sh_up, w_sh_down, post_ffn_norm):
    bsz, seq, d = x.shape
    n = bsz * seq
    row = lambda v: v.reshape(1, -1).astype(F32)
    assert d == D_MODEL and w_in.shape[0] == 1, "one layer of width 1024 is supported"
    assert seq % MIX_ROWS == 0 and seq % ATTN_BLOCK == 0 and n % DISPATCH_TOKENS == 0
    assert (n + BLOCK) % (ROPE_STEPS * LANES) == 0

    wi = w_in[0]
    o_kv, o_pe, o_b = Q_LORA, Q_LORA + KV_LORA, Q_LORA + KV_LORA + QK_ROPE
    k_pe = wi[:, o_pe:o_b]
    w1 = jnp.concatenate([wi[:, :o_pe], k_pe, _swap_halves(k_pe), jnp.zeros((d, LANES - 2 * QK_ROPE), F32),
                          wi[:, o_b:]], axis=1).astype(BF16)
    wq = w_uq[0].reshape(Q_LORA, N_HEADS, QK_NOPE + QK_ROPE)
    wqn = wq[:, :, :QK_NOPE].reshape(Q_LORA, N_HEADS * QK_NOPE).astype(F32)
    wq_rope = wq[:, :, QK_NOPE:]
    wqr = jnp.concatenate([wq_rope, _swap_halves(wq_rope),
                           jnp.zeros((Q_LORA, N_HEADS, LANES - 2 * QK_ROPE), F32)], axis=2)
    wqr = wqr.reshape(Q_LORA, N_HEADS * LANES).astype(F32)
    wkv = w_ukv[0].reshape(KV_LORA, N_HEADS, QK_NOPE + V_DIM)
    bduk = _block_diag(jnp.transpose(wkv[:, :, :QK_NOPE], (1, 2, 0))).astype(F32)
    wuv = _block_diag(jnp.transpose(wkv[:, :, QK_NOPE:], (1, 0, 2))).astype(BF16)
    premix_w = (row(pre_mix_norm[0]), w1, row(q_norm[0]), row(kv_norm[0]), wqn, bduk, wqr,
                conv_w[0].astype(F32), row(conv_out_norm[0]))
    wr = jnp.concatenate([w_group_router[0], jnp.zeros((d, LANES - N_GROUPS), F32),
                          w_expert_router[0], jnp.zeros((d, LANES - N_EXPERTS), F32)], axis=1).astype(BF16)
    br = jnp.concatenate([b_group_router[0], jnp.zeros((LANES - N_GROUPS,), F32),
                          b_expert_router[0], jnp.zeros((LANES - N_EXPERTS,), F32)]).reshape(1, -1).astype(F32)
    postmix_w = (wuv, row(attn_out_norm[0]), w_out[0].astype(BF16), row(post_mix_norm[0]), row(pre_ffn_norm[0]),
                 wr, br)
    expert_w = (w_gate[0], w_up[0], w_down[0])
    shared_w = (w_sh_gate[0].astype(BF16), w_sh_up[0].astype(BF16), w_sh_down[0].astype(BF16))

    pos_all = jnp.concatenate([positions.astype(jnp.int32).reshape(-1) + N_META,
                               jnp.zeros((PAD_FRONT,), jnp.int32), jnp.arange(N_META, dtype=jnp.int32)])
    inv_freq = 1.0 / (ROPE_THETA ** (jnp.arange(0, QK_ROPE, 2, dtype=F32) / QK_ROPE))
    n_tab = n + BLOCK
    tab = _rope_table(pos_all.reshape(1, n_tab), inv_freq.reshape(-1, 1), n_tab // ROPE_STEPS)

    meta_blk = jnp.pad(meta_tokens.astype(F32), ((PAD_FRONT, 0), (0, 0)))[None]
    zero_carry = jnp.zeros((8, CONV_DIM), F32)
    _, kmeta, _, meta_tail = _premix(meta_blk, tab, n // BLOCK, zero_carry, premix_w, BLOCK, 1)
    qp, kc, convn, _ = _premix(x, tab, 0, meta_tail, premix_w, MIX_ROWS, MIX_SUB_BLOCKS)

    olat = _attention(qp, kc, kmeta[0], ATTN_BLOCK, ATTN_BLOCK, ATTN_CHUNKS)

    h1, trow, oh1, oh2, wts = _postmix(x.reshape(n, d), olat.reshape(n, -1), convn.reshape(n, -1), postmix_w,
                                       MIX_ROWS, MIX_SUB_BLOCKS)

    sorted_rows = TOP_K * n + N_EXPERTS * EXPERT_TILE
    n_tiles = sorted_rows // EXPERT_TILE
    pos2, te8, misc = _plan(oh1, oh2, n_tiles, PLAN_CHUNK)
    pos = pos2.reshape(TOP_K, n)
    xs, shared = _dispatch(misc[0, :N_EXPERTS + 1], misc[1, :N_EXPERTS], pos, trow, shared_w, sorted_rows,
                           DISPATCH_TOKENS)
    tile_info = [te8[r, :n_tiles] for r in range(5)]
    ys = _experts(tile_info, misc[0, N_EXPERTS:N_EXPERTS + 1], xs, *expert_w, n_tiles)
    out = _combine(pos, shared, h1, wts, ys, row(post_ffn_norm[0]), COMBINE_TOKENS)
    return out.reshape(bsz, seq, d)
```
